```python
import math
import jax, jax.numpy as jnp
from jax import lax
import numpy as np

D_MODEL = 1024
BATCH = 2
SEQ = 8192
DEPTH = 1
DEC_BATCH = 32
DEC_SEQ = 4
PAST_LEN = 16384
PAGE_SIZE = 128

RWKV_HEAD_DIM = 64
RWKV_HEADS = D_MODEL // RWKV_HEAD_DIM
RWKV_WIDTH = RWKV_HEADS * RWKV_HEAD_DIM
DECAY_LORA = 64
AAA_LORA = 64
GATE_LORA = 128
LNX_EPS = 64e-5
RWKV_COLS = 3 * RWKV_WIDTH + DECAY_LORA + AAA_LORA + GATE_LORA

ATTN_HEAD_DIM = 64
HEADS_PER_GROUP = 4
WINDOWS = (128, 512, 2048)
DILATIONS = (1, 4, 16)
N_GROUPS = len(WINDOWS)
ATTN_WIDTH = N_GROUPS * HEADS_PER_GROUP * ATTN_HEAD_DIM
ATTN_OUT_WIDTH = HEADS_PER_GROUP * ATTN_HEAD_DIM
ATTN_SCALE = ATTN_HEAD_DIM ** -0.5
BLOCK = 128
NEG = -1e30

D_FF = 4 * D_MODEL
PLE_DIM = 256
NORM_EPS = 1e-6
IN_COLS = RWKV_COLS + 3 * ATTN_WIDTH + 2 * D_MODEL

kernel_name = "rwkv7_dilated_swa_hybrid_step"


def rmsnorm(x, g):
    xf = x.astype(jnp.float32)
    xf = xf * lax.rsqrt(jnp.mean(xf * xf, axis=-1, keepdims=True) + NORM_EPS)
    return (xf * g.astype(jnp.float32)).astype(x.dtype)


def wkv_scan(r, w, k, v, a, b, s0):
    xs = tuple(jnp.moveaxis(t.astype(jnp.float32), 1, 0) for t in (r, w, k, v, a, b))

    def step(s, inp):
        r_t, w_t, k_t, v_t, a_t, b_t = inp
        sa = jnp.einsum('nhvk,nhk->nhv', s, a_t)
        s = s * w_t[:, :, None, :] + sa[..., None] * b_t[:, :, None, :] + v_t[..., None] * k_t[:, :, None, :]
        return s, jnp.einsum('nhvk,nhk->nhv', s, r_t)

    s_T, ys = lax.scan(step, s0.astype(jnp.float32), xs)
    return jnp.moveaxis(ys, 0, 1), s_T


def rwkv_branch(z, z_prev, wkv0, shift_mu, decay_w0, decay_w2, aaa_a0, aaa_a2, gate_g2,
                k_k, k_a, r_k, lnx_w, lnx_b):
    N, T, _ = z.shape
    prev = jnp.concatenate([z_prev[:, None].astype(z.dtype), z[:, :-1]], axis=1)
    zs = z + (prev - z) * shift_mu
    r, k, v, xw, xa, xg = jnp.split(
        zs, [RWKV_WIDTH, 2 * RWKV_WIDTH, 3 * RWKV_WIDTH,
             3 * RWKV_WIDTH + DECAY_LORA, 3 * RWKV_WIDTH + DECAY_LORA + AAA_LORA], axis=-1)
    logw = -jax.nn.softplus(-(decay_w0 + jnp.tanh(xw) @ decay_w2)) - 0.5
    decay = jnp.exp(-jnp.exp(logw.astype(jnp.float32)))
    a = jax.nn.sigmoid(aaa_a0 + xa @ aaa_a2)
    g = jax.nn.sigmoid(xg) @ gate_g2

    def heads(t):
        return t.reshape(N, T, RWKV_HEADS, RWKV_HEAD_DIM)

    kk = heads(k * k_k).astype(jnp.float32)
    kk = kk / jnp.maximum(jnp.sqrt(jnp.sum(kk * kk, axis=-1, keepdims=True)), 1e-12)
    k = k * (1 + (a - 1) * k_a)
    r_h, k_h, v_h, a_h, w_h = heads(r), heads(k), heads(v), heads(a), heads(decay)
    y, s_T = wkv_scan(r_h, w_h, k_h, v_h, -kk, kk * a_h.astype(jnp.float32), wkv0)
    mu = jnp.mean(y, axis=-1, keepdims=True)
    var = jnp.mean(jnp.square(y - mu), axis=-1, keepdims=True)
    y = ((y - mu) * lax.rsqrt(var + LNX_EPS)).reshape(N, T, RWKV_WIDTH)
    y = y * lnx_w.astype(jnp.float32) + lnx_b.astype(jnp.float32)
    bonus = jnp.sum((r_h * k_h * r_k).astype(jnp.float32), axis=-1, keepdims=True) * v_h.astype(jnp.float32)
    out = ((y + bonus.reshape(N, T, RWKV_WIDTH)) * g.astype(jnp.float32)).astype(z.dtype)
    return out, s_T.astype(z.dtype), z[:, -1]


def dilated_attn_prompt(q, k, v, dilation, n_back):
    N, T, H, Dh = q.shape
    span = dilation * BLOCK
    Tp = -(-T // span) * span
    S = Tp // dilation
    nb = S // BLOCK

    def to_blocks(t):
        t = jnp.pad(t, ((0, 0), (0, Tp - T), (0, 0), (0, 0)))
        t = t.reshape(N, S, dilation, H, Dh).transpose(0, 2, 1, 3, 4)
        return t.reshape(N, dilation, nb, BLOCK, H, Dh).astype(jnp.float32)

    def with_prev(t):
        prev = jnp.pad(t[:, :, :-1], ((0, 0), (0, 0), (1, 0), (0, 0), (0, 0), (0, 0)))
        return jnp.concatenate([prev, t], axis=3)

    qb = to_blocks(q)
    kb = with_prev(to_blocks(k))
    vb = with_prev(to_blocks(v))
    s = jnp.einsum('ndbqhe,ndbkhe->ndbhqk', qb, kb) * ATTN_SCALE
    qi = jnp.arange(BLOCK)[:, None]
    ki = jnp.arange(2 * BLOCK)[None, :]
    dist = qi + BLOCK - ki
    band = (dist >= 0) & (dist <= n_back)
    has_prev = (jnp.arange(nb)[:, None, None] > 0) | (ki >= BLOCK)[None]
    mask = band[None] & has_prev
    s = jnp.where(mask[None, None, :, None], s, NEG)
    m = jnp.max(s, axis=-1, keepdims=True)
    e = jnp.exp(s - m)
    den = jnp.sum(e, axis=-1, keepdims=True)
    o = jnp.einsum('ndbhqk,ndbkhe->ndbqhe', e / den, vb)
    lse = (m + jnp.log(den))[..., 0].transpose(0, 1, 2, 4, 3)

    def from_blocks(t):
        rest = t.shape[4:]
        t = t.reshape(N, dilation, S, *rest).swapaxes(1, 2)
        return t.reshape(N, Tp, *rest)[:, :T]

    return from_blocks(o), from_blocks(lse)


def dilated_attn_sample(q, k, v, kv_cache, dilation, n_back):
    N, T = q.shape[:2]
    L = kv_cache.shape[1]
    keys = jnp.concatenate([kv_cache[:, :, 0], k], axis=1)
    vals = jnp.concatenate([kv_cache[:, :, 1], v], axis=1)
    idx = L + jnp.arange(T)[:, None] - dilation * jnp.arange(n_back + 1)[None, :]
    valid = idx >= 0
    idx = jnp.maximum(idx, 0)
    kg = keys[:, idx].astype(jnp.float32)
    vg = vals[:, idx].astype(jnp.float32)
    s = jnp.einsum('nthe,ntjhe->nthj', q.astype(jnp.float32), kg) * ATTN_SCALE
    s = jnp.where(valid[None, :, None, :], s, NEG)
    m = jnp.max(s, axis=-1, keepdims=True)
    e = jnp.exp(s - m)
    den = jnp.sum(e, axis=-1, keepdims=True)
    o = jnp.einsum('nthj,ntjhe->nthe', e / den, vg)
    return o, (m + jnp.log(den))[..., 0]


def decoder_layer(x, p, wkv0, shift0, kv_caches, prm):
    (norm_mix, w_in, shift_mu, decay_w0, decay_w2, aaa_a0, aaa_a2, gate_g2, k_k, k_a, r_k,
     lnx_w, lnx_b, w_branch_rwkv, w_branch_attn, w_out, norm_mlp, w_ff1, w_ff2,
     norm_ple, w_ple, w_ple_gate) = prm
    N, T, _ = x.shape
    h = rmsnorm(x, norm_mix)
    z = h @ w_in
    z_rwkv, z_q, z_k, z_v, z_gate = jnp.split(
        z, [RWKV_COLS, RWKV_COLS + ATTN_WIDTH, RWKV_COLS + 2 * ATTN_WIDTH,
            RWKV_COLS + 3 * ATTN_WIDTH], axis=-1)

    rwkv_out, wkv_T, shift_T = rwkv_branch(z_rwkv, shift0, wkv0, shift_mu, decay_w0, decay_w2,
                                           aaa_a0, aaa_a2, gate_g2, k_k, k_a, r_k, lnx_w, lnx_b)

    gshape = (N, T, N_GROUPS, HEADS_PER_GROUP, ATTN_HEAD_DIM)
    q, k, v = z_q.reshape(gshape), z_k.reshape(gshape), z_v.reshape(gshape)
    outs, lses, new_kv = [], [], []
    for gi in range(N_GROUPS):
        d = DILATIONS[gi]
        n_back = WINDOWS[gi] // d
        qg, kg, vg = q[:, :, gi], k[:, :, gi], v[:, :, gi]
        if kv_caches is None:
            o, l = dilated_attn_prompt(qg, kg, vg, d, n_back)
            keep = min(WINDOWS[gi], T)
            new_kv.append(jnp.stack([kg[:, T - keep:], vg[:, T - keep:]], axis=2))
        else:
            o, l = dilated_attn_sample(qg, kg, vg, kv_caches[gi], d, n_back)
            new_kv.append(jnp.stack([kg, vg], axis=2))
        outs.append(o)
        lses.append(l)
    alpha = jax.nn.softmax(jnp.stack(lses, axis=2), axis=2)
    attn = jnp.einsum('ntgh,ntghe->nthe', alpha, jnp.stack(outs, axis=2))
    attn = attn.reshape(N, T, ATTN_OUT_WIDTH).astype(x.dtype)

    gates = jax.nn.sigmoid(z_gate)
    mixed = gates[..., :D_MODEL] * (rwkv_out @ w_branch_rwkv) + gates[..., D_MODEL:] * (attn @ w_branch_attn)
    x = x + mixed @ w_out

    h = rmsnorm(x, norm_mlp)
    x = x + jnp.square(jax.nn.relu(h @ w_ff1)) @ w_ff2

    h = rmsnorm(x, norm_ple)
    x = x + jax.nn.sigmoid(h @ w_ple_gate) * (p @ w_ple)
    return x, new_kv, wkv_T, shift_T


def setup_inputs(seed: int = 0) -> dict:
    key = jax.random.key(seed)
    ks = iter(jax.random.split(key, 48))
    f32 = jnp.float32

    def nrm(shape, scale=1.0):
        return jax.random.normal(next(ks), shape, f32) * scale

    def gain(shape):
        return 1.0 + 0.05 * jax.random.normal(next(ks), shape, f32)

    cl = [min(w, PAST_LEN) for w in WINDOWS]
    kvs = (HEADS_PER_GROUP, ATTN_HEAD_DIM)
    return {
        "x_prompt": nrm((BATCH, SEQ, D_MODEL)),
        "x_sample": nrm((DEC_BATCH, DEC_SEQ, D_MODEL)),
        "cache_kv_g0": nrm((DEPTH, DEC_BATCH, cl[0], 2) + kvs),
        "cache_kv_g1": nrm((DEPTH, DEC_BATCH, cl[1], 2) + kvs),
        "cache_kv_g2": nrm((DEPTH, DEC_BATCH, cl[2], 2) + kvs),
        "state_wkv": nrm((DEPTH, DEC_BATCH, RWKV_HEADS, RWKV_HEAD_DIM, RWKV_HEAD_DIM), 0.5),
        "state_shift": nrm((DEPTH, DEC_BATCH, RWKV_COLS)),
        "p_prompt": nrm((DEPTH, BATCH, SEQ, PLE_DIM)),
        "p_sample": nrm((DEPTH, DEC_BATCH, DEC_SEQ, PLE_DIM)),
        "norm_mix": gain((DEPTH, D_MODEL)),
        "w_in": nrm((DEPTH, D_MODEL, IN_COLS), D_MODEL ** -0.5),
        "shift_mu": jax.random.uniform(next(ks), (DEPTH, RWKV_COLS), f32),
        "decay_w0": jax.random.uniform(next(ks), (DEPTH, RWKV_WIDTH), f32, -6.0, -1.0),
        "decay_w2": nrm((DEPTH, DECAY_LORA, RWKV_WIDTH), 0.1),
        "aaa_a0": nrm((DEPTH, RWKV_WIDTH), 0.5),
        "aaa_a2": nrm((DEPTH, AAA_LORA, RWKV_WIDTH), 0.1),
        "gate_g2": nrm((DEPTH, GATE_LORA, RWKV_WIDTH), GATE_LORA ** -0.5),
        "k_k": 0.85 + 0.05 * jax.random.normal(next(ks), (DEPTH, RWKV_WIDTH), f32),
        "k_a": gain((DEPTH, RWKV_WIDTH)),
        "r_k": nrm((DEPTH, RWKV_HEADS, RWKV_HEAD_DIM), 0.1),
        "lnx_w": gain((DEPTH, RWKV_WIDTH)),
        "lnx_b": nrm((DEPTH, RWKV_WIDTH), 0.01),
        "w_branch_rwkv": nrm((DEPTH, RWKV_WIDTH, D_MODEL), RWKV_WIDTH ** -0.5),
        "w_branch_attn": nrm((DEPTH, ATTN_OUT_WIDTH, D_MODEL), ATTN_OUT_WIDTH ** -0.5),
        "w_out": nrm((DEPTH, D_MODEL, D_MODEL), D_MODEL ** -0.5),
        "norm_mlp": gain((DEPTH, D_MODEL)),
        "w_ff1": nrm((DEPTH, D_MODEL, D_FF), D_MODEL ** -0.5),
        "w_ff2": nrm((DEPTH, D_FF, D_MODEL), D_FF ** -0.5),
        "norm_ple": gain((DEPTH, D_MODEL)),
        "w_ple": nrm((DEPTH, PLE_DIM, D_MODEL), PLE_DIM ** -0.5),
        "w_ple_gate": nrm((DEPTH, D_MODEL, D_MODEL), D_MODEL ** -0.5),
        "norm_final": gain((D_MODEL,)),
    }


def reference(x_prompt, x_sample, cache_kv_g0, cache_kv_g1, cache_kv_g2, state_wkv, state_shift,
              p_prompt, p_sample, norm_mix, w_in, shift_mu, decay_w0, decay_w2, aaa_a0, aaa_a2,
              gate_g2, k_k, k_a, r_k, lnx_w, lnx_b, w_branch_rwkv, w_branch_attn, w_out,
              norm_mlp, w_ff1, w_ff2, norm_ple, w_ple, w_ple_gate, norm_final):
    yp, ys = x_prompt, x_sample
    nb_p = x_prompt.shape[0]
    kvp = [[], [], []]
    kvs = [[], [], []]
    wkv_p, sh_p, wkv_s, sh_s = [], [], [], []
    for i in range(DEPTH):
        prm = (norm_mix[i], w_in[i], shift_mu[i], decay_w0[i], decay_w2[i], aaa_a0[i], aaa_a2[i],
               gate_g2[i], k_k[i], k_a[i], r_k[i], lnx_w[i], lnx_b[i], w_branch_rwkv[i],
               w_branch_attn[i], w_out[i], norm_mlp[i], w_ff1[i], w_ff2[i], norm_ple[i],
               w_ple[i], w_ple_gate[i])
        wkv0 = jnp.zeros((nb_p, RWKV_HEADS, RWKV_HEAD_DIM, RWKV_HEAD_DIM), x_prompt.dtype)
        shift0 = jnp.zeros((nb_p, RWKV_COLS), x_prompt.dtype)
        yp, nkv_p, w_p, s_p = decoder_layer(yp, p_prompt[i], wkv0, shift0, None, prm)
        ys, nkv_s, w_s, s_s = decoder_layer(
            ys, p_sample[i], state_wkv[i], state_shift[i],
            (cache_kv_g0[i], cache_kv_g1[i], cache_kv_g2[i]), prm)
        for gi in range(N_GROUPS):
            kvp[gi].append(nkv_p[gi])
            kvs[gi].append(nkv_s[gi])
        wkv_p.append(w_p)
        sh_p.append(s_p)
        wkv_s.append(w_s)
        sh_s.append(s_s)
    y_prompt = rmsnorm(yp, norm_final)
    y_sample = rmsnorm(ys, norm_final)
    return (y_prompt, y_sample,
            jnp.stack(kvp[0]), jnp.stack(kvp[1]), jnp.stack(kvp[2]),
            jnp.stack(wkv_p), jnp.stack(sh_p),
            jnp.stack(kvs[0]), jnp.stack(kvs[1]), jnp.stack(kvs[2]),
            jnp.stack(wkv_s), jnp.stack(sh_s))
```

```python
import functools

import jax
import jax.numpy as jnp
from jax import lax
from jax.experimental import pallas as pl
from jax.experimental.pallas import tpu as pltpu

F32 = jnp.float32
BF16 = jnp.bfloat16

HEAD = 64
LANES = 128
SUBLANES = 8
ATTN_HEADS = 4
ATTN_BLOCK = 128
WINDOWS = (128, 512, 2048)
DILATIONS = (1, 4, 16)
NORM_EPS = 1e-6
LNX_EPS = 64e-5
NEG = -1e30
VMEM_LIMIT = 56 * 1024 * 1024


def _cparams(sem):
    return pltpu.CompilerParams(dimension_semantics=sem, vmem_limit_bytes=VMEM_LIMIT)


def _dot(a, b):
    return jnp.dot(a, b, preferred_element_type=F32)


def _split(x):
    hi = x.astype(BF16)
    lo = (x - hi.astype(F32)).astype(BF16)
    return hi, lo


def _dot3(a, b):
    a_hi, a_lo = _split(a)
    b_hi, b_lo = _split(b)
    return _dot(a_hi, b_hi) + (_dot(a_lo, b_hi) + _dot(a_hi, b_lo))


def _pair_ones():
    r = lax.broadcasted_iota(jnp.int32, (LANES, LANES), 0) // HEAD
    c = lax.broadcasted_iota(jnp.int32, (LANES, LANES), 1) // HEAD
    return (r == c).astype(BF16)


def _head_sum(x, ones):
    outs = []
    for c in range(x.shape[1] // LANES):
        hi, lo = _split(x[:, c * LANES:(c + 1) * LANES])
        outs.append(_dot(hi, ones) + _dot(lo, ones))
    return jnp.concatenate(outs, axis=1)


def _rms(x, g):
    return x * lax.rsqrt(jnp.mean(x * x, axis=-1, keepdims=True) + NORM_EPS) * g


def _sigmoid(x):
    return 1.0 / (1.0 + jnp.exp(-x))


def _softplus(x):
    return jnp.maximum(x, 0.0) + jnp.log(1.0 + jnp.exp(-jnp.abs(x)))


def _norm_matmul_kernel(x_ref, g_ref, w_ref, o_ref, h_scr):
    @pl.when(pl.program_id(1) == 0)
    def _():
        h_scr[...] = _rms(x_ref[...], g_ref[...]).astype(BF16)

    o_ref[...] = _dot(h_scr[...], w_ref[...])


def _norm_matmul(x, g, w, tm, tn):
    rows, d = x.shape
    n_out = w.shape[1]
    return pl.pallas_call(
        _norm_matmul_kernel,
        grid=(rows // tm, n_out // tn),
        in_specs=[pl.BlockSpec((tm, d), lambda i, j: (i, 0)),
                  pl.BlockSpec((1, d), lambda i, j: (0, 0)),
                  pl.BlockSpec((d, tn), lambda i, j: (0, j))],
        out_specs=pl.BlockSpec((tm, tn), lambda i, j: (i, j)),
        out_shape=jax.ShapeDtypeStruct((rows, n_out), F32),
        scratch_shapes=[pltpu.VMEM((tm, d), BF16)],
        compiler_params=_cparams(("arbitrary", "arbitrary")),
        name="norm_matmul",
    )(x, g.reshape(1, d), w)


def _rwkv_prep_kernel(z_ref, sh_ref, mu_ref, w0_ref, w2_ref, a0_ref, a2_ref, g2_ref, kk_ref, ka_ref,
                      rk_ref, r_o, w_o, k_o, v_o, a_o, b_o, g_o, bon_o, last_scr, *, width):
    n = pl.program_id(0)
    i = pl.program_id(1)
    z = z_ref[0]
    tt = z.shape[0]

    @pl.when(i == 0)
    def _():
        last_scr[...] = sh_ref[pl.ds(n, 1), :]

    row = lax.broadcasted_iota(jnp.int32, z.shape, 0)
    prev = jnp.where(row == 0, last_scr[...], pltpu.roll(z, 1, axis=0))
    last_scr[...] = z[tt - 1:tt, :]
    zs = z + (prev - z) * mu_ref[...]

    c0 = 3 * width
    c1 = c0 + w2_ref.shape[0]
    c2 = c1 + a2_ref.shape[0]
    r = zs[:, 0:width]
    k = zs[:, width:2 * width]
    v = zs[:, 2 * width:c0]
    xw = zs[:, c0:c1]
    xa = zs[:, c1:c2]
    xg = zs[:, c2:]

    logw = -_softplus(-(w0_ref[...] + _dot3(jnp.tanh(xw), w2_ref[...]))) - 0.5
    decay = jnp.exp(-jnp.exp(logw))
    a = _sigmoid(a0_ref[...] + _dot3(xa, a2_ref[...]))
    g = _dot3(_sigmoid(xg), g2_ref[...])

    ones = _pair_ones()
    kk = k * kk_ref[...]
    kk = kk / jnp.maximum(jnp.sqrt(_head_sum(kk * kk, ones)), 1e-12)
    k2 = k * (1.0 + (a - 1.0) * ka_ref[...])
    bonus = _head_sum(r * k2 * rk_ref[...], ones) * v

    r_o[0] = r
    w_o[0] = decay
    k_o[0] = k2
    v_o[0] = v
    a_o[0] = -kk
    b_o[0] = kk * a
    g_o[0] = g
    bon_o[0] = bonus


def _rwkv_prep(z_rwkv, shift0, prm, tt):
    n, t, cols = z_rwkv.shape
    width = prm["decay_w0"].shape[-1]
    row = lambda x: x.reshape(1, -1)
    full = lambda a: pl.BlockSpec(a.shape, lambda b, i: (0,) * a.ndim)
    args = [shift0, row(prm["shift_mu"]), row(prm["decay_w0"]), prm["decay_w2"], row(prm["aaa_a0"]),
            prm["aaa_a2"], prm["gate_g2"], row(prm["k_k"]), row(prm["k_a"]), row(prm["r_k"])]
    out_spec = pl.BlockSpec((1, tt, width), lambda b, i: (b, i, 0))
    out_sds = jax.ShapeDtypeStruct((n, t, width), F32)
    return pl.pallas_call(
        functools.partial(_rwkv_prep_kernel, width=width),
        grid=(n, t // tt),
        in_specs=[pl.BlockSpec((1, tt, cols), lambda b, i: (b, i, 0))] + [full(a) for a in args],
        out_specs=[out_spec] * 8,
        out_shape=[out_sds] * 8,
        scratch_shapes=[pltpu.VMEM((1, cols), F32)],
        compiler_params=_cparams(("arbitrary", "arbitrary")),
        name="rwkv_prep",
    )(z_rwkv, *args)


def _wkv_kernel(r_ref, w_ref, k_ref, v_ref, a_ref, b_ref, s0_ref, y_ref, st_ref, s_scr, *,
                group, valid):
    i = pl.program_id(2)

    @pl.when(i == 0)
    def _():
        s_scr[...] = s0_ref[0, 0]

    left = lax.broadcasted_iota(jnp.int32, (HEAD, LANES), 1) < HEAD

    def col_tile(xt, j):
        return jnp.where(left, xt[0:HEAD, j:j + 1], xt[HEAD:2 * HEAD, j:j + 1])

    def step(gi, s):
        rows = pl.ds(pl.multiple_of(gi * group, group), group)
        rt = r_ref[0, rows, :].T
        wt = w_ref[0, rows, :].T
        kt = k_ref[0, rows, :].T
        at = a_ref[0, rows, :].T
        bt = b_ref[0, rows, :].T
        vg = v_ref[0, rows, :]
        ys = []
        for j in range(group):
            if j < valid:
                sa = jnp.sum(s * col_tile(at, j), axis=0, keepdims=True)
                s = s * col_tile(wt, j) + col_tile(bt, j) * sa + col_tile(kt, j) * vg[j:j + 1, :]
                ys.append(jnp.sum(s * col_tile(rt, j), axis=0, keepdims=True))
            else:
                ys.append(jnp.zeros((1, LANES), F32))
        y_ref[0, rows, :] = jnp.concatenate(ys, axis=0)
        return s

    s = lax.fori_loop(0, y_ref.shape[1] // group, step, s_scr[...])
    s_scr[...] = s

    @pl.when(i == pl.num_programs(2) - 1)
    def _():
        st_ref[0, 0] = s


def _wkv(r, w, k, v, a, b, s0, tb, valid):
    n, t, c = r.shape
    pairs = c // LANES
    tok = pl.BlockSpec((1, tb, LANES), lambda bi, p, i: (bi, i, p))
    st = pl.BlockSpec((1, 1, HEAD, LANES), lambda bi, p, i: (bi, p, 0, 0))
    return pl.pallas_call(
        functools.partial(_wkv_kernel, group=SUBLANES, valid=valid),
        grid=(n, pairs, t // tb),
        in_specs=[tok] * 6 + [st],
        out_specs=[tok, st],
        out_shape=[jax.ShapeDtypeStruct((n, t, c), F32), jax.ShapeDtypeStruct(s0.shape, F32)],
        scratch_shapes=[pltpu.VMEM((HEAD, LANES), F32)],
        compiler_params=_cparams(("arbitrary", "arbitrary", "arbitrary")),
        name="wkv_scan",
    )(r, w, k, v, a, b, s0)


def _pack_state(s):
    n, h, vd, kd = s.shape
    return s.reshape(n, h // 2, 2, vd, kd).transpose(0, 1, 4, 2, 3).reshape(n, h // 2, kd, 2 * vd)


def _unpack_state(s, h):
    n, _, kd, _ = s.shape
    return s.reshape(n, h // 2, kd, 2, HEAD).transpose(0, 1, 3, 4, 2).reshape(n, h, HEAD, kd)


def _attn_prompt_kernel(q_ref, kc_ref, kp_ref, vc_ref, vp_ref, o_ref, l_ref, *, n_back, scale):
    blk = pl.program_id(2)
    q = q_ref[0]
    kcat = jnp.concatenate([kp_ref[0], kc_ref[0]], axis=0)
    vcat = jnp.concatenate([vp_ref[0], vc_ref[0]], axis=0)
    qi = lax.broadcasted_iota(jnp.int32, (ATTN_BLOCK, 2 * ATTN_BLOCK), 0)
    ki = lax.broadcasted_iota(jnp.int32, (ATTN_BLOCK, 2 * ATTN_BLOCK), 1)
    dist = qi + ATTN_BLOCK - ki
    mask = (dist >= 0) & (dist <= n_back) & ((ki >= ATTN_BLOCK) | (blk > 0))
    outs, lses = [], []
    for h in range(ATTN_HEADS):
        sl = slice(h * HEAD, (h + 1) * HEAD)
        s = lax.dot_general(q[:, sl].astype(BF16), kcat[:, sl].astype(BF16),
                            (((1,), (1,)), ((), ())), preferred_element_type=F32) * scale
        s = jnp.where(mask, s, NEG)
        m = jnp.max(s, axis=-1, keepdims=True)
        e = jnp.exp(s - m)
        den = jnp.sum(e, axis=-1, keepdims=True)
        outs.append(_dot((e / den).astype(BF16), vcat[:, sl].astype(BF16)))
        lses.append(jnp.broadcast_to(m + jnp.log(den), (ATTN_BLOCK, HEAD)))
    o_ref[0] = jnp.concatenate(outs, axis=1)
    l_ref[0] = jnp.concatenate(lses, axis=1)


def _attn_prompt(zqkv, gi):
    n, t, cols = zqkv.shape
    d = DILATIONS[gi]
    n_back = WINDOWS[gi] // d
    aw = ATTN_HEADS * HEAD
    s_len = t // d
    nb = s_len // ATTN_BLOCK
    zr = zqkv.reshape(n, s_len, d * cols)
    cb = cols // aw
    ng = len(DILATIONS)

    def spec(which, prev):
        off = which * ng + gi
        if prev:
            return pl.BlockSpec((1, ATTN_BLOCK, aw), lambda b, r, i: (b, jnp.maximum(i - 1, 0), r * cb + off))
        return pl.BlockSpec((1, ATTN_BLOCK, aw), lambda b, r, i: (b, i, r * cb + off))

    out_spec = pl.BlockSpec((1, ATTN_BLOCK, aw), lambda b, r, i: (b, i, r))
    out_sds = jax.ShapeDtypeStruct((n, s_len, d * aw), F32)
    o, l = pl.pallas_call(
        functools.partial(_attn_prompt_kernel, n_back=n_back, scale=HEAD ** -0.5),
        grid=(n, d, nb),
        in_specs=[spec(0, False), spec(1, False), spec(1, True), spec(2, False), spec(2, True)],
        out_specs=[out_spec, out_spec],
        out_shape=[out_sds, out_sds],
        compiler_params=_cparams(("arbitrary", "arbitrary", "arbitrary")),
        name=f"attn_prompt_g{gi}",
    )(zr, zr, zr, zr, zr)
    return o.reshape(n, t, aw), l.reshape(n, t, aw)


def _attn_sample_kernel(q_ref, k_ref, v_ref, c_ref, o_ref, l_ref, *, dil, n_back, valid, scale):
    q = q_ref[0]
    tp = q.shape[0]
    cache = c_ref[0]
    cl = cache.shape[0]
    aw = ATTN_HEADS * HEAD
    keys = jnp.concatenate([cache[:, :aw], k_ref[0]], axis=0)
    vals = jnp.concatenate([cache[:, aw:], v_ref[0]], axis=0)
    ti = lax.broadcasted_iota(jnp.int32, (tp, cl + tp), 0)
    pi = lax.broadcasted_iota(jnp.int32, (tp, cl + tp), 1)
    dist = cl + ti - pi
    mask = (dist >= 0) & (dist % dil == 0) & (dist <= dil * n_back) & (pi < cl + valid)
    outs, lses = [], []
    for h in range(ATTN_HEADS):
        sl = slice(h * HEAD, (h + 1) * HEAD)
        s = lax.dot_general(q[:, sl].astype(BF16), keys[:, sl].astype(BF16),
                            (((1,), (1,)), ((), ())), preferred_element_type=F32) * scale
        s = jnp.where(mask, s, NEG)
        m = jnp.max(s, axis=-1, keepdims=True)
        e = jnp.exp(s - m)
        den = jnp.sum(e, axis=-1, keepdims=True)
        outs.append(_dot((e / den).astype(BF16), vals[:, sl].astype(BF16)))
        lses.append(jnp.broadcast_to(m + jnp.log(den), (tp, HEAD)))
    o_ref[0] = jnp.concatenate(outs, axis=1)
    l_ref[0] = jnp.concatenate(lses, axis=1)


def _attn_sample(zqkv, cache, gi, valid):
    n, tp, cols = zqkv.shape
    aw = ATTN_HEADS * HEAD
    cl = cache.shape[1]
    ng = len(DILATIONS)
    d = DILATIONS[gi]
    spec = lambda which: pl.BlockSpec((1, tp, aw), lambda b: (b, 0, which * ng + gi))
    out_spec = pl.BlockSpec((1, tp, aw), lambda b: (b, 0, 0))
    out_sds = jax.ShapeDtypeStruct((n, tp, aw), F32)
    return pl.pallas_call(
        functools.partial(_attn_sample_kernel, dil=d, n_back=WINDOWS[gi] // d, valid=valid,
                          scale=HEAD ** -0.5),
        grid=(n,),
        in_specs=[spec(0), spec(1), spec(2), pl.BlockSpec((1, cl, 2 * aw), lambda b: (b, 0, 0))],
        out_specs=[out_spec, out_spec],
        out_shape=[out_sds, out_sds],
        compiler_params=_cparams(("arbitrary",)),
        name=f"attn_sample_g{gi}",
    )(zqkv, zqkv, zqkv, cache.reshape(n, cl, 2 * aw))


def _merge_kernel(x_ref, y_ref, bon_ref, g_ref, lw_ref, lb_ref, o0, o1, o2, l0, l1, l2, zg_ref,
                  wbr_ref, wba_ref, wout_ref, out_ref):
    ones = _pair_ones()
    y = y_ref[...]
    mu = _head_sum(y, ones) * (1.0 / HEAD)
    yc = y - mu
    var = _head_sum(yc * yc, ones) * (1.0 / HEAD)
    yn = yc * lax.rsqrt(var + LNX_EPS) * lw_ref[...] + lb_ref[...]
    rwkv = (yn + bon_ref[...]) * g_ref[...]

    ls = [l0[...], l1[...], l2[...]]
    m = jnp.maximum(jnp.maximum(ls[0], ls[1]), ls[2])
    es = [jnp.exp(l - m) for l in ls]
    attn = (es[0] * o0[...] + es[1] * o1[...] + es[2] * o2[...]) / (es[0] + es[1] + es[2])

    d = x_ref.shape[1]
    gates = _sigmoid(zg_ref[...])
    mixed = (gates[:, :d] * _dot(rwkv.astype(BF16), wbr_ref[...])
             + gates[:, d:] * _dot(attn.astype(BF16), wba_ref[...]))
    out_ref[...] = x_ref[...] + _dot(mixed.astype(BF16), wout_ref[...])


def _merge(x, y, bonus, g, lnx_w, lnx_b, os_, ls_, zg, wbr, wba, wout, tm):
    rows, d = x.shape
    rowblk = lambda a: pl.BlockSpec((tm, a.shape[1]), lambda i: (i, 0))
    full = lambda a: pl.BlockSpec(a.shape, lambda i: (0, 0))
    lw, lb = lnx_w.reshape(1, -1), lnx_b.reshape(1, -1)
    args = [x, y, bonus, g, lw, lb, *os_, *ls_, zg, wbr, wba, wout]
    specs = [rowblk(x), rowblk(y), rowblk(bonus), rowblk(g), full(lw), full(lb)] + \
            [rowblk(a) for a in (*os_, *ls_)] + [rowblk(zg), full(wbr), full(wba), full(wout)]
    return pl.pallas_call(
        _merge_kernel,
        grid=(rows // tm,),
        in_specs=specs,
        out_specs=pl.BlockSpec((tm, d), lambda i: (i, 0)),
        out_shape=jax.ShapeDtypeStruct((rows, d), F32),
        compiler_params=_cparams(("arbitrary",)),
        name="merge_out",
    )(*args)


def _mlp_kernel(x_ref, g_ref, w1_ref, w2_ref, o_ref, h_scr, acc_scr):
    j = pl.program_id(1)

    @pl.when(j == 0)
    def _():
        h_scr[...] = _rms(x_ref[...], g_ref[...]).astype(BF16)
        acc_scr[...] = x_ref[...]

    u = jnp.maximum(_dot(h_scr[...], w1_ref[...]), 0.0)
    acc_scr[...] += _dot((u * u).astype(BF16), w2_ref[...])

    @pl.when(j == pl.num_programs(1) - 1)
    def _():
        o_ref[...] = acc_scr[...]


def _mlp(x, g, w1, w2, tm, tf):
    rows, d = x.shape
    dff = w1.shape[1]
    return pl.pallas_call(
        _mlp_kernel,
        grid=(rows // tm, dff // tf),
        in_specs=[pl.BlockSpec((tm, d), lambda i, j: (i, 0)),
                  pl.BlockSpec((1, d), lambda i, j: (0, 0)),
                  pl.BlockSpec((d, tf), lambda i, j: (0, j)),
                  pl.BlockSpec((tf, d), lambda i, j: (j, 0))],
        out_specs=pl.BlockSpec((tm, d), lambda i, j: (i, 0)),
        out_shape=jax.ShapeDtypeStruct((rows, d), F32),
        scratch_shapes=[pltpu.VMEM((tm, d), BF16), pltpu.VMEM((tm, d), F32)],
        compiler_params=_cparams(("arbitrary", "arbitrary")),
        name="mlp",
    )(x, g.reshape(1, d), w1, w2)


def _ple_kernel(x_ref, p_ref, g_ref, wg_ref, wp_ref, gf_ref, o_ref, *, final):
    x = x_ref[...]
    h = _rms(x, g_ref[...]).astype(BF16)
    x = x + _sigmoid(_dot(h, wg_ref[...])) * _dot(p_ref[...].astype(BF16), wp_ref[...])
    if final:
        x = _rms(x, gf_ref[...])
    o_ref[...] = x


def _ple(x, p, g, wg, wp, gf, final, tm):
    rows, d = x.shape
    rowblk = lambda a: pl.BlockSpec((tm, a.shape[1]), lambda i: (i, 0))
    full = lambda a: pl.BlockSpec(a.shape, lambda i: (0, 0))
    g, gf = g.reshape(1, d), gf.reshape(1, d)
    return pl.pallas_call(
        functools.partial(_ple_kernel, final=final),
        grid=(rows // tm,),
        in_specs=[rowblk(x), rowblk(p), full(g), full(wg), full(wp), full(gf)],
        out_specs=pl.BlockSpec((tm, d), lambda i: (i, 0)),
        out_shape=jax.ShapeDtypeStruct((rows, d), F32),
        compiler_params=_cparams(("arbitrary",)),
        name="ple",
    )(x, p, g, wg, wp, gf)


def _layer(x, p, wkv0, shift0, caches, prm, norm_final, final, valid):
    n, t, d = x.shape
    rows = n * t
    width = prm["decay_w0"].shape[-1]
    rwkv_cols = prm["shift_mu"].shape[-1]
    aw = ATTN_HEADS * HEAD
    qkv_cols = 3 * len(DILATIONS) * aw
    tm = min(512, rows)
    x2 = x.reshape(rows, d)

    w_in = prm["w_in"]
    pick = lambda cols, opts: next(o for o in opts if cols % o == 0)
    z_rwkv = _norm_matmul(x2, prm["norm_mix"], w_in[:, :rwkv_cols].astype(BF16), tm,
                          pick(rwkv_cols, (1664, 128)))
    z_qkv = _norm_matmul(x2, prm["norm_mix"], w_in[:, rwkv_cols:rwkv_cols + qkv_cols].astype(BF16),
                         tm, pick(qkv_cols, (1152, 256)))
    z_gate = _norm_matmul(x2, prm["norm_mix"], w_in[:, rwkv_cols + qkv_cols:].astype(BF16), tm,
                          pick(2 * d, (1024, 128)))
    z_rwkv = z_rwkv.reshape(n, t, rwkv_cols)
    z_qkv = z_qkv.reshape(n, t, qkv_cols)

    r, w, k, v, a, b, g, bonus = _rwkv_prep(z_rwkv, shift0, prm, min(256, t))
    heads = width // HEAD
    tb = min(ATTN_BLOCK, t)
    y, s_t = _wkv(r, w, k, v, a, b, _pack_state(wkv0), tb, min(valid, SUBLANES))
    wkv_t = _unpack_state(s_t, heads)
    shift_t = z_rwkv[:, valid - 1, :]

    os_, ls_, new_kv = [], [], []
    ng = len(DILATIONS)
    for gi in range(ng):
        kg = z_qkv[:, :, (ng + gi) * aw:(ng + gi + 1) * aw].reshape(n, t, ATTN_HEADS, HEAD)
        vg = z_qkv[:, :, (2 * ng + gi) * aw:(2 * ng + gi + 1) * aw].reshape(n, t, ATTN_HEADS, HEAD)
        if caches is None:
            o, l = _attn_prompt(z_qkv, gi)
            keep = min(WINDOWS[gi], t)
            new_kv.append(jnp.stack([kg[:, t - keep:], vg[:, t - keep:]], axis=2))
        else:
            o, l = _attn_sample(z_qkv, caches[gi], gi, valid)
            new_kv.append(jnp.stack([kg[:, :valid], vg[:, :valid]], axis=2))
        os_.append(o.reshape(rows, aw))
        ls_.append(l.reshape(rows, aw))

    flat = lambda u: u.reshape(rows, -1)
    x2 = _merge(x2, flat(y), flat(bonus), flat(g), prm["lnx_w"], prm["lnx_b"], os_, ls_, z_gate,
                prm["w_branch_rwkv"].astype(BF16), prm["w_branch_attn"].astype(BF16),
                prm["w_out"].astype(BF16), min(256, rows))
    x2 = _mlp(x2, prm["norm_mlp"], prm["w_ff1"].astype(BF16), prm["w_ff2"].astype(BF16), tm, 1024)
    x2 = _ple(x2, p.reshape(rows, -1), prm["norm_ple"], prm["w_ple_gate"].astype(BF16),
              prm["w_ple"].astype(BF16), norm_final, final, tm)
    return x2.reshape(n, t, d), new_kv, wkv_t, shift_t


_LAYER_PARAMS = ("norm_mix", "w_in", "shift_mu", "decay_w0", "decay_w2", "aaa_a0", "aaa_a2", "gate_g2",
                 "k_k", "k_a", "r_k", "lnx_w", "lnx_b", "w_branch_rwkv", "w_branch_attn", "w_out",
                 "norm_mlp", "w_ff1", "w_ff2", "norm_ple", "w_ple", "w_ple_gate")


def kernel(x_prompt, x_sample, cache_kv_g0, cache_kv_g1, cache_kv_g2, state_wkv, state_shift, p_prompt, p_sample, norm_mix, w_in, shift_mu, decay_w0, decay_w2, aaa_a0, aaa_a2, gate_g2, k_k, k_a, r_k, lnx_w, lnx_b, w_branch_rwkv, w_branch_attn, w_out, norm_mlp, w_ff1, w_ff2, norm_ple, w_ple, w_ple_gate, norm_final):
    stacked = dict(zip(_LAYER_PARAMS, (norm_mix, w_in, shift_mu, decay_w0, decay_w2, aaa_a0, aaa_a2,
                                       gate_g2, k_k, k_a, r_k, lnx_w, lnx_b, w_branch_rwkv,
                                       w_branch_attn, w_out, norm_mlp, w_ff1, w_ff2, norm_ple, w_ple,
                                       w_ple_gate)))
    depth = norm_mix.shape[0]
    nb_p, _, d = x_prompt.shape
    nb_s, t_s, _ = x_sample.shape
    heads = decay_w0.shape[-1] // HEAD
    rwkv_cols = shift_mu.shape[-1]
    t_pad = -(-t_s // SUBLANES) * SUBLANES
    pad_t = lambda u: jnp.pad(u, ((0, 0), (0, t_pad - t_s), (0, 0)))

    yp, ys = x_prompt, pad_t(x_sample)
    kvp, kvs = [[], [], []], [[], [], []]
    wkv_p, sh_p, wkv_s, sh_s = [], [], [], []
    for i in range(depth):
        prm = {name: val[i] for name, val in stacked.items()}
        prm["r_k"] = prm["r_k"].reshape(-1)
        last = i == depth - 1
        wkv0 = jnp.zeros((nb_p, heads, HEAD, HEAD), F32)
        shift0 = jnp.zeros((nb_p, rwkv_cols), F32)
        yp, nkv_p, w_p, s_p = _layer(yp, p_prompt[i], wkv0, shift0, None, prm, norm_final, last,
                                     x_prompt.shape[1])
        ys, nkv_s, w_s, s_s = _layer(ys, pad_t(p_sample[i]), state_wkv[i], state_shift[i],
                                     (cache_kv_g0[i], cache_kv_g1[i], cache_kv_g2[i]), prm, norm_final,
                                     last, t_s)
        for gi in range(3):
            kvp[gi].append(nkv_p[gi])
            kvs[gi].append(nkv_s[gi])
        wkv_p.append(w_p)
        sh_p.append(s_p)
        wkv_s.append(w_s)
        sh_s.append(s_s)
    return (yp, ys[:, :t_s],
            jnp.stack(kvp[0]), jnp.stack(kvp[1]), jnp.stack(kvp[2]),
            jnp.stack(wkv_p), jnp.stack(sh_p),
            jnp.stack(kvs[0]), jnp.stack(kvs[1]), jnp.stack(kvs[2]),
            jnp.stack(wkv_s), jnp.stack(sh_s))
```

```python
import functools

import jax
import jax.numpy as jnp
from jax import lax
from jax.experimental import pallas as pl
from jax.experimental.pallas import tpu as pltpu

F32 = jnp.float32
BF16 = jnp.bfloat16

HEAD = 64
LANES = 128
SUBLANES = 8
ATTN_HEADS = 4
ATTN_BLOCK = 128
WINDOWS = (128, 512, 2048)
DILATIONS = (1, 4, 16)
NORM_EPS = 1e-6
LNX_EPS = 64e-5
NEG = -1e30
VMEM_LIMIT = 56 * 1024 * 1024


def _cparams(sem):
    return pltpu.CompilerParams(dimension_semantics=sem, vmem_limit_bytes=VMEM_LIMIT)


def _dot(a, b):
    return jnp.dot(a, b, preferred_element_type=F32)


def _split(x):
    hi = x.astype(BF16)
    lo = (x - hi.astype(F32)).astype(BF16)
    return hi, lo


def _dot3(a, b):
    a_hi, a_lo = _split(a)
    b_hi, b_lo = _split(b)
    return _dot(a_hi, b_hi) + (_dot(a_lo, b_hi) + _dot(a_hi, b_lo))


def _pair_ones():
    r = lax.broadcasted_iota(jnp.int32, (LANES, LANES), 0) // HEAD
    c = lax.broadcasted_iota(jnp.int32, (LANES, LANES), 1) // HEAD
    return (r == c).astype(BF16)


def _head_sum(x, ones):
    outs = []
    for c in range(x.shape[1] // LANES):
        hi, lo = _split(x[:, c * LANES:(c + 1) * LANES])
        outs.append(_dot(hi, ones) + _dot(lo, ones))
    return jnp.concatenate(outs, axis=1)


def _rms(x, g):
    return x * lax.rsqrt(jnp.mean(x * x, axis=-1, keepdims=True) + NORM_EPS) * g


def _sigmoid(x):
    return 1.0 / (1.0 + jnp.exp(-x))


def _softplus(x):
    return jnp.maximum(x, 0.0) + jnp.log(1.0 + jnp.exp(-jnp.abs(x)))


def _norm_matmul_kernel(x_ref, g_ref, w_ref, o_ref, h_scr):
    @pl.when(pl.program_id(1) == 0)
    def _():
        h_scr[...] = _rms(x_ref[...], g_ref[...]).astype(BF16)

    o_ref[...] = _dot(h_scr[...], w_ref[...])


def _norm_matmul(x, g, w, tm, tn):
    rows, d = x.shape
    n_out = w.shape[1]
    return pl.pallas_call(
        _norm_matmul_kernel,
        grid=(rows // tm, n_out // tn),
        in_specs=[pl.BlockSpec((tm, d), lambda i, j: (i, 0)),
                  pl.BlockSpec((1, d), lambda i, j: (0, 0)),
                  pl.BlockSpec((d, tn), lambda i, j: (0, j))],
        out_specs=pl.BlockSpec((tm, tn), lambda i, j: (i, j)),
        out_shape=jax.ShapeDtypeStruct((rows, n_out), F32),
        scratch_shapes=[pltpu.VMEM((tm, d), BF16)],
        compiler_params=_cparams(("arbitrary", "arbitrary")),
        name="norm_matmul",
    )(x, g.reshape(1, d), w)


def _rwkv_prep_kernel(z_ref, sh_ref, mu_ref, w0_ref, w2_ref, a0_ref, a2_ref, g2_ref, kk_ref, ka_ref,
                      rk_ref, r_o, w_o, k_o, v_o, a_o, b_o, g_o, bon_o, last_scr, *, width):
    n = pl.program_id(0)
    i = pl.program_id(1)
    z = z_ref[0]
    tt = z.shape[0]

    @pl.when(i == 0)
    def _():
        last_scr[...] = sh_ref[pl.ds(n, 1), :]

    row = lax.broadcasted_iota(jnp.int32, z.shape, 0)
    prev = jnp.where(row == 0, last_scr[...], pltpu.roll(z, 1, axis=0))
    last_scr[...] = z[tt - 1:tt, :]
    zs = z + (prev - z) * mu_ref[...]

    c0 = 3 * width
    c1 = c0 + w2_ref.shape[0]
    c2 = c1 + a2_ref.shape[0]
    r = zs[:, 0:width]
    k = zs[:, width:2 * width]
    v = zs[:, 2 * width:c0]
    xw = zs[:, c0:c1]
    xa = zs[:, c1:c2]
    xg = zs[:, c2:]

    logw = -_softplus(-(w0_ref[...] + _dot3(jnp.tanh(xw), w2_ref[...]))) - 0.5
    a = _sigmoid(a0_ref[...] + _dot3(xa, a2_ref[...]))
    g = _dot3(_sigmoid(xg), g2_ref[...])

    ones = _pair_ones()
    kk = k * kk_ref[...]
    kk = kk / jnp.maximum(jnp.sqrt(_head_sum(kk * kk, ones)), 1e-12)
    k2 = k * (1.0 + (a - 1.0) * ka_ref[...])
    bonus = _head_sum(r * k2 * rk_ref[...], ones) * v

    r_o[0] = r
    w_o[0] = -jnp.exp(logw)
    k_o[0] = k2
    v_o[0] = v
    a_o[0] = -kk
    b_o[0] = kk * a
    g_o[0] = g
    bon_o[0] = bonus


def _rwkv_prep(z_rwkv, shift0, prm, tt):
    n, t, cols = z_rwkv.shape
    width = prm["decay_w0"].shape[-1]
    row = lambda x: x.reshape(1, -1)
    full = lambda a: pl.BlockSpec(a.shape, lambda b, i: (0,) * a.ndim)
    args = [shift0, row(prm["shift_mu"]), row(prm["decay_w0"]), prm["decay_w2"], row(prm["aaa_a0"]),
            prm["aaa_a2"], prm["gate_g2"], row(prm["k_k"]), row(prm["k_a"]), row(prm["r_k"])]
    out_spec = pl.BlockSpec((1, tt, width), lambda b, i: (b, i, 0))
    out_sds = jax.ShapeDtypeStruct((n, t, width), F32)
    return pl.pallas_call(
        functools.partial(_rwkv_prep_kernel, width=width),
        grid=(n, t // tt),
        in_specs=[pl.BlockSpec((1, tt, cols), lambda b, i: (b, i, 0))] + [full(a) for a in args],
        out_specs=[out_spec] * 8,
        out_shape=[out_sds] * 8,
        scratch_shapes=[pltpu.VMEM((1, cols), F32)],
        compiler_params=_cparams(("arbitrary", "arbitrary")),
        name="rwkv_prep",
    )(z_rwkv, *args)


def _wkv_kernel(r_ref, w_ref, k_ref, v_ref, a_ref, b_ref, s0_ref, y_ref, st_ref, s_scr, *,
                group, valid):
    i = pl.program_id(2)

    @pl.when(i == 0)
    def _():
        s_scr[...] = s0_ref[0, 0]

    left = lax.broadcasted_iota(jnp.int32, (HEAD, LANES), 1) < HEAD

    def col_tile(xt, j):
        return jnp.where(left, xt[0:HEAD, j:j + 1], xt[HEAD:2 * HEAD, j:j + 1])

    def step(gi, s):
        rows = pl.ds(pl.multiple_of(gi * group, group), group)
        rt = r_ref[0, rows, :].T
        wt = jnp.exp(w_ref[0, rows, :]).T
        kt = k_ref[0, rows, :].T
        at = a_ref[0, rows, :].T
        bt = b_ref[0, rows, :].T
        vg = v_ref[0, rows, :]
        ys = []
        for j in range(group):
            if j < valid:
                sa = jnp.sum(s * col_tile(at, j), axis=0, keepdims=True)
                s = s * col_tile(wt, j) + col_tile(bt, j) * sa + col_tile(kt, j) * vg[j:j + 1, :]
                ys.append(jnp.sum(s * col_tile(rt, j), axis=0, keepdims=True))
            else:
                ys.append(jnp.zeros((1, LANES), F32))
        y_ref[0, rows, :] = jnp.concatenate(ys, axis=0)
        return s

    s = lax.fori_loop(0, y_ref.shape[1] // group, step, s_scr[...])
    s_scr[...] = s

    @pl.when(i == pl.num_programs(2) - 1)
    def _():
        st_ref[0, 0] = s


def _wkv(r, w, k, v, a, b, s0, tb, valid):
    n, t, c = r.shape
    pairs = c // LANES
    tok = pl.BlockSpec((1, tb, LANES), lambda bi, p, i: (bi, i, p))
    st = pl.BlockSpec((1, 1, HEAD, LANES), lambda bi, p, i: (bi, p, 0, 0))
    return pl.pallas_call(
        functools.partial(_wkv_kernel, group=SUBLANES, valid=valid),
        grid=(n, pairs, t // tb),
        in_specs=[tok] * 6 + [st],
        out_specs=[tok, st],
        out_shape=[jax.ShapeDtypeStruct((n, t, c), F32), jax.ShapeDtypeStruct(s0.shape, F32)],
        scratch_shapes=[pltpu.VMEM((HEAD, LANES), F32)],
        compiler_params=_cparams(("arbitrary", "arbitrary", "arbitrary")),
        name="wkv_scan",
    )(r, w, k, v, a, b, s0)


WKV_CHUNK = 64


def _bd(x):
    xb = x.astype(BF16)
    left = lax.broadcasted_iota(jnp.int32, x.shape, 1) < HEAD
    zero = jnp.zeros_like(xb)
    return jnp.concatenate([jnp.where(left, xb, zero), jnp.where(left, zero, xb)], axis=0)


def _dot_nt(a, b):
    return lax.dot_general(a, b, (((1,), (1,)), ((), ())), preferred_element_type=F32)


def _dot_tn(a, b):
    return lax.dot_general(a, b, (((0,), (0,)), ((), ())), preferred_element_type=F32)


def _wkv_chunk_local(r, lw, k, v, a, b, consts):
    tri, eye2, strict, incl, blocks = consts
    c = WKV_CHUNK
    n = range(len(r))
    zero = jnp.zeros((c, LANES), F32)

    def cumsum(x):
        hi = x.astype(BF16)
        r1 = x - hi.astype(F32)
        mid = r1.astype(BF16)
        lo = (r1 - mid.astype(F32)).astype(BF16)
        return _dot(tri, hi) + (_dot(tri, mid) + _dot(tri, lo))

    lc = [cumsum(lw[i]) for i in n]
    ltot = [lc[i][c - 1:c, :] for i in n]
    w_inv = [jnp.exp(-lc[i]) for i in n]
    ah = [a[i] * jnp.exp(lc[i] - lw[i]) for i in n]
    rh = [r[i] * jnp.exp(lc[i]) for i in n]
    sc = [_dot_nt(jnp.concatenate([ah[i], rh[i]], axis=0).astype(BF16),
                  jnp.concatenate([_bd(b[i] * w_inv[i]), _bd(k[i] * w_inv[i])], axis=0)) for i in n]
    aab = [jnp.where(strict, sc[i][:c, :LANES], zero) for i in n]

    t = [eye2 + jnp.where(blocks[0], aab[i], zero) for i in n]
    for lvl in range(1, len(blocks)):
        grow = blocks[lvl] & ~blocks[lvl - 1]
        x = [_dot(t[i].astype(BF16), _bd(jnp.where(grow, aab[i], zero))) for i in n]
        t = [t[i] + _dot(x[i].astype(BF16), _bd(t[i])) for i in n]

    vb = [_bd(v[i]) for i in n]
    akv = [_dot(jnp.where(strict, sc[i][:c, LANES:], zero).astype(BF16), vb[i]) for i in n]
    pu = [_dot(t[i].astype(BF16), jnp.concatenate([_bd(ah[i]), _bd(akv[i])], axis=1)) for i in n]

    left2 = (lax.broadcasted_iota(jnp.int32, (HEAD, 2 * LANES), 1) % LANES) < HEAD
    g, h, q, y0 = [], [], [], []
    for i in n:
        w_rem = jnp.exp(ltot[i] - lc[i])
        lhs_t = jnp.concatenate([b[i] * w_rem, k[i] * w_rem], axis=0).astype(BF16)
        rhs = jnp.concatenate([pu[i], jnp.concatenate([zero, v[i]], axis=1)], axis=0).astype(BF16)
        full = _dot_tn(lhs_t, rhs)
        gh = jnp.where(left2, full[:HEAD], full[HEAD:])
        g.append(gh[:, :LANES] + eye2 * jnp.exp(ltot[i]))
        h.append(gh[:, LANES:])
    zero_b = jnp.zeros((2 * c, LANES), BF16)
    for i in n:
        mrb = jnp.where(incl, sc[i][c:, :LANES], zero)
        mrk = jnp.where(incl, sc[i][c:, LANES:], zero)
        rhs2 = jnp.concatenate([jnp.concatenate([_bd(pu[i][:, :LANES]), _bd(pu[i][:, LANES:])], axis=1),
                                jnp.concatenate([zero_b, vb[i]], axis=1)], axis=0)
        qy = _dot(jnp.concatenate([mrb, mrk], axis=1).astype(BF16), rhs2)
        q.append(rh[i] + qy[:, :LANES])
        y0.append(qy[:, LANES:])
    return g, h, q, y0


def _wkv_chunked_kernel(r_ref, w_ref, k_ref, v_ref, a_ref, b_ref, s0_ref, y_ref, st_ref, s_scr):
    i = pl.program_id(2)
    c = WKV_CHUNK

    @pl.when(i == 0)
    def _():
        s_scr[...] = s0_ref[0, 0]

    ti = lax.broadcasted_iota(jnp.int32, (c, LANES), 0)
    si = lax.broadcasted_iota(jnp.int32, (c, LANES), 1) % HEAD
    tri = (lax.broadcasted_iota(jnp.int32, (c, c), 0) >= lax.broadcasted_iota(jnp.int32, (c, c), 1)
           ).astype(BF16)
    eye2 = (ti == si).astype(F32)
    sizes = [2 << l for l in range(c.bit_length() - 1)]
    blocks = [(ti // s) == (si // s) for s in sizes]
    consts = (tri, eye2, ti > si, ti >= si, blocks)

    n_chunks = y_ref.shape[1] // c
    chunks = lambda ref: [ref[0, ci * c:(ci + 1) * c, :] for ci in range(n_chunks)]
    g, h, q, y0 = _wkv_chunk_local(chunks(r_ref), chunks(w_ref), chunks(k_ref), chunks(v_ref),
                                   chunks(a_ref), chunks(b_ref), consts)
    st = s_scr[...]
    for ci in range(n_chunks):
        out = _dot(jnp.concatenate([q[ci], g[ci]], axis=0).astype(BF16), _bd(st))
        y_ref[0, ci * c:(ci + 1) * c, :] = out[:c] + y0[ci]
        st = out[c:] + h[ci]
    s_scr[...] = st

    @pl.when(i == pl.num_programs(2) - 1)
    def _():
        st_ref[0, 0] = st


def _wkv_chunked(r, lw, k, v, a, b, s0, tb):
    n, t, c = r.shape
    pairs = c // LANES
    tok = pl.BlockSpec((1, tb, LANES), lambda bi, p, i: (bi, i, p))
    st = pl.BlockSpec((1, 1, HEAD, LANES), lambda bi, p, i: (bi, p, 0, 0))
    return pl.pallas_call(
        _wkv_chunked_kernel,
        grid=(n, pairs, t // tb),
        in_specs=[tok] * 6 + [st],
        out_specs=[tok, st],
        out_shape=[jax.ShapeDtypeStruct((n, t, c), F32), jax.ShapeDtypeStruct(s0.shape, F32)],
        scratch_shapes=[pltpu.VMEM((HEAD, LANES), F32)],
        compiler_params=_cparams(("arbitrary", "arbitrary", "arbitrary")),
        name="wkv_chunked",
    )(r, lw, k, v, a, b, s0)


def _pack_state(s):
    n, h, vd, kd = s.shape
    return s.reshape(n, h // 2, 2, vd, kd).transpose(0, 1, 4, 2, 3).reshape(n, h // 2, kd, 2 * vd)


def _unpack_state(s, h):
    n, _, kd, _ = s.shape
    return s.reshape(n, h // 2, kd, 2, HEAD).transpose(0, 1, 3, 4, 2).reshape(n, h, HEAD, kd)


def _attn_prompt_kernel(q_ref, kc_ref, kp_ref, vc_ref, vp_ref, o_ref, l_ref, *, n_back, scale):
    blk = pl.program_id(2)
    q = q_ref[0]
    kcat = jnp.concatenate([kp_ref[0], kc_ref[0]], axis=0)
    vcat = jnp.concatenate([vp_ref[0], vc_ref[0]], axis=0)
    qi = lax.broadcasted_iota(jnp.int32, (ATTN_BLOCK, 2 * ATTN_BLOCK), 0)
    ki = lax.broadcasted_iota(jnp.int32, (ATTN_BLOCK, 2 * ATTN_BLOCK), 1)
    dist = qi + ATTN_BLOCK - ki
    mask = (dist >= 0) & (dist <= n_back) & ((ki >= ATTN_BLOCK) | (blk > 0))
    outs, lses = [], []
    for h in range(ATTN_HEADS):
        sl = slice(h * HEAD, (h + 1) * HEAD)
        s = lax.dot_general(q[:, sl].astype(BF16), kcat[:, sl].astype(BF16),
                            (((1,), (1,)), ((), ())), preferred_element_type=F32) * scale
        s = jnp.where(mask, s, NEG)
        m = jnp.max(s, axis=-1, keepdims=True)
        e = jnp.exp(s - m)
        den = jnp.sum(e, axis=-1, keepdims=True)
        outs.append(_dot((e / den).astype(BF16), vcat[:, sl].astype(BF16)))
        lses.append(jnp.broadcast_to(m + jnp.log(den), (ATTN_BLOCK, HEAD)))
    o_ref[0] = jnp.concatenate(outs, axis=1)
    l_ref[0] = jnp.concatenate(lses, axis=1)


def _attn_prompt(zqkv, gi):
    n, t, cols = zqkv.shape
    d = DILATIONS[gi]
    n_back = WINDOWS[gi] // d
    aw = ATTN_HEADS * HEAD
    s_len = t // d
    nb = s_len // ATTN_BLOCK
    zr = zqkv.reshape(n, s_len, d * cols)
    cb = cols // aw
    ng = len(DILATIONS)

    def spec(which, prev):
        off = which * ng + gi
        if prev:
            return pl.BlockSpec((1, ATTN_BLOCK, aw), lambda b, r, i: (b, jnp.maximum(i - 1, 0), r * cb + off))
        return pl.BlockSpec((1, ATTN_BLOCK, aw), lambda b, r, i: (b, i, r * cb + off))

    out_spec = pl.BlockSpec((1, ATTN_BLOCK, aw), lambda b, r, i: (b, i, r))
    out_sds = jax.ShapeDtypeStruct((n, s_len, d * aw), F32)
    o, l = pl.pallas_call(
        functools.partial(_attn_prompt_kernel, n_back=n_back, scale=HEAD ** -0.5),
        grid=(n, d, nb),
        in_specs=[spec(0, False), spec(1, False), spec(1, True), spec(2, False), spec(2, True)],
        out_specs=[out_spec, out_spec],
        out_shape=[out_sds, out_sds],
        compiler_params=_cparams(("arbitrary", "arbitrary", "arbitrary")),
        name=f"attn_prompt_g{gi}",
    )(zr, zr, zr, zr, zr)
    return o.reshape(n, t, aw), l.reshape(n, t, aw)


def _attn_sample_kernel(q_ref, k_ref, v_ref, c_ref, o_ref, l_ref, *, dil, n_back, valid, scale):
    q = q_ref[0]
    tp = q.shape[0]
    cache = c_ref[0]
    cl = cache.shape[0]
    aw = ATTN_HEADS * HEAD
    keys = jnp.concatenate([cache[:, :aw], k_ref[0]], axis=0)
    vals = jnp.concatenate([cache[:, aw:], v_ref[0]], axis=0)
    ti = lax.broadcasted_iota(jnp.int32, (tp, cl + tp), 0)
    pi = lax.broadcasted_iota(jnp.int32, (tp, cl + tp), 1)
    dist = cl + ti - pi
    mask = (dist >= 0) & (dist % dil == 0) & (dist <= dil * n_back) & (pi < cl + valid)
    outs, lses = [], []
    for h in range(ATTN_HEADS):
        sl = slice(h * HEAD, (h + 1) * HEAD)
        s = lax.dot_general(q[:, sl].astype(BF16), keys[:, sl].astype(BF16),
                            (((1,), (1,)), ((), ())), preferred_element_type=F32) * scale
        s = jnp.where(mask, s, NEG)
        m = jnp.max(s, axis=-1, keepdims=True)
        e = jnp.exp(s - m)
        den = jnp.sum(e, axis=-1, keepdims=True)
        outs.append(_dot((e / den).astype(BF16), vals[:, sl].astype(BF16)))
        lses.append(jnp.broadcast_to(m + jnp.log(den), (tp, HEAD)))
    o_ref[0] = jnp.concatenate(outs, axis=1)
    l_ref[0] = jnp.concatenate(lses, axis=1)


def _attn_sample(zqkv, cache, gi, valid):
    n, tp, cols = zqkv.shape
    aw = ATTN_HEADS * HEAD
    cl = cache.shape[1]
    ng = len(DILATIONS)
    d = DILATIONS[gi]
    spec = lambda which: pl.BlockSpec((1, tp, aw), lambda b: (b, 0, which * ng + gi))
    out_spec = pl.BlockSpec((1, tp, aw), lambda b: (b, 0, 0))
    out_sds = jax.ShapeDtypeStruct((n, tp, aw), F32)
    return pl.pallas_call(
        functools.partial(_attn_sample_kernel, dil=d, n_back=WINDOWS[gi] // d, valid=valid,
                          scale=HEAD ** -0.5),
        grid=(n,),
        in_specs=[spec(0), spec(1), spec(2), pl.BlockSpec((1, cl, 2 * aw), lambda b: (b, 0, 0))],
        out_specs=[out_spec, out_spec],
        out_shape=[out_sds, out_sds],
        compiler_params=_cparams(("arbitrary",)),
        name=f"attn_sample_g{gi}",
    )(zqkv, zqkv, zqkv, cache.reshape(n, cl, 2 * aw))


def _merge_kernel(x_ref, y_ref, bon_ref, g_ref, lw_ref, lb_ref, o0, o1, o2, l0, l1, l2, zg_ref,
                  wbr_ref, wba_ref, wout_ref, out_ref):
    ones = _pair_ones()
    y = y_ref[...]
    mu = _head_sum(y, ones) * (1.0 / HEAD)
    yc = y - mu
    var = _head_sum(yc * yc, ones) * (1.0 / HEAD)
    yn = yc * lax.rsqrt(var + LNX_EPS) * lw_ref[...] + lb_ref[...]
    rwkv = (yn + bon_ref[...]) * g_ref[...]

    ls = [l0[...], l1[...], l2[...]]
    m = jnp.maximum(jnp.maximum(ls[0], ls[1]), ls[2])
    es = [jnp.exp(l - m) for l in ls]
    attn = (es[0] * o0[...] + es[1] * o1[...] + es[2] * o2[...]) / (es[0] + es[1] + es[2])

    d = x_ref.shape[1]
    gates = _sigmoid(zg_ref[...])
    mixed = (gates[:, :d] * _dot(rwkv.astype(BF16), wbr_ref[...])
             + gates[:, d:] * _dot(attn.astype(BF16), wba_ref[...]))
    out_ref[...] = x_ref[...] + _dot(mixed.astype(BF16), wout_ref[...])


def _merge(x, y, bonus, g, lnx_w, lnx_b, os_, ls_, zg, wbr, wba, wout, tm):
    rows, d = x.shape
    rowblk = lambda a: pl.BlockSpec((tm, a.shape[1]), lambda i: (i, 0))
    full = lambda a: pl.BlockSpec(a.shape, lambda i: (0, 0))
    lw, lb = lnx_w.reshape(1, -1), lnx_b.reshape(1, -1)
    args = [x, y, bonus, g, lw, lb, *os_, *ls_, zg, wbr, wba, wout]
    specs = [rowblk(x), rowblk(y), rowblk(bonus), rowblk(g), full(lw), full(lb)] + \
            [rowblk(a) for a in (*os_, *ls_)] + [rowblk(zg), full(wbr), full(wba), full(wout)]
    return pl.pallas_call(
        _merge_kernel,
        grid=(rows // tm,),
        in_specs=specs,
        out_specs=pl.BlockSpec((tm, d), lambda i: (i, 0)),
        out_shape=jax.ShapeDtypeStruct((rows, d), F32),
        compiler_params=_cparams(("arbitrary",)),
        name="merge_out",
    )(*args)


def _mlp_kernel(x_ref, g_ref, w1_ref, w2_ref, o_ref, h_scr, acc_scr):
    j = pl.program_id(1)

    @pl.when(j == 0)
    def _():
        h_scr[...] = _rms(x_ref[...], g_ref[...]).astype(BF16)
        acc_scr[...] = x_ref[...]

    u = jnp.maximum(_dot(h_scr[...], w1_ref[...]), 0.0)
    acc_scr[...] += _dot((u * u).astype(BF16), w2_ref[...])

    @pl.when(j == pl.num_programs(1) - 1)
    def _():
        o_ref[...] = acc_scr[...]


def _mlp(x, g, w1, w2, tm, tf):
    rows, d = x.shape
    dff = w1.shape[1]
    return pl.pallas_call(
        _mlp_kernel,
        grid=(rows // tm, dff // tf),
        in_specs=[pl.BlockSpec((tm, d), lambda i, j: (i, 0)),
                  pl.BlockSpec((1, d), lambda i, j: (0, 0)),
                  pl.BlockSpec((d, tf), lambda i, j: (0, j)),
                  pl.BlockSpec((tf, d), lambda i, j: (j, 0))],
        out_specs=pl.BlockSpec((tm, d), lambda i, j: (i, 0)),
        out_shape=jax.ShapeDtypeStruct((rows, d), F32),
        scratch_shapes=[pltpu.VMEM((tm, d), BF16), pltpu.VMEM((tm, d), F32)],
        compiler_params=_cparams(("arbitrary", "arbitrary")),
        name="mlp",
    )(x, g.reshape(1, d), w1, w2)


def _ple_kernel(x_ref, p_ref, g_ref, wg_ref, wp_ref, gf_ref, o_ref, *, final):
    x = x_ref[...]
    h = _rms(x, g_ref[...]).astype(BF16)
    x = x + _sigmoid(_dot(h, wg_ref[...])) * _dot(p_ref[...].astype(BF16), wp_ref[...])
    if final:
        x = _rms(x, gf_ref[...])
    o_ref[...] = x


def _ple(x, p, g, wg, wp, gf, final, tm):
    rows, d = x.shape
    rowblk = lambda a: pl.BlockSpec((tm, a.shape[1]), lambda i: (i, 0))
    full = lambda a: pl.BlockSpec(a.shape, lambda i: (0, 0))
    g, gf = g.reshape(1, d), gf.reshape(1, d)
    return pl.pallas_call(
        functools.partial(_ple_kernel, final=final),
        grid=(rows // tm,),
        in_specs=[rowblk(x), rowblk(p), full(g), full(wg), full(wp), full(gf)],
        out_specs=pl.BlockSpec((tm, d), lambda i: (i, 0)),
        out_shape=jax.ShapeDtypeStruct((rows, d), F32),
        compiler_params=_cparams(("arbitrary",)),
        name="ple",
    )(x, p, g, wg, wp, gf)


def _layer(x, p, wkv0, shift0, caches, prm, norm_final, final, valid):
    n, t, d = x.shape
    rows = n * t
    width = prm["decay_w0"].shape[-1]
    rwkv_cols = prm["shift_mu"].shape[-1]
    aw = ATTN_HEADS * HEAD
    qkv_cols = 3 * len(DILATIONS) * aw
    tm = min(512, rows)
    x2 = x.reshape(rows, d)

    w_in = prm["w_in"]
    pick = lambda cols, opts: next(o for o in opts if cols % o == 0)
    z_rwkv = _norm_matmul(x2, prm["norm_mix"], w_in[:, :rwkv_cols].astype(BF16), tm,
                          pick(rwkv_cols, (1664, 128)))
    z_qkv = _norm_matmul(x2, prm["norm_mix"], w_in[:, rwkv_cols:rwkv_cols + qkv_cols].astype(BF16),
                         tm, pick(qkv_cols, (1152, 256)))
    z_gate = _norm_matmul(x2, prm["norm_mix"], w_in[:, rwkv_cols + qkv_cols:].astype(BF16), tm,
                          pick(2 * d, (1024, 128)))
    z_rwkv = z_rwkv.reshape(n, t, rwkv_cols)
    z_qkv = z_qkv.reshape(n, t, qkv_cols)

    r, w, k, v, a, b, g, bonus = _rwkv_prep(z_rwkv, shift0, prm, min(256, t))
    heads = width // HEAD
    if valid == t and t % WKV_CHUNK == 0:
        y, s_t = _wkv_chunked(r, w, k, v, a, b, _pack_state(wkv0), pick(t, (512, 256, WKV_CHUNK)))
    else:
        y, s_t = _wkv(r, w, k, v, a, b, _pack_state(wkv0), min(ATTN_BLOCK, t), min(valid, SUBLANES))
    wkv_t = _unpack_state(s_t, heads)
    shift_t = z_rwkv[:, valid - 1, :]

    os_, ls_, new_kv = [], [], []
    ng = len(DILATIONS)
    for gi in range(ng):
        kg = z_qkv[:, :, (ng + gi) * aw:(ng + gi + 1) * aw].reshape(n, t, ATTN_HEADS, HEAD)
        vg = z_qkv[:, :, (2 * ng + gi) * aw:(2 * ng + gi + 1) * aw].reshape(n, t, ATTN_HEADS, HEAD)
        if caches is None:
            o, l = _attn_prompt(z_qkv, gi)
            keep = min(WINDOWS[gi], t)
            new_kv.append(jnp.stack([kg[:, t - keep:], vg[:, t - keep:]], axis=2))
        else:
            o, l = _attn_sample(z_qkv, caches[gi], gi, valid)
            new_kv.append(jnp.stack([kg[:, :valid], vg[:, :valid]], axis=2))
        os_.append(o.reshape(rows, aw))
        ls_.append(l.reshape(rows, aw))

    flat = lambda u: u.reshape(rows, -1)
    x2 = _merge(x2, flat(y), flat(bonus), flat(g), prm["lnx_w"], prm["lnx_b"], os_, ls_, z_gate,
                prm["w_branch_rwkv"].astype(BF16), prm["w_branch_attn"].astype(BF16),
                prm["w_out"].astype(BF16), min(256, rows))
    x2 = _mlp(x2, prm["norm_mlp"], prm["w_ff1"].astype(BF16), prm["w_ff2"].astype(BF16), tm, 1024)
    x2 = _ple(x2, p.reshape(rows, -1), prm["norm_ple"], prm["w_ple_gate"].astype(BF16),
              prm["w_ple"].astype(BF16), norm_final, final, tm)
    return x2.reshape(n, t, d), new_kv, wkv_t, shift_t


_LAYER_PARAMS = ("norm_mix", "w_in", "shift_mu", "decay_w0", "decay_w2", "aaa_a0", "aaa_a2", "gate_g2",
                 "k_k", "k_a", "r_k", "lnx_w", "lnx_b", "w_branch_rwkv", "w_branch_attn", "w_out",
                 "norm_mlp", "w_ff1", "w_ff2", "norm_ple", "w_ple", "w_ple_gate")


def kernel(x_prompt, x_sample, cache_kv_g0, cache_kv_g1, cache_kv_g2, state_wkv, state_shift, p_prompt, p_sample, norm_mix, w_in, shift_mu, decay_w0, decay_w2, aaa_a0, aaa_a2, gate_g2, k_k, k_a, r_k, lnx_w, lnx_b, w_branch_rwkv, w_branch_attn, w_out, norm_mlp, w_ff1, w_ff2, norm_ple, w_ple, w_ple_gate, norm_final):
    stacked = dict(zip(_LAYER_PARAMS, (norm_mix, w_in, shift_mu, decay_w0, decay_w2, aaa_a0, aaa_a2,
                                       gate_g2, k_k, k_a, r_k, lnx_w, lnx_b, w_branch_rwkv,
                                       w_branch_attn, w_out, norm_mlp, w_ff1, w_ff2, norm_ple, w_ple,
                                       w_ple_gate)))
    depth = norm_mix.shape[0]
    nb_p, _, d = x_prompt.shape
    nb_s, t_s, _ = x_sample.shape
    heads = decay_w0.shape[-1] // HEAD
    rwkv_cols = shift_mu.shape[-1]
    t_pad = -(-t_s // SUBLANES) * SUBLANES
    pad_t = lambda u: jnp.pad(u, ((0, 0), (0, t_pad - t_s), (0, 0)))

    yp, ys = x_prompt, pad_t(x_sample)
    kvp, kvs = [[], [], []], [[], [], []]
    wkv_p, sh_p, wkv_s, sh_s = [], [], [], []
    for i in range(depth):
        prm = {name: val[i] for name, val in stacked.items()}
        prm["r_k"] = prm["r_k"].reshape(-1)
        last = i == depth - 1
        wkv0 = jnp.zeros((nb_p, heads, HEAD, HEAD), F32)
        shift0 = jnp.zeros((nb_p, rwkv_cols), F32)
        yp, nkv_p, w_p, s_p = _layer(yp, p_prompt[i], wkv0, shift0, None, prm, norm_final, last,
                                     x_prompt.shape[1])
        ys, nkv_s, w_s, s_s = _layer(ys, pad_t(p_sample[i]), state_wkv[i], state_shift[i],
                                     (cache_kv_g0[i], cache_kv_g1[i], cache_kv_g2[i]), prm, norm_final,
                                     last, t_s)
        for gi in range(3):
            kvp[gi].append(nkv_p[gi])
            kvs[gi].append(nkv_s[gi])
        wkv_p.append(w_p)
        sh_p.append(s_p)
        wkv_s.append(w_s)
        sh_s.append(s_s)
    return (yp, ys[:, :t_s],
            jnp.stack(kvp[0]), jnp.stack(kvp[1]), jnp.stack(kvp[2]),
            jnp.stack(wkv_p), jnp.stack(sh_p),
            jnp.stack(kvs[0]), jnp.stack(kvs[1]), jnp.stack(kvs[2]),
            jnp.stack(wkv_s), jnp.stack(sh_s))
```

```python
import functools

import jax
import jax.numpy as jnp
from jax import lax
from jax.experimental import pallas as pl
from jax.experimental.pallas import tpu as pltpu

F32 = jnp.float32
BF16 = jnp.bfloat16

HEAD = 64
LANES = 128
SUBLANES = 8
ATTN_HEADS = 4
ATTN_BLOCK = 128
WINDOWS = (128, 512, 2048)
DILATIONS = (1, 4, 16)
NORM_EPS = 1e-6
LNX_EPS = 64e-5
NEG = -1e30
VMEM_LIMIT = 56 * 1024 * 1024


def _cparams(sem):
    return pltpu.CompilerParams(dimension_semantics=sem, vmem_limit_bytes=VMEM_LIMIT)


def _dot(a, b):
    return jnp.dot(a, b, preferred_element_type=F32)


def _split(x):
    hi = x.astype(BF16)
    lo = (x - hi.astype(F32)).astype(BF16)
    return hi, lo


def _dot3(a, b):
    a_hi, a_lo = _split(a)
    b_hi, b_lo = _split(b)
    return _dot(a_hi, b_hi) + (_dot(a_lo, b_hi) + _dot(a_hi, b_lo))


def _pair_ones():
    r = lax.broadcasted_iota(jnp.int32, (LANES, LANES), 0) // HEAD
    c = lax.broadcasted_iota(jnp.int32, (LANES, LANES), 1) // HEAD
    return (r == c).astype(BF16)


def _head_sum(x, ones):
    outs = []
    for c in range(x.shape[1] // LANES):
        hi, lo = _split(x[:, c * LANES:(c + 1) * LANES])
        outs.append(_dot(hi, ones) + _dot(lo, ones))
    return jnp.concatenate(outs, axis=1)


def _rms(x, g):
    return x * lax.rsqrt(jnp.mean(x * x, axis=-1, keepdims=True) + NORM_EPS) * g


def _sigmoid(x):
    return 1.0 / (1.0 + jnp.exp(-x))


def _softplus(x):
    return jnp.maximum(x, 0.0) + jnp.log(1.0 + jnp.exp(-jnp.abs(x)))


def _norm_matmul_kernel(x_ref, g_ref, w_ref, o_ref, h_scr):
    @pl.when(pl.program_id(1) == 0)
    def _():
        h_scr[...] = _rms(x_ref[...], g_ref[...]).astype(BF16)

    o_ref[...] = _dot(h_scr[...], w_ref[...])


def _norm_matmul(x, g, w, tm, tn):
    rows, d = x.shape
    n_out = w.shape[1]
    return pl.pallas_call(
        _norm_matmul_kernel,
        grid=(rows // tm, n_out // tn),
        in_specs=[pl.BlockSpec((tm, d), lambda i, j: (i, 0)),
                  pl.BlockSpec((1, d), lambda i, j: (0, 0)),
                  pl.BlockSpec((d, tn), lambda i, j: (0, j))],
        out_specs=pl.BlockSpec((tm, tn), lambda i, j: (i, j)),
        out_shape=jax.ShapeDtypeStruct((rows, n_out), F32),
        scratch_shapes=[pltpu.VMEM((tm, d), BF16)],
        compiler_params=_cparams(("arbitrary", "arbitrary")),
        name="norm_matmul",
    )(x, g.reshape(1, d), w)


def _rwkv_prep_kernel(z_ref, sh_ref, mu_ref, w0_ref, w2_ref, a0_ref, a2_ref, g2_ref, kk_ref, ka_ref,
                      rk_ref, r_o, w_o, k_o, v_o, a_o, b_o, g_o, bon_o, last_scr, *, width):
    n = pl.program_id(0)
    i = pl.program_id(1)
    z = z_ref[0]
    tt = z.shape[0]

    @pl.when(i == 0)
    def _():
        last_scr[...] = sh_ref[pl.ds(n, 1), :]

    row = lax.broadcasted_iota(jnp.int32, z.shape, 0)
    prev = jnp.where(row == 0, last_scr[...], pltpu.roll(z, 1, axis=0))
    last_scr[...] = z[tt - 1:tt, :]
    zs = z + (prev - z) * mu_ref[...]

    c0 = 3 * width
    c1 = c0 + w2_ref.shape[0]
    c2 = c1 + a2_ref.shape[0]
    r = zs[:, 0:width]
    k = zs[:, width:2 * width]
    v = zs[:, 2 * width:c0]
    xw = zs[:, c0:c1]
    xa = zs[:, c1:c2]
    xg = zs[:, c2:]

    logw = -_softplus(-(w0_ref[...] + _dot3(jnp.tanh(xw), w2_ref[...]))) - 0.5
    a = _sigmoid(a0_ref[...] + _dot3(xa, a2_ref[...]))
    g = _dot3(_sigmoid(xg), g2_ref[...])

    ones = _pair_ones()
    kk = k * kk_ref[...]
    kk = kk / jnp.maximum(jnp.sqrt(_head_sum(kk * kk, ones)), 1e-12)
    k2 = k * (1.0 + (a - 1.0) * ka_ref[...])
    bonus = _head_sum(r * k2 * rk_ref[...], ones) * v

    r_o[0] = r
    w_o[0] = -jnp.exp(logw)
    k_o[0] = k2
    v_o[0] = v
    a_o[0] = -kk
    b_o[0] = kk * a
    g_o[0] = g
    bon_o[0] = bonus


def _rwkv_prep(z_rwkv, shift0, prm, tt):
    n, t, cols = z_rwkv.shape
    width = prm["decay_w0"].shape[-1]
    row = lambda x: x.reshape(1, -1)
    full = lambda a: pl.BlockSpec(a.shape, lambda b, i: (0,) * a.ndim)
    args = [shift0, row(prm["shift_mu"]), row(prm["decay_w0"]), prm["decay_w2"], row(prm["aaa_a0"]),
            prm["aaa_a2"], prm["gate_g2"], row(prm["k_k"]), row(prm["k_a"]), row(prm["r_k"])]
    out_spec = pl.BlockSpec((1, tt, width), lambda b, i: (b, i, 0))
    out_sds = jax.ShapeDtypeStruct((n, t, width), F32)
    return pl.pallas_call(
        functools.partial(_rwkv_prep_kernel, width=width),
        grid=(n, t // tt),
        in_specs=[pl.BlockSpec((1, tt, cols), lambda b, i: (b, i, 0))] + [full(a) for a in args],
        out_specs=[out_spec] * 8,
        out_shape=[out_sds] * 8,
        scratch_shapes=[pltpu.VMEM((1, cols), F32)],
        compiler_params=_cparams(("arbitrary", "arbitrary")),
        name="rwkv_prep",
    )(z_rwkv, *args)


def _wkv_kernel(r_ref, w_ref, k_ref, v_ref, a_ref, b_ref, s0_ref, y_ref, st_ref, *, valid):
    left = lax.broadcasted_iota(jnp.int32, (HEAD, LANES), 1) < HEAD

    def col_tile(xt, j):
        return jnp.where(left, xt[0:HEAD, j:j + 1], xt[HEAD:2 * HEAD, j:j + 1])

    for p in range(s0_ref.shape[1]):
        cols = slice(p * LANES, (p + 1) * LANES)
        rt = r_ref[0, :, cols].T
        wt = jnp.exp(w_ref[0, :, cols]).T
        kt = k_ref[0, :, cols].T
        at = a_ref[0, :, cols].T
        bt = b_ref[0, :, cols].T
        vg = v_ref[0, :, cols]
        s = s0_ref[0, p]
        ys = []
        for j in range(y_ref.shape[1]):
            if j < valid:
                sa = jnp.sum(s * col_tile(at, j), axis=0, keepdims=True)
                s = s * col_tile(wt, j) + col_tile(bt, j) * sa + col_tile(kt, j) * vg[j:j + 1, :]
                ys.append(jnp.sum(s * col_tile(rt, j), axis=0, keepdims=True))
            else:
                ys.append(jnp.zeros((1, LANES), F32))
        y_ref[0, :, cols] = jnp.concatenate(ys, axis=0)
        st_ref[0, p] = s


def _wkv(r, w, k, v, a, b, s0, valid):
    n, t, c = r.shape
    assert t == SUBLANES
    tok = pl.BlockSpec((1, t, c), lambda bi: (bi, 0, 0))
    st = pl.BlockSpec((1,) + s0.shape[1:], lambda bi: (bi, 0, 0, 0))
    return pl.pallas_call(
        functools.partial(_wkv_kernel, valid=valid),
        grid=(n,),
        in_specs=[tok] * 6 + [st],
        out_specs=[tok, st],
        out_shape=[jax.ShapeDtypeStruct((n, t, c), F32), jax.ShapeDtypeStruct(s0.shape, F32)],
        compiler_params=_cparams(("arbitrary",)),
        name="wkv_scan",
    )(r, w, k, v, a, b, s0)


WKV_CHUNK = 64


def _bd(x):
    xb = x.astype(BF16)
    left = lax.broadcasted_iota(jnp.int32, x.shape, 1) < HEAD
    zero = jnp.zeros_like(xb)
    return jnp.concatenate([jnp.where(left, xb, zero), jnp.where(left, zero, xb)], axis=0)


def _dot_nt(a, b):
    return lax.dot_general(a, b, (((1,), (1,)), ((), ())), preferred_element_type=F32)


def _dot_tn(a, b):
    return lax.dot_general(a, b, (((0,), (0,)), ((), ())), preferred_element_type=F32)


def _wkv_chunk_local(r, lw, k, v, a, b, consts):
    tri, eye2, strict, incl, blocks = consts
    c = WKV_CHUNK
    n = range(len(r))
    zero = jnp.zeros((c, LANES), F32)

    def cumsum(x):
        hi = x.astype(BF16)
        r1 = x - hi.astype(F32)
        mid = r1.astype(BF16)
        lo = (r1 - mid.astype(F32)).astype(BF16)
        return _dot(tri, hi) + (_dot(tri, mid) + _dot(tri, lo))

    lc = [cumsum(lw[i]) for i in n]
    ltot = [lc[i][c - 1:c, :] for i in n]
    w_inv = [jnp.exp(-lc[i]) for i in n]
    ah = [a[i] * jnp.exp(lc[i] - lw[i]) for i in n]
    rh = [r[i] * jnp.exp(lc[i]) for i in n]
    sc = [_dot_nt(jnp.concatenate([ah[i], rh[i]], axis=0).astype(BF16),
                  jnp.concatenate([_bd(b[i] * w_inv[i]), _bd(k[i] * w_inv[i])], axis=0)) for i in n]
    aab = [jnp.where(strict, sc[i][:c, :LANES], zero) for i in n]

    t = [eye2 + jnp.where(blocks[0], aab[i], zero) for i in n]
    for lvl in range(1, len(blocks)):
        grow = blocks[lvl] & ~blocks[lvl - 1]
        x = [_dot(t[i].astype(BF16), _bd(jnp.where(grow, aab[i], zero))) for i in n]
        t = [t[i] + _dot(x[i].astype(BF16), _bd(t[i])) for i in n]

    vb = [_bd(v[i]) for i in n]
    akv = [_dot(jnp.where(strict, sc[i][:c, LANES:], zero).astype(BF16), vb[i]) for i in n]
    pu = [_dot(t[i].astype(BF16), jnp.concatenate([_bd(ah[i]), _bd(akv[i])], axis=1)) for i in n]

    left2 = (lax.broadcasted_iota(jnp.int32, (HEAD, 2 * LANES), 1) % LANES) < HEAD
    g, h, q, y0 = [], [], [], []
    for i in n:
        w_rem = jnp.exp(ltot[i] - lc[i])
        lhs_t = jnp.concatenate([b[i] * w_rem, k[i] * w_rem], axis=0).astype(BF16)
        rhs = jnp.concatenate([pu[i], jnp.concatenate([zero, v[i]], axis=1)], axis=0).astype(BF16)
        full = _dot_tn(lhs_t, rhs)
        gh = jnp.where(left2, full[:HEAD], full[HEAD:])
        g.append(gh[:, :LANES] + eye2 * jnp.exp(ltot[i]))
        h.append(gh[:, LANES:])
    zero_b = jnp.zeros((2 * c, LANES), BF16)
    for i in n:
        mrb = jnp.where(incl, sc[i][c:, :LANES], zero)
        mrk = jnp.where(incl, sc[i][c:, LANES:], zero)
        rhs2 = jnp.concatenate([jnp.concatenate([_bd(pu[i][:, :LANES]), _bd(pu[i][:, LANES:])], axis=1),
                                jnp.concatenate([zero_b, vb[i]], axis=1)], axis=0)
        qy = _dot(jnp.concatenate([mrb, mrk], axis=1).astype(BF16), rhs2)
        q.append(rh[i] + qy[:, :LANES])
        y0.append(qy[:, LANES:])
    return g, h, q, y0


def _wkv_chunked_kernel(r_ref, w_ref, k_ref, v_ref, a_ref, b_ref, s0_ref, y_ref, st_ref, s_scr):
    i = pl.program_id(2)
    c = WKV_CHUNK

    @pl.when(i == 0)
    def _():
        s_scr[...] = s0_ref[0, 0]

    ti = lax.broadcasted_iota(jnp.int32, (c, LANES), 0)
    si = lax.broadcasted_iota(jnp.int32, (c, LANES), 1) % HEAD
    tri = (lax.broadcasted_iota(jnp.int32, (c, c), 0) >= lax.broadcasted_iota(jnp.int32, (c, c), 1)
           ).astype(BF16)
    eye2 = (ti == si).astype(F32)
    sizes = [2 << l for l in range(c.bit_length() - 1)]
    blocks = [(ti // s) == (si // s) for s in sizes]
    consts = (tri, eye2, ti > si, ti >= si, blocks)

    n_chunks = y_ref.shape[1] // c
    chunks = lambda ref: [ref[0, ci * c:(ci + 1) * c, :] for ci in range(n_chunks)]
    g, h, q, y0 = _wkv_chunk_local(chunks(r_ref), chunks(w_ref), chunks(k_ref), chunks(v_ref),
                                   chunks(a_ref), chunks(b_ref), consts)
    st = s_scr[...]
    for ci in range(n_chunks):
        out = _dot(jnp.concatenate([q[ci], g[ci]], axis=0).astype(BF16), _bd(st))
        y_ref[0, ci * c:(ci + 1) * c, :] = out[:c] + y0[ci]
        st = out[c:] + h[ci]
    s_scr[...] = st

    @pl.when(i == pl.num_programs(2) - 1)
    def _():
        st_ref[0, 0] = st


def _wkv_chunked(r, lw, k, v, a, b, s0, tb):
    n, t, c = r.shape
    pairs = c // LANES
    tok = pl.BlockSpec((1, tb, LANES), lambda bi, p, i: (bi, i, p))
    st = pl.BlockSpec((1, 1, HEAD, LANES), lambda bi, p, i: (bi, p, 0, 0))
    return pl.pallas_call(
        _wkv_chunked_kernel,
        grid=(n, pairs, t // tb),
        in_specs=[tok] * 6 + [st],
        out_specs=[tok, st],
        out_shape=[jax.ShapeDtypeStruct((n, t, c), F32), jax.ShapeDtypeStruct(s0.shape, F32)],
        scratch_shapes=[pltpu.VMEM((HEAD, LANES), F32)],
        compiler_params=_cparams(("arbitrary", "arbitrary", "arbitrary")),
        name="wkv_chunked",
    )(r, lw, k, v, a, b, s0)


def _pack_state(s):
    n, h, vd, kd = s.shape
    return s.reshape(n, h // 2, 2, vd, kd).transpose(0, 1, 4, 2, 3).reshape(n, h // 2, kd, 2 * vd)


def _unpack_state(s, h):
    n, _, kd, _ = s.shape
    return s.reshape(n, h // 2, kd, 2, HEAD).transpose(0, 1, 3, 4, 2).reshape(n, h, HEAD, kd)


def _attn_prompt_kernel(q_ref, k_ref, kp_ref, v_ref, vp_ref, o_ref, l_ref, *, dil, sub, n_back, scale):
    first = pl.program_id(1) == 0
    qi = lax.broadcasted_iota(jnp.int32, (ATTN_BLOCK, 2 * ATTN_BLOCK), 0)
    ki = lax.broadcasted_iota(jnp.int32, (ATTN_BLOCK, 2 * ATTN_BLOCK), 1)
    dist = qi + ATTN_BLOCK - ki
    band = (dist >= 0) & (dist <= n_back)
    own = ki >= ATTN_BLOCK

    def rows(j, r):
        if dil == 1:
            return pl.ds(j * ATTN_BLOCK, ATTN_BLOCK)
        return pl.ds(j * ATTN_BLOCK * dil + r, ATTN_BLOCK, stride=dil)

    def unit(j, r):
        q = (q_ref[0, rows(j, r), :] * scale).astype(BF16)
        if j == 0:
            kp, vp = kp_ref[0, rows(0, r), :], vp_ref[0, rows(0, r), :]
            mask = band & (own | jnp.logical_not(first))
        else:
            kp, vp, mask = k_ref[0, rows(j - 1, r), :], v_ref[0, rows(j - 1, r), :], band
        kcat = jnp.concatenate([kp, k_ref[0, rows(j, r), :]], axis=0).astype(BF16)
        vcat = jnp.concatenate([vp, v_ref[0, rows(j, r), :]], axis=0).astype(BF16)
        outs, lses = [], []
        for h in range(LANES // HEAD):
            sl = slice(h * HEAD, (h + 1) * HEAD)
            s = jnp.where(mask, _dot_nt(q[:, sl], kcat[:, sl]), NEG)
            m = jnp.max(s, axis=-1, keepdims=True)
            e = jnp.exp(s - m)
            den = jnp.sum(e, axis=-1, keepdims=True)
            outs.append(_dot(e.astype(BF16), vcat[:, sl]) / den)
            lses.append(jnp.broadcast_to(m + jnp.log(den), (ATTN_BLOCK, HEAD)))
        o_ref[0, rows(j, r), :] = jnp.concatenate(outs, axis=1)
        l_ref[0, rows(j, r), :] = jnp.concatenate(lses, axis=1)

    for j in range(sub):
        if dil <= 4:
            for r in range(dil):
                unit(j, r)
        else:
            def body(r, carry, j=j):
                unit(j, r)
                return carry
            lax.fori_loop(0, dil, body, 0)


def _attn_prompt(zqkv, gi, span):
    n, t, cols = zqkv.shape
    d = DILATIONS[gi]
    aw = ATTN_HEADS * HEAD
    ng = len(DILATIONS)
    back = ATTN_BLOCK * d
    sub = span // back
    halves = aw // LANES
    col = lambda which, hp: (which * ng + gi) * halves + hp
    cur = lambda which: pl.BlockSpec((1, span, LANES), lambda b, i, hp: (b, i, col(which, hp)))
    prev = lambda which: pl.BlockSpec((1, back, LANES),
                                      lambda b, i, hp: (b, jnp.maximum(i * sub - 1, 0), col(which, hp)))
    out_spec = pl.BlockSpec((1, span, LANES), lambda b, i, hp: (b, i, hp))
    out_sds = jax.ShapeDtypeStruct((n, t, aw), F32)
    return pl.pallas_call(
        functools.partial(_attn_prompt_kernel, dil=d, sub=sub, n_back=WINDOWS[gi] // d, scale=HEAD ** -0.5),
        grid=(n, t // span, halves),
        in_specs=[cur(0), cur(1), prev(1), cur(2), prev(2)],
        out_specs=[out_spec, out_spec],
        out_shape=[out_sds, out_sds],
        compiler_params=_cparams(("arbitrary", "arbitrary", "arbitrary")),
        name=f"attn_prompt_g{gi}",
    )(zqkv, zqkv, zqkv, zqkv, zqkv)


def _attn_sample_kernel(q_ref, k_ref, v_ref, c_ref, o_ref, l_ref, *, dil, n_back, valid, scale):
    q = q_ref[0]
    tp = q.shape[0]
    cache = c_ref[0]
    cl = cache.shape[0]
    aw = ATTN_HEADS * HEAD
    keys = jnp.concatenate([cache[:, :aw], k_ref[0]], axis=0)
    vals = jnp.concatenate([cache[:, aw:], v_ref[0]], axis=0)
    ti = lax.broadcasted_iota(jnp.int32, (tp, cl + tp), 0)
    pi = lax.broadcasted_iota(jnp.int32, (tp, cl + tp), 1)
    dist = cl + ti - pi
    mask = (dist >= 0) & (dist % dil == 0) & (dist <= dil * n_back) & (pi < cl + valid)
    outs, lses = [], []
    for h in range(ATTN_HEADS):
        sl = slice(h * HEAD, (h + 1) * HEAD)
        s = lax.dot_general(q[:, sl].astype(BF16), keys[:, sl].astype(BF16),
                            (((1,), (1,)), ((), ())), preferred_element_type=F32) * scale
        s = jnp.where(mask, s, NEG)
        m = jnp.max(s, axis=-1, keepdims=True)
        e = jnp.exp(s - m)
        den = jnp.sum(e, axis=-1, keepdims=True)
        outs.append(_dot((e / den).astype(BF16), vals[:, sl].astype(BF16)))
        lses.append(jnp.broadcast_to(m + jnp.log(den), (tp, HEAD)))
    o_ref[0] = jnp.concatenate(outs, axis=1)
    l_ref[0] = jnp.concatenate(lses, axis=1)


def _attn_sample(zqkv, cache, gi, valid):
    n, tp, cols = zqkv.shape
    aw = ATTN_HEADS * HEAD
    cl = cache.shape[1]
    ng = len(DILATIONS)
    d = DILATIONS[gi]
    spec = lambda which: pl.BlockSpec((1, tp, aw), lambda b: (b, 0, which * ng + gi))
    out_spec = pl.BlockSpec((1, tp, aw), lambda b: (b, 0, 0))
    out_sds = jax.ShapeDtypeStruct((n, tp, aw), F32)
    return pl.pallas_call(
        functools.partial(_attn_sample_kernel, dil=d, n_back=WINDOWS[gi] // d, valid=valid,
                          scale=HEAD ** -0.5),
        grid=(n,),
        in_specs=[spec(0), spec(1), spec(2), pl.BlockSpec((1, cl, 2 * aw), lambda b: (b, 0, 0))],
        out_specs=[out_spec, out_spec],
        out_shape=[out_sds, out_sds],
        compiler_params=_cparams(("arbitrary",)),
        name=f"attn_sample_g{gi}",
    )(zqkv, zqkv, zqkv, cache.reshape(n, cl, 2 * aw))


def _merge_kernel(x_ref, y_ref, bon_ref, g_ref, lw_ref, lb_ref, o0, o1, o2, l0, l1, l2, zg_ref,
                  wbr_ref, wba_ref, wout_ref, out_ref):
    ones = _pair_ones()
    y = y_ref[...]
    mu = _head_sum(y, ones) * (1.0 / HEAD)
    yc = y - mu
    var = _head_sum(yc * yc, ones) * (1.0 / HEAD)
    yn = yc * lax.rsqrt(var + LNX_EPS) * lw_ref[...] + lb_ref[...]
    rwkv = (yn + bon_ref[...]) * g_ref[...]

    ls = [l0[...], l1[...], l2[...]]
    m = jnp.maximum(jnp.maximum(ls[0], ls[1]), ls[2])
    es = [jnp.exp(l - m) for l in ls]
    attn = (es[0] * o0[...] + es[1] * o1[...] + es[2] * o2[...]) / (es[0] + es[1] + es[2])

    d = x_ref.shape[1]
    gates = _sigmoid(zg_ref[...])
    mixed = (gates[:, :d] * _dot(rwkv.astype(BF16), wbr_ref[...])
             + gates[:, d:] * _dot(attn.astype(BF16), wba_ref[...]))
    out_ref[...] = x_ref[...] + _dot(mixed.astype(BF16), wout_ref[...])


def _merge(x, y, bonus, g, lnx_w, lnx_b, os_, ls_, zg, wbr, wba, wout, tm):
    rows, d = x.shape
    rowblk = lambda a: pl.BlockSpec((tm, a.shape[1]), lambda i: (i, 0))
    full = lambda a: pl.BlockSpec(a.shape, lambda i: (0, 0))
    lw, lb = lnx_w.reshape(1, -1), lnx_b.reshape(1, -1)
    args = [x, y, bonus, g, lw, lb, *os_, *ls_, zg, wbr, wba, wout]
    specs = [rowblk(x), rowblk(y), rowblk(bonus), rowblk(g), full(lw), full(lb)] + \
            [rowblk(a) for a in (*os_, *ls_)] + [rowblk(zg), full(wbr), full(wba), full(wout)]
    return pl.pallas_call(
        _merge_kernel,
        grid=(rows // tm,),
        in_specs=specs,
        out_specs=pl.BlockSpec((tm, d), lambda i: (i, 0)),
        out_shape=jax.ShapeDtypeStruct((rows, d), F32),
        compiler_params=_cparams(("arbitrary",)),
        name="merge_out",
    )(*args)


def _mlp_kernel(x_ref, g_ref, w1_ref, w2_ref, o_ref, h_scr, acc_scr):
    j = pl.program_id(1)

    @pl.when(j == 0)
    def _():
        h_scr[...] = _rms(x_ref[...], g_ref[...]).astype(BF16)
        acc_scr[...] = x_ref[...]

    u = jnp.maximum(_dot(h_scr[...], w1_ref[...]), 0.0)
    acc_scr[...] += _dot((u * u).astype(BF16), w2_ref[...])

    @pl.when(j == pl.num_programs(1) - 1)
    def _():
        o_ref[...] = acc_scr[...]


def _mlp(x, g, w1, w2, tm, tf):
    rows, d = x.shape
    dff = w1.shape[1]
    return pl.pallas_call(
        _mlp_kernel,
        grid=(rows // tm, dff // tf),
        in_specs=[pl.BlockSpec((tm, d), lambda i, j: (i, 0)),
                  pl.BlockSpec((1, d), lambda i, j: (0, 0)),
                  pl.BlockSpec((d, tf), lambda i, j: (0, j)),
                  pl.BlockSpec((tf, d), lambda i, j: (j, 0))],
        out_specs=pl.BlockSpec((tm, d), lambda i, j: (i, 0)),
        out_shape=jax.ShapeDtypeStruct((rows, d), F32),
        scratch_shapes=[pltpu.VMEM((tm, d), BF16), pltpu.VMEM((tm, d), F32)],
        compiler_params=_cparams(("arbitrary", "arbitrary")),
        name="mlp",
    )(x, g.reshape(1, d), w1, w2)


def _ple_kernel(x_ref, p_ref, g_ref, wg_ref, wp_ref, gf_ref, o_ref, *, final):
    x = x_ref[...]
    h = _rms(x, g_ref[...]).astype(BF16)
    x = x + _sigmoid(_dot(h, wg_ref[...])) * _dot(p_ref[...].astype(BF16), wp_ref[...])
    if final:
        x = _rms(x, gf_ref[...])
    o_ref[...] = x


def _ple(x, p, g, wg, wp, gf, final, tm):
    rows, d = x.shape
    rowblk = lambda a: pl.BlockSpec((tm, a.shape[1]), lambda i: (i, 0))
    full = lambda a: pl.BlockSpec(a.shape, lambda i: (0, 0))
    g, gf = g.reshape(1, d), gf.reshape(1, d)
    return pl.pallas_call(
        functools.partial(_ple_kernel, final=final),
        grid=(rows // tm,),
        in_specs=[rowblk(x), rowblk(p), full(g), full(wg), full(wp), full(gf)],
        out_specs=pl.BlockSpec((tm, d), lambda i: (i, 0)),
        out_shape=jax.ShapeDtypeStruct((rows, d), F32),
        compiler_params=_cparams(("arbitrary",)),
        name="ple",
    )(x, p, g, wg, wp, gf)


def _layer(x, p, wkv0, shift0, caches, prm, norm_final, final, valid):
    n, t, d = x.shape
    rows = n * t
    width = prm["decay_w0"].shape[-1]
    rwkv_cols = prm["shift_mu"].shape[-1]
    aw = ATTN_HEADS * HEAD
    qkv_cols = 3 * len(DILATIONS) * aw
    tm = min(512, rows)
    x2 = x.reshape(rows, d)

    w_in = prm["w_in"]
    pick = lambda cols, opts: next(o for o in opts if cols % o == 0)
    z_rwkv = _norm_matmul(x2, prm["norm_mix"], w_in[:, :rwkv_cols].astype(BF16), tm,
                          pick(rwkv_cols, (1664, 128)))
    z_qkv = _norm_matmul(x2, prm["norm_mix"], w_in[:, rwkv_cols:rwkv_cols + qkv_cols].astype(BF16),
                         tm, pick(qkv_cols, (1152, 256)))
    z_gate = _norm_matmul(x2, prm["norm_mix"], w_in[:, rwkv_cols + qkv_cols:].astype(BF16), tm,
                          pick(2 * d, (1024, 128)))
    z_rwkv = z_rwkv.reshape(n, t, rwkv_cols)
    z_qkv = z_qkv.reshape(n, t, qkv_cols)

    r, w, k, v, a, b, g, bonus = _rwkv_prep(z_rwkv, shift0, prm, min(256, t))
    heads = width // HEAD
    if valid == t and t % WKV_CHUNK == 0:
        y, s_t = _wkv_chunked(r, w, k, v, a, b, _pack_state(wkv0), pick(t, (512, 256, WKV_CHUNK)))
    else:
        y, s_t = _wkv(r, w, k, v, a, b, _pack_state(wkv0), valid)
    wkv_t = _unpack_state(s_t, heads)
    shift_t = z_rwkv[:, valid - 1, :]

    os_, ls_, new_kv = [], [], []
    ng = len(DILATIONS)
    for gi in range(ng):
        kg = z_qkv[:, :, (ng + gi) * aw:(ng + gi + 1) * aw].reshape(n, t, ATTN_HEADS, HEAD)
        vg = z_qkv[:, :, (2 * ng + gi) * aw:(2 * ng + gi + 1) * aw].reshape(n, t, ATTN_HEADS, HEAD)
        if caches is None:
            o, l = _attn_prompt(z_qkv, gi, max(1024, ATTN_BLOCK * DILATIONS[gi]))
            keep = min(WINDOWS[gi], t)
            new_kv.append(jnp.stack([kg[:, t - keep:], vg[:, t - keep:]], axis=2))
        else:
            o, l = _attn_sample(z_qkv, caches[gi], gi, valid)
            new_kv.append(jnp.stack([kg[:, :valid], vg[:, :valid]], axis=2))
        os_.append(o.reshape(rows, aw))
        ls_.append(l.reshape(rows, aw))

    flat = lambda u: u.reshape(rows, -1)
    x2 = _merge(x2, flat(y), flat(bonus), flat(g), prm["lnx_w"], prm["lnx_b"], os_, ls_, z_gate,
                prm["w_branch_rwkv"].astype(BF16), prm["w_branch_attn"].astype(BF16),
                prm["w_out"].astype(BF16), min(256, rows))
    x2 = _mlp(x2, prm["norm_mlp"], prm["w_ff1"].astype(BF16), prm["w_ff2"].astype(BF16), tm, 1024)
    x2 = _ple(x2, p.reshape(rows, -1), prm["norm_ple"], prm["w_ple_gate"].astype(BF16),
              prm["w_ple"].astype(BF16), norm_final, final, tm)
    return x2.reshape(n, t, d), new_kv, wkv_t, shift_t


_LAYER_PARAMS = ("norm_mix", "w_in", "shift_mu", "decay_w0", "decay_w2", "aaa_a0", "aaa_a2", "gate_g2",
                 "k_k", "k_a", "r_k", "lnx_w", "lnx_b", "w_branch_rwkv", "w_branch_attn", "w_out",
                 "norm_mlp", "w_ff1", "w_ff2", "norm_ple", "w_ple", "w_ple_gate")


def kernel(x_prompt, x_sample, cache_kv_g0, cache_kv_g1, cache_kv_g2, state_wkv, state_shift, p_prompt, p_sample, norm_mix, w_in, shift_mu, decay_w0, decay_w2, aaa_a0, aaa_a2, gate_g2, k_k, k_a, r_k, lnx_w, lnx_b, w_branch_rwkv, w_branch_attn, w_out, norm_mlp, w_ff1, w_ff2, norm_ple, w_ple, w_ple_gate, norm_final):
    stacked = dict(zip(_LAYER_PARAMS, (norm_mix, w_in, shift_mu, decay_w0, decay_w2, aaa_a0, aaa_a2,
                                       gate_g2, k_k, k_a, r_k, lnx_w, lnx_b, w_branch_rwkv,
                                       w_branch_attn, w_out, norm_mlp, w_ff1, w_ff2, norm_ple, w_ple,
                                       w_ple_gate)))
    depth = norm_mix.shape[0]
    nb_p, _, d = x_prompt.shape
    nb_s, t_s, _ = x_sample.shape
    heads = decay_w0.shape[-1] // HEAD
    rwkv_cols = shift_mu.shape[-1]
    t_pad = -(-t_s // SUBLANES) * SUBLANES
    pad_t = lambda u: jnp.pad(u, ((0, 0), (0, t_pad - t_s), (0, 0)))

    yp, ys = x_prompt, pad_t(x_sample)
    kvp, kvs = [[], [], []], [[], [], []]
    wkv_p, sh_p, wkv_s, sh_s = [], [], [], []
    for i in range(depth):
        prm = {name: val[i] for name, val in stacked.items()}
        prm["r_k"] = prm["r_k"].reshape(-1)
        last = i == depth - 1
        wkv0 = jnp.zeros((nb_p, heads, HEAD, HEAD), F32)
        shift0 = jnp.zeros((nb_p, rwkv_cols), F32)
        yp, nkv_p, w_p, s_p = _layer(yp, p_prompt[i], wkv0, shift0, None, prm, norm_final, last,
                                     x_prompt.shape[1])
        ys, nkv_s, w_s, s_s = _layer(ys, pad_t(p_sample[i]), state_wkv[i], state_shift[i],
                                     (cache_kv_g0[i], cache_kv_g1[i], cache_kv_g2[i]), prm, norm_final,
                                     last, t_s)
        for gi in range(3):
            kvp[gi].append(nkv_p[gi])
            kvs[gi].append(nkv_s[gi])
        wkv_p.append(w_p)
        sh_p.append(s_p)
        wkv_s.append(w_s)
        sh_s.append(s_s)
    return (yp, ys[:, :t_s],
            jnp.stack(kvp[0]), jnp.stack(kvp[1]), jnp.stack(kvp[2]),
            jnp.stack(wkv_p), jnp.stack(sh_p),
            jnp.stack(kvs[0]), jnp.stack(kvs[1]), jnp.stack(kvs[2]),
            jnp.stack(wkv_s), jnp.stack(sh_s))
```

```python
import functools

import jax
import jax.numpy as jnp
from jax import lax
from jax.experimental import pallas as pl
from jax.experimental.pallas import tpu as pltpu

F32 = jnp.float32
BF16 = jnp.bfloat16

HEAD = 64
LANES = 128
SUBLANES = 8
ATTN_HEADS = 4
ATTN_BLOCK = 128
WINDOWS = (128, 512, 2048)
DILATIONS = (1, 4, 16)
NORM_EPS = 1e-6
LNX_EPS = 64e-5
NEG = -1e30
VMEM_LIMIT = 56 * 1024 * 1024


def _cparams(sem):
    return pltpu.CompilerParams(dimension_semantics=sem, vmem_limit_bytes=VMEM_LIMIT)


def _dot(a, b):
    return jnp.dot(a, b, preferred_element_type=F32)


def _split(x):
    hi = x.astype(BF16)
    lo = (x - hi.astype(F32)).astype(BF16)
    return hi, lo


def _dot3(a, b):
    a_hi, a_lo = _split(a)
    b_hi, b_lo = _split(b)
    return _dot(a_hi, b_hi) + (_dot(a_lo, b_hi) + _dot(a_hi, b_lo))


def _pair_ones():
    r = lax.broadcasted_iota(jnp.int32, (LANES, LANES), 0) // HEAD
    c = lax.broadcasted_iota(jnp.int32, (LANES, LANES), 1) // HEAD
    return (r == c).astype(BF16)


def _head_sum(x, ones):
    outs = []
    for c in range(x.shape[1] // LANES):
        hi, lo = _split(x[:, c * LANES:(c + 1) * LANES])
        outs.append(_dot(hi, ones) + _dot(lo, ones))
    return jnp.concatenate(outs, axis=1)


def _rms(x, g):
    return x * lax.rsqrt(jnp.mean(x * x, axis=-1, keepdims=True) + NORM_EPS) * g


def _sigmoid(x):
    return 1.0 / (1.0 + jnp.exp(-x))


def _softplus(x):
    return jnp.maximum(x, 0.0) + jnp.log(1.0 + jnp.exp(-jnp.abs(x)))


def _in_proj_kernel(x_ref, g_ref, w_ref, *o_refs):
    h = _rms(x_ref[...], g_ref[...]).astype(BF16)
    c0 = 0
    for o_ref in o_refs:
        c1 = c0 + o_ref.shape[1]
        o_ref[...] = _dot(h, w_ref[:, c0:c1])
        c0 = c1


def _in_proj(x, g, w, splits, tm):
    rows, d = x.shape
    assert sum(splits) == w.shape[1]
    return pl.pallas_call(
        _in_proj_kernel,
        grid=(rows // tm,),
        in_specs=[pl.BlockSpec((tm, d), lambda i: (i, 0)),
                  pl.BlockSpec((1, d), lambda i: (0, 0)),
                  pl.BlockSpec(w.shape, lambda i: (0, 0), pipeline_mode=pl.Buffered(1))],
        out_specs=[pl.BlockSpec((tm, c), lambda i: (i, 0)) for c in splits],
        out_shape=[jax.ShapeDtypeStruct((rows, c), F32) for c in splits],
        compiler_params=_cparams(("arbitrary",)),
        name="in_proj",
    )(x, g.reshape(1, d), w)


def _rwkv_prep_kernel(z_ref, sh_ref, mu_ref, w0_ref, w2_ref, a0_ref, a2_ref, g2_ref, kk_ref, ka_ref,
                      rk_ref, r_o, w_o, k_o, v_o, a_o, b_o, g_o, bon_o, last_scr, *, width):
    n = pl.program_id(0)
    i = pl.program_id(1)
    z = z_ref[0]
    tt = z.shape[0]

    @pl.when(i == 0)
    def _():
        last_scr[...] = sh_ref[pl.ds(n, 1), :]

    row = lax.broadcasted_iota(jnp.int32, z.shape, 0)
    prev = jnp.where(row == 0, last_scr[...], pltpu.roll(z, 1, axis=0))
    last_scr[...] = z[tt - 1:tt, :]
    zs = z + (prev - z) * mu_ref[...]

    c0 = 3 * width
    c1 = c0 + w2_ref.shape[0]
    c2 = c1 + a2_ref.shape[0]
    r = zs[:, 0:width]
    k = zs[:, width:2 * width]
    v = zs[:, 2 * width:c0]
    xw = zs[:, c0:c1]
    xa = zs[:, c1:c2]
    xg = zs[:, c2:]

    logw = -_softplus(-(w0_ref[...] + _dot3(jnp.tanh(xw), w2_ref[...]))) - 0.5
    a = _sigmoid(a0_ref[...] + _dot3(xa, a2_ref[...]))
    g = _dot3(_sigmoid(xg), g2_ref[...])

    ones = _pair_ones()
    kk = k * kk_ref[...]
    kk = kk / jnp.maximum(jnp.sqrt(_head_sum(kk * kk, ones)), 1e-12)
    k2 = k * (1.0 + (a - 1.0) * ka_ref[...])
    bonus = _head_sum(r * k2 * rk_ref[...], ones) * v

    r_o[0] = r
    w_o[0] = -jnp.exp(logw)
    k_o[0] = k2
    v_o[0] = v
    a_o[0] = -kk
    b_o[0] = kk * a
    g_o[0] = g
    bon_o[0] = bonus


def _rwkv_prep(z_rwkv, shift0, prm, tt):
    n, t, cols = z_rwkv.shape
    width = prm["decay_w0"].shape[-1]
    row = lambda x: x.reshape(1, -1)
    full = lambda a: pl.BlockSpec(a.shape, lambda b, i: (0,) * a.ndim)
    args = [shift0, row(prm["shift_mu"]), row(prm["decay_w0"]), prm["decay_w2"], row(prm["aaa_a0"]),
            prm["aaa_a2"], prm["gate_g2"], row(prm["k_k"]), row(prm["k_a"]), row(prm["r_k"])]
    out_spec = pl.BlockSpec((1, tt, width), lambda b, i: (b, i, 0))
    out_sds = jax.ShapeDtypeStruct((n, t, width), F32)
    return pl.pallas_call(
        functools.partial(_rwkv_prep_kernel, width=width),
        grid=(n, t // tt),
        in_specs=[pl.BlockSpec((1, tt, cols), lambda b, i: (b, i, 0))] + [full(a) for a in args],
        out_specs=[out_spec] * 8,
        out_shape=[out_sds] * 8,
        scratch_shapes=[pltpu.VMEM((1, cols), F32)],
        compiler_params=_cparams(("arbitrary", "arbitrary")),
        name="rwkv_prep",
    )(z_rwkv, *args)


def _wkv_kernel(r_ref, w_ref, k_ref, v_ref, a_ref, b_ref, s0_ref, y_ref, st_ref, *, valid):
    left = lax.broadcasted_iota(jnp.int32, (HEAD, LANES), 1) < HEAD

    def col_tile(xt, j):
        return jnp.where(left, xt[0:HEAD, j:j + 1], xt[HEAD:2 * HEAD, j:j + 1])

    for p in range(s0_ref.shape[1]):
        cols = slice(p * LANES, (p + 1) * LANES)
        rt = r_ref[0, :, cols].T
        wt = jnp.exp(w_ref[0, :, cols]).T
        kt = k_ref[0, :, cols].T
        at = a_ref[0, :, cols].T
        bt = b_ref[0, :, cols].T
        vg = v_ref[0, :, cols]
        s = s0_ref[0, p]
        ys = []
        for j in range(y_ref.shape[1]):
            if j < valid:
                sa = jnp.sum(s * col_tile(at, j), axis=0, keepdims=True)
                s = s * col_tile(wt, j) + col_tile(bt, j) * sa + col_tile(kt, j) * vg[j:j + 1, :]
                ys.append(jnp.sum(s * col_tile(rt, j), axis=0, keepdims=True))
            else:
                ys.append(jnp.zeros((1, LANES), F32))
        y_ref[0, :, cols] = jnp.concatenate(ys, axis=0)
        st_ref[0, p] = s


def _wkv(r, w, k, v, a, b, s0, valid):
    n, t, c = r.shape
    assert t == SUBLANES
    tok = pl.BlockSpec((1, t, c), lambda bi: (bi, 0, 0))
    st = pl.BlockSpec((1,) + s0.shape[1:], lambda bi: (bi, 0, 0, 0))
    return pl.pallas_call(
        functools.partial(_wkv_kernel, valid=valid),
        grid=(n,),
        in_specs=[tok] * 6 + [st],
        out_specs=[tok, st],
        out_shape=[jax.ShapeDtypeStruct((n, t, c), F32), jax.ShapeDtypeStruct(s0.shape, F32)],
        compiler_params=_cparams(("arbitrary",)),
        name="wkv_scan",
    )(r, w, k, v, a, b, s0)


WKV_CHUNK = 64


def _bd(x):
    xb = x.astype(BF16)
    left = lax.broadcasted_iota(jnp.int32, x.shape, 1) < HEAD
    zero = jnp.zeros_like(xb)
    return jnp.concatenate([jnp.where(left, xb, zero), jnp.where(left, zero, xb)], axis=0)


def _dot_nt(a, b):
    return lax.dot_general(a, b, (((1,), (1,)), ((), ())), preferred_element_type=F32)


def _dot_tn(a, b):
    return lax.dot_general(a, b, (((0,), (0,)), ((), ())), preferred_element_type=F32)


def _wkv_chunk_local(r, lw, k, v, a, b, consts):
    tri, eye2, strict, incl, blocks = consts
    c = WKV_CHUNK
    n = range(len(r))
    zero = jnp.zeros((c, LANES), F32)

    def cumsum(x):
        hi = x.astype(BF16)
        r1 = x - hi.astype(F32)
        mid = r1.astype(BF16)
        lo = (r1 - mid.astype(F32)).astype(BF16)
        return _dot(tri, hi) + (_dot(tri, mid) + _dot(tri, lo))

    lc = [cumsum(lw[i]) for i in n]
    ltot = [lc[i][c - 1:c, :] for i in n]
    w_inv = [jnp.exp(-lc[i]) for i in n]
    ah = [a[i] * jnp.exp(lc[i] - lw[i]) for i in n]
    rh = [r[i] * jnp.exp(lc[i]) for i in n]
    sc = [_dot_nt(jnp.concatenate([ah[i], rh[i]], axis=0).astype(BF16),
                  jnp.concatenate([_bd(b[i] * w_inv[i]), _bd(k[i] * w_inv[i])], axis=0)) for i in n]
    aab = [jnp.where(strict, sc[i][:c, :LANES], zero) for i in n]

    t = [eye2 + jnp.where(blocks[0], aab[i], zero) for i in n]
    for lvl in range(1, len(blocks)):
        grow = blocks[lvl] & ~blocks[lvl - 1]
        x = [_dot(t[i].astype(BF16), _bd(jnp.where(grow, aab[i], zero))) for i in n]
        t = [t[i] + _dot(x[i].astype(BF16), _bd(t[i])) for i in n]

    vb = [_bd(v[i]) for i in n]
    akv = [_dot(jnp.where(strict, sc[i][:c, LANES:], zero).astype(BF16), vb[i]) for i in n]
    pu = [_dot(t[i].astype(BF16), jnp.concatenate([_bd(ah[i]), _bd(akv[i])], axis=1)) for i in n]

    left2 = (lax.broadcasted_iota(jnp.int32, (HEAD, 2 * LANES), 1) % LANES) < HEAD
    g, h, q, y0 = [], [], [], []
    for i in n:
        w_rem = jnp.exp(ltot[i] - lc[i])
        lhs_t = jnp.concatenate([b[i] * w_rem, k[i] * w_rem], axis=0).astype(BF16)
        rhs = jnp.concatenate([pu[i], jnp.concatenate([zero, v[i]], axis=1)], axis=0).astype(BF16)
        full = _dot_tn(lhs_t, rhs)
        gh = jnp.where(left2, full[:HEAD], full[HEAD:])
        g.append(gh[:, :LANES] + eye2 * jnp.exp(ltot[i]))
        h.append(gh[:, LANES:])
    zero_b = jnp.zeros((2 * c, LANES), BF16)
    for i in n:
        mrb = jnp.where(incl, sc[i][c:, :LANES], zero)
        mrk = jnp.where(incl, sc[i][c:, LANES:], zero)
        rhs2 = jnp.concatenate([jnp.concatenate([_bd(pu[i][:, :LANES]), _bd(pu[i][:, LANES:])], axis=1),
                                jnp.concatenate([zero_b, vb[i]], axis=1)], axis=0)
        qy = _dot(jnp.concatenate([mrb, mrk], axis=1).astype(BF16), rhs2)
        q.append(rh[i] + qy[:, :LANES])
        y0.append(qy[:, LANES:])
    return g, h, q, y0


def _wkv_chunked_kernel(r_ref, w_ref, k_ref, v_ref, a_ref, b_ref, s0_ref, y_ref, st_ref, s_scr):
    i = pl.program_id(2)
    c = WKV_CHUNK

    @pl.when(i == 0)
    def _():
        s_scr[...] = s0_ref[0, 0]

    ti = lax.broadcasted_iota(jnp.int32, (c, LANES), 0)
    si = lax.broadcasted_iota(jnp.int32, (c, LANES), 1) % HEAD
    tri = (lax.broadcasted_iota(jnp.int32, (c, c), 0) >= lax.broadcasted_iota(jnp.int32, (c, c), 1)
           ).astype(BF16)
    eye2 = (ti == si).astype(F32)
    sizes = [2 << l for l in range(c.bit_length() - 1)]
    blocks = [(ti // s) == (si // s) for s in sizes]
    consts = (tri, eye2, ti > si, ti >= si, blocks)

    n_chunks = y_ref.shape[1] // c
    chunks = lambda ref: [ref[0, ci * c:(ci + 1) * c, :] for ci in range(n_chunks)]
    g, h, q, y0 = _wkv_chunk_local(chunks(r_ref), chunks(w_ref), chunks(k_ref), chunks(v_ref),
                                   chunks(a_ref), chunks(b_ref), consts)
    st = s_scr[...]
    for ci in range(n_chunks):
        out = _dot(jnp.concatenate([q[ci], g[ci]], axis=0).astype(BF16), _bd(st))
        y_ref[0, ci * c:(ci + 1) * c, :] = out[:c] + y0[ci]
        st = out[c:] + h[ci]
    s_scr[...] = st

    @pl.when(i == pl.num_programs(2) - 1)
    def _():
        st_ref[0, 0] = st


def _wkv_chunked(r, lw, k, v, a, b, s0, tb):
    n, t, c = r.shape
    pairs = c // LANES
    tok = pl.BlockSpec((1, tb, LANES), lambda bi, p, i: (bi, i, p))
    st = pl.BlockSpec((1, 1, HEAD, LANES), lambda bi, p, i: (bi, p, 0, 0))
    return pl.pallas_call(
        _wkv_chunked_kernel,
        grid=(n, pairs, t // tb),
        in_specs=[tok] * 6 + [st],
        out_specs=[tok, st],
        out_shape=[jax.ShapeDtypeStruct((n, t, c), F32), jax.ShapeDtypeStruct(s0.shape, F32)],
        scratch_shapes=[pltpu.VMEM((HEAD, LANES), F32)],
        compiler_params=_cparams(("arbitrary", "arbitrary", "arbitrary")),
        name="wkv_chunked",
    )(r, lw, k, v, a, b, s0)


def _pack_state(s):
    n, h, vd, kd = s.shape
    return s.reshape(n, h // 2, 2, vd, kd).transpose(0, 1, 4, 2, 3).reshape(n, h // 2, kd, 2 * vd)


def _unpack_state(s, h):
    n, _, kd, _ = s.shape
    return s.reshape(n, h // 2, kd, 2, HEAD).transpose(0, 1, 3, 4, 2).reshape(n, h, HEAD, kd)


def _attn_prompt_kernel(q_ref, k_ref, kp_ref, v_ref, vp_ref, o_ref, l_ref, *, dil, sub, n_back, scale):
    first = pl.program_id(1) == 0
    qi = lax.broadcasted_iota(jnp.int32, (ATTN_BLOCK, 2 * ATTN_BLOCK), 0)
    ki = lax.broadcasted_iota(jnp.int32, (ATTN_BLOCK, 2 * ATTN_BLOCK), 1)
    dist = qi + ATTN_BLOCK - ki
    band = (dist >= 0) & (dist <= n_back)
    own = ki >= ATTN_BLOCK

    def rows(j, r):
        if dil == 1:
            return pl.ds(j * ATTN_BLOCK, ATTN_BLOCK)
        return pl.ds(j * ATTN_BLOCK * dil + r, ATTN_BLOCK, stride=dil)

    def unit(j, r):
        q = (q_ref[0, rows(j, r), :] * scale).astype(BF16)
        if j == 0:
            kp, vp = kp_ref[0, rows(0, r), :], vp_ref[0, rows(0, r), :]
            mask = band & (own | jnp.logical_not(first))
        else:
            kp, vp, mask = k_ref[0, rows(j - 1, r), :], v_ref[0, rows(j - 1, r), :], band
        kcat = jnp.concatenate([kp, k_ref[0, rows(j, r), :]], axis=0).astype(BF16)
        vcat = jnp.concatenate([vp, v_ref[0, rows(j, r), :]], axis=0).astype(BF16)
        outs, lses = [], []
        for h in range(LANES // HEAD):
            sl = slice(h * HEAD, (h + 1) * HEAD)
            s = jnp.where(mask, _dot_nt(q[:, sl], kcat[:, sl]), NEG)
            m = jnp.max(s, axis=-1, keepdims=True)
            e = jnp.exp(s - m)
            den = jnp.sum(e, axis=-1, keepdims=True)
            outs.append(_dot(e.astype(BF16), vcat[:, sl]) / den)
            lses.append(jnp.broadcast_to(m + jnp.log(den), (ATTN_BLOCK, HEAD)))
        o_ref[0, rows(j, r), :] = jnp.concatenate(outs, axis=1)
        l_ref[0, rows(j, r), :] = jnp.concatenate(lses, axis=1)

    for j in range(sub):
        if dil <= 4:
            for r in range(dil):
                unit(j, r)
        else:
            def body(r4, carry, j=j):
                for u in range(4):
                    unit(j, r4 * 4 + u)
                return carry
            lax.fori_loop(0, dil // 4, body, 0)


def _attn_prompt(zqkv, gi, span):
    n, t, cols = zqkv.shape
    d = DILATIONS[gi]
    aw = ATTN_HEADS * HEAD
    ng = len(DILATIONS)
    back = ATTN_BLOCK * d
    sub = span // back
    halves = aw // LANES
    col = lambda which, hp: (which * ng + gi) * halves + hp
    cur = lambda which: pl.BlockSpec((1, span, LANES), lambda b, i, hp: (b, i, col(which, hp)))
    prev = lambda which: pl.BlockSpec((1, back, LANES),
                                      lambda b, i, hp: (b, jnp.maximum(i * sub - 1, 0), col(which, hp)))
    out_spec = pl.BlockSpec((1, span, LANES), lambda b, i, hp: (b, i, hp))
    out_sds = jax.ShapeDtypeStruct((n, t, aw), F32)
    return pl.pallas_call(
        functools.partial(_attn_prompt_kernel, dil=d, sub=sub, n_back=WINDOWS[gi] // d, scale=HEAD ** -0.5),
        grid=(n, t // span, halves),
        in_specs=[cur(0), cur(1), prev(1), cur(2), prev(2)],
        out_specs=[out_spec, out_spec],
        out_shape=[out_sds, out_sds],
        compiler_params=_cparams(("arbitrary", "arbitrary", "arbitrary")),
        name=f"attn_prompt_g{gi}",
    )(zqkv, zqkv, zqkv, zqkv, zqkv)


def _attn_sample_kernel(q_ref, k_ref, v_ref, c_ref, o_ref, l_ref, *, dil, keep, cl, n_back, valid, scale):
    q = (q_ref[0] * scale).astype(BF16)
    tp = q.shape[0]
    cache = c_ref[0]
    rows = cache.shape[0]
    aw = ATTN_HEADS * HEAD
    keys = jnp.concatenate([cache[:, :aw], k_ref[0]], axis=0).astype(BF16)
    vals = jnp.concatenate([cache[:, aw:], v_ref[0]], axis=0).astype(BF16)
    ti = lax.broadcasted_iota(jnp.int32, (tp, rows + tp), 0)
    ci = lax.broadcasted_iota(jnp.int32, (tp, rows + tp), 1)
    pos = jnp.where(ci < rows, (ci // keep) * dil + ci % keep, cl + ci - rows)
    dist = cl + ti - pos
    mask = (dist >= 0) & (dist % dil == 0) & (dist <= dil * n_back) & (pos < cl + valid)
    outs, lses = [], []
    for h in range(ATTN_HEADS):
        sl = slice(h * HEAD, (h + 1) * HEAD)
        s = jnp.where(mask, _dot_nt(q[:, sl], keys[:, sl]), NEG)
        m = jnp.max(s, axis=-1, keepdims=True)
        e = jnp.exp(s - m)
        den = jnp.sum(e, axis=-1, keepdims=True)
        outs.append(_dot(e.astype(BF16), vals[:, sl]) / den)
        lses.append(jnp.broadcast_to(m + jnp.log(den), (tp, HEAD)))
    o_ref[0] = jnp.concatenate(outs, axis=1)
    l_ref[0] = jnp.concatenate(lses, axis=1)


def _attn_sample(zqkv, cache, gi, valid):
    n, tp, cols = zqkv.shape
    aw = ATTN_HEADS * HEAD
    cl = cache.shape[1]
    ng = len(DILATIONS)
    d = DILATIONS[gi]
    assert cl % d == 0
    keep = min(d, valid)
    rows = (cl // d) * keep
    cache = cache.reshape(n, cl // d, d, 2 * aw)[:, :, :keep].reshape(n, rows, 2 * aw)
    spec = lambda which: pl.BlockSpec((1, tp, aw), lambda b: (b, 0, which * ng + gi))
    out_spec = pl.BlockSpec((1, tp, aw), lambda b: (b, 0, 0))
    out_sds = jax.ShapeDtypeStruct((n, tp, aw), F32)
    return pl.pallas_call(
        functools.partial(_attn_sample_kernel, dil=d, keep=keep, cl=cl, n_back=WINDOWS[gi] // d,
                          valid=valid, scale=HEAD ** -0.5),
        grid=(n,),
        in_specs=[spec(0), spec(1), spec(2), pl.BlockSpec((1, rows, 2 * aw), lambda b: (b, 0, 0))],
        out_specs=[out_spec, out_spec],
        out_shape=[out_sds, out_sds],
        compiler_params=_cparams(("arbitrary",)),
        name=f"attn_sample_g{gi}",
    )(zqkv, zqkv, zqkv, cache)


def _merge_kernel(x_ref, y_ref, bon_ref, g_ref, lw_ref, lb_ref, o0, o1, o2, l0, l1, l2, zg_ref,
                  wbr_ref, wba_ref, wout_ref, out_ref):
    ones = _pair_ones()
    y = y_ref[...]
    mu = _head_sum(y, ones) * (1.0 / HEAD)
    yc = y - mu
    var = _head_sum(yc * yc, ones) * (1.0 / HEAD)
    yn = yc * lax.rsqrt(var + LNX_EPS) * lw_ref[...] + lb_ref[...]
    rwkv = (yn + bon_ref[...]) * g_ref[...]

    ls = [l0[...], l1[...], l2[...]]
    m = jnp.maximum(jnp.maximum(ls[0], ls[1]), ls[2])
    es = [jnp.exp(l - m) for l in ls]
    attn = (es[0] * o0[...] + es[1] * o1[...] + es[2] * o2[...]) / (es[0] + es[1] + es[2])

    d = x_ref.shape[1]
    gates = _sigmoid(zg_ref[...])
    mixed = (gates[:, :d] * _dot(rwkv.astype(BF16), wbr_ref[...])
             + gates[:, d:] * _dot(attn.astype(BF16), wba_ref[...]))
    out_ref[...] = x_ref[...] + _dot(mixed.astype(BF16), wout_ref[...])


def _merge(x, y, bonus, g, lnx_w, lnx_b, os_, ls_, zg, wbr, wba, wout, tm):
    rows, d = x.shape
    rowblk = lambda a: pl.BlockSpec((tm, a.shape[1]), lambda i: (i, 0))
    full = lambda a: pl.BlockSpec(a.shape, lambda i: (0, 0))
    lw, lb = lnx_w.reshape(1, -1), lnx_b.reshape(1, -1)
    args = [x, y, bonus, g, lw, lb, *os_, *ls_, zg, wbr, wba, wout]
    specs = [rowblk(x), rowblk(y), rowblk(bonus), rowblk(g), full(lw), full(lb)] + \
            [rowblk(a) for a in (*os_, *ls_)] + [rowblk(zg), full(wbr), full(wba), full(wout)]
    return pl.pallas_call(
        _merge_kernel,
        grid=(rows // tm,),
        in_specs=specs,
        out_specs=pl.BlockSpec((tm, d), lambda i: (i, 0)),
        out_shape=jax.ShapeDtypeStruct((rows, d), F32),
        compiler_params=_cparams(("arbitrary",)),
        name="merge_out",
    )(*args)


def _mlp_kernel(x_ref, g_ref, w1_ref, w2_ref, o_ref, h_scr, acc_scr):
    j = pl.program_id(1)

    @pl.when(j == 0)
    def _():
        h_scr[...] = _rms(x_ref[...], g_ref[...]).astype(BF16)
        acc_scr[...] = x_ref[...]

    u = jnp.maximum(_dot(h_scr[...], w1_ref[...]), 0.0)
    acc_scr[...] += _dot((u * u).astype(BF16), w2_ref[...])

    @pl.when(j == pl.num_programs(1) - 1)
    def _():
        o_ref[...] = acc_scr[...]


def _mlp(x, g, w1, w2, tm, tf):
    rows, d = x.shape
    dff = w1.shape[1]
    return pl.pallas_call(
        _mlp_kernel,
        grid=(rows // tm, dff // tf),
        in_specs=[pl.BlockSpec((tm, d), lambda i, j: (i, 0)),
                  pl.BlockSpec((1, d), lambda i, j: (0, 0)),
                  pl.BlockSpec((d, tf), lambda i, j: (0, j)),
                  pl.BlockSpec((tf, d), lambda i, j: (j, 0))],
        out_specs=pl.BlockSpec((tm, d), lambda i, j: (i, 0)),
        out_shape=jax.ShapeDtypeStruct((rows, d), F32),
        scratch_shapes=[pltpu.VMEM((tm, d), BF16), pltpu.VMEM((tm, d), F32)],
        compiler_params=_cparams(("arbitrary", "arbitrary")),
        name="mlp",
    )(x, g.reshape(1, d), w1, w2)


def _ple_kernel(x_ref, p_ref, g_ref, wg_ref, wp_ref, gf_ref, o_ref, *, final):
    x = x_ref[...]
    h = _rms(x, g_ref[...]).astype(BF16)
    x = x + _sigmoid(_dot(h, wg_ref[...])) * _dot(p_ref[...].astype(BF16), wp_ref[...])
    if final:
        x = _rms(x, gf_ref[...])
    o_ref[...] = x


def _ple(x, p, g, wg, wp, gf, final, tm):
    rows, d = x.shape
    rowblk = lambda a: pl.BlockSpec((tm, a.shape[1]), lambda i: (i, 0))
    full = lambda a: pl.BlockSpec(a.shape, lambda i: (0, 0))
    g, gf = g.reshape(1, d), gf.reshape(1, d)
    return pl.pallas_call(
        functools.partial(_ple_kernel, final=final),
        grid=(rows // tm,),
        in_specs=[rowblk(x), rowblk(p), full(g), full(wg), full(wp), full(gf)],
        out_specs=pl.BlockSpec((tm, d), lambda i: (i, 0)),
        out_shape=jax.ShapeDtypeStruct((rows, d), F32),
        compiler_params=_cparams(("arbitrary",)),
        name="ple",
    )(x, p, g, wg, wp, gf)


def _layer(x, p, wkv0, shift0, caches, prm, norm_final, final, valid):
    n, t, d = x.shape
    rows = n * t
    width = prm["decay_w0"].shape[-1]
    rwkv_cols = prm["shift_mu"].shape[-1]
    aw = ATTN_HEADS * HEAD
    qkv_cols = 3 * len(DILATIONS) * aw
    tm = min(512, rows)
    x2 = x.reshape(rows, d)

    pick = lambda cols, opts: next(o for o in opts if cols % o == 0)
    z_rwkv, z_qkv, z_gate = _in_proj(x2, prm["norm_mix"], prm["w_in"].astype(BF16),
                                     (rwkv_cols, qkv_cols, 2 * d), min(256, rows))
    z_rwkv = z_rwkv.reshape(n, t, rwkv_cols)
    z_qkv = z_qkv.reshape(n, t, qkv_cols)

    r, w, k, v, a, b, g, bonus = _rwkv_prep(z_rwkv, shift0, prm, min(256, t))
    heads = width // HEAD
    if valid == t and t % WKV_CHUNK == 0:
        y, s_t = _wkv_chunked(r, w, k, v, a, b, _pack_state(wkv0), pick(t, (512, 256, WKV_CHUNK)))
    else:
        y, s_t = _wkv(r, w, k, v, a, b, _pack_state(wkv0), valid)
    wkv_t = _unpack_state(s_t, heads)
    shift_t = z_rwkv[:, valid - 1, :]

    os_, ls_, new_kv = [], [], []
    ng = len(DILATIONS)
    for gi in range(ng):
        kg = z_qkv[:, :, (ng + gi) * aw:(ng + gi + 1) * aw].reshape(n, t, ATTN_HEADS, HEAD)
        vg = z_qkv[:, :, (2 * ng + gi) * aw:(2 * ng + gi + 1) * aw].reshape(n, t, ATTN_HEADS, HEAD)
        if caches is None:
            o, l = _attn_prompt(z_qkv, gi, max(1024, ATTN_BLOCK * DILATIONS[gi]))
            keep = min(WINDOWS[gi], t)
            new_kv.append(jnp.stack([kg[:, t - keep:], vg[:, t - keep:]], axis=2))
        else:
            o, l = _attn_sample(z_qkv, caches[gi], gi, valid)
            new_kv.append(jnp.stack([kg[:, :valid], vg[:, :valid]], axis=2))
        os_.append(o.reshape(rows, aw))
        ls_.append(l.reshape(rows, aw))

    flat = lambda u: u.reshape(rows, -1)
    x2 = _merge(x2, flat(y), flat(bonus), flat(g), prm["lnx_w"], prm["lnx_b"], os_, ls_, z_gate,
                prm["w_branch_rwkv"].astype(BF16), prm["w_branch_attn"].astype(BF16),
                prm["w_out"].astype(BF16), min(256, rows))
    x2 = _mlp(x2, prm["norm_mlp"], prm["w_ff1"].astype(BF16), prm["w_ff2"].astype(BF16), tm, 1024)
    x2 = _ple(x2, p.reshape(rows, -1), prm["norm_ple"], prm["w_ple_gate"].astype(BF16),
              prm["w_ple"].astype(BF16), norm_final, final, tm)
    return x2.reshape(n, t, d), new_kv, wkv_t, shift_t


_LAYER_PARAMS = ("norm_mix", "w_in", "shift_mu", "decay_w0", "decay_w2", "aaa_a0", "aaa_a2", "gate_g2",
                 "k_k", "k_a", "r_k", "lnx_w", "lnx_b", "w_branch_rwkv", "w_branch_attn", "w_out",
                 "norm_mlp", "w_ff1", "w_ff2", "norm_ple", "w_ple", "w_ple_gate")


def kernel(x_prompt, x_sample, cache_kv_g0, cache_kv_g1, cache_kv_g2, state_wkv, state_shift, p_prompt, p_sample, norm_mix, w_in, shift_mu, decay_w0, decay_w2, aaa_a0, aaa_a2, gate_g2, k_k, k_a, r_k, lnx_w, lnx_b, w_branch_rwkv, w_branch_attn, w_out, norm_mlp, w_ff1, w_ff2, norm_ple, w_ple, w_ple_gate, norm_final):
    stacked = dict(zip(_LAYER_PARAMS, (norm_mix, w_in, shift_mu, decay_w0, decay_w2, aaa_a0, aaa_a2,
                                       gate_g2, k_k, k_a, r_k, lnx_w, lnx_b, w_branch_rwkv,
                                       w_branch_attn, w_out, norm_mlp, w_ff1, w_ff2, norm_ple, w_ple,
                                       w_ple_gate)))
    depth = norm_mix.shape[0]
    nb_p, _, d = x_prompt.shape
    nb_s, t_s, _ = x_sample.shape
    heads = decay_w0.shape[-1] // HEAD
    rwkv_cols = shift_mu.shape[-1]
    t_pad = -(-t_s // SUBLANES) * SUBLANES
    pad_t = lambda u: jnp.pad(u, ((0, 0), (0, t_pad - t_s), (0, 0)))

    yp, ys = x_prompt, pad_t(x_sample)
    kvp, kvs = [[], [], []], [[], [], []]
    wkv_p, sh_p, wkv_s, sh_s = [], [], [], []
    for i in range(depth):
        prm = {name: val[i] for name, val in stacked.items()}
        prm["r_k"] = prm["r_k"].reshape(-1)
        last = i == depth - 1
        wkv0 = jnp.zeros((nb_p, heads, HEAD, HEAD), F32)
        shift0 = jnp.zeros((nb_p, rwkv_cols), F32)
        yp, nkv_p, w_p, s_p = _layer(yp, p_prompt[i], wkv0, shift0, None, prm, norm_final, last,
                                     x_prompt.shape[1])
        ys, nkv_s, w_s, s_s = _layer(ys, pad_t(p_sample[i]), state_wkv[i], state_shift[i],
                                     (cache_kv_g0[i], cache_kv_g1[i], cache_kv_g2[i]), prm, norm_final,
                                     last, t_s)
        for gi in range(3):
            kvp[gi].append(nkv_p[gi])
            kvs[gi].append(nkv_s[gi])
        wkv_p.append(w_p)
        sh_p.append(s_p)
        wkv_s.append(w_s)
        sh_s.append(s_s)
    return (yp, ys[:, :t_s],
            jnp.stack(kvp[0]), jnp.stack(kvp[1]), jnp.stack(kvp[2]),
            jnp.stack(wkv_p), jnp.stack(sh_p),
            jnp.stack(kvs[0]), jnp.stack(kvs[1]), jnp.stack(kvs[2]),
            jnp.stack(wkv_s), jnp.stack(sh_s))
```

```python
import functools

import jax
import jax.numpy as jnp
from jax import lax
from jax.experimental import pallas as pl
from jax.experimental.pallas import tpu as pltpu

F32 = jnp.float32
BF16 = jnp.bfloat16

HEAD = 64
LANES = 128
SUBLANES = 8
ATTN_HEADS = 4
ATTN_BLOCK = 128
WINDOWS = (128, 512, 2048)
DILATIONS = (1, 4, 16)
NORM_EPS = 1e-6
LNX_EPS = 64e-5
NEG = -1e30
VMEM_LIMIT = 56 * 1024 * 1024


def _cparams(sem):
    return pltpu.CompilerParams(dimension_semantics=sem, vmem_limit_bytes=VMEM_LIMIT)


def _dot(a, b):
    return jnp.dot(a, b, preferred_element_type=F32)


def _split(x):
    hi = x.astype(BF16)
    lo = (x - hi.astype(F32)).astype(BF16)
    return hi, lo


def _dot3(a, b):
    a_hi, a_lo = _split(a)
    b_hi, b_lo = _split(b)
    return _dot(a_hi, b_hi) + (_dot(a_lo, b_hi) + _dot(a_hi, b_lo))


def _pair_ones():
    r = lax.broadcasted_iota(jnp.int32, (LANES, LANES), 0) // HEAD
    c = lax.broadcasted_iota(jnp.int32, (LANES, LANES), 1) // HEAD
    return (r == c).astype(BF16)


def _head_sum(x, ones):
    outs = []
    for c in range(x.shape[1] // LANES):
        hi, lo = _split(x[:, c * LANES:(c + 1) * LANES])
        outs.append(_dot(hi, ones) + _dot(lo, ones))
    return jnp.concatenate(outs, axis=1)


def _rms(x, g):
    return x * lax.rsqrt(jnp.mean(x * x, axis=-1, keepdims=True) + NORM_EPS) * g


def _sigmoid(x):
    return 1.0 / (1.0 + jnp.exp(-x))


def _softplus(x):
    return jnp.maximum(x, 0.0) + jnp.log(1.0 + jnp.exp(-jnp.abs(x)))


def _in_proj_kernel(x_ref, g_ref, w_ref, *o_refs):
    h = _rms(x_ref[...], g_ref[...]).astype(BF16)
    c0 = 0
    for o_ref in o_refs:
        c1 = c0 + o_ref.shape[1]
        o_ref[...] = _dot(h, w_ref[:, c0:c1])
        c0 = c1


def _in_proj(x, g, w, splits, tm):
    rows, d = x.shape
    assert sum(splits) == w.shape[1]
    return pl.pallas_call(
        _in_proj_kernel,
        grid=(rows // tm,),
        in_specs=[pl.BlockSpec((tm, d), lambda i: (i, 0)),
                  pl.BlockSpec((1, d), lambda i: (0, 0)),
                  pl.BlockSpec(w.shape, lambda i: (0, 0), pipeline_mode=pl.Buffered(1))],
        out_specs=[pl.BlockSpec((tm, c), lambda i: (i, 0)) for c in splits],
        out_shape=[jax.ShapeDtypeStruct((rows, c), F32) for c in splits],
        compiler_params=_cparams(("arbitrary",)),
        name="in_proj",
    )(x, g.reshape(1, d), w)


def _rwkv_prep_kernel(z_ref, sh_ref, mu_ref, w0_ref, w2_ref, a0_ref, a2_ref, g2_ref, kk_ref, ka_ref,
                      rk_ref, r_o, w_o, k_o, v_o, a_o, b_o, g_o, bon_o, last_scr, *, width):
    n = pl.program_id(0)
    i = pl.program_id(1)
    z = z_ref[0]
    tt = z.shape[0]

    @pl.when(i == 0)
    def _():
        last_scr[...] = sh_ref[pl.ds(n, 1), :]

    row = lax.broadcasted_iota(jnp.int32, z.shape, 0)
    prev = jnp.where(row == 0, last_scr[...], pltpu.roll(z, 1, axis=0))
    last_scr[...] = z[tt - 1:tt, :]
    zs = z + (prev - z) * mu_ref[...]

    c0 = 3 * width
    c1 = c0 + w2_ref.shape[0]
    c2 = c1 + a2_ref.shape[0]
    r = zs[:, 0:width]
    k = zs[:, width:2 * width]
    v = zs[:, 2 * width:c0]
    xw = zs[:, c0:c1]
    xa = zs[:, c1:c2]
    xg = zs[:, c2:]

    logw = -_softplus(-(w0_ref[...] + _dot3(jnp.tanh(xw), w2_ref[...]))) - 0.5
    a = _sigmoid(a0_ref[...] + _dot3(xa, a2_ref[...]))
    g = _dot3(_sigmoid(xg), g2_ref[...])

    ones = _pair_ones()
    kk = k * kk_ref[...]
    kk = kk / jnp.maximum(jnp.sqrt(_head_sum(kk * kk, ones)), 1e-12)
    k2 = k * (1.0 + (a - 1.0) * ka_ref[...])
    bonus = _head_sum(r * k2 * rk_ref[...], ones) * v

    r_o[0] = r
    w_o[0] = -jnp.exp(logw)
    k_o[0] = k2
    v_o[0] = v
    a_o[0] = -kk
    b_o[0] = kk * a
    g_o[0] = g
    bon_o[0] = bonus


def _rwkv_prep(z_rwkv, shift0, prm, tt):
    n, t, cols = z_rwkv.shape
    width = prm["decay_w0"].shape[-1]
    row = lambda x: x.reshape(1, -1)
    full = lambda a: pl.BlockSpec(a.shape, lambda b, i: (0,) * a.ndim)
    args = [shift0, row(prm["shift_mu"]), row(prm["decay_w0"]), prm["decay_w2"], row(prm["aaa_a0"]),
            prm["aaa_a2"], prm["gate_g2"], row(prm["k_k"]), row(prm["k_a"]), row(prm["r_k"])]
    out_spec = pl.BlockSpec((1, tt, width), lambda b, i: (b, i, 0))
    out_sds = jax.ShapeDtypeStruct((n, t, width), F32)
    return pl.pallas_call(
        functools.partial(_rwkv_prep_kernel, width=width),
        grid=(n, t // tt),
        in_specs=[pl.BlockSpec((1, tt, cols), lambda b, i: (b, i, 0))] + [full(a) for a in args],
        out_specs=[out_spec] * 8,
        out_shape=[out_sds] * 8,
        scratch_shapes=[pltpu.VMEM((1, cols), F32)],
        compiler_params=_cparams(("arbitrary", "arbitrary")),
        name="rwkv_prep",
    )(z_rwkv, *args)


def _wkv_kernel(r_ref, w_ref, k_ref, v_ref, a_ref, b_ref, s0_ref, y_ref, st_ref, *, valid):
    left = lax.broadcasted_iota(jnp.int32, (HEAD, LANES), 1) < HEAD

    def col_tile(xt, j):
        return jnp.where(left, xt[0:HEAD, j:j + 1], xt[HEAD:2 * HEAD, j:j + 1])

    for p in range(s0_ref.shape[1]):
        cols = slice(p * LANES, (p + 1) * LANES)
        rt = r_ref[0, :, cols].T
        wt = jnp.exp(w_ref[0, :, cols]).T
        kt = k_ref[0, :, cols].T
        at = a_ref[0, :, cols].T
        bt = b_ref[0, :, cols].T
        vg = v_ref[0, :, cols]
        s = s0_ref[0, p]
        ys = []
        for j in range(y_ref.shape[1]):
            if j < valid:
                sa = jnp.sum(s * col_tile(at, j), axis=0, keepdims=True)
                s = s * col_tile(wt, j) + col_tile(bt, j) * sa + col_tile(kt, j) * vg[j:j + 1, :]
                ys.append(jnp.sum(s * col_tile(rt, j), axis=0, keepdims=True))
            else:
                ys.append(jnp.zeros((1, LANES), F32))
        y_ref[0, :, cols] = jnp.concatenate(ys, axis=0)
        st_ref[0, p] = s


def _wkv(r, w, k, v, a, b, s0, valid):
    n, t, c = r.shape
    assert t == SUBLANES
    tok = pl.BlockSpec((1, t, c), lambda bi: (bi, 0, 0))
    st = pl.BlockSpec((1,) + s0.shape[1:], lambda bi: (bi, 0, 0, 0))
    return pl.pallas_call(
        functools.partial(_wkv_kernel, valid=valid),
        grid=(n,),
        in_specs=[tok] * 6 + [st],
        out_specs=[tok, st],
        out_shape=[jax.ShapeDtypeStruct((n, t, c), F32), jax.ShapeDtypeStruct(s0.shape, F32)],
        compiler_params=_cparams(("arbitrary",)),
        name="wkv_scan",
    )(r, w, k, v, a, b, s0)


WKV_CHUNK = 64


def _bd(x):
    xb = x.astype(BF16)
    left = lax.broadcasted_iota(jnp.int32, x.shape, 1) < HEAD
    zero = jnp.zeros_like(xb)
    return jnp.concatenate([jnp.where(left, xb, zero), jnp.where(left, zero, xb)], axis=0)


def _dot_nt(a, b):
    return lax.dot_general(a, b, (((1,), (1,)), ((), ())), preferred_element_type=F32)


def _dot_tn(a, b):
    return lax.dot_general(a, b, (((0,), (0,)), ((), ())), preferred_element_type=F32)


def _wkv_chunk_local(r, lw, k, v, a, b, consts):
    tri, eye2, strict, incl, blocks = consts
    c = WKV_CHUNK
    n = range(len(r))
    zero = jnp.zeros((c, LANES), F32)

    def cumsum(x):
        hi = x.astype(BF16)
        r1 = x - hi.astype(F32)
        mid = r1.astype(BF16)
        lo = (r1 - mid.astype(F32)).astype(BF16)
        return _dot(tri, hi) + (_dot(tri, mid) + _dot(tri, lo))

    lc = [cumsum(lw[i]) for i in n]
    ltot = [lc[i][c - 1:c, :] for i in n]
    w_inv = [jnp.exp(-lc[i]) for i in n]
    ah = [a[i] * jnp.exp(lc[i] - lw[i]) for i in n]
    rh = [r[i] * jnp.exp(lc[i]) for i in n]
    sc = [_dot_nt(jnp.concatenate([ah[i], rh[i]], axis=0).astype(BF16),
                  jnp.concatenate([_bd(b[i] * w_inv[i]), _bd(k[i] * w_inv[i])], axis=0)) for i in n]
    aab = [jnp.where(strict, sc[i][:c, :LANES], zero) for i in n]

    t = [eye2 + jnp.where(blocks[0], aab[i], zero) for i in n]
    for lvl in range(1, len(blocks)):
        grow = blocks[lvl] & ~blocks[lvl - 1]
        x = [_dot(t[i].astype(BF16), _bd(jnp.where(grow, aab[i], zero))) for i in n]
        t = [t[i] + _dot(x[i].astype(BF16), _bd(t[i])) for i in n]

    vb = [_bd(v[i]) for i in n]
    akv = [_dot(jnp.where(strict, sc[i][:c, LANES:], zero).astype(BF16), vb[i]) for i in n]
    pu = [_dot(t[i].astype(BF16), jnp.concatenate([_bd(ah[i]), _bd(akv[i])], axis=1)) for i in n]

    left2 = (lax.broadcasted_iota(jnp.int32, (HEAD, 2 * LANES), 1) % LANES) < HEAD
    g, h, q, y0 = [], [], [], []
    for i in n:
        w_rem = jnp.exp(ltot[i] - lc[i])
        lhs_t = jnp.concatenate([b[i] * w_rem, k[i] * w_rem], axis=0).astype(BF16)
        rhs = jnp.concatenate([pu[i], jnp.concatenate([zero, v[i]], axis=1)], axis=0).astype(BF16)
        full = _dot_tn(lhs_t, rhs)
        gh = jnp.where(left2, full[:HEAD], full[HEAD:])
        g.append(gh[:, :LANES] + eye2 * jnp.exp(ltot[i]))
        h.append(gh[:, LANES:])
    zero_b = jnp.zeros((2 * c, LANES), BF16)
    for i in n:
        mrb = jnp.where(incl, sc[i][c:, :LANES], zero)
        mrk = jnp.where(incl, sc[i][c:, LANES:], zero)
        rhs2 = jnp.concatenate([jnp.concatenate([_bd(pu[i][:, :LANES]), _bd(pu[i][:, LANES:])], axis=1),
                                jnp.concatenate([zero_b, vb[i]], axis=1)], axis=0)
        qy = _dot(jnp.concatenate([mrb, mrk], axis=1).astype(BF16), rhs2)
        q.append(rh[i] + qy[:, :LANES])
        y0.append(qy[:, LANES:])
    return g, h, q, y0


def _wkv_chunked_kernel(r_ref, w_ref, k_ref, v_ref, a_ref, b_ref, s0_ref, y_ref, st_ref, s_scr):
    i = pl.program_id(1)
    c = WKV_CHUNK

    @pl.when(i == 0)
    def _():
        s_scr[...] = s0_ref[:, 0]

    ti = lax.broadcasted_iota(jnp.int32, (c, LANES), 0)
    si = lax.broadcasted_iota(jnp.int32, (c, LANES), 1) % HEAD
    tri = (lax.broadcasted_iota(jnp.int32, (c, c), 0) >= lax.broadcasted_iota(jnp.int32, (c, c), 1)
           ).astype(BF16)
    eye2 = (ti == si).astype(F32)
    sizes = [2 << l for l in range(c.bit_length() - 1)]
    blocks = [(ti // s) == (si // s) for s in sizes]
    consts = (tri, eye2, ti > si, ti >= si, blocks)

    n_seq = y_ref.shape[0]
    n_chunks = y_ref.shape[1] // c
    chunks = lambda ref: [ref[b, ci * c:(ci + 1) * c, :] for ci in range(n_chunks) for b in range(n_seq)]
    g, h, q, y0 = _wkv_chunk_local(chunks(r_ref), chunks(w_ref), chunks(k_ref), chunks(v_ref),
                                   chunks(a_ref), chunks(b_ref), consts)
    st = [s_scr[b] for b in range(n_seq)]
    for ci in range(n_chunks):
        for b in range(n_seq):
            u = ci * n_seq + b
            out = _dot(jnp.concatenate([q[u], g[u]], axis=0).astype(BF16), _bd(st[b]))
            y_ref[b, ci * c:(ci + 1) * c, :] = out[:c] + y0[u]
            st[b] = out[c:] + h[u]
    for b in range(n_seq):
        s_scr[b] = st[b]

    @pl.when(i == pl.num_programs(1) - 1)
    def _():
        for b in range(n_seq):
            st_ref[b, 0] = st[b]


def _wkv_chunked(r, lw, k, v, a, b, s0, tb):
    n, t, c = r.shape
    pairs = c // LANES
    tok = pl.BlockSpec((n, tb, LANES), lambda p, i: (0, i, p))
    st = pl.BlockSpec((n, 1, HEAD, LANES), lambda p, i: (0, p, 0, 0))
    return pl.pallas_call(
        _wkv_chunked_kernel,
        grid=(pairs, t // tb),
        in_specs=[tok] * 6 + [st],
        out_specs=[tok, st],
        out_shape=[jax.ShapeDtypeStruct((n, t, c), F32), jax.ShapeDtypeStruct(s0.shape, F32)],
        scratch_shapes=[pltpu.VMEM((n, HEAD, LANES), F32)],
        compiler_params=_cparams(("arbitrary", "arbitrary")),
        name="wkv_chunked",
    )(r, lw, k, v, a, b, s0)


def _pack_state(s):
    n, h, vd, kd = s.shape
    return s.reshape(n, h // 2, 2, vd, kd).transpose(0, 1, 4, 2, 3).reshape(n, h // 2, kd, 2 * vd)


def _unpack_state(s, h):
    n, _, kd, _ = s.shape
    return s.reshape(n, h // 2, kd, 2, HEAD).transpose(0, 1, 3, 4, 2).reshape(n, h, HEAD, kd)


def _attn_prompt_kernel(q_ref, k_ref, kp_ref, v_ref, vp_ref, o_ref, l_ref, *, dil, sub, n_back, scale):
    first = pl.program_id(1) == 0
    qi = lax.broadcasted_iota(jnp.int32, (ATTN_BLOCK, 2 * ATTN_BLOCK), 0)
    ki = lax.broadcasted_iota(jnp.int32, (ATTN_BLOCK, 2 * ATTN_BLOCK), 1)
    dist = qi + ATTN_BLOCK - ki
    band = (dist >= 0) & (dist <= n_back)
    own = ki >= ATTN_BLOCK

    def rows(j, r):
        if dil == 1:
            return pl.ds(j * ATTN_BLOCK, ATTN_BLOCK)
        return pl.ds(j * ATTN_BLOCK * dil + r, ATTN_BLOCK, stride=dil)

    def unit(j, r):
        q = (q_ref[0, rows(j, r), :] * scale).astype(BF16)
        if j == 0:
            kp, vp = kp_ref[0, rows(0, r), :], vp_ref[0, rows(0, r), :]
            mask = band & (own | jnp.logical_not(first))
        else:
            kp, vp, mask = k_ref[0, rows(j - 1, r), :], v_ref[0, rows(j - 1, r), :], band
        kcat = jnp.concatenate([kp, k_ref[0, rows(j, r), :]], axis=0).astype(BF16)
        vcat = jnp.concatenate([vp, v_ref[0, rows(j, r), :]], axis=0).astype(BF16)
        outs, lses = [], []
        for h in range(LANES // HEAD):
            sl = slice(h * HEAD, (h + 1) * HEAD)
            s = jnp.where(mask, _dot_nt(q[:, sl], kcat[:, sl]), NEG)
            m = jnp.max(s, axis=-1, keepdims=True)
            e = jnp.exp(s - m)
            den = jnp.sum(e, axis=-1, keepdims=True)
            outs.append(_dot(e.astype(BF16), vcat[:, sl]) / den)
            lses.append(jnp.broadcast_to(m + jnp.log(den), (ATTN_BLOCK, HEAD)))
        o_ref[0, rows(j, r), :] = jnp.concatenate(outs, axis=1)
        l_ref[0, rows(j, r), :] = jnp.concatenate(lses, axis=1)

    for j in range(sub):
        if dil <= 4:
            for r in range(dil):
                unit(j, r)
        else:
            def body(r4, carry, j=j):
                for u in range(4):
                    unit(j, r4 * 4 + u)
                return carry
            lax.fori_loop(0, dil // 4, body, 0)


def _attn_prompt(zqkv, gi, span):
    n, t, cols = zqkv.shape
    d = DILATIONS[gi]
    aw = ATTN_HEADS * HEAD
    ng = len(DILATIONS)
    back = ATTN_BLOCK * d
    sub = span // back
    halves = aw // LANES
    col = lambda which, hp: (which * ng + gi) * halves + hp
    cur = lambda which: pl.BlockSpec((1, span, LANES), lambda b, i, hp: (b, i, col(which, hp)))
    prev = lambda which: pl.BlockSpec((1, back, LANES),
                                      lambda b, i, hp: (b, jnp.maximum(i * sub - 1, 0), col(which, hp)))
    out_spec = pl.BlockSpec((1, span, LANES), lambda b, i, hp: (b, i, hp))
    out_sds = jax.ShapeDtypeStruct((n, t, aw), F32)
    return pl.pallas_call(
        functools.partial(_attn_prompt_kernel, dil=d, sub=sub, n_back=WINDOWS[gi] // d, scale=HEAD ** -0.5),
        grid=(n, t // span, halves),
        in_specs=[cur(0), cur(1), prev(1), cur(2), prev(2)],
        out_specs=[out_spec, out_spec],
        out_shape=[out_sds, out_sds],
        compiler_params=_cparams(("arbitrary", "arbitrary", "arbitrary")),
        name=f"attn_prompt_g{gi}",
    )(zqkv, zqkv, zqkv, zqkv, zqkv)


def _attn_sample_kernel(q_ref, k_ref, v_ref, c_ref, o_ref, l_ref, *, dil, keep, cl, n_back, valid, scale):
    q = (q_ref[0] * scale).astype(BF16)
    tp = q.shape[0]
    cache = c_ref[0]
    rows = cache.shape[0]
    aw = ATTN_HEADS * HEAD
    keys = jnp.concatenate([cache[:, :aw], k_ref[0]], axis=0).astype(BF16)
    vals = jnp.concatenate([cache[:, aw:], v_ref[0]], axis=0).astype(BF16)
    ti = lax.broadcasted_iota(jnp.int32, (tp, rows + tp), 0)
    ci = lax.broadcasted_iota(jnp.int32, (tp, rows + tp), 1)
    pos = jnp.where(ci < rows, (ci // keep) * dil + ci % keep, cl + ci - rows)
    dist = cl + ti - pos
    mask = (dist >= 0) & (dist % dil == 0) & (dist <= dil * n_back) & (pos < cl + valid)
    outs, lses = [], []
    for h in range(ATTN_HEADS):
        sl = slice(h * HEAD, (h + 1) * HEAD)
        s = jnp.where(mask, _dot_nt(q[:, sl], keys[:, sl]), NEG)
        m = jnp.max(s, axis=-1, keepdims=True)
        e = jnp.exp(s - m)
        den = jnp.sum(e, axis=-1, keepdims=True)
        outs.append(_dot(e.astype(BF16), vals[:, sl]) / den)
        lses.append(jnp.broadcast_to(m + jnp.log(den), (tp, HEAD)))
    o_ref[0] = jnp.concatenate(outs, axis=1)
    l_ref[0] = jnp.concatenate(lses, axis=1)


def _attn_sample(zqkv, cache, gi, valid):
    n, tp, cols = zqkv.shape
    aw = ATTN_HEADS * HEAD
    cl = cache.shape[1]
    ng = len(DILATIONS)
    d = DILATIONS[gi]
    assert cl % d == 0
    keep = min(d, valid)
    rows = (cl // d) * keep
    cache = cache.reshape((n, cl // d, d) + cache.shape[2:])[:, :, :keep].reshape(n, rows, 2 * aw)
    spec = lambda which: pl.BlockSpec((1, tp, aw), lambda b: (b, 0, which * ng + gi))
    out_spec = pl.BlockSpec((1, tp, aw), lambda b: (b, 0, 0))
    out_sds = jax.ShapeDtypeStruct((n, tp, aw), F32)
    return pl.pallas_call(
        functools.partial(_attn_sample_kernel, dil=d, keep=keep, cl=cl, n_back=WINDOWS[gi] // d,
                          valid=valid, scale=HEAD ** -0.5),
        grid=(n,),
        in_specs=[spec(0), spec(1), spec(2), pl.BlockSpec((1, rows, 2 * aw), lambda b: (b, 0, 0))],
        out_specs=[out_spec, out_spec],
        out_shape=[out_sds, out_sds],
        compiler_params=_cparams(("arbitrary",)),
        name=f"attn_sample_g{gi}",
    )(zqkv, zqkv, zqkv, cache)


def _merge_kernel(x_ref, y_ref, bon_ref, g_ref, lw_ref, lb_ref, o0, o1, o2, l0, l1, l2, zg_ref,
                  wbr_ref, wba_ref, wout_ref, out_ref):
    ones = _pair_ones()
    y = y_ref[...]
    mu = _head_sum(y, ones) * (1.0 / HEAD)
    yc = y - mu
    var = _head_sum(yc * yc, ones) * (1.0 / HEAD)
    yn = yc * lax.rsqrt(var + LNX_EPS) * lw_ref[...] + lb_ref[...]
    rwkv = (yn + bon_ref[...]) * g_ref[...]

    ls = [l0[...], l1[...], l2[...]]
    m = jnp.maximum(jnp.maximum(ls[0], ls[1]), ls[2])
    es = [jnp.exp(l - m) for l in ls]
    attn = (es[0] * o0[...] + es[1] * o1[...] + es[2] * o2[...]) / (es[0] + es[1] + es[2])

    d = x_ref.shape[1]
    gates = _sigmoid(zg_ref[...])
    mixed = (gates[:, :d] * _dot(rwkv.astype(BF16), wbr_ref[...])
             + gates[:, d:] * _dot(attn.astype(BF16), wba_ref[...]))
    out_ref[...] = x_ref[...] + _dot(mixed.astype(BF16), wout_ref[...])


def _merge(x, y, bonus, g, lnx_w, lnx_b, os_, ls_, zg, wbr, wba, wout, tm):
    rows, d = x.shape
    rowblk = lambda a: pl.BlockSpec((tm, a.shape[1]), lambda i: (i, 0))
    full = lambda a: pl.BlockSpec(a.shape, lambda i: (0, 0))
    lw, lb = lnx_w.reshape(1, -1), lnx_b.reshape(1, -1)
    args = [x, y, bonus, g, lw, lb, *os_, *ls_, zg, wbr, wba, wout]
    specs = [rowblk(x), rowblk(y), rowblk(bonus), rowblk(g), full(lw), full(lb)] + \
            [rowblk(a) for a in (*os_, *ls_)] + [rowblk(zg), full(wbr), full(wba), full(wout)]
    return pl.pallas_call(
        _merge_kernel,
        grid=(rows // tm,),
        in_specs=specs,
        out_specs=pl.BlockSpec((tm, d), lambda i: (i, 0)),
        out_shape=jax.ShapeDtypeStruct((rows, d), F32),
        compiler_params=_cparams(("arbitrary",)),
        name="merge_out",
    )(*args)


def _mlp_ple_kernel(x_ref, g_ref, w1_ref, w2_ref, p_ref, gp_ref, wg_ref, wp_ref, gf_ref, o_ref,
                    h_scr, acc_scr, *, final):
    j = pl.program_id(1)

    @pl.when(j == 0)
    def _():
        h_scr[...] = _rms(x_ref[...], g_ref[...]).astype(BF16)
        acc_scr[...] = x_ref[...]

    u = jnp.maximum(_dot(h_scr[...], w1_ref[...]), 0.0)
    acc_scr[...] += _dot((u * u).astype(BF16), w2_ref[...])

    @pl.when(j == pl.num_programs(1) - 1)
    def _():
        x = acc_scr[...]
        h = _rms(x, gp_ref[...]).astype(BF16)
        x = x + _sigmoid(_dot(h, wg_ref[...])) * _dot(p_ref[...].astype(BF16), wp_ref[...])
        if final:
            x = _rms(x, gf_ref[...])
        o_ref[...] = x


def _mlp_ple(x, g, w1, w2, p, gp, wg, wp, gf, final, tm, tf):
    rows, d = x.shape
    dff = w1.shape[1]
    vec = lambda a: a.reshape(1, d)
    full = lambda a: pl.BlockSpec(a.shape, lambda i, j: (0, 0))
    return pl.pallas_call(
        functools.partial(_mlp_ple_kernel, final=final),
        grid=(rows // tm, dff // tf),
        in_specs=[pl.BlockSpec((tm, d), lambda i, j: (i, 0)),
                  full(vec(g)),
                  pl.BlockSpec((d, tf), lambda i, j: (0, j)),
                  pl.BlockSpec((tf, d), lambda i, j: (j, 0)),
                  pl.BlockSpec((tm, p.shape[1]), lambda i, j: (i, 0)),
                  full(vec(gp)), full(wg), full(wp), full(vec(gf))],
        out_specs=pl.BlockSpec((tm, d), lambda i, j: (i, 0)),
        out_shape=jax.ShapeDtypeStruct((rows, d), F32),
        scratch_shapes=[pltpu.VMEM((tm, d), BF16), pltpu.VMEM((tm, d), F32)],
        compiler_params=_cparams(("arbitrary", "arbitrary")),
        name="mlp_ple",
    )(x, vec(g), w1, w2, p, vec(gp), wg, wp, vec(gf))


def _layer(x, p, wkv0, shift0, caches, prm, norm_final, final, valid):
    n, t, d = x.shape
    rows = n * t
    width = prm["decay_w0"].shape[-1]
    rwkv_cols = prm["shift_mu"].shape[-1]
    aw = ATTN_HEADS * HEAD
    qkv_cols = 3 * len(DILATIONS) * aw
    tm = min(512, rows)
    x2 = x.reshape(rows, d)

    pick = lambda cols, opts: next(o for o in opts if cols % o == 0)
    z_rwkv, z_qkv, z_gate = _in_proj(x2, prm["norm_mix"], prm["w_in"].astype(BF16),
                                     (rwkv_cols, qkv_cols, 2 * d), min(256, rows))
    z_rwkv = z_rwkv.reshape(n, t, rwkv_cols)
    z_qkv = z_qkv.reshape(n, t, qkv_cols)

    r, w, k, v, a, b, g, bonus = _rwkv_prep(z_rwkv, shift0, prm, min(256, t))
    heads = width // HEAD
    if valid == t and t % WKV_CHUNK == 0:
        y, s_t = _wkv_chunked(r, w, k, v, a, b, _pack_state(wkv0), pick(t, (512, 256, WKV_CHUNK)))
    else:
        y, s_t = _wkv(r, w, k, v, a, b, _pack_state(wkv0), valid)
    wkv_t = _unpack_state(s_t, heads)
    shift_t = z_rwkv[:, valid - 1, :]

    os_, ls_, new_kv = [], [], []
    ng = len(DILATIONS)
    for gi in range(ng):
        if caches is None:
            o, l = _attn_prompt(z_qkv, gi, max(1024, ATTN_BLOCK * DILATIONS[gi]))
            keep = min(WINDOWS[gi], t)
            kv_rows = z_qkv[:, t - keep:t]
        else:
            o, l = _attn_sample(z_qkv, caches[gi], gi, valid)
            kv_rows = z_qkv[:, :valid]
        new_kv.append(jnp.stack([kv_rows[:, :, (which * ng + gi) * aw:(which * ng + gi + 1) * aw]
                                 .reshape(n, -1, ATTN_HEADS, HEAD) for which in (1, 2)], axis=2))
        os_.append(o.reshape(rows, aw))
        ls_.append(l.reshape(rows, aw))

    flat = lambda u: u.reshape(rows, -1)
    x2 = _merge(x2, flat(y), flat(bonus), flat(g), prm["lnx_w"], prm["lnx_b"], os_, ls_, z_gate,
                prm["w_branch_rwkv"].astype(BF16), prm["w_branch_attn"].astype(BF16),
                prm["w_out"].astype(BF16), min(256, rows))
    x2 = _mlp_ple(x2, prm["norm_mlp"], prm["w_ff1"].astype(BF16), prm["w_ff2"].astype(BF16),
                  p.reshape(rows, -1), prm["norm_ple"], prm["w_ple_gate"].astype(BF16),
                  prm["w_ple"].astype(BF16), norm_final, final, tm, 1024)
    return x2.reshape(n, t, d), new_kv, wkv_t, shift_t


_LAYER_PARAMS = ("norm_mix", "w_in", "shift_mu", "decay_w0", "decay_w2", "aaa_a0", "aaa_a2", "gate_g2",
                 "k_k", "k_a", "r_k", "lnx_w", "lnx_b", "w_branch_rwkv", "w_branch_attn", "w_out",
                 "norm_mlp", "w_ff1", "w_ff2", "norm_ple", "w_ple", "w_ple_gate")


def kernel(x_prompt, x_sample, cache_kv_g0, cache_kv_g1, cache_kv_g2, state_wkv, state_shift, p_prompt, p_sample, norm_mix, w_in, shift_mu, decay_w0, decay_w2, aaa_a0, aaa_a2, gate_g2, k_k, k_a, r_k, lnx_w, lnx_b, w_branch_rwkv, w_branch_attn, w_out, norm_mlp, w_ff1, w_ff2, norm_ple, w_ple, w_ple_gate, norm_final):
    stacked = dict(zip(_LAYER_PARAMS, (norm_mix, w_in, shift_mu, decay_w0, decay_w2, aaa_a0, aaa_a2,
                                       gate_g2, k_k, k_a, r_k, lnx_w, lnx_b, w_branch_rwkv,
                                       w_branch_attn, w_out, norm_mlp, w_ff1, w_ff2, norm_ple, w_ple,
                                       w_ple_gate)))
    depth = norm_mix.shape[0]
    nb_p, _, d = x_prompt.shape
    nb_s, t_s, _ = x_sample.shape
    heads = decay_w0.shape[-1] // HEAD
    rwkv_cols = shift_mu.shape[-1]
    t_pad = -(-t_s // SUBLANES) * SUBLANES
    pad_t = lambda u: jnp.pad(u, ((0, 0), (0, t_pad - t_s), (0, 0)))

    yp, ys = x_prompt, pad_t(x_sample)
    kvp, kvs = [[], [], []], [[], [], []]
    wkv_p, sh_p, wkv_s, sh_s = [], [], [], []
    for i in range(depth):
        prm = {name: val[i] for name, val in stacked.items()}
        prm["r_k"] = prm["r_k"].reshape(-1)
        last = i == depth - 1
        wkv0 = jnp.zeros((nb_p, heads, HEAD, HEAD), F32)
        shift0 = jnp.zeros((nb_p, rwkv_cols), F32)
        yp, nkv_p, w_p, s_p = _layer(yp, p_prompt[i], wkv0, shift0, None, prm, norm_final, last,
                                     x_prompt.shape[1])
        ys, nkv_s, w_s, s_s = _layer(ys, pad_t(p_sample[i]), state_wkv[i], state_shift[i],
                                     (cache_kv_g0[i], cache_kv_g1[i], cache_kv_g2[i]), prm, norm_final,
                                     last, t_s)
        for gi in range(3):
            kvp[gi].append(nkv_p[gi])
            kvs[gi].append(nkv_s[gi])
        wkv_p.append(w_p)
        sh_p.append(s_p)
        wkv_s.append(w_s)
        sh_s.append(s_s)
    return (yp, ys[:, :t_s],
            jnp.stack(kvp[0]), jnp.stack(kvp[1]), jnp.stack(kvp[2]),
            jnp.stack(wkv_p), jnp.stack(sh_p),
            jnp.stack(kvs[0]), jnp.stack(kvs[1]), jnp.stack(kvs[2]),
            jnp.stack(wkv_s), jnp.stack(sh_s))
```

```python
import functools

import jax
import jax.numpy as jnp
from jax import lax
from jax.experimental import pallas as pl
from jax.experimental.pallas import tpu as pltpu

F32 = jnp.float32
BF16 = jnp.bfloat16

HEAD = 64
LANES = 128
SUBLANES = 8
ATTN_HEADS = 4
ATTN_BLOCK = 128
WINDOWS = (128, 512, 2048)
DILATIONS = (1, 4, 16)
NORM_EPS = 1e-6
LNX_EPS = 64e-5
NEG = -1e30
VMEM_LIMIT = 56 * 1024 * 1024


def _cparams(sem):
    return pltpu.CompilerParams(dimension_semantics=sem, vmem_limit_bytes=VMEM_LIMIT)


def _dot(a, b):
    return jnp.dot(a, b, preferred_element_type=F32)


def _split(x):
    hi = x.astype(BF16)
    lo = (x - hi.astype(F32)).astype(BF16)
    return hi, lo


def _dot3(a, b):
    a_hi, a_lo = _split(a)
    b_hi, b_lo = _split(b)
    return _dot(a_hi, b_hi) + (_dot(a_lo, b_hi) + _dot(a_hi, b_lo))


def _pair_ones():
    r = lax.broadcasted_iota(jnp.int32, (LANES, LANES), 0) // HEAD
    c = lax.broadcasted_iota(jnp.int32, (LANES, LANES), 1) // HEAD
    return (r == c).astype(BF16)


def _head_sum(x, ones):
    outs = []
    for c in range(x.shape[1] // LANES):
        hi, lo = _split(x[:, c * LANES:(c + 1) * LANES])
        outs.append(_dot(hi, ones) + _dot(lo, ones))
    return jnp.concatenate(outs, axis=1)


def _rms(x, g):
    return x * lax.rsqrt(jnp.mean(x * x, axis=-1, keepdims=True) + NORM_EPS) * g


def _sigmoid(x):
    return 1.0 / (1.0 + jnp.exp(-x))


def _softplus(x):
    return jnp.maximum(x, 0.0) + jnp.log(1.0 + jnp.exp(-jnp.abs(x)))


def _in_proj_kernel(x_ref, g_ref, w_ref, *o_refs):
    h = _rms(x_ref[...], g_ref[...]).astype(BF16)
    c0 = 0
    for o_ref in o_refs:
        c1 = c0 + o_ref.shape[1]
        o_ref[...] = _dot(h, w_ref[:, c0:c1])
        c0 = c1


def _in_proj(x, g, w, splits, tm):
    rows, d = x.shape
    assert sum(splits) == w.shape[1]
    return pl.pallas_call(
        _in_proj_kernel,
        grid=(rows // tm,),
        in_specs=[pl.BlockSpec((tm, d), lambda i: (i, 0)),
                  pl.BlockSpec((1, d), lambda i: (0, 0)),
                  pl.BlockSpec(w.shape, lambda i: (0, 0), pipeline_mode=pl.Buffered(1))],
        out_specs=[pl.BlockSpec((tm, c), lambda i: (i, 0)) for c in splits],
        out_shape=[jax.ShapeDtypeStruct((rows, c), F32) for c in splits],
        compiler_params=_cparams(("arbitrary",)),
        name="in_proj",
    )(x, g.reshape(1, d), w)


def _rwkv_prep_kernel(z_ref, sh_ref, mu_ref, w0_ref, w2_ref, a0_ref, a2_ref, g2_ref, kk_ref, ka_ref,
                      rk_ref, r_o, w_o, k_o, v_o, a_o, b_o, g_o, bon_o, last_scr, *, width):
    n = pl.program_id(0)
    i = pl.program_id(1)
    z = z_ref[0]
    tt = z.shape[0]

    @pl.when(i == 0)
    def _():
        last_scr[...] = sh_ref[pl.ds(n, 1), :]

    row = lax.broadcasted_iota(jnp.int32, z.shape, 0)
    prev = jnp.where(row == 0, last_scr[...], pltpu.roll(z, 1, axis=0))
    last_scr[...] = z[tt - 1:tt, :]
    zs = z + (prev - z) * mu_ref[...]

    c0 = 3 * width
    c1 = c0 + w2_ref.shape[0]
    c2 = c1 + a2_ref.shape[0]
    r = zs[:, 0:width]
    k = zs[:, width:2 * width]
    v = zs[:, 2 * width:c0]
    xw = zs[:, c0:c1]
    xa = zs[:, c1:c2]
    xg = zs[:, c2:]

    logw = -_softplus(-(w0_ref[...] + _dot3(jnp.tanh(xw), w2_ref[...]))) - 0.5
    a = _sigmoid(a0_ref[...] + _dot3(xa, a2_ref[...]))
    g = _dot3(_sigmoid(xg), g2_ref[...])

    ones = _pair_ones()
    kk = k * kk_ref[...]
    kk = kk / jnp.maximum(jnp.sqrt(_head_sum(kk * kk, ones)), 1e-12)
    k2 = k * (1.0 + (a - 1.0) * ka_ref[...])
    bonus = _head_sum(r * k2 * rk_ref[...], ones) * v

    r_o[0] = r
    w_o[0] = -jnp.exp(logw)
    k_o[0] = k2
    v_o[0] = v
    a_o[0] = -kk
    b_o[0] = kk * a
    g_o[0] = g
    bon_o[0] = bonus


def _rwkv_prep(z_rwkv, shift0, prm, tt):
    n, t, cols = z_rwkv.shape
    width = prm["decay_w0"].shape[-1]
    row = lambda x: x.reshape(1, -1)
    full = lambda a: pl.BlockSpec(a.shape, lambda b, i: (0,) * a.ndim)
    args = [shift0, row(prm["shift_mu"]), row(prm["decay_w0"]), prm["decay_w2"], row(prm["aaa_a0"]),
            prm["aaa_a2"], prm["gate_g2"], row(prm["k_k"]), row(prm["k_a"]), row(prm["r_k"])]
    out_spec = pl.BlockSpec((1, tt, width), lambda b, i: (b, i, 0))
    out_sds = jax.ShapeDtypeStruct((n, t, width), F32)
    return pl.pallas_call(
        functools.partial(_rwkv_prep_kernel, width=width),
        grid=(n, t // tt),
        in_specs=[pl.BlockSpec((1, tt, cols), lambda b, i: (b, i, 0))] + [full(a) for a in args],
        out_specs=[out_spec] * 8,
        out_shape=[out_sds] * 8,
        scratch_shapes=[pltpu.VMEM((1, cols), F32)],
        compiler_params=_cparams(("arbitrary", "arbitrary")),
        name="rwkv_prep",
    )(z_rwkv, *args)


def _wkv_kernel(r_ref, w_ref, k_ref, v_ref, a_ref, b_ref, s0_ref, y_ref, st_ref, *, valid):
    left = lax.broadcasted_iota(jnp.int32, (HEAD, LANES), 1) < HEAD

    def col_tile(xt, j):
        return jnp.where(left, xt[0:HEAD, j:j + 1], xt[HEAD:2 * HEAD, j:j + 1])

    for p in range(s0_ref.shape[1]):
        cols = slice(p * LANES, (p + 1) * LANES)
        rt = r_ref[0, :, cols].T
        wt = jnp.exp(w_ref[0, :, cols]).T
        kt = k_ref[0, :, cols].T
        at = a_ref[0, :, cols].T
        bt = b_ref[0, :, cols].T
        vg = v_ref[0, :, cols]
        s = s0_ref[0, p]
        ys = []
        for j in range(y_ref.shape[1]):
            if j < valid:
                sa = jnp.sum(s * col_tile(at, j), axis=0, keepdims=True)
                s = s * col_tile(wt, j) + col_tile(bt, j) * sa + col_tile(kt, j) * vg[j:j + 1, :]
                ys.append(jnp.sum(s * col_tile(rt, j), axis=0, keepdims=True))
            else:
                ys.append(jnp.zeros((1, LANES), F32))
        y_ref[0, :, cols] = jnp.concatenate(ys, axis=0)
        st_ref[0, p] = s


def _wkv(r, w, k, v, a, b, s0, valid):
    n, t, c = r.shape
    assert t == SUBLANES
    tok = pl.BlockSpec((1, t, c), lambda bi: (bi, 0, 0))
    st = pl.BlockSpec((1,) + s0.shape[1:], lambda bi: (bi, 0, 0, 0))
    return pl.pallas_call(
        functools.partial(_wkv_kernel, valid=valid),
        grid=(n,),
        in_specs=[tok] * 6 + [st],
        out_specs=[tok, st],
        out_shape=[jax.ShapeDtypeStruct((n, t, c), F32), jax.ShapeDtypeStruct(s0.shape, F32)],
        compiler_params=_cparams(("arbitrary",)),
        name="wkv_scan",
    )(r, w, k, v, a, b, s0)


WKV_CHUNK = 64


def _bd(x):
    xb = x.astype(BF16)
    left = lax.broadcasted_iota(jnp.int32, x.shape, 1) < HEAD
    zero = jnp.zeros_like(xb)
    return jnp.concatenate([jnp.where(left, xb, zero), jnp.where(left, zero, xb)], axis=0)


def _dot_nt(a, b):
    return lax.dot_general(a, b, (((1,), (1,)), ((), ())), preferred_element_type=F32)


def _dot_tn(a, b):
    return lax.dot_general(a, b, (((0,), (0,)), ((), ())), preferred_element_type=F32)


def _wkv_chunk_local(r, lw, k, v, a, b, consts):
    tri, eye2, strict, incl, blocks = consts
    c = WKV_CHUNK
    n = range(len(r))
    zero = jnp.zeros((c, LANES), F32)

    def cumsum(x):
        hi = x.astype(BF16)
        r1 = x - hi.astype(F32)
        mid = r1.astype(BF16)
        lo = (r1 - mid.astype(F32)).astype(BF16)
        return _dot(tri, hi) + (_dot(tri, mid) + _dot(tri, lo))

    lc = [cumsum(lw[i]) for i in n]
    ltot = [lc[i][c - 1:c, :] for i in n]
    w_inv = [jnp.exp(-lc[i]) for i in n]
    ah = [a[i] * jnp.exp(lc[i] - lw[i]) for i in n]
    rh = [r[i] * jnp.exp(lc[i]) for i in n]
    sc = [_dot_nt(jnp.concatenate([ah[i], rh[i]], axis=0).astype(BF16),
                  jnp.concatenate([_bd(b[i] * w_inv[i]), _bd(k[i] * w_inv[i])], axis=0)) for i in n]
    aab = [jnp.where(strict, sc[i][:c, :LANES], zero) for i in n]

    t = [eye2 + jnp.where(blocks[0], aab[i], zero) for i in n]
    for lvl in range(1, len(blocks)):
        grow = blocks[lvl] & ~blocks[lvl - 1]
        x = [_dot(t[i].astype(BF16), _bd(jnp.where(grow, aab[i], zero))) for i in n]
        t = [t[i] + _dot(x[i].astype(BF16), _bd(t[i])) for i in n]

    vb = [_bd(v[i]) for i in n]
    akv = [_dot(jnp.where(strict, sc[i][:c, LANES:], zero).astype(BF16), vb[i]) for i in n]
    pu = [_dot(t[i].astype(BF16), jnp.concatenate([_bd(ah[i]), _bd(akv[i])], axis=1)) for i in n]

    left2 = (lax.broadcasted_iota(jnp.int32, (HEAD, 2 * LANES), 1) % LANES) < HEAD
    g, h, q, y0 = [], [], [], []
    for i in n:
        w_rem = jnp.exp(ltot[i] - lc[i])
        lhs_t = jnp.concatenate([b[i] * w_rem, k[i] * w_rem], axis=0).astype(BF16)
        rhs = jnp.concatenate([pu[i], jnp.concatenate([zero, v[i]], axis=1)], axis=0).astype(BF16)
        full = _dot_tn(lhs_t, rhs)
        gh = jnp.where(left2, full[:HEAD], full[HEAD:])
        g.append(gh[:, :LANES] + eye2 * jnp.exp(ltot[i]))
        h.append(gh[:, LANES:])
    zero_b = jnp.zeros((2 * c, LANES), BF16)
    for i in n:
        mrb = jnp.where(incl, sc[i][c:, :LANES], zero)
        mrk = jnp.where(incl, sc[i][c:, LANES:], zero)
        rhs2 = jnp.concatenate([jnp.concatenate([_bd(pu[i][:, :LANES]), _bd(pu[i][:, LANES:])], axis=1),
                                jnp.concatenate([zero_b, vb[i]], axis=1)], axis=0)
        qy = _dot(jnp.concatenate([mrb, mrk], axis=1).astype(BF16), rhs2)
        q.append(rh[i] + qy[:, :LANES])
        y0.append(qy[:, LANES:])
    return g, h, q, y0


def _wkv_chunked_kernel(r_ref, w_ref, k_ref, v_ref, a_ref, b_ref, s0_ref, y_ref, st_ref, s_scr):
    i = pl.program_id(1)
    c = WKV_CHUNK

    @pl.when(i == 0)
    def _():
        s_scr[...] = s0_ref[:, 0]

    ti = lax.broadcasted_iota(jnp.int32, (c, LANES), 0)
    si = lax.broadcasted_iota(jnp.int32, (c, LANES), 1) % HEAD
    tri = (lax.broadcasted_iota(jnp.int32, (c, c), 0) >= lax.broadcasted_iota(jnp.int32, (c, c), 1)
           ).astype(BF16)
    eye2 = (ti == si).astype(F32)
    sizes = [2 << l for l in range(c.bit_length() - 1)]
    blocks = [(ti // s) == (si // s) for s in sizes]
    consts = (tri, eye2, ti > si, ti >= si, blocks)

    n_seq = y_ref.shape[0]
    n_chunks = y_ref.shape[1] // c
    chunks = lambda ref: [ref[b, ci * c:(ci + 1) * c, :] for ci in range(n_chunks) for b in range(n_seq)]
    g, h, q, y0 = _wkv_chunk_local(chunks(r_ref), chunks(w_ref), chunks(k_ref), chunks(v_ref),
                                   chunks(a_ref), chunks(b_ref), consts)
    st = [s_scr[b] for b in range(n_seq)]
    for ci in range(n_chunks):
        for b in range(n_seq):
            u = ci * n_seq + b
            out = _dot(jnp.concatenate([q[u], g[u]], axis=0).astype(BF16), _bd(st[b]))
            y_ref[b, ci * c:(ci + 1) * c, :] = out[:c] + y0[u]
            st[b] = out[c:] + h[u]
    for b in range(n_seq):
        s_scr[b] = st[b]

    @pl.when(i == pl.num_programs(1) - 1)
    def _():
        for b in range(n_seq):
            st_ref[b, 0] = st[b]


def _wkv_chunked(r, lw, k, v, a, b, s0, tb):
    n, t, c = r.shape
    pairs = c // LANES
    tok = pl.BlockSpec((n, tb, LANES), lambda p, i: (0, i, p))
    st = pl.BlockSpec((n, 1, HEAD, LANES), lambda p, i: (0, p, 0, 0))
    return pl.pallas_call(
        _wkv_chunked_kernel,
        grid=(pairs, t // tb),
        in_specs=[tok] * 6 + [st],
        out_specs=[tok, st],
        out_shape=[jax.ShapeDtypeStruct((n, t, c), F32), jax.ShapeDtypeStruct(s0.shape, F32)],
        scratch_shapes=[pltpu.VMEM((n, HEAD, LANES), F32)],
        compiler_params=_cparams(("arbitrary", "arbitrary")),
        name="wkv_chunked",
    )(r, lw, k, v, a, b, s0)


def _pack_state(s):
    n, h, vd, kd = s.shape
    return s.reshape(n, h // 2, 2, vd, kd).transpose(0, 1, 4, 2, 3).reshape(n, h // 2, kd, 2 * vd)


def _unpack_state(s, h):
    n, _, kd, _ = s.shape
    return s.reshape(n, h // 2, kd, 2, HEAD).transpose(0, 1, 3, 4, 2).reshape(n, h, HEAD, kd)


def _attn_prompt_kernel(q_ref, k_ref, kp_ref, v_ref, vp_ref, o_ref, l_ref, *, dil, sub, n_back, scale):
    first = pl.program_id(1) == 0
    qi = lax.broadcasted_iota(jnp.int32, (ATTN_BLOCK, 2 * ATTN_BLOCK), 0)
    ki = lax.broadcasted_iota(jnp.int32, (ATTN_BLOCK, 2 * ATTN_BLOCK), 1)
    dist = qi + ATTN_BLOCK - ki
    band = (dist >= 0) & (dist <= n_back)
    own = ki >= ATTN_BLOCK

    def rows(j, r):
        if dil == 1:
            return pl.ds(j * ATTN_BLOCK, ATTN_BLOCK)
        return pl.ds(j * ATTN_BLOCK * dil + r, ATTN_BLOCK, stride=dil)

    def unit(j, r):
        q = (q_ref[0, rows(j, r), :] * scale).astype(BF16)
        if j == 0:
            kp, vp = kp_ref[0, rows(0, r), :], vp_ref[0, rows(0, r), :]
            mask = band & (own | jnp.logical_not(first))
        else:
            kp, vp, mask = k_ref[0, rows(j - 1, r), :], v_ref[0, rows(j - 1, r), :], band
        kcat = jnp.concatenate([kp, k_ref[0, rows(j, r), :]], axis=0).astype(BF16)
        vcat = jnp.concatenate([vp, v_ref[0, rows(j, r), :]], axis=0).astype(BF16)
        outs, lses = [], []
        for h in range(LANES // HEAD):
            sl = slice(h * HEAD, (h + 1) * HEAD)
            s = jnp.where(mask, _dot_nt(q[:, sl], kcat[:, sl]), NEG)
            m = jnp.max(s, axis=-1, keepdims=True)
            e = jnp.exp(s - m)
            den = jnp.sum(e, axis=-1, keepdims=True)
            outs.append(_dot(e.astype(BF16), vcat[:, sl]) / den)
            lses.append(jnp.broadcast_to(m + jnp.log(den), (ATTN_BLOCK, HEAD)))
        o_ref[0, rows(j, r), :] = jnp.concatenate(outs, axis=1)
        l_ref[0, rows(j, r), :] = jnp.concatenate(lses, axis=1)

    for j in range(sub):
        if dil <= 4:
            for r in range(dil):
                unit(j, r)
        else:
            def body(r4, carry, j=j):
                for u in range(4):
                    unit(j, r4 * 4 + u)
                return carry
            lax.fori_loop(0, dil // 4, body, 0)


def _attn_prompt(zqkv, gi, span):
    n, t, cols = zqkv.shape
    d = DILATIONS[gi]
    aw = ATTN_HEADS * HEAD
    ng = len(DILATIONS)
    back = ATTN_BLOCK * d
    sub = span // back
    halves = aw // LANES
    col = lambda which, hp: (which * ng + gi) * halves + hp
    cur = lambda which: pl.BlockSpec((1, span, LANES), lambda b, i, hp: (b, i, col(which, hp)))
    prev = lambda which: pl.BlockSpec((1, back, LANES),
                                      lambda b, i, hp: (b, jnp.maximum(i * sub - 1, 0), col(which, hp)))
    out_spec = pl.BlockSpec((1, span, LANES), lambda b, i, hp: (b, i, hp))
    out_sds = jax.ShapeDtypeStruct((n, t, aw), F32)
    return pl.pallas_call(
        functools.partial(_attn_prompt_kernel, dil=d, sub=sub, n_back=WINDOWS[gi] // d, scale=HEAD ** -0.5),
        grid=(n, t // span, halves),
        in_specs=[cur(0), cur(1), prev(1), cur(2), prev(2)],
        out_specs=[out_spec, out_spec],
        out_shape=[out_sds, out_sds],
        compiler_params=_cparams(("arbitrary", "arbitrary", "arbitrary")),
        name=f"attn_prompt_g{gi}",
    )(zqkv, zqkv, zqkv, zqkv, zqkv)


def _attn_sample_kernel(q_ref, k_ref, v_ref, c_ref, o_ref, l_ref, *, dil, cl, n_back, valid, scale):
    q = (q_ref[0] * scale).astype(BF16)
    tp = q.shape[0]
    nblk, keep = c_ref.shape[1], c_ref.shape[2]
    rows = nblk * keep
    ti = lax.broadcasted_iota(jnp.int32, (tp, rows + tp), 0)
    ci = lax.broadcasted_iota(jnp.int32, (tp, rows + tp), 1)
    pos = jnp.where(ci < rows, (ci % nblk) * dil + ci // nblk, cl + ci - rows)
    dist = cl + ti - pos
    mask = (dist >= 0) & (dist % dil == 0) & (dist <= dil * n_back) & (pos < cl + valid)
    outs, lses = [], []
    for h in range(ATTN_HEADS):
        sl = slice(h * HEAD, (h + 1) * HEAD)
        keys = jnp.concatenate([c_ref[0, :, rho, 0, h, :] for rho in range(keep)] + [k_ref[0, :, sl]],
                               axis=0).astype(BF16)
        vals = jnp.concatenate([c_ref[0, :, rho, 1, h, :] for rho in range(keep)] + [v_ref[0, :, sl]],
                               axis=0).astype(BF16)
        s = jnp.where(mask, _dot_nt(q[:, sl], keys), NEG)
        m = jnp.max(s, axis=-1, keepdims=True)
        e = jnp.exp(s - m)
        den = jnp.sum(e, axis=-1, keepdims=True)
        outs.append(_dot(e.astype(BF16), vals) / den)
        lses.append(jnp.broadcast_to(m + jnp.log(den), (tp, HEAD)))
    o_ref[0] = jnp.concatenate(outs, axis=1)
    l_ref[0] = jnp.concatenate(lses, axis=1)


def _attn_sample(zqkv, cache, gi, valid):
    n, tp, cols = zqkv.shape
    aw = ATTN_HEADS * HEAD
    cl = cache.shape[1]
    ng = len(DILATIONS)
    d = DILATIONS[gi]
    assert cl % d == 0
    keep = min(d, valid)
    cache = cache.reshape((n, cl // d, d) + cache.shape[2:])
    spec = lambda which: pl.BlockSpec((1, tp, aw), lambda b: (b, 0, which * ng + gi))
    out_spec = pl.BlockSpec((1, tp, aw), lambda b: (b, 0, 0))
    out_sds = jax.ShapeDtypeStruct((n, tp, aw), F32)
    return pl.pallas_call(
        functools.partial(_attn_sample_kernel, dil=d, cl=cl, n_back=WINDOWS[gi] // d, valid=valid,
                          scale=HEAD ** -0.5),
        grid=(n,),
        in_specs=[spec(0), spec(1), spec(2),
                  pl.BlockSpec((1, cl // d, keep) + cache.shape[3:], lambda b: (b, 0, 0, 0, 0, 0))],
        out_specs=[out_spec, out_spec],
        out_shape=[out_sds, out_sds],
        compiler_params=_cparams(("arbitrary",)),
        name=f"attn_sample_g{gi}",
    )(zqkv, zqkv, zqkv, cache)


def _merge_kernel(x_ref, y_ref, bon_ref, g_ref, lw_ref, lb_ref, o0, o1, o2, l0, l1, l2, zg_ref,
                  wbr_ref, wba_ref, wout_ref, out_ref):
    ones = _pair_ones()
    y = y_ref[...]
    mu = _head_sum(y, ones) * (1.0 / HEAD)
    yc = y - mu
    var = _head_sum(yc * yc, ones) * (1.0 / HEAD)
    yn = yc * lax.rsqrt(var + LNX_EPS) * lw_ref[...] + lb_ref[...]
    rwkv = (yn + bon_ref[...]) * g_ref[...]

    ls = [l0[...], l1[...], l2[...]]
    m = jnp.maximum(jnp.maximum(ls[0], ls[1]), ls[2])
    es = [jnp.exp(l - m) for l in ls]
    attn = (es[0] * o0[...] + es[1] * o1[...] + es[2] * o2[...]) / (es[0] + es[1] + es[2])

    d = x_ref.shape[1]
    gates = _sigmoid(zg_ref[...])
    mixed = (gates[:, :d] * _dot(rwkv.astype(BF16), wbr_ref[...])
             + gates[:, d:] * _dot(attn.astype(BF16), wba_ref[...]))
    out_ref[...] = x_ref[...] + _dot(mixed.astype(BF16), wout_ref[...])


def _merge(x, y, bonus, g, lnx_w, lnx_b, os_, ls_, zg, wbr, wba, wout, tm):
    rows, d = x.shape
    rowblk = lambda a: pl.BlockSpec((tm, a.shape[1]), lambda i: (i, 0))
    full = lambda a: pl.BlockSpec(a.shape, lambda i: (0, 0))
    lw, lb = lnx_w.reshape(1, -1), lnx_b.reshape(1, -1)
    args = [x, y, bonus, g, lw, lb, *os_, *ls_, zg, wbr, wba, wout]
    specs = [rowblk(x), rowblk(y), rowblk(bonus), rowblk(g), full(lw), full(lb)] + \
            [rowblk(a) for a in (*os_, *ls_)] + [rowblk(zg), full(wbr), full(wba), full(wout)]
    return pl.pallas_call(
        _merge_kernel,
        grid=(rows // tm,),
        in_specs=specs,
        out_specs=pl.BlockSpec((tm, d), lambda i: (i, 0)),
        out_shape=jax.ShapeDtypeStruct((rows, d), F32),
        compiler_params=_cparams(("arbitrary",)),
        name="merge_out",
    )(*args)


def _mlp_ple_kernel(x_ref, g_ref, w1_ref, w2_ref, p_ref, gp_ref, wg_ref, wp_ref, gf_ref, o_ref,
                    h_scr, acc_scr, *, final):
    j = pl.program_id(1)

    @pl.when(j == 0)
    def _():
        h_scr[...] = _rms(x_ref[...], g_ref[...]).astype(BF16)
        acc_scr[...] = x_ref[...]

    u = jnp.maximum(_dot(h_scr[...], w1_ref[...]), 0.0)
    acc_scr[...] += _dot((u * u).astype(BF16), w2_ref[...])

    @pl.when(j == pl.num_programs(1) - 1)
    def _():
        x = acc_scr[...]
        h = _rms(x, gp_ref[...]).astype(BF16)
        x = x + _sigmoid(_dot(h, wg_ref[...])) * _dot(p_ref[...].astype(BF16), wp_ref[...])
        if final:
            x = _rms(x, gf_ref[...])
        o_ref[...] = x


def _mlp_ple(x, g, w1, w2, p, gp, wg, wp, gf, final, tm, tf):
    rows, d = x.shape
    dff = w1.shape[1]
    vec = lambda a: a.reshape(1, d)
    full = lambda a: pl.BlockSpec(a.shape, lambda i, j: (0, 0))
    return pl.pallas_call(
        functools.partial(_mlp_ple_kernel, final=final),
        grid=(rows // tm, dff // tf),
        in_specs=[pl.BlockSpec((tm, d), lambda i, j: (i, 0)),
                  full(vec(g)),
                  pl.BlockSpec((d, tf), lambda i, j: (0, j)),
                  pl.BlockSpec((tf, d), lambda i, j: (j, 0)),
                  pl.BlockSpec((tm, p.shape[1]), lambda i, j: (i, 0)),
                  full(vec(gp)), full(wg), full(wp), full(vec(gf))],
        out_specs=pl.BlockSpec((tm, d), lambda i, j: (i, 0)),
        out_shape=jax.ShapeDtypeStruct((rows, d), F32),
        scratch_shapes=[pltpu.VMEM((tm, d), BF16), pltpu.VMEM((tm, d), F32)],
        compiler_params=_cparams(("arbitrary", "arbitrary")),
        name="mlp_ple",
    )(x, vec(g), w1, w2, p, vec(gp), wg, wp, vec(gf))


def _layer(x, p, wkv0, shift0, caches, prm, norm_final, final, valid):
    n, t, d = x.shape
    rows = n * t
    width = prm["decay_w0"].shape[-1]
    rwkv_cols = prm["shift_mu"].shape[-1]
    aw = ATTN_HEADS * HEAD
    qkv_cols = 3 * len(DILATIONS) * aw
    tm = min(512, rows)
    x2 = x.reshape(rows, d)

    pick = lambda cols, opts: next(o for o in opts if cols % o == 0)
    z_rwkv, z_qkv, z_gate = _in_proj(x2, prm["norm_mix"], prm["w_in"].astype(BF16),
                                     (rwkv_cols, qkv_cols, 2 * d), min(256, rows))
    z_rwkv = z_rwkv.reshape(n, t, rwkv_cols)
    z_qkv = z_qkv.reshape(n, t, qkv_cols)

    r, w, k, v, a, b, g, bonus = _rwkv_prep(z_rwkv, shift0, prm, min(256, t))
    heads = width // HEAD
    if valid == t and t % WKV_CHUNK == 0:
        y, s_t = _wkv_chunked(r, w, k, v, a, b, _pack_state(wkv0), pick(t, (512, 256, WKV_CHUNK)))
    else:
        y, s_t = _wkv(r, w, k, v, a, b, _pack_state(wkv0), valid)
    wkv_t = _unpack_state(s_t, heads)
    shift_t = z_rwkv[:, valid - 1, :]

    os_, ls_, new_kv = [], [], []
    ng = len(DILATIONS)
    for gi in range(ng):
        if caches is None:
            o, l = _attn_prompt(z_qkv, gi, max(1024, ATTN_BLOCK * DILATIONS[gi]))
            keep = min(WINDOWS[gi], t)
            kv_rows = z_qkv[:, t - keep:t]
        else:
            o, l = _attn_sample(z_qkv, caches[gi], gi, valid)
            kv_rows = z_qkv[:, :valid]
        new_kv.append(jnp.stack([kv_rows[:, :, (which * ng + gi) * aw:(which * ng + gi + 1) * aw]
                                 .reshape(n, -1, ATTN_HEADS, HEAD) for which in (1, 2)], axis=2))
        os_.append(o.reshape(rows, aw))
        ls_.append(l.reshape(rows, aw))

    flat = lambda u: u.reshape(rows, -1)
    x2 = _merge(x2, flat(y), flat(bonus), flat(g), prm["lnx_w"], prm["lnx_b"], os_, ls_, z_gate,
                prm["w_branch_rwkv"].astype(BF16), prm["w_branch_attn"].astype(BF16),
                prm["w_out"].astype(BF16), min(256, rows))
    x2 = _mlp_ple(x2, prm["norm_mlp"], prm["w_ff1"].astype(BF16), prm["w_ff2"].astype(BF16),
                  p.reshape(rows, -1), prm["norm_ple"], prm["w_ple_gate"].astype(BF16),
                  prm["w_ple"].astype(BF16), norm_final, final, tm, 1024)
    return x2.reshape(n, t, d), new_kv, wkv_t, shift_t


_LAYER_PARAMS = ("norm_mix", "w_in", "shift_mu", "decay_w0", "decay_w2", "aaa_a0", "aaa_a2", "gate_g2",
                 "k_k", "k_a", "r_k", "lnx_w", "lnx_b", "w_branch_rwkv", "w_branch_attn", "w_out",
                 "norm_mlp", "w_ff1", "w_ff2", "norm_ple", "w_ple", "w_ple_gate")


def kernel(x_prompt, x_sample, cache_kv_g0, cache_kv_g1, cache_kv_g2, state_wkv, state_shift, p_prompt, p_sample, norm_mix, w_in, shift_mu, decay_w0, decay_w2, aaa_a0, aaa_a2, gate_g2, k_k, k_a, r_k, lnx_w, lnx_b, w_branch_rwkv, w_branch_attn, w_out, norm_mlp, w_ff1, w_ff2, norm_ple, w_ple, w_ple_gate, norm_final):
    stacked = dict(zip(_LAYER_PARAMS, (norm_mix, w_in, shift_mu, decay_w0, decay_w2, aaa_a0, aaa_a2,
                                       gate_g2, k_k, k_a, r_k, lnx_w, lnx_b, w_branch_rwkv,
                                       w_branch_attn, w_out, norm_mlp, w_ff1, w_ff2, norm_ple, w_ple,
                                       w_ple_gate)))
    depth = norm_mix.shape[0]
    nb_p, _, d = x_prompt.shape
    nb_s, t_s, _ = x_sample.shape
    heads = decay_w0.shape[-1] // HEAD
    rwkv_cols = shift_mu.shape[-1]
    t_pad = -(-t_s // SUBLANES) * SUBLANES
    pad_t = lambda u: jnp.pad(u, ((0, 0), (0, t_pad - t_s), (0, 0)))

    yp, ys = x_prompt, pad_t(x_sample)
    kvp, kvs = [[], [], []], [[], [], []]
    wkv_p, sh_p, wkv_s, sh_s = [], [], [], []
    for i in range(depth):
        prm = {name: val[i] for name, val in stacked.items()}
        prm["r_k"] = prm["r_k"].reshape(-1)
        last = i == depth - 1
        wkv0 = jnp.zeros((nb_p, heads, HEAD, HEAD), F32)
        shift0 = jnp.zeros((nb_p, rwkv_cols), F32)
        yp, nkv_p, w_p, s_p = _layer(yp, p_prompt[i], wkv0, shift0, None, prm, norm_final, last,
                                     x_prompt.shape[1])
        ys, nkv_s, w_s, s_s = _layer(ys, pad_t(p_sample[i]), state_wkv[i], state_shift[i],
                                     (cache_kv_g0[i], cache_kv_g1[i], cache_kv_g2[i]), prm, norm_final,
                                     last, t_s)
        for gi in range(3):
            kvp[gi].append(nkv_p[gi])
            kvs[gi].append(nkv_s[gi])
        wkv_p.append(w_p)
        sh_p.append(s_p)
        wkv_s.append(w_s)
        sh_s.append(s_s)
    return (yp, ys[:, :t_s],
            jnp.stack(kvp[0]), jnp.stack(kvp[1]), jnp.stack(kvp[2]),
            jnp.stack(wkv_p), jnp.stack(sh_p),
            jnp.stack(kvs[0]), jnp.stack(kvs[1]), jnp.stack(kvs[2]),
            jnp.stack(wkv_s), jnp.stack(sh_s))
```

```python
import functools

import jax
import jax.numpy as jnp
from jax import lax
from jax.experimental import pallas as pl
from jax.experimental.pallas import tpu as pltpu

F32 = jnp.float32
BF16 = jnp.bfloat16

HEAD = 64
LANES = 128
SUBLANES = 8
ATTN_HEADS = 4
ATTN_BLOCK = 128
WINDOWS = (128, 512, 2048)
DILATIONS = (1, 4, 16)
NORM_EPS = 1e-6
LNX_EPS = 64e-5
NEG = -1e30
VMEM_LIMIT = 56 * 1024 * 1024


def _cparams(sem):
    return pltpu.CompilerParams(dimension_semantics=sem, vmem_limit_bytes=VMEM_LIMIT)


def _dot(a, b):
    return jnp.dot(a, b, preferred_element_type=F32)


def _split(x):
    hi = x.astype(BF16)
    lo = (x - hi.astype(F32)).astype(BF16)
    return hi, lo


def _dot3(a, b):
    a_hi, a_lo = _split(a)
    b_hi, b_lo = _split(b)
    return _dot(a_hi, b_hi) + (_dot(a_lo, b_hi) + _dot(a_hi, b_lo))


def _pair_ones():
    r = lax.broadcasted_iota(jnp.int32, (LANES, LANES), 0) // HEAD
    c = lax.broadcasted_iota(jnp.int32, (LANES, LANES), 1) // HEAD
    return (r == c).astype(BF16)


def _head_sum(x, ones):
    outs = []
    for c in range(x.shape[1] // LANES):
        hi, lo = _split(x[:, c * LANES:(c + 1) * LANES])
        outs.append(_dot(hi, ones) + _dot(lo, ones))
    return jnp.concatenate(outs, axis=1)


def _rms(x, g):
    return x * lax.rsqrt(jnp.mean(x * x, axis=-1, keepdims=True) + NORM_EPS) * g


def _sigmoid(x):
    return 1.0 / (1.0 + jnp.exp(-x))


def _softplus(x):
    return jnp.maximum(x, 0.0) + jnp.log(1.0 + jnp.exp(-jnp.abs(x)))


def _in_proj_kernel(x_ref, g_ref, w_ref, *o_refs):
    h = _rms(x_ref[...], g_ref[...]).astype(BF16)
    c0 = 0
    for o_ref in o_refs:
        c1 = c0 + o_ref.shape[1]
        o_ref[...] = _dot(h, w_ref[:, c0:c1])
        c0 = c1


def _in_proj(x, g, w, splits, tm):
    rows, d = x.shape
    assert sum(splits) == w.shape[1]
    return pl.pallas_call(
        _in_proj_kernel,
        grid=(rows // tm,),
        in_specs=[pl.BlockSpec((tm, d), lambda i: (i, 0)),
                  pl.BlockSpec((1, d), lambda i: (0, 0)),
                  pl.BlockSpec(w.shape, lambda i: (0, 0), pipeline_mode=pl.Buffered(1))],
        out_specs=[pl.BlockSpec((tm, c), lambda i: (i, 0)) for c in splits],
        out_shape=[jax.ShapeDtypeStruct((rows, c), F32) for c in splits],
        compiler_params=_cparams(("arbitrary",)),
        name="in_proj",
    )(x, g.reshape(1, d), w)


def _rwkv_prep_kernel(z_ref, sh_ref, mu_ref, w0_ref, w2_ref, a0_ref, a2_ref, g2_ref, kk_ref, ka_ref,
                      rk_ref, r_o, w_o, k_o, v_o, a_o, b_o, g_o, bon_o, last_scr, *, width):
    n = pl.program_id(0)
    i = pl.program_id(1)
    z = z_ref[0]
    tt = z.shape[0]

    @pl.when(i == 0)
    def _():
        last_scr[...] = sh_ref[pl.ds(n, 1), :]

    row = lax.broadcasted_iota(jnp.int32, z.shape, 0)
    prev = jnp.where(row == 0, last_scr[...], pltpu.roll(z, 1, axis=0))
    last_scr[...] = z[tt - 1:tt, :]
    zs = z + (prev - z) * mu_ref[...]

    c0 = 3 * width
    c1 = c0 + w2_ref.shape[0]
    c2 = c1 + a2_ref.shape[0]
    r = zs[:, 0:width]
    k = zs[:, width:2 * width]
    v = zs[:, 2 * width:c0]
    xw = zs[:, c0:c1]
    xa = zs[:, c1:c2]
    xg = zs[:, c2:]

    logw = -_softplus(-(w0_ref[...] + _dot3(jnp.tanh(xw), w2_ref[...]))) - 0.5
    a = _sigmoid(a0_ref[...] + _dot3(xa, a2_ref[...]))
    g = _dot3(_sigmoid(xg), g2_ref[...])

    ones = _pair_ones()
    kk = k * kk_ref[...]
    kk = kk / jnp.maximum(jnp.sqrt(_head_sum(kk * kk, ones)), 1e-12)
    k2 = k * (1.0 + (a - 1.0) * ka_ref[...])
    bonus = _head_sum(r * k2 * rk_ref[...], ones) * v

    r_o[0] = r
    w_o[0] = -jnp.exp(logw)
    k_o[0] = k2
    v_o[0] = v
    a_o[0] = -kk
    b_o[0] = kk * a
    g_o[0] = g
    bon_o[0] = bonus


def _rwkv_prep(z_rwkv, shift0, prm, tt):
    n, t, cols = z_rwkv.shape
    width = prm["decay_w0"].shape[-1]
    row = lambda x: x.reshape(1, -1)
    full = lambda a: pl.BlockSpec(a.shape, lambda b, i: (0,) * a.ndim)
    args = [shift0, row(prm["shift_mu"]), row(prm["decay_w0"]), prm["decay_w2"], row(prm["aaa_a0"]),
            prm["aaa_a2"], prm["gate_g2"], row(prm["k_k"]), row(prm["k_a"]), row(prm["r_k"])]
    out_spec = pl.BlockSpec((1, tt, width), lambda b, i: (b, i, 0))
    out_sds = jax.ShapeDtypeStruct((n, t, width), F32)
    return pl.pallas_call(
        functools.partial(_rwkv_prep_kernel, width=width),
        grid=(n, t // tt),
        in_specs=[pl.BlockSpec((1, tt, cols), lambda b, i: (b, i, 0))] + [full(a) for a in args],
        out_specs=[out_spec] * 8,
        out_shape=[out_sds] * 8,
        scratch_shapes=[pltpu.VMEM((1, cols), F32)],
        compiler_params=_cparams(("arbitrary", "arbitrary")),
        name="rwkv_prep",
    )(z_rwkv, *args)


def _wkv_kernel(r_ref, w_ref, k_ref, v_ref, a_ref, b_ref, s0_ref, y_ref, st_ref, *, valid):
    left = lax.broadcasted_iota(jnp.int32, (HEAD, LANES), 1) < HEAD

    def col_tile(xt, j):
        return jnp.where(left, xt[0:HEAD, j:j + 1], xt[HEAD:2 * HEAD, j:j + 1])

    for p in range(s0_ref.shape[1]):
        cols = slice(p * LANES, (p + 1) * LANES)
        rt = r_ref[0, :, cols].T
        wt = jnp.exp(w_ref[0, :, cols]).T
        kt = k_ref[0, :, cols].T
        at = a_ref[0, :, cols].T
        bt = b_ref[0, :, cols].T
        vg = v_ref[0, :, cols]
        s = s0_ref[0, p]
        ys = []
        for j in range(y_ref.shape[1]):
            if j < valid:
                sa = jnp.sum(s * col_tile(at, j), axis=0, keepdims=True)
                s = s * col_tile(wt, j) + col_tile(bt, j) * sa + col_tile(kt, j) * vg[j:j + 1, :]
                ys.append(jnp.sum(s * col_tile(rt, j), axis=0, keepdims=True))
            else:
                ys.append(jnp.zeros((1, LANES), F32))
        y_ref[0, :, cols] = jnp.concatenate(ys, axis=0)
        st_ref[0, p] = s


def _wkv(r, w, k, v, a, b, s0, valid):
    n, t, c = r.shape
    assert t == SUBLANES
    tok = pl.BlockSpec((1, t, c), lambda bi: (bi, 0, 0))
    st = pl.BlockSpec((1,) + s0.shape[1:], lambda bi: (bi, 0, 0, 0))
    return pl.pallas_call(
        functools.partial(_wkv_kernel, valid=valid),
        grid=(n,),
        in_specs=[tok] * 6 + [st],
        out_specs=[tok, st],
        out_shape=[jax.ShapeDtypeStruct((n, t, c), F32), jax.ShapeDtypeStruct(s0.shape, F32)],
        compiler_params=_cparams(("arbitrary",)),
        name="wkv_scan",
    )(r, w, k, v, a, b, s0)


WKV_CHUNK = 64


def _bd(x):
    xb = x.astype(BF16)
    left = lax.broadcasted_iota(jnp.int32, x.shape, 1) < HEAD
    zero = jnp.zeros_like(xb)
    return jnp.concatenate([jnp.where(left, xb, zero), jnp.where(left, zero, xb)], axis=0)


def _dot_nt(a, b):
    return lax.dot_general(a, b, (((1,), (1,)), ((), ())), preferred_element_type=F32)


def _dot_tn(a, b):
    return lax.dot_general(a, b, (((0,), (0,)), ((), ())), preferred_element_type=F32)


def _wkv_chunk_local(r, lw, k, v, a, b, consts):
    tri, eye2, strict, incl, blocks = consts
    c = WKV_CHUNK
    n = range(len(r))
    zero = jnp.zeros((c, LANES), F32)

    def cumsum(x):
        hi = x.astype(BF16)
        r1 = x - hi.astype(F32)
        mid = r1.astype(BF16)
        lo = (r1 - mid.astype(F32)).astype(BF16)
        return _dot(tri, hi) + (_dot(tri, mid) + _dot(tri, lo))

    lc = [cumsum(lw[i]) for i in n]
    ltot = [lc[i][c - 1:c, :] for i in n]
    w_inv = [jnp.exp(-lc[i]) for i in n]
    ah = [a[i] * jnp.exp(lc[i] - lw[i]) for i in n]
    rh = [r[i] * jnp.exp(lc[i]) for i in n]
    sc = [_dot_nt(jnp.concatenate([ah[i], rh[i]], axis=0).astype(BF16),
                  jnp.concatenate([_bd(b[i] * w_inv[i]), _bd(k[i] * w_inv[i])], axis=0)) for i in n]
    aab = [jnp.where(strict, sc[i][:c, :LANES], zero) for i in n]

    t = [eye2 + jnp.where(blocks[0], aab[i], zero) for i in n]
    for lvl in range(1, len(blocks)):
        grow = blocks[lvl] & ~blocks[lvl - 1]
        x = [_dot(t[i].astype(BF16), _bd(jnp.where(grow, aab[i], zero))) for i in n]
        t = [t[i] + _dot(x[i].astype(BF16), _bd(t[i])) for i in n]

    vb = [_bd(v[i]) for i in n]
    akv = [_dot(jnp.where(strict, sc[i][:c, LANES:], zero).astype(BF16), vb[i]) for i in n]
    pu = [_dot(t[i].astype(BF16), jnp.concatenate([_bd(ah[i]), _bd(akv[i])], axis=1)) for i in n]

    left2 = (lax.broadcasted_iota(jnp.int32, (HEAD, 2 * LANES), 1) % LANES) < HEAD
    g, h, q, y0 = [], [], [], []
    for i in n:
        w_rem = jnp.exp(ltot[i] - lc[i])
        lhs_t = jnp.concatenate([b[i] * w_rem, k[i] * w_rem], axis=0).astype(BF16)
        rhs = jnp.concatenate([pu[i], jnp.concatenate([zero, v[i]], axis=1)], axis=0).astype(BF16)
        full = _dot_tn(lhs_t, rhs)
        gh = jnp.where(left2, full[:HEAD], full[HEAD:])
        g.append(gh[:, :LANES] + eye2 * jnp.exp(ltot[i]))
        h.append(gh[:, LANES:])
    zero_b = jnp.zeros((2 * c, LANES), BF16)
    for i in n:
        mrb = jnp.where(incl, sc[i][c:, :LANES], zero)
        mrk = jnp.where(incl, sc[i][c:, LANES:], zero)
        rhs2 = jnp.concatenate([jnp.concatenate([_bd(pu[i][:, :LANES]), _bd(pu[i][:, LANES:])], axis=1),
                                jnp.concatenate([zero_b, vb[i]], axis=1)], axis=0)
        qy = _dot(jnp.concatenate([mrb, mrk], axis=1).astype(BF16), rhs2)
        q.append(rh[i] + qy[:, :LANES])
        y0.append(qy[:, LANES:])
    return g, h, q, y0


def _wkv_chunked_kernel(r_ref, w_ref, k_ref, v_ref, a_ref, b_ref, s0_ref, y_ref, st_ref, s_scr):
    i = pl.program_id(1)
    c = WKV_CHUNK

    @pl.when(i == 0)
    def _():
        s_scr[...] = s0_ref[:, 0]

    ti = lax.broadcasted_iota(jnp.int32, (c, LANES), 0)
    si = lax.broadcasted_iota(jnp.int32, (c, LANES), 1) % HEAD
    tri = (lax.broadcasted_iota(jnp.int32, (c, c), 0) >= lax.broadcasted_iota(jnp.int32, (c, c), 1)
           ).astype(BF16)
    eye2 = (ti == si).astype(F32)
    sizes = [2 << l for l in range(c.bit_length() - 1)]
    blocks = [(ti // s) == (si // s) for s in sizes]
    consts = (tri, eye2, ti > si, ti >= si, blocks)

    n_seq = y_ref.shape[0]
    n_chunks = y_ref.shape[1] // c
    chunks = lambda ref: [ref[b, ci * c:(ci + 1) * c, :] for ci in range(n_chunks) for b in range(n_seq)]
    g, h, q, y0 = _wkv_chunk_local(chunks(r_ref), chunks(w_ref), chunks(k_ref), chunks(v_ref),
                                   chunks(a_ref), chunks(b_ref), consts)
    st = [s_scr[b] for b in range(n_seq)]
    for ci in range(n_chunks):
        for b in range(n_seq):
            u = ci * n_seq + b
            out = _dot(jnp.concatenate([q[u], g[u]], axis=0).astype(BF16), _bd(st[b]))
            y_ref[b, ci * c:(ci + 1) * c, :] = out[:c] + y0[u]
            st[b] = out[c:] + h[u]
    for b in range(n_seq):
        s_scr[b] = st[b]

    @pl.when(i == pl.num_programs(1) - 1)
    def _():
        for b in range(n_seq):
            st_ref[b, 0] = st[b]


def _wkv_chunked(r, lw, k, v, a, b, s0, tb):
    n, t, c = r.shape
    pairs = c // LANES
    tok = pl.BlockSpec((n, tb, LANES), lambda p, i: (0, i, p))
    st = pl.BlockSpec((n, 1, HEAD, LANES), lambda p, i: (0, p, 0, 0))
    return pl.pallas_call(
        _wkv_chunked_kernel,
        grid=(pairs, t // tb),
        in_specs=[tok] * 6 + [st],
        out_specs=[tok, st],
        out_shape=[jax.ShapeDtypeStruct((n, t, c), F32), jax.ShapeDtypeStruct(s0.shape, F32)],
        scratch_shapes=[pltpu.VMEM((n, HEAD, LANES), F32)],
        compiler_params=_cparams(("arbitrary", "arbitrary")),
        name="wkv_chunked",
    )(r, lw, k, v, a, b, s0)


def _pack_state(s):
    n, h, vd, kd = s.shape
    return s.reshape(n, h // 2, 2, vd, kd).transpose(0, 1, 4, 2, 3).reshape(n, h // 2, kd, 2 * vd)


def _unpack_state(s, h):
    n, _, kd, _ = s.shape
    return s.reshape(n, h // 2, kd, 2, HEAD).transpose(0, 1, 3, 4, 2).reshape(n, h, HEAD, kd)


def _attn_prompt_kernel(q_ref, k_ref, kp_ref, v_ref, vp_ref, o_ref, l_ref, *, dil, sub, n_back, scale):
    first = pl.program_id(1) == 0
    qi = lax.broadcasted_iota(jnp.int32, (ATTN_BLOCK, 2 * ATTN_BLOCK), 0)
    ki = lax.broadcasted_iota(jnp.int32, (ATTN_BLOCK, 2 * ATTN_BLOCK), 1)
    dist = qi + ATTN_BLOCK - ki
    band = (dist >= 0) & (dist <= n_back)
    own = ki >= ATTN_BLOCK

    def rows(j, r):
        if dil == 1:
            return pl.ds(j * ATTN_BLOCK, ATTN_BLOCK)
        return pl.ds(j * ATTN_BLOCK * dil + r, ATTN_BLOCK, stride=dil)

    def unit(j, r):
        q = (q_ref[0, rows(j, r), :] * scale).astype(BF16)
        if j == 0:
            kp, vp = kp_ref[0, rows(0, r), :], vp_ref[0, rows(0, r), :]
            mask = band & (own | jnp.logical_not(first))
        else:
            kp, vp, mask = k_ref[0, rows(j - 1, r), :], v_ref[0, rows(j - 1, r), :], band
        kcat = jnp.concatenate([kp, k_ref[0, rows(j, r), :]], axis=0).astype(BF16)
        vcat = jnp.concatenate([vp, v_ref[0, rows(j, r), :]], axis=0).astype(BF16)
        outs, lses = [], []
        for h in range(LANES // HEAD):
            sl = slice(h * HEAD, (h + 1) * HEAD)
            s = jnp.where(mask, _dot_nt(q[:, sl], kcat[:, sl]), NEG)
            m = jnp.max(s, axis=-1, keepdims=True)
            e = jnp.exp(s - m)
            den = jnp.sum(e, axis=-1, keepdims=True)
            outs.append(_dot(e.astype(BF16), vcat[:, sl]) / den)
            lses.append(jnp.broadcast_to(m + jnp.log(den), (ATTN_BLOCK, HEAD)))
        o_ref[0, rows(j, r), :] = jnp.concatenate(outs, axis=1)
        l_ref[0, rows(j, r), :] = jnp.concatenate(lses, axis=1)

    for j in range(sub):
        if dil <= 4:
            for r in range(dil):
                unit(j, r)
        else:
            def body(r4, carry, j=j):
                for u in range(4):
                    unit(j, r4 * 4 + u)
                return carry
            lax.fori_loop(0, dil // 4, body, 0)


def _attn_prompt(zqkv, gi, span):
    n, t, cols = zqkv.shape
    d = DILATIONS[gi]
    aw = ATTN_HEADS * HEAD
    ng = len(DILATIONS)
    back = ATTN_BLOCK * d
    sub = span // back
    halves = aw // LANES
    col = lambda which, hp: (which * ng + gi) * halves + hp
    cur = lambda which: pl.BlockSpec((1, span, LANES), lambda b, i, hp: (b, i, col(which, hp)))
    prev = lambda which: pl.BlockSpec((1, back, LANES),
                                      lambda b, i, hp: (b, jnp.maximum(i * sub - 1, 0), col(which, hp)))
    out_spec = pl.BlockSpec((1, span, LANES), lambda b, i, hp: (b, i, hp))
    out_sds = jax.ShapeDtypeStruct((n, t, aw), F32)
    return pl.pallas_call(
        functools.partial(_attn_prompt_kernel, dil=d, sub=sub, n_back=WINDOWS[gi] // d, scale=HEAD ** -0.5),
        grid=(n, t // span, halves),
        in_specs=[cur(0), cur(1), prev(1), cur(2), prev(2)],
        out_specs=[out_spec, out_spec],
        out_shape=[out_sds, out_sds],
        compiler_params=_cparams(("arbitrary", "arbitrary", "arbitrary")),
        name=f"attn_prompt_g{gi}",
    )(zqkv, zqkv, zqkv, zqkv, zqkv)


def _attn_sample_kernel(q_ref, k_ref, v_ref, c_ref, o_ref, l_ref, *, dil, n_back, valid, scale):
    q = (q_ref[0] * scale).astype(BF16)
    tp = q.shape[0]
    cl = c_ref.shape[4]
    dist_c = (cl + lax.broadcasted_iota(jnp.int32, (tp, cl), 0)
              - lax.broadcasted_iota(jnp.int32, (tp, cl), 1))
    mask_c = (dist_c % dil == 0) & (dist_c <= dil * n_back)
    pn = lax.broadcasted_iota(jnp.int32, (tp, tp), 1)
    dist_n = lax.broadcasted_iota(jnp.int32, (tp, tp), 0) - pn
    mask_n = (dist_n >= 0) & (dist_n % dil == 0) & (dist_n <= dil * n_back) & (pn < valid)
    outs, lses = [], []
    for h in range(ATTN_HEADS):
        sl = slice(h * HEAD, (h + 1) * HEAD)
        s_c = jnp.where(mask_c, _dot(q[:, sl], c_ref[0, 0, h].astype(BF16)), NEG)
        s_n = jnp.where(mask_n, _dot_nt(q[:, sl], k_ref[0, :, sl].astype(BF16)), NEG)
        m = jnp.maximum(jnp.max(s_c, axis=-1, keepdims=True), jnp.max(s_n, axis=-1, keepdims=True))
        e_c = jnp.exp(s_c - m)
        e_n = jnp.exp(s_n - m)
        den = jnp.sum(e_c, axis=-1, keepdims=True) + jnp.sum(e_n, axis=-1, keepdims=True)
        o = (_dot_nt(e_c.astype(BF16), c_ref[0, 1, h].astype(BF16))
             + _dot(e_n.astype(BF16), v_ref[0, :, sl].astype(BF16)))
        outs.append(o / den)
        lses.append(jnp.broadcast_to(m + jnp.log(den), (tp, HEAD)))
    o_ref[0] = jnp.concatenate(outs, axis=1)
    l_ref[0] = jnp.concatenate(lses, axis=1)


def _attn_sample(zqkv, cache, gi, valid):
    n, tp, cols = zqkv.shape
    aw = ATTN_HEADS * HEAD
    ng = len(DILATIONS)
    d = DILATIONS[gi]
    cache_t = cache.transpose(0, 2, 3, 4, 1)
    spec = lambda which: pl.BlockSpec((1, tp, aw), lambda b: (b, 0, which * ng + gi))
    out_spec = pl.BlockSpec((1, tp, aw), lambda b: (b, 0, 0))
    out_sds = jax.ShapeDtypeStruct((n, tp, aw), F32)
    return pl.pallas_call(
        functools.partial(_attn_sample_kernel, dil=d, n_back=WINDOWS[gi] // d, valid=valid,
                          scale=HEAD ** -0.5),
        grid=(n,),
        in_specs=[spec(0), spec(1), spec(2),
                  pl.BlockSpec((1,) + cache_t.shape[1:], lambda b: (b, 0, 0, 0, 0))],
        out_specs=[out_spec, out_spec],
        out_shape=[out_sds, out_sds],
        compiler_params=_cparams(("arbitrary",)),
        name=f"attn_sample_g{gi}",
    )(zqkv, zqkv, zqkv, cache_t)


def _merge_kernel(x_ref, y_ref, bon_ref, g_ref, lw_ref, lb_ref, o0, o1, o2, l0, l1, l2, zg_ref,
                  wbr_ref, wba_ref, wout_ref, out_ref):
    ones = _pair_ones()
    y = y_ref[...]
    mu = _head_sum(y, ones) * (1.0 / HEAD)
    yc = y - mu
    var = _head_sum(yc * yc, ones) * (1.0 / HEAD)
    yn = yc * lax.rsqrt(var + LNX_EPS) * lw_ref[...] + lb_ref[...]
    rwkv = (yn + bon_ref[...]) * g_ref[...]

    ls = [l0[...], l1[...], l2[...]]
    m = jnp.maximum(jnp.maximum(ls[0], ls[1]), ls[2])
    es = [jnp.exp(l - m) for l in ls]
    attn = (es[0] * o0[...] + es[1] * o1[...] + es[2] * o2[...]) / (es[0] + es[1] + es[2])

    d = x_ref.shape[1]
    gates = _sigmoid(zg_ref[...])
    mixed = (gates[:, :d] * _dot(rwkv.astype(BF16), wbr_ref[...])
             + gates[:, d:] * _dot(attn.astype(BF16), wba_ref[...]))
    out_ref[...] = x_ref[...] + _dot(mixed.astype(BF16), wout_ref[...])


def _merge(x, y, bonus, g, lnx_w, lnx_b, os_, ls_, zg, wbr, wba, wout, tm):
    rows, d = x.shape
    rowblk = lambda a: pl.BlockSpec((tm, a.shape[1]), lambda i: (i, 0))
    full = lambda a: pl.BlockSpec(a.shape, lambda i: (0, 0))
    lw, lb = lnx_w.reshape(1, -1), lnx_b.reshape(1, -1)
    args = [x, y, bonus, g, lw, lb, *os_, *ls_, zg, wbr, wba, wout]
    specs = [rowblk(x), rowblk(y), rowblk(bonus), rowblk(g), full(lw), full(lb)] + \
            [rowblk(a) for a in (*os_, *ls_)] + [rowblk(zg), full(wbr), full(wba), full(wout)]
    return pl.pallas_call(
        _merge_kernel,
        grid=(rows // tm,),
        in_specs=specs,
        out_specs=pl.BlockSpec((tm, d), lambda i: (i, 0)),
        out_shape=jax.ShapeDtypeStruct((rows, d), F32),
        compiler_params=_cparams(("arbitrary",)),
        name="merge_out",
    )(*args)


def _mlp_ple_kernel(x_ref, g_ref, w1_ref, w2_ref, p_ref, gp_ref, wg_ref, wp_ref, gf_ref, o_ref,
                    h_scr, acc_scr, *, final):
    j = pl.program_id(1)

    @pl.when(j == 0)
    def _():
        h_scr[...] = _rms(x_ref[...], g_ref[...]).astype(BF16)
        acc_scr[...] = x_ref[...]

    u = jnp.maximum(_dot(h_scr[...], w1_ref[...]), 0.0)
    acc_scr[...] += _dot((u * u).astype(BF16), w2_ref[...])

    @pl.when(j == pl.num_programs(1) - 1)
    def _():
        x = acc_scr[...]
        h = _rms(x, gp_ref[...]).astype(BF16)
        x = x + _sigmoid(_dot(h, wg_ref[...])) * _dot(p_ref[...].astype(BF16), wp_ref[...])
        if final:
            x = _rms(x, gf_ref[...])
        o_ref[...] = x


def _mlp_ple(x, g, w1, w2, p, gp, wg, wp, gf, final, tm, tf):
    rows, d = x.shape
    dff = w1.shape[1]
    vec = lambda a: a.reshape(1, d)
    full = lambda a: pl.BlockSpec(a.shape, lambda i, j: (0, 0))
    return pl.pallas_call(
        functools.partial(_mlp_ple_kernel, final=final),
        grid=(rows // tm, dff // tf),
        in_specs=[pl.BlockSpec((tm, d), lambda i, j: (i, 0)),
                  full(vec(g)),
                  pl.BlockSpec((d, tf), lambda i, j: (0, j)),
                  pl.BlockSpec((tf, d), lambda i, j: (j, 0)),
                  pl.BlockSpec((tm, p.shape[1]), lambda i, j: (i, 0)),
                  full(vec(gp)), full(wg), full(wp), full(vec(gf))],
        out_specs=pl.BlockSpec((tm, d), lambda i, j: (i, 0)),
        out_shape=jax.ShapeDtypeStruct((rows, d), F32),
        scratch_shapes=[pltpu.VMEM((tm, d), BF16), pltpu.VMEM((tm, d), F32)],
        compiler_params=_cparams(("arbitrary", "arbitrary")),
        name="mlp_ple",
    )(x, vec(g), w1, w2, p, vec(gp), wg, wp, vec(gf))


def _layer(x, p, wkv0, shift0, caches, prm, norm_final, final, valid):
    n, t, d = x.shape
    rows = n * t
    width = prm["decay_w0"].shape[-1]
    rwkv_cols = prm["shift_mu"].shape[-1]
    aw = ATTN_HEADS * HEAD
    qkv_cols = 3 * len(DILATIONS) * aw
    tm = min(512, rows)
    x2 = x.reshape(rows, d)

    pick = lambda cols, opts: next(o for o in opts if cols % o == 0)
    z_rwkv, z_qkv, z_gate = _in_proj(x2, prm["norm_mix"], prm["w_in"].astype(BF16),
                                     (rwkv_cols, qkv_cols, 2 * d), min(256, rows))
    z_rwkv = z_rwkv.reshape(n, t, rwkv_cols)
    z_qkv = z_qkv.reshape(n, t, qkv_cols)

    r, w, k, v, a, b, g, bonus = _rwkv_prep(z_rwkv, shift0, prm, min(256, t))
    heads = width // HEAD
    if valid == t and t % WKV_CHUNK == 0:
        y, s_t = _wkv_chunked(r, w, k, v, a, b, _pack_state(wkv0), pick(t, (512, 256, WKV_CHUNK)))
    else:
        y, s_t = _wkv(r, w, k, v, a, b, _pack_state(wkv0), valid)
    wkv_t = _unpack_state(s_t, heads)
    shift_t = z_rwkv[:, valid - 1, :]

    os_, ls_, new_kv = [], [], []
    ng = len(DILATIONS)
    for gi in range(ng):
        if caches is None:
            o, l = _attn_prompt(z_qkv, gi, max(1024, ATTN_BLOCK * DILATIONS[gi]))
            keep = min(WINDOWS[gi], t)
            kv_rows = z_qkv[:, t - keep:t]
        else:
            o, l = _attn_sample(z_qkv, caches[gi], gi, valid)
            kv_rows = z_qkv[:, :valid]
        new_kv.append(jnp.stack([kv_rows[:, :, (which * ng + gi) * aw:(which * ng + gi + 1) * aw]
                                 .reshape(n, -1, ATTN_HEADS, HEAD) for which in (1, 2)], axis=2))
        os_.append(o.reshape(rows, aw))
        ls_.append(l.reshape(rows, aw))

    flat = lambda u: u.reshape(rows, -1)
    x2 = _merge(x2, flat(y), flat(bonus), flat(g), prm["lnx_w"], prm["lnx_b"], os_, ls_, z_gate,
                prm["w_branch_rwkv"].astype(BF16), prm["w_branch_attn"].astype(BF16),
                prm["w_out"].astype(BF16), min(256, rows))
    x2 = _mlp_ple(x2, prm["norm_mlp"], prm["w_ff1"].astype(BF16), prm["w_ff2"].astype(BF16),
                  p.reshape(rows, -1), prm["norm_ple"], prm["w_ple_gate"].astype(BF16),
                  prm["w_ple"].astype(BF16), norm_final, final, tm, 1024)
    return x2.reshape(n, t, d), new_kv, wkv_t, shift_t


_LAYER_PARAMS = ("norm_mix", "w_in", "shift_mu", "decay_w0", "decay_w2", "aaa_a0", "aaa_a2", "gate_g2",
                 "k_k", "k_a", "r_k", "lnx_w", "lnx_b", "w_branch_rwkv", "w_branch_attn", "w_out",
                 "norm_mlp", "w_ff1", "w_ff2", "norm_ple", "w_ple", "w_ple_gate")


def kernel(x_prompt, x_sample, cache_kv_g0, cache_kv_g1, cache_kv_g2, state_wkv, state_shift, p_prompt, p_sample, norm_mix, w_in, shift_mu, decay_w0, decay_w2, aaa_a0, aaa_a2, gate_g2, k_k, k_a, r_k, lnx_w, lnx_b, w_branch_rwkv, w_branch_attn, w_out, norm_mlp, w_ff1, w_ff2, norm_ple, w_ple, w_ple_gate, norm_final):
    stacked = dict(zip(_LAYER_PARAMS, (norm_mix, w_in, shift_mu, decay_w0, decay_w2, aaa_a0, aaa_a2,
                                       gate_g2, k_k, k_a, r_k, lnx_w, lnx_b, w_branch_rwkv,
                                       w_branch_attn, w_out, norm_mlp, w_ff1, w_ff2, norm_ple, w_ple,
                                       w_ple_gate)))
    depth = norm_mix.shape[0]
    nb_p, _, d = x_prompt.shape
    nb_s, t_s, _ = x_sample.shape
    heads = decay_w0.shape[-1] // HEAD
    rwkv_cols = shift_mu.shape[-1]
    t_pad = -(-t_s // SUBLANES) * SUBLANES
    pad_t = lambda u: jnp.pad(u, ((0, 0), (0, t_pad - t_s), (0, 0)))

    yp, ys = x_prompt, pad_t(x_sample)
    kvp, kvs = [[], [], []], [[], [], []]
    wkv_p, sh_p, wkv_s, sh_s = [], [], [], []
    for i in range(depth):
        prm = {name: val[i] for name, val in stacked.items()}
        prm["r_k"] = prm["r_k"].reshape(-1)
        last = i == depth - 1
        wkv0 = jnp.zeros((nb_p, heads, HEAD, HEAD), F32)
        shift0 = jnp.zeros((nb_p, rwkv_cols), F32)
        yp, nkv_p, w_p, s_p = _layer(yp, p_prompt[i], wkv0, shift0, None, prm, norm_final, last,
                                     x_prompt.shape[1])
        ys, nkv_s, w_s, s_s = _layer(ys, pad_t(p_sample[i]), state_wkv[i], state_shift[i],
                                     (cache_kv_g0[i], cache_kv_g1[i], cache_kv_g2[i]), prm, norm_final,
                                     last, t_s)
        for gi in range(3):
            kvp[gi].append(nkv_p[gi])
            kvs[gi].append(nkv_s[gi])
        wkv_p.append(w_p)
        sh_p.append(s_p)
        wkv_s.append(w_s)
        sh_s.append(s_s)
    return (yp, ys[:, :t_s],
            jnp.stack(kvp[0]), jnp.stack(kvp[1]), jnp.stack(kvp[2]),
            jnp.stack(wkv_p), jnp.stack(sh_p),
            jnp.stack(kvs[0]), jnp.stack(kvs[1]), jnp.stack(kvs[2]),
            jnp.stack(wkv_s), jnp.stack(sh_s))
```

```python
import functools

import jax
import jax.numpy as jnp
from jax import lax
from jax.experimental import pallas as pl
from jax.experimental.pallas import tpu as pltpu

F32 = jnp.float32
BF16 = jnp.bfloat16

HEAD = 64
LANES = 128
SUBLANES = 8
ATTN_HEADS = 4
ATTN_BLOCK = 128
WINDOWS = (128, 512, 2048)
DILATIONS = (1, 4, 16)
NORM_EPS = 1e-6
LNX_EPS = 64e-5
NEG = -1e30
VMEM_LIMIT = 56 * 1024 * 1024


def _cparams(sem):
    return pltpu.CompilerParams(dimension_semantics=sem, vmem_limit_bytes=VMEM_LIMIT)


def _dot(a, b):
    return jnp.dot(a, b, preferred_element_type=F32)


def _split(x):
    hi = x.astype(BF16)
    lo = (x - hi.astype(F32)).astype(BF16)
    return hi, lo


def _dot_bf16(a, b):
    return _dot(a.astype(BF16), b.astype(BF16))


def _pair_ones():
    r = lax.broadcasted_iota(jnp.int32, (LANES, LANES), 0) // HEAD
    c = lax.broadcasted_iota(jnp.int32, (LANES, LANES), 1) // HEAD
    return (r == c).astype(BF16)


def _head_sum(x, ones):
    outs = []
    for c in range(x.shape[1] // LANES):
        hi, lo = _split(x[:, c * LANES:(c + 1) * LANES])
        outs.append(_dot(hi, ones) + _dot(lo, ones))
    return jnp.concatenate(outs, axis=1)


def _rms(x, g):
    return x * lax.rsqrt(jnp.mean(x * x, axis=-1, keepdims=True) + NORM_EPS) * g


def _sigmoid(x):
    return 1.0 / (1.0 + jnp.exp(-x))


def _softplus(x):
    return jnp.maximum(x, 0.0) + jnp.log(1.0 + jnp.exp(-jnp.abs(x)))


def _in_proj_kernel(x_ref, g_ref, w_ref, *o_refs):
    h = _rms(x_ref[...], g_ref[...]).astype(BF16)
    c0 = 0
    for o_ref in o_refs:
        c1 = c0 + o_ref.shape[1]
        o_ref[...] = _dot(h, w_ref[:, c0:c1])
        c0 = c1


def _in_proj(x, g, w, splits, tm):
    rows, d = x.shape
    assert sum(splits) == w.shape[1]
    return pl.pallas_call(
        _in_proj_kernel,
        grid=(rows // tm,),
        in_specs=[pl.BlockSpec((tm, d), lambda i: (i, 0)),
                  pl.BlockSpec((1, d), lambda i: (0, 0)),
                  pl.BlockSpec(w.shape, lambda i: (0, 0), pipeline_mode=pl.Buffered(1))],
        out_specs=[pl.BlockSpec((tm, c), lambda i: (i, 0)) for c in splits],
        out_shape=[jax.ShapeDtypeStruct((rows, c), F32) for c in splits],
        compiler_params=_cparams(("arbitrary",)),
        name="in_proj",
    )(x, g.reshape(1, d), w)


def _rwkv_features(r, k, v, xw, xa, xg, w0, w2, a0, a2, g2, k_k, k_a, r_k, ones):
    logw = -_softplus(-(w0 + _dot_bf16(jnp.tanh(xw), w2))) - 0.5
    a = _sigmoid(a0 + _dot_bf16(xa, a2))
    g = _dot_bf16(_sigmoid(xg), g2)
    kk = k * k_k
    kk = kk / jnp.maximum(jnp.sqrt(_head_sum(kk * kk, ones)), 1e-12)
    k2 = k * (1.0 + (a - 1.0) * k_a)
    bonus = _head_sum(r * k2 * r_k, ones) * v
    return -jnp.exp(logw), k2, -kk, kk * a, g, bonus


def _rwkv_out(y, bonus, g, lnx_w, lnx_b, ones):
    mu = _head_sum(y, ones) * (1.0 / HEAD)
    yc = y - mu
    var = _head_sum(yc * yc, ones) * (1.0 / HEAD)
    return (yc * lax.rsqrt(var + LNX_EPS) * lnx_w + lnx_b + bonus) * g


def _rwkv_prep_kernel(z_ref, sh_ref, mu_ref, w0_ref, w2_ref, a0_ref, a2_ref, g2_ref, kk_ref, ka_ref,
                      rk_ref, r_o, w_o, k_o, v_o, a_o, b_o, g_o, bon_o, last_scr, *, width):
    n = pl.program_id(0)
    i = pl.program_id(1)
    z = z_ref[0]
    tt = z.shape[0]

    @pl.when(i == 0)
    def _():
        last_scr[...] = sh_ref[pl.ds(n, 1), :]

    row = lax.broadcasted_iota(jnp.int32, z.shape, 0)
    prev = jnp.where(row == 0, last_scr[...], pltpu.roll(z, 1, axis=0))
    last_scr[...] = z[tt - 1:tt, :]
    zs = z + (prev - z) * mu_ref[...]

    c0 = 3 * width
    c1 = c0 + w2_ref.shape[0]
    c2 = c1 + a2_ref.shape[0]
    r = zs[:, 0:width]
    k = zs[:, width:2 * width]
    v = zs[:, 2 * width:c0]
    xw = zs[:, c0:c1]
    xa = zs[:, c1:c2]
    xg = zs[:, c2:]

    lw, k2, a_neg, b, g, bonus = _rwkv_features(
        r, k, v, xw, xa, xg, w0_ref[...], w2_ref[...], a0_ref[...], a2_ref[...], g2_ref[...],
        kk_ref[...], ka_ref[...], rk_ref[...], _pair_ones())
    r_o[0] = r
    w_o[0] = lw
    k_o[0] = k2
    v_o[0] = v
    a_o[0] = a_neg
    b_o[0] = b
    g_o[0] = g
    bon_o[0] = bonus


def _rwkv_prep(z_rwkv, shift0, prm, tt):
    n, t, cols = z_rwkv.shape
    width = prm["decay_w0"].shape[-1]
    row = lambda x: x.reshape(1, -1)
    full = lambda a: pl.BlockSpec(a.shape, lambda b, i: (0,) * a.ndim)
    args = [shift0, row(prm["shift_mu"]), row(prm["decay_w0"]), prm["decay_w2"], row(prm["aaa_a0"]),
            prm["aaa_a2"], prm["gate_g2"], row(prm["k_k"]), row(prm["k_a"]), row(prm["r_k"])]
    out_spec = pl.BlockSpec((1, tt, width), lambda b, i: (b, i, 0))
    out_sds = jax.ShapeDtypeStruct((n, t, width), F32)
    return pl.pallas_call(
        functools.partial(_rwkv_prep_kernel, width=width),
        grid=(n, t // tt),
        in_specs=[pl.BlockSpec((1, tt, cols), lambda b, i: (b, i, 0))] + [full(a) for a in args],
        out_specs=[out_spec] * 8,
        out_shape=[out_sds] * 8,
        scratch_shapes=[pltpu.VMEM((1, cols), F32)],
        compiler_params=_cparams(("arbitrary", "arbitrary")),
        name="rwkv_prep",
    )(z_rwkv, *args)


def _wkv_kernel(r_ref, w_ref, k_ref, v_ref, a_ref, b_ref, g_ref, bon_ref, lnw_ref, lnb_ref, s0_ref,
                y_ref, st_ref, *, valid):
    ones = _pair_ones()
    row = lax.broadcasted_iota(jnp.int32, (HEAD, LANES), 0)
    lane = lax.broadcasted_iota(jnp.int32, (HEAD, LANES), 1)
    left = lane < HEAD
    diag = (row == lane % HEAD).astype(F32)

    def key_sum(x):
        hi = x.astype(BF16)
        r1 = x - hi.astype(F32)
        mid = r1.astype(BF16)
        lo = (r1 - mid.astype(F32)).astype(BF16)
        return _dot(hi, ones) + (_dot(mid, ones) + _dot(lo, ones))

    pairs = range(s0_ref.shape[1] // 2)
    cols = [slice(p * LANES, (p + 1) * LANES) for p in pairs]
    w = jnp.exp(w_ref[0])
    vt = [v_ref[0, :, cols[p]].T for p in pairs]
    s = [jnp.concatenate([s0_ref[0, 2 * p], s0_ref[0, 2 * p + 1]], axis=1) for p in pairs]
    ys = [[] for _ in pairs]
    for j in range(y_ref.shape[1]):
        for p in pairs:
            if j < valid:
                row_of = lambda ref_or_val: ref_or_val[j:j + 1, cols[p]]
                v_col = jnp.where(left, vt[p][0:HEAD, j:j + 1], vt[p][HEAD:2 * HEAD, j:j + 1])
                sa = key_sum(s[p] * row_of(a_ref[0]))
                s[p] = s[p] * row_of(w) + sa * row_of(b_ref[0]) + v_col * row_of(k_ref[0])
                y_col = key_sum(s[p] * row_of(r_ref[0]))
                ys[p].append(jnp.sum(y_col * diag, axis=0, keepdims=True))
            else:
                ys[p].append(jnp.zeros((1, LANES), F32))
    for p in pairs:
        y_ref[0, :, cols[p]] = _rwkv_out(jnp.concatenate(ys[p], axis=0), bon_ref[0, :, cols[p]],
                                         g_ref[0, :, cols[p]], lnw_ref[:, cols[p]], lnb_ref[:, cols[p]], ones)
        st_ref[0, 2 * p] = s[p][:, :HEAD]
        st_ref[0, 2 * p + 1] = s[p][:, HEAD:]


def _wkv(r, w, k, v, a, b, g, bonus, lnx_w, lnx_b, s0, valid):
    n, t, c = r.shape
    assert t == SUBLANES
    tok = pl.BlockSpec((1, t, c), lambda bi: (bi, 0, 0))
    vec = pl.BlockSpec((1, c), lambda bi: (0, 0))
    st = pl.BlockSpec((1,) + s0.shape[1:], lambda bi: (bi, 0, 0, 0))
    return pl.pallas_call(
        functools.partial(_wkv_kernel, valid=valid),
        grid=(n,),
        in_specs=[tok] * 8 + [vec, vec, st],
        out_specs=[tok, st],
        out_shape=[jax.ShapeDtypeStruct((n, t, c), F32), jax.ShapeDtypeStruct(s0.shape, F32)],
        compiler_params=_cparams(("arbitrary",)),
        name="wkv_scan",
    )(r, w, k, v, a, b, g, bonus, lnx_w.reshape(1, c), lnx_b.reshape(1, c), s0)


WKV_CHUNK = 64


def _bd(x):
    xb = x.astype(BF16)
    left = lax.broadcasted_iota(jnp.int32, x.shape, 1) < HEAD
    zero = jnp.zeros_like(xb)
    return jnp.concatenate([jnp.where(left, xb, zero), jnp.where(left, zero, xb)], axis=0)


def _dot_nt(a, b):
    return lax.dot_general(a, b, (((1,), (1,)), ((), ())), preferred_element_type=F32)


def _dot_tn(a, b):
    return lax.dot_general(a, b, (((0,), (0,)), ((), ())), preferred_element_type=F32)


def _wkv_chunk_local(r, lw, k, v, a, b, consts):
    tri, eye2, strict, incl, blocks = consts
    c = WKV_CHUNK
    n = range(len(r))
    zero = jnp.zeros((c, LANES), F32)

    def cumsum(x):
        hi = x.astype(BF16)
        r1 = x - hi.astype(F32)
        mid = r1.astype(BF16)
        lo = (r1 - mid.astype(F32)).astype(BF16)
        return _dot(tri, hi) + (_dot(tri, mid) + _dot(tri, lo))

    lc = [cumsum(lw[i]) for i in n]
    ltot = [lc[i][c - 1:c, :] for i in n]
    w_inv = [jnp.exp(-lc[i]) for i in n]
    ah = [a[i] * jnp.exp(lc[i] - lw[i]) for i in n]
    rh = [r[i] * jnp.exp(lc[i]) for i in n]
    sc = [_dot_nt(jnp.concatenate([ah[i], rh[i]], axis=0).astype(BF16),
                  jnp.concatenate([_bd(b[i] * w_inv[i]), _bd(k[i] * w_inv[i])], axis=0)) for i in n]
    aab = [jnp.where(strict, sc[i][:c, :LANES], zero) for i in n]

    t = [eye2 + jnp.where(blocks[0], aab[i], zero) for i in n]
    for lvl in range(1, len(blocks)):
        grow = blocks[lvl] & ~blocks[lvl - 1]
        x = [_dot(t[i].astype(BF16), _bd(jnp.where(grow, aab[i], zero))) for i in n]
        t = [t[i] + _dot(x[i].astype(BF16), _bd(t[i])) for i in n]

    vb = [_bd(v[i]) for i in n]
    akv = [_dot(jnp.where(strict, sc[i][:c, LANES:], zero).astype(BF16), vb[i]) for i in n]
    pu = [_dot(t[i].astype(BF16), jnp.concatenate([_bd(ah[i]), _bd(akv[i])], axis=1)) for i in n]

    left2 = (lax.broadcasted_iota(jnp.int32, (HEAD, 2 * LANES), 1) % LANES) < HEAD
    g, h, q, y0 = [], [], [], []
    for i in n:
        w_rem = jnp.exp(ltot[i] - lc[i])
        lhs_t = jnp.concatenate([b[i] * w_rem, k[i] * w_rem], axis=0).astype(BF16)
        rhs = jnp.concatenate([pu[i], jnp.concatenate([zero, v[i]], axis=1)], axis=0).astype(BF16)
        full = _dot_tn(lhs_t, rhs)
        gh = jnp.where(left2, full[:HEAD], full[HEAD:])
        g.append(gh[:, :LANES] + eye2 * jnp.exp(ltot[i]))
        h.append(gh[:, LANES:])
    zero_b = jnp.zeros((2 * c, LANES), BF16)
    for i in n:
        mrb = jnp.where(incl, sc[i][c:, :LANES], zero)
        mrk = jnp.where(incl, sc[i][c:, LANES:], zero)
        rhs2 = jnp.concatenate([jnp.concatenate([_bd(pu[i][:, :LANES]), _bd(pu[i][:, LANES:])], axis=1),
                                jnp.concatenate([zero_b, vb[i]], axis=1)], axis=0)
        qy = _dot(jnp.concatenate([mrb, mrk], axis=1).astype(BF16), rhs2)
        q.append(rh[i] + qy[:, :LANES])
        y0.append(qy[:, LANES:])
    return g, h, q, y0


def _wkv_fused_kernel(zr_ref, zk_ref, zv_ref, zl_ref, sr_ref, sk_ref, sv_ref, sl_ref,
                      mr_ref, mk_ref, mv_ref, ml_ref, w0_ref, w2_ref, a0_ref, a2_ref, g2_ref,
                      kk_ref, ka_ref, rk_ref, lnw_ref, lnb_ref, s0_ref, out_ref, st_ref,
                      s_scr, pr_scr, pk_scr, pv_scr, pl_scr):
    i = pl.program_id(1)
    c = WKV_CHUNK
    n_seq, tb = out_ref.shape[0], out_ref.shape[1]

    @pl.when(i == 0)
    def _():
        s_scr[...] = s0_ref[:, 0]
        pr_scr[...] = sr_ref[...]
        pk_scr[...] = sk_ref[...]
        pv_scr[...] = sv_ref[...]
        pl_scr[...] = sl_ref[...]

    def shifted(z_ref, p_scr, mu_ref, b):
        z = z_ref[b]
        first = lax.broadcasted_iota(jnp.int32, z.shape, 0) == 0
        prev = jnp.where(first, p_scr[b:b + 1, :], pltpu.roll(z, 1, axis=0))
        p_scr[b:b + 1, :] = z[tb - 1:tb, :]
        return z + (prev - z) * mu_ref[...]

    ones = _pair_ones()
    c1 = w2_ref.shape[0]
    c2 = c1 + a2_ref.shape[0]
    feats = []
    for b in range(n_seq):
        r = shifted(zr_ref, pr_scr, mr_ref, b)
        k = shifted(zk_ref, pk_scr, mk_ref, b)
        v = shifted(zv_ref, pv_scr, mv_ref, b)
        zl = shifted(zl_ref, pl_scr, ml_ref, b)
        lw, k2, a_neg, bb, gate, bonus = _rwkv_features(
            r, k, v, zl[:, :c1], zl[:, c1:c2], zl[:, c2:], w0_ref[...], w2_ref[...], a0_ref[...],
            a2_ref[...], g2_ref[...], kk_ref[...], ka_ref[...], rk_ref[...], ones)
        feats.append((r, lw, k2, v, a_neg, bb, gate, bonus))

    ti = lax.broadcasted_iota(jnp.int32, (c, LANES), 0)
    si = lax.broadcasted_iota(jnp.int32, (c, LANES), 1) % HEAD
    tri = (lax.broadcasted_iota(jnp.int32, (c, c), 0) >= lax.broadcasted_iota(jnp.int32, (c, c), 1)
           ).astype(BF16)
    eye2 = (ti == si).astype(F32)
    sizes = [2 << l for l in range(c.bit_length() - 1)]
    blocks = [(ti // s) == (si // s) for s in sizes]
    consts = (tri, eye2, ti > si, ti >= si, blocks)

    n_chunks = tb // c
    chunks = lambda which: [feats[b][which][ci * c:(ci + 1) * c, :]
                            for ci in range(n_chunks) for b in range(n_seq)]
    g, h, q, y0 = _wkv_chunk_local(*(chunks(which) for which in range(6)), consts)
    st = [s_scr[b] for b in range(n_seq)]
    ys = [[] for _ in range(n_seq)]
    for ci in range(n_chunks):
        for b in range(n_seq):
            u = ci * n_seq + b
            out = _dot(jnp.concatenate([q[u], g[u]], axis=0).astype(BF16), _bd(st[b]))
            ys[b].append(out[:c] + y0[u])
            st[b] = out[c:] + h[u]
    for b in range(n_seq):
        s_scr[b] = st[b]
        out_ref[b] = _rwkv_out(jnp.concatenate(ys[b], axis=0), feats[b][7], feats[b][6],
                               lnw_ref[...], lnb_ref[...], ones)

    @pl.when(i == pl.num_programs(1) - 1)
    def _():
        for b in range(n_seq):
            st_ref[b, 0] = st[b]


def _wkv_fused(z_rwkv, shift0, s0, prm, tb):
    n, t, cols = z_rwkv.shape
    width = prm["decay_w0"].shape[-1]
    pairs = width // LANES
    lora = cols - 3 * width
    assert (3 * width) % lora == 0 and lora % LANES == 0
    lblk = 3 * width // lora
    row = lambda x: x.reshape(1, -1)
    mu = row(prm["shift_mu"])
    tok = lambda off: pl.BlockSpec((n, tb, LANES), lambda p, i: (0, i, off + p))
    vec = lambda rows_, off: pl.BlockSpec((rows_, LANES), lambda p, i: (0, off + p))
    st = pl.BlockSpec((n, 1, HEAD, LANES), lambda p, i: (0, p, 0, 0))
    in_specs = [tok(0), tok(pairs), tok(2 * pairs), pl.BlockSpec((n, tb, lora), lambda p, i: (0, i, lblk)),
                vec(n, 0), vec(n, pairs), vec(n, 2 * pairs), pl.BlockSpec((n, lora), lambda p, i: (0, lblk)),
                vec(1, 0), vec(1, pairs), vec(1, 2 * pairs), pl.BlockSpec((1, lora), lambda p, i: (0, lblk)),
                vec(1, 0), vec(prm["decay_w2"].shape[0], 0), vec(1, 0), vec(prm["aaa_a2"].shape[0], 0),
                vec(prm["gate_g2"].shape[0], 0), vec(1, 0), vec(1, 0), vec(1, 0), vec(1, 0), vec(1, 0), st]
    args = [z_rwkv] * 4 + [shift0] * 4 + [mu] * 4 + [
        row(prm["decay_w0"]), prm["decay_w2"], row(prm["aaa_a0"]), prm["aaa_a2"], prm["gate_g2"],
        row(prm["k_k"]), row(prm["k_a"]), row(prm["r_k"]), row(prm["lnx_w"]), row(prm["lnx_b"]), s0]
    return pl.pallas_call(
        _wkv_fused_kernel,
        grid=(pairs, t // tb),
        in_specs=in_specs,
        out_specs=[tok(0), st],
        out_shape=[jax.ShapeDtypeStruct((n, t, width), F32), jax.ShapeDtypeStruct(s0.shape, F32)],
        scratch_shapes=[pltpu.VMEM((n, HEAD, LANES), F32), pltpu.VMEM((n, LANES), F32),
                        pltpu.VMEM((n, LANES), F32), pltpu.VMEM((n, LANES), F32),
                        pltpu.VMEM((n, lora), F32)],
        compiler_params=_cparams(("arbitrary", "arbitrary")),
        name="wkv_fused",
    )(*args)


def _pack_state(s):
    n, h, vd, kd = s.shape
    return s.reshape(n, h // 2, 2, vd, kd).transpose(0, 1, 4, 2, 3).reshape(n, h // 2, kd, 2 * vd)


def _unpack_state(s, h):
    n, _, kd, _ = s.shape
    return s.reshape(n, h // 2, kd, 2, HEAD).transpose(0, 1, 3, 4, 2).reshape(n, h, HEAD, kd)


def _attn_prompt_kernel(q_ref, k_ref, kp_ref, v_ref, vp_ref, o_ref, l_ref, *, dil, sub, n_back, scale):
    first = pl.program_id(1) == 0
    qi = lax.broadcasted_iota(jnp.int32, (ATTN_BLOCK, 2 * ATTN_BLOCK), 0)
    ki = lax.broadcasted_iota(jnp.int32, (ATTN_BLOCK, 2 * ATTN_BLOCK), 1)
    dist = qi + ATTN_BLOCK - ki
    band = (dist >= 0) & (dist <= n_back)
    own = ki >= ATTN_BLOCK

    def rows(j, r):
        if dil == 1:
            return pl.ds(j * ATTN_BLOCK, ATTN_BLOCK)
        return pl.ds(j * ATTN_BLOCK * dil + r, ATTN_BLOCK, stride=dil)

    def unit(j, r):
        q = (q_ref[0, rows(j, r), :] * scale).astype(BF16)
        if j == 0:
            kp, vp = kp_ref[0, rows(0, r), :], vp_ref[0, rows(0, r), :]
            mask = band & (own | jnp.logical_not(first))
        else:
            kp, vp, mask = k_ref[0, rows(j - 1, r), :], v_ref[0, rows(j - 1, r), :], band
        kcat = jnp.concatenate([kp, k_ref[0, rows(j, r), :]], axis=0).astype(BF16)
        vcat = jnp.concatenate([vp, v_ref[0, rows(j, r), :]], axis=0).astype(BF16)
        outs, lses = [], []
        for h in range(LANES // HEAD):
            sl = slice(h * HEAD, (h + 1) * HEAD)
            s = jnp.where(mask, _dot_nt(q[:, sl], kcat[:, sl]), NEG)
            m = jnp.max(s, axis=-1, keepdims=True)
            e = jnp.exp(s - m)
            den = jnp.sum(e, axis=-1, keepdims=True)
            outs.append(_dot(e.astype(BF16), vcat[:, sl]) / den)
            lses.append(jnp.broadcast_to(m + jnp.log(den), (ATTN_BLOCK, HEAD)))
        o_ref[0, rows(j, r), :] = jnp.concatenate(outs, axis=1)
        l_ref[0, rows(j, r), :] = jnp.concatenate(lses, axis=1)

    for j in range(sub):
        if dil <= 4:
            for r in range(dil):
                unit(j, r)
        else:
            def body(r4, carry, j=j):
                for u in range(4):
                    unit(j, r4 * 4 + u)
                return carry
            lax.fori_loop(0, dil // 4, body, 0)


def _attn_prompt(zqkv, gi, span):
    n, t, cols = zqkv.shape
    d = DILATIONS[gi]
    aw = ATTN_HEADS * HEAD
    ng = len(DILATIONS)
    back = ATTN_BLOCK * d
    sub = span // back
    halves = aw // LANES
    col = lambda which, hp: (which * ng + gi) * halves + hp
    cur = lambda which: pl.BlockSpec((1, span, LANES), lambda b, i, hp: (b, i, col(which, hp)))
    prev = lambda which: pl.BlockSpec((1, back, LANES),
                                      lambda b, i, hp: (b, jnp.maximum(i * sub - 1, 0), col(which, hp)))
    out_spec = pl.BlockSpec((1, span, LANES), lambda b, i, hp: (b, i, hp))
    out_sds = jax.ShapeDtypeStruct((n, t, aw), F32)
    return pl.pallas_call(
        functools.partial(_attn_prompt_kernel, dil=d, sub=sub, n_back=WINDOWS[gi] // d, scale=HEAD ** -0.5),
        grid=(n, t // span, halves),
        in_specs=[cur(0), cur(1), prev(1), cur(2), prev(2)],
        out_specs=[out_spec, out_spec],
        out_shape=[out_sds, out_sds],
        compiler_params=_cparams(("arbitrary", "arbitrary", "arbitrary")),
        name=f"attn_prompt_g{gi}",
    )(zqkv, zqkv, zqkv, zqkv, zqkv)


def _attn_sample_kernel(q_ref, k_ref, v_ref, c_ref, o_ref, l_ref, *, dil, n_back, valid, scale):
    q = (q_ref[0] * scale).astype(BF16)
    tp = q.shape[0]
    cl = c_ref.shape[4]
    dist_c = (cl + lax.broadcasted_iota(jnp.int32, (tp, cl), 0)
              - lax.broadcasted_iota(jnp.int32, (tp, cl), 1))
    mask_c = (dist_c % dil == 0) & (dist_c <= dil * n_back)
    pn = lax.broadcasted_iota(jnp.int32, (tp, tp), 1)
    dist_n = lax.broadcasted_iota(jnp.int32, (tp, tp), 0) - pn
    mask_n = (dist_n >= 0) & (dist_n % dil == 0) & (dist_n <= dil * n_back) & (pn < valid)
    outs, lses = [], []
    for h in range(ATTN_HEADS):
        sl = slice(h * HEAD, (h + 1) * HEAD)
        s_c = jnp.where(mask_c, _dot(q[:, sl], c_ref[0, 0, h].astype(BF16)), NEG)
        s_n = jnp.where(mask_n, _dot_nt(q[:, sl], k_ref[0, :, sl].astype(BF16)), NEG)
        m = jnp.maximum(jnp.max(s_c, axis=-1, keepdims=True), jnp.max(s_n, axis=-1, keepdims=True))
        e_c = jnp.exp(s_c - m)
        e_n = jnp.exp(s_n - m)
        den = jnp.sum(e_c, axis=-1, keepdims=True) + jnp.sum(e_n, axis=-1, keepdims=True)
        o = (_dot_nt(e_c.astype(BF16), c_ref[0, 1, h].astype(BF16))
             + _dot(e_n.astype(BF16), v_ref[0, :, sl].astype(BF16)))
        outs.append(o / den)
        lses.append(jnp.broadcast_to(m + jnp.log(den), (tp, HEAD)))
    o_ref[0] = jnp.concatenate(outs, axis=1)
    l_ref[0] = jnp.concatenate(lses, axis=1)


def _attn_sample(zqkv, cache, gi, valid):
    n, tp, cols = zqkv.shape
    aw = ATTN_HEADS * HEAD
    ng = len(DILATIONS)
    d = DILATIONS[gi]
    cache_t = cache.transpose(0, 2, 3, 4, 1)
    spec = lambda which: pl.BlockSpec((1, tp, aw), lambda b: (b, 0, which * ng + gi))
    out_spec = pl.BlockSpec((1, tp, aw), lambda b: (b, 0, 0))
    out_sds = jax.ShapeDtypeStruct((n, tp, aw), F32)
    return pl.pallas_call(
        functools.partial(_attn_sample_kernel, dil=d, n_back=WINDOWS[gi] // d, valid=valid,
                          scale=HEAD ** -0.5),
        grid=(n,),
        in_specs=[spec(0), spec(1), spec(2),
                  pl.BlockSpec((1,) + cache_t.shape[1:], lambda b: (b, 0, 0, 0, 0))],
        out_specs=[out_spec, out_spec],
        out_shape=[out_sds, out_sds],
        compiler_params=_cparams(("arbitrary",)),
        name=f"attn_sample_g{gi}",
    )(zqkv, zqkv, zqkv, cache_t)


def _merge_kernel(x_ref, rwkv_ref, o0, o1, o2, l0, l1, l2, zg_ref, wbr_ref, wba_ref, wout_ref, out_ref):
    rwkv = rwkv_ref[...]
    ls = [l0[...], l1[...], l2[...]]
    m = jnp.maximum(jnp.maximum(ls[0], ls[1]), ls[2])
    es = [jnp.exp(l - m) for l in ls]
    attn = (es[0] * o0[...] + es[1] * o1[...] + es[2] * o2[...]) / (es[0] + es[1] + es[2])

    d = x_ref.shape[1]
    gates = _sigmoid(zg_ref[...])
    mixed = (gates[:, :d] * _dot(rwkv.astype(BF16), wbr_ref[...])
             + gates[:, d:] * _dot(attn.astype(BF16), wba_ref[...]))
    out_ref[...] = x_ref[...] + _dot(mixed.astype(BF16), wout_ref[...])


def _merge(x, rwkv, os_, ls_, zg, wbr, wba, wout, tm):
    rows, d = x.shape
    rowblk = lambda a: pl.BlockSpec((tm, a.shape[1]), lambda i: (i, 0))
    full = lambda a: pl.BlockSpec(a.shape, lambda i: (0, 0))
    args = [x, rwkv, *os_, *ls_, zg, wbr, wba, wout]
    specs = [rowblk(a) for a in (x, rwkv, *os_, *ls_, zg)] + [full(wbr), full(wba), full(wout)]
    return pl.pallas_call(
        _merge_kernel,
        grid=(rows // tm,),
        in_specs=specs,
        out_specs=pl.BlockSpec((tm, d), lambda i: (i, 0)),
        out_shape=jax.ShapeDtypeStruct((rows, d), F32),
        compiler_params=_cparams(("arbitrary",)),
        name="merge_out",
    )(*args)


def _mlp_ple_kernel(x_ref, g_ref, w1_ref, w2_ref, p_ref, gp_ref, wg_ref, wp_ref, gf_ref, o_ref,
                    h_scr, acc_scr, *, final):
    j = pl.program_id(1)

    @pl.when(j == 0)
    def _():
        h_scr[...] = _rms(x_ref[...], g_ref[...]).astype(BF16)
        acc_scr[...] = x_ref[...]

    u = jnp.maximum(_dot(h_scr[...], w1_ref[...]), 0.0)
    acc_scr[...] += _dot((u * u).astype(BF16), w2_ref[...])

    @pl.when(j == pl.num_programs(1) - 1)
    def _():
        x = acc_scr[...]
        h = _rms(x, gp_ref[...]).astype(BF16)
        x = x + _sigmoid(_dot(h, wg_ref[...])) * _dot(p_ref[...].astype(BF16), wp_ref[...])
        if final:
            x = _rms(x, gf_ref[...])
        o_ref[...] = x


def _mlp_ple(x, g, w1, w2, p, gp, wg, wp, gf, final, tm, tf):
    rows, d = x.shape
    dff = w1.shape[1]
    vec = lambda a: a.reshape(1, d)
    full = lambda a: pl.BlockSpec(a.shape, lambda i, j: (0, 0))
    return pl.pallas_call(
        functools.partial(_mlp_ple_kernel, final=final),
        grid=(rows // tm, dff // tf),
        in_specs=[pl.BlockSpec((tm, d), lambda i, j: (i, 0)),
                  full(vec(g)),
                  pl.BlockSpec((d, tf), lambda i, j: (0, j)),
                  pl.BlockSpec((tf, d), lambda i, j: (j, 0)),
                  pl.BlockSpec((tm, p.shape[1]), lambda i, j: (i, 0)),
                  full(vec(gp)), full(wg), full(wp), full(vec(gf))],
        out_specs=pl.BlockSpec((tm, d), lambda i, j: (i, 0)),
        out_shape=jax.ShapeDtypeStruct((rows, d), F32),
        scratch_shapes=[pltpu.VMEM((tm, d), BF16), pltpu.VMEM((tm, d), F32)],
        compiler_params=_cparams(("arbitrary", "arbitrary")),
        name="mlp_ple",
    )(x, vec(g), w1, w2, p, vec(gp), wg, wp, vec(gf))


def _layer(x, p, wkv0, shift0, caches, prm, norm_final, final, valid):
    n, t, d = x.shape
    rows = n * t
    width = prm["decay_w0"].shape[-1]
    rwkv_cols = prm["shift_mu"].shape[-1]
    aw = ATTN_HEADS * HEAD
    qkv_cols = 3 * len(DILATIONS) * aw
    tm = min(512, rows)
    x2 = x.reshape(rows, d)

    pick = lambda cols, opts: next(o for o in opts if cols % o == 0)
    z_rwkv, z_qkv, z_gate = _in_proj(x2, prm["norm_mix"], prm["w_in"].astype(BF16),
                                     (rwkv_cols, qkv_cols, 2 * d), min(256, rows))
    z_rwkv = z_rwkv.reshape(n, t, rwkv_cols)
    z_qkv = z_qkv.reshape(n, t, qkv_cols)

    if valid == t and t % WKV_CHUNK == 0:
        rwkv, s_t = _wkv_fused(z_rwkv, shift0, _pack_state(wkv0), prm, pick(t, (512, 256, WKV_CHUNK)))
        wkv_t = _unpack_state(s_t, width // HEAD)
    else:
        feats = _rwkv_prep(z_rwkv, shift0, prm, min(256, t))
        rwkv, wkv_t = _wkv(*feats, prm["lnx_w"], prm["lnx_b"], wkv0, valid)
    shift_t = z_rwkv[:, valid - 1, :]

    os_, ls_, new_kv = [], [], []
    ng = len(DILATIONS)
    for gi in range(ng):
        if caches is None:
            o, l = _attn_prompt(z_qkv, gi, max(1024, ATTN_BLOCK * DILATIONS[gi]))
            keep = min(WINDOWS[gi], t)
            kv_rows = z_qkv[:, t - keep:t]
        else:
            o, l = _attn_sample(z_qkv, caches[gi], gi, valid)
            kv_rows = z_qkv[:, :valid]
        new_kv.append(jnp.stack([kv_rows[:, :, (which * ng + gi) * aw:(which * ng + gi + 1) * aw]
                                 .reshape(n, -1, ATTN_HEADS, HEAD) for which in (1, 2)], axis=2))
        os_.append(o.reshape(rows, aw))
        ls_.append(l.reshape(rows, aw))

    flat = lambda u: u.reshape(rows, -1)
    x2 = _merge(x2, flat(rwkv), os_, ls_, z_gate,
                prm["w_branch_rwkv"].astype(BF16), prm["w_branch_attn"].astype(BF16),
                prm["w_out"].astype(BF16), min(256, rows))
    x2 = _mlp_ple(x2, prm["norm_mlp"], prm["w_ff1"].astype(BF16), prm["w_ff2"].astype(BF16),
                  p.reshape(rows, -1), prm["norm_ple"], prm["w_ple_gate"].astype(BF16),
                  prm["w_ple"].astype(BF16), norm_final, final, tm, 1024)
    return x2.reshape(n, t, d), new_kv, wkv_t, shift_t


_LAYER_PARAMS = ("norm_mix", "w_in", "shift_mu", "decay_w0", "decay_w2", "aaa_a0", "aaa_a2", "gate_g2",
                 "k_k", "k_a", "r_k", "lnx_w", "lnx_b", "w_branch_rwkv", "w_branch_attn", "w_out",
                 "norm_mlp", "w_ff1", "w_ff2", "norm_ple", "w_ple", "w_ple_gate")


def kernel(x_prompt, x_sample, cache_kv_g0, cache_kv_g1, cache_kv_g2, state_wkv, state_shift, p_prompt, p_sample, norm_mix, w_in, shift_mu, decay_w0, decay_w2, aaa_a0, aaa_a2, gate_g2, k_k, k_a, r_k, lnx_w, lnx_b, w_branch_rwkv, w_branch_attn, w_out, norm_mlp, w_ff1, w_ff2, norm_ple, w_ple, w_ple_gate, norm_final):
    stacked = dict(zip(_LAYER_PARAMS, (norm_mix, w_in, shift_mu, decay_w0, decay_w2, aaa_a0, aaa_a2,
                                       gate_g2, k_k, k_a, r_k, lnx_w, lnx_b, w_branch_rwkv,
                                       w_branch_attn, w_out, norm_mlp, w_ff1, w_ff2, norm_ple, w_ple,
                                       w_ple_gate)))
    depth = norm_mix.shape[0]
    nb_p, _, d = x_prompt.shape
    nb_s, t_s, _ = x_sample.shape
    heads = decay_w0.shape[-1] // HEAD
    rwkv_cols = shift_mu.shape[-1]
    t_pad = -(-t_s // SUBLANES) * SUBLANES
    pad_t = lambda u: jnp.pad(u, ((0, 0), (0, t_pad - t_s), (0, 0)))

    yp, ys = x_prompt, pad_t(x_sample)
    kvp, kvs = [[], [], []], [[], [], []]
    wkv_p, sh_p, wkv_s, sh_s = [], [], [], []
    for i in range(depth):
        prm = {name: val[i] for name, val in stacked.items()}
        prm["r_k"] = prm["r_k"].reshape(-1)
        last = i == depth - 1
        wkv0 = jnp.zeros((nb_p, heads, HEAD, HEAD), F32)
        shift0 = jnp.zeros((nb_p, rwkv_cols), F32)
        yp, nkv_p, w_p, s_p = _layer(yp, p_prompt[i], wkv0, shift0, None, prm, norm_final, last,
                                     x_prompt.shape[1])
        ys, nkv_s, w_s, s_s = _layer(ys, pad_t(p_sample[i]), state_wkv[i], state_shift[i],
                                     (cache_kv_g0[i], cache_kv_g1[i], cache_kv_g2[i]), prm, norm_final,
                                     last, t_s)
        for gi in range(3):
            kvp[gi].append(nkv_p[gi])
            kvs[gi].append(nkv_s[gi])
        wkv_p.append(w_p)
        sh_p.append(s_p)
        wkv_s.append(w_s)
        sh_s.append(s_s)
    return (yp, ys[:, :t_s],
            jnp.stack(kvp[0]), jnp.stack(kvp[1]), jnp.stack(kvp[2]),
            jnp.stack(wkv_p), jnp.stack(sh_p),
            jnp.stack(kvs[0]), jnp.stack(kvs[1]), jnp.stack(kvs[2]),
            jnp.stack(wkv_s), jnp.stack(sh_s))
```

```python
import functools

import jax
import jax.numpy as jnp
from jax import lax
from jax.experimental import pallas as pl
from jax.experimental.pallas import tpu as pltpu

F32 = jnp.float32
BF16 = jnp.bfloat16

HEAD = 64
LANES = 128
SUBLANES = 8
ATTN_HEADS = 4
ATTN_BLOCK = 128
WINDOWS = (128, 512, 2048)
DILATIONS = (1, 4, 16)
NORM_EPS = 1e-6
LNX_EPS = 64e-5
NEG = -1e30
VMEM_LIMIT = 56 * 1024 * 1024


def _cparams(sem):
    return pltpu.CompilerParams(dimension_semantics=sem, vmem_limit_bytes=VMEM_LIMIT)


def _dot(a, b):
    return jnp.dot(a, b, preferred_element_type=F32)


def _dot_bf16(a, b):
    return _dot(a.astype(BF16), b.astype(BF16))


def _pair_ones():
    r = lax.broadcasted_iota(jnp.int32, (LANES, LANES), 0) // HEAD
    c = lax.broadcasted_iota(jnp.int32, (LANES, LANES), 1) // HEAD
    return (r == c).astype(BF16)


def _head_sum(x, ones):
    outs = []
    for c in range(x.shape[1] // LANES):
        outs.append(_dot(x[:, c * LANES:(c + 1) * LANES].astype(BF16), ones))
    return jnp.concatenate(outs, axis=1)


def _rms(x, g):
    return x * lax.rsqrt(jnp.mean(x * x, axis=-1, keepdims=True) + NORM_EPS) * g


def _sigmoid(x):
    return 1.0 / (1.0 + jnp.exp(-x))


def _softplus(x):
    return jnp.maximum(x, 0.0) + jnp.log(1.0 + jnp.exp(-jnp.abs(x)))


def _in_proj_kernel(x_ref, g_ref, w_ref, zr_ref, zq_ref, gate_ref):
    h = _rms(x_ref[...], g_ref[...]).astype(BF16)
    c1 = zr_ref.shape[1]
    c2 = c1 + zq_ref.shape[1]
    zr_ref[...] = _dot(h, w_ref[:, :c1])
    zq_ref[...] = _dot(h, w_ref[:, c1:c2])
    gate_ref[...] = _sigmoid(_dot(h, w_ref[:, c2:])).astype(BF16)


def _in_proj(x, g, w, splits, tm):
    rows, d = x.shape
    assert sum(splits) == w.shape[1] and len(splits) == 3
    return pl.pallas_call(
        _in_proj_kernel,
        grid=(rows // tm,),
        in_specs=[pl.BlockSpec((tm, d), lambda i: (i, 0)),
                  pl.BlockSpec((1, d), lambda i: (0, 0)),
                  pl.BlockSpec(w.shape, lambda i: (0, 0), pipeline_mode=pl.Buffered(1))],
        out_specs=[pl.BlockSpec((tm, c), lambda i: (i, 0)) for c in splits],
        out_shape=[jax.ShapeDtypeStruct((rows, c), dt) for c, dt in zip(splits, (F32, F32, BF16))],
        compiler_params=_cparams(("arbitrary",)),
        name="in_proj",
    )(x, g.reshape(1, d), w)


def _rwkv_features(r, k, v, xw, xa, xg, w0, w2, a0, a2, g2, k_k, k_a, r_k, ones):
    logw = -_softplus(-(w0 + _dot_bf16(jnp.tanh(xw), w2))) - 0.5
    a = _sigmoid(a0 + _dot_bf16(xa, a2))
    g = _dot_bf16(_sigmoid(xg), g2)
    kk = k * k_k
    kk = kk / jnp.maximum(jnp.sqrt(_head_sum(kk * kk, ones)), 1e-12)
    k2 = k * (1.0 + (a - 1.0) * k_a)
    bonus = _head_sum(r * k2 * r_k, ones) * v
    return -jnp.exp(logw), k2, -kk, kk * a, g, bonus


def _rwkv_out(y, bonus, g, lnx_w, lnx_b, ones):
    mu = _head_sum(y, ones) * (1.0 / HEAD)
    yc = y - mu
    var = _head_sum(yc * yc, ones) * (1.0 / HEAD)
    return (yc * lax.rsqrt(var + LNX_EPS) * lnx_w + lnx_b + bonus) * g


def _rwkv_prep_kernel(z_ref, sh_ref, mu_ref, w0_ref, w2_ref, a0_ref, a2_ref, g2_ref, kk_ref, ka_ref,
                      rk_ref, r_o, w_o, k_o, v_o, a_o, b_o, g_o, bon_o, last_scr, *, width):
    n = pl.program_id(0)
    i = pl.program_id(1)
    z = z_ref[0]
    tt = z.shape[0]

    @pl.when(i == 0)
    def _():
        last_scr[...] = sh_ref[pl.ds(n, 1), :]

    row = lax.broadcasted_iota(jnp.int32, z.shape, 0)
    prev = jnp.where(row == 0, last_scr[...], pltpu.roll(z, 1, axis=0))
    last_scr[...] = z[tt - 1:tt, :]
    zs = z + (prev - z) * mu_ref[...]

    c0 = 3 * width
    c1 = c0 + w2_ref.shape[0]
    c2 = c1 + a2_ref.shape[0]
    r = zs[:, 0:width]
    k = zs[:, width:2 * width]
    v = zs[:, 2 * width:c0]
    xw = zs[:, c0:c1]
    xa = zs[:, c1:c2]
    xg = zs[:, c2:]

    lw, k2, a_neg, b, g, bonus = _rwkv_features(
        r, k, v, xw, xa, xg, w0_ref[...], w2_ref[...], a0_ref[...], a2_ref[...], g2_ref[...],
        kk_ref[...], ka_ref[...], rk_ref[...], _pair_ones())
    r_o[0] = r
    w_o[0] = lw
    k_o[0] = k2
    v_o[0] = v
    a_o[0] = a_neg
    b_o[0] = b
    g_o[0] = g
    bon_o[0] = bonus


def _rwkv_prep(z_rwkv, shift0, prm, tt):
    n, t, cols = z_rwkv.shape
    width = prm["decay_w0"].shape[-1]
    row = lambda x: x.reshape(1, -1)
    full = lambda a: pl.BlockSpec(a.shape, lambda b, i: (0,) * a.ndim)
    args = [shift0, row(prm["shift_mu"]), row(prm["decay_w0"]), prm["decay_w2"], row(prm["aaa_a0"]),
            prm["aaa_a2"], prm["gate_g2"], row(prm["k_k"]), row(prm["k_a"]), row(prm["r_k"])]
    out_spec = pl.BlockSpec((1, tt, width), lambda b, i: (b, i, 0))
    out_sds = jax.ShapeDtypeStruct((n, t, width), F32)
    return pl.pallas_call(
        functools.partial(_rwkv_prep_kernel, width=width),
        grid=(n, t // tt),
        in_specs=[pl.BlockSpec((1, tt, cols), lambda b, i: (b, i, 0))] + [full(a) for a in args],
        out_specs=[out_spec] * 8,
        out_shape=[out_sds] * 8,
        scratch_shapes=[pltpu.VMEM((1, cols), F32)],
        compiler_params=_cparams(("arbitrary", "arbitrary")),
        name="rwkv_prep",
    )(z_rwkv, *args)


def _wkv_kernel(r_ref, w_ref, k_ref, v_ref, a_ref, b_ref, g_ref, bon_ref, lnw_ref, lnb_ref, s0_ref,
                y_ref, st_ref, *, valid):
    ones = _pair_ones()
    row = lax.broadcasted_iota(jnp.int32, (HEAD, LANES), 0)
    lane = lax.broadcasted_iota(jnp.int32, (HEAD, LANES), 1)
    left = lane < HEAD
    diag = (row == lane % HEAD).astype(F32)

    def key_sum(x):
        hi = x.astype(BF16)
        r1 = x - hi.astype(F32)
        mid = r1.astype(BF16)
        lo = (r1 - mid.astype(F32)).astype(BF16)
        return _dot(hi, ones) + (_dot(mid, ones) + _dot(lo, ones))

    chains = [(b, p) for b in range(y_ref.shape[0]) for p in range(s0_ref.shape[1] // 2)]
    cols = lambda p: slice(p * LANES, (p + 1) * LANES)
    vt = [v_ref[b, :, cols(p)].T for b, p in chains]
    s = [jnp.concatenate([s0_ref[b, 2 * p], s0_ref[b, 2 * p + 1]], axis=1) for b, p in chains]
    ys = [[] for _ in chains]
    for j in range(y_ref.shape[1]):
        for ch, (b, p) in enumerate(chains):
            if j < valid:
                row_of = lambda ref: ref[b, j:j + 1, cols(p)]
                v_col = jnp.where(left, vt[ch][0:HEAD, j:j + 1], vt[ch][HEAD:2 * HEAD, j:j + 1])
                sa = key_sum(s[ch] * row_of(a_ref))
                s[ch] = s[ch] * jnp.exp(row_of(w_ref)) + sa * row_of(b_ref) + v_col * row_of(k_ref)
                y_col = key_sum(s[ch] * row_of(r_ref))
                ys[ch].append(jnp.sum(y_col * diag, axis=0, keepdims=True))
            else:
                ys[ch].append(jnp.zeros((1, LANES), F32))
    for ch, (b, p) in enumerate(chains):
        y_ref[b, :, cols(p)] = _rwkv_out(jnp.concatenate(ys[ch], axis=0), bon_ref[b, :, cols(p)],
                                         g_ref[b, :, cols(p)], lnw_ref[:, cols(p)], lnb_ref[:, cols(p)], ones)
        st_ref[b, 2 * p] = s[ch][:, :HEAD]
        st_ref[b, 2 * p + 1] = s[ch][:, HEAD:]


def _wkv(r, w, k, v, a, b, g, bonus, lnx_w, lnx_b, s0, valid):
    n, t, c = r.shape
    assert t == SUBLANES
    nb = 2 if n % 2 == 0 else 1
    tok = pl.BlockSpec((nb, t, c), lambda bi: (bi, 0, 0))
    vec = pl.BlockSpec((1, c), lambda bi: (0, 0))
    st = pl.BlockSpec((nb,) + s0.shape[1:], lambda bi: (bi, 0, 0, 0))
    return pl.pallas_call(
        functools.partial(_wkv_kernel, valid=valid),
        grid=(n // nb,),
        in_specs=[tok] * 8 + [vec, vec, st],
        out_specs=[tok, st],
        out_shape=[jax.ShapeDtypeStruct((n, t, c), F32), jax.ShapeDtypeStruct(s0.shape, F32)],
        compiler_params=_cparams(("arbitrary",)),
        name="wkv_scan",
    )(r, w, k, v, a, b, g, bonus, lnx_w.reshape(1, c), lnx_b.reshape(1, c), s0)


WKV_CHUNK = 64


def _bd(x):
    xb = x.astype(BF16)
    left = lax.broadcasted_iota(jnp.int32, x.shape, 1) < HEAD
    zero = jnp.zeros_like(xb)
    return jnp.concatenate([jnp.where(left, xb, zero), jnp.where(left, zero, xb)], axis=0)


def _dot_nt(a, b):
    return lax.dot_general(a, b, (((1,), (1,)), ((), ())), preferred_element_type=F32)


def _dot_tn(a, b):
    return lax.dot_general(a, b, (((0,), (0,)), ((), ())), preferred_element_type=F32)


def _wkv_chunk_local(r, lw, k, v, a, b, consts):
    tri, eye2, strict, incl, blocks = consts
    c = WKV_CHUNK
    n = range(len(r))
    zero = jnp.zeros((c, LANES), F32)

    def cumsum(x):
        hi = x.astype(BF16)
        r1 = x - hi.astype(F32)
        mid = r1.astype(BF16)
        lo = (r1 - mid.astype(F32)).astype(BF16)
        return _dot(tri, hi) + (_dot(tri, mid) + _dot(tri, lo))

    lc = [cumsum(lw[i]) for i in n]
    ltot = [lc[i][c - 1:c, :] for i in n]
    w_inv = [jnp.exp(-lc[i]) for i in n]
    ah = [a[i] * jnp.exp(lc[i] - lw[i]) for i in n]
    rh = [r[i] * jnp.exp(lc[i]) for i in n]
    sc = [_dot_nt(jnp.concatenate([ah[i], rh[i]], axis=0).astype(BF16),
                  jnp.concatenate([_bd(b[i] * w_inv[i]), _bd(k[i] * w_inv[i])], axis=0)) for i in n]
    aab = [jnp.where(strict, sc[i][:c, :LANES], zero) for i in n]

    t = [eye2 + jnp.where(blocks[0], aab[i], zero) for i in n]
    for lvl in range(1, len(blocks)):
        grow = blocks[lvl] & ~blocks[lvl - 1]
        x = [_dot(t[i].astype(BF16), _bd(jnp.where(grow, aab[i], zero))) for i in n]
        t = [t[i] + _dot(x[i].astype(BF16), _bd(t[i])) for i in n]

    vb = [_bd(v[i]) for i in n]
    akv = [_dot(jnp.where(strict, sc[i][:c, LANES:], zero).astype(BF16), vb[i]) for i in n]
    pu = [_dot(t[i].astype(BF16), jnp.concatenate([_bd(ah[i]), _bd(akv[i])], axis=1)) for i in n]

    left2 = (lax.broadcasted_iota(jnp.int32, (HEAD, 2 * LANES), 1) % LANES) < HEAD
    g, h, q, y0 = [], [], [], []
    for i in n:
        w_rem = jnp.exp(ltot[i] - lc[i])
        lhs_t = jnp.concatenate([b[i] * w_rem, k[i] * w_rem], axis=0).astype(BF16)
        rhs = jnp.concatenate([pu[i], jnp.concatenate([zero, v[i]], axis=1)], axis=0).astype(BF16)
        full = _dot_tn(lhs_t, rhs)
        gh = jnp.where(left2, full[:HEAD], full[HEAD:])
        g.append(gh[:, :LANES] + eye2 * jnp.exp(ltot[i]))
        h.append(gh[:, LANES:])
    zero_b = jnp.zeros((2 * c, LANES), BF16)
    for i in n:
        mrb = jnp.where(incl, sc[i][c:, :LANES], zero)
        mrk = jnp.where(incl, sc[i][c:, LANES:], zero)
        rhs2 = jnp.concatenate([jnp.concatenate([_bd(pu[i][:, :LANES]), _bd(pu[i][:, LANES:])], axis=1),
                                jnp.concatenate([zero_b, vb[i]], axis=1)], axis=0)
        qy = _dot(jnp.concatenate([mrb, mrk], axis=1).astype(BF16), rhs2)
        q.append(rh[i] + qy[:, :LANES])
        y0.append(qy[:, LANES:])
    return g, h, q, y0


def _wkv_fused_kernel(zr_ref, zk_ref, zv_ref, zl_ref, sr_ref, sk_ref, sv_ref, sl_ref,
                      mr_ref, mk_ref, mv_ref, ml_ref, w0_ref, w2_ref, a0_ref, a2_ref, g2_ref,
                      kk_ref, ka_ref, rk_ref, lnw_ref, lnb_ref, s0_ref, out_ref, st_ref,
                      s_scr, pr_scr, pk_scr, pv_scr, pl_scr):
    i = pl.program_id(1)
    c = WKV_CHUNK
    n_seq, tb = out_ref.shape[0], out_ref.shape[1]

    @pl.when(i == 0)
    def _():
        s_scr[...] = s0_ref[...]
        pr_scr[...] = sr_ref[...]
        pk_scr[...] = sk_ref[...]
        pv_scr[...] = sv_ref[...]
        pl_scr[...] = sl_ref[...]

    def shifted(z_ref, p_scr, mu_ref, b):
        z = z_ref[b]
        first = lax.broadcasted_iota(jnp.int32, z.shape, 0) == 0
        prev = jnp.where(first, p_scr[b:b + 1, :], pltpu.roll(z, 1, axis=0))
        p_scr[b:b + 1, :] = z[tb - 1:tb, :]
        return z + (prev - z) * mu_ref[...]

    ones = _pair_ones()
    c1 = w2_ref.shape[0]
    c2 = c1 + a2_ref.shape[0]
    feats = []
    for b in range(n_seq):
        r = shifted(zr_ref, pr_scr, mr_ref, b)
        k = shifted(zk_ref, pk_scr, mk_ref, b)
        v = shifted(zv_ref, pv_scr, mv_ref, b)
        zl = shifted(zl_ref, pl_scr, ml_ref, b)
        lw, k2, a_neg, bb, gate, bonus = _rwkv_features(
            r, k, v, zl[:, :c1], zl[:, c1:c2], zl[:, c2:], w0_ref[...], w2_ref[...], a0_ref[...],
            a2_ref[...], g2_ref[...], kk_ref[...], ka_ref[...], rk_ref[...], ones)
        feats.append((r, lw, k2, v, a_neg, bb, gate, bonus))

    ti = lax.broadcasted_iota(jnp.int32, (c, LANES), 0)
    si = lax.broadcasted_iota(jnp.int32, (c, LANES), 1) % HEAD
    tri = (lax.broadcasted_iota(jnp.int32, (c, c), 0) >= lax.broadcasted_iota(jnp.int32, (c, c), 1)
           ).astype(BF16)
    eye2 = (ti == si).astype(F32)
    sizes = [2 << l for l in range(c.bit_length() - 1)]
    blocks = [(ti // s) == (si // s) for s in sizes]
    consts = (tri, eye2, ti > si, ti >= si, blocks)

    n_chunks = tb // c
    chains = [(b, sp) for b in range(n_seq) for sp in range(out_ref.shape[2] // LANES)]
    chunks = lambda which: [feats[b][which][ci * c:(ci + 1) * c, sp * LANES:(sp + 1) * LANES]
                            for ci in range(n_chunks) for b, sp in chains]
    g, h, q, y0 = _wkv_chunk_local(*(chunks(which) for which in range(6)), consts)
    st = [s_scr[b, sp] for b, sp in chains]
    ys = [[] for _ in chains]
    for ci in range(n_chunks):
        for ch in range(len(chains)):
            u = ci * len(chains) + ch
            out = _dot(jnp.concatenate([q[u], g[u]], axis=0).astype(BF16), _bd(st[ch]))
            ys[ch].append(out[:c] + y0[u])
            st[ch] = out[c:] + h[u]
    for ch, (b, sp) in enumerate(chains):
        s_scr[b, sp] = st[ch]
    for b in range(n_seq):
        y = jnp.concatenate([jnp.concatenate(ys[ch], axis=0) for ch, (bb_, _) in enumerate(chains) if bb_ == b],
                            axis=1)
        out_ref[b] = _rwkv_out(y, feats[b][7], feats[b][6], lnw_ref[...], lnb_ref[...], ones)

    @pl.when(i == pl.num_programs(1) - 1)
    def _():
        for ch, (b, sp) in enumerate(chains):
            st_ref[b, sp] = st[ch]


def _wkv_fused(z_rwkv, shift0, s0, prm, tb, pp):
    n, t, cols = z_rwkv.shape
    width = prm["decay_w0"].shape[-1]
    bw = pp * LANES
    nblk = width // bw
    lora = cols - 3 * width
    assert width % bw == 0 and (3 * width) % lora == 0 and lora % LANES == 0
    lblk = 3 * width // lora
    row = lambda x: x.reshape(1, -1)
    mu = row(prm["shift_mu"])
    tok = lambda off: pl.BlockSpec((n, tb, bw), lambda p, i: (0, i, off + p))
    vec = lambda rows_, off: pl.BlockSpec((rows_, bw), lambda p, i: (0, off + p))
    st = pl.BlockSpec((n, pp, HEAD, LANES), lambda p, i: (0, p, 0, 0))
    in_specs = [tok(0), tok(nblk), tok(2 * nblk), pl.BlockSpec((n, tb, lora), lambda p, i: (0, i, lblk)),
                vec(n, 0), vec(n, nblk), vec(n, 2 * nblk), pl.BlockSpec((n, lora), lambda p, i: (0, lblk)),
                vec(1, 0), vec(1, nblk), vec(1, 2 * nblk), pl.BlockSpec((1, lora), lambda p, i: (0, lblk)),
                vec(1, 0), vec(prm["decay_w2"].shape[0], 0), vec(1, 0), vec(prm["aaa_a2"].shape[0], 0),
                vec(prm["gate_g2"].shape[0], 0), vec(1, 0), vec(1, 0), vec(1, 0), vec(1, 0), vec(1, 0), st]
    args = [z_rwkv] * 4 + [shift0] * 4 + [mu] * 4 + [
        row(prm["decay_w0"]), prm["decay_w2"], row(prm["aaa_a0"]), prm["aaa_a2"], prm["gate_g2"],
        row(prm["k_k"]), row(prm["k_a"]), row(prm["r_k"]), row(prm["lnx_w"]), row(prm["lnx_b"]), s0]
    return pl.pallas_call(
        _wkv_fused_kernel,
        grid=(nblk, t // tb),
        in_specs=in_specs,
        out_specs=[tok(0), st],
        out_shape=[jax.ShapeDtypeStruct((n, t, width), F32), jax.ShapeDtypeStruct(s0.shape, F32)],
        scratch_shapes=[pltpu.VMEM((n, pp, HEAD, LANES), F32), pltpu.VMEM((n, bw), F32),
                        pltpu.VMEM((n, bw), F32), pltpu.VMEM((n, bw), F32),
                        pltpu.VMEM((n, lora), F32)],
        compiler_params=_cparams(("arbitrary", "arbitrary")),
        name="wkv_fused",
    )(*args)


def _pack_state(s):
    n, h, vd, kd = s.shape
    return s.reshape(n, h // 2, 2, vd, kd).transpose(0, 1, 4, 2, 3).reshape(n, h // 2, kd, 2 * vd)


def _unpack_state(s, h):
    n, _, kd, _ = s.shape
    return s.reshape(n, h // 2, kd, 2, HEAD).transpose(0, 1, 3, 4, 2).reshape(n, h, HEAD, kd)


def _attn_prompt_kernel(q_ref, k_ref, kp_ref, v_ref, vp_ref, o_ref, l_ref, *, dil, sub, n_back, scale):
    first = pl.program_id(1) == 0
    qi = lax.broadcasted_iota(jnp.int32, (ATTN_BLOCK, 2 * ATTN_BLOCK), 0)
    ki = lax.broadcasted_iota(jnp.int32, (ATTN_BLOCK, 2 * ATTN_BLOCK), 1)
    dist = qi + ATTN_BLOCK - ki
    band = (dist >= 0) & (dist <= n_back)
    own = ki >= ATTN_BLOCK

    def rows(j, r):
        if dil == 1:
            return pl.ds(j * ATTN_BLOCK, ATTN_BLOCK)
        return pl.ds(j * ATTN_BLOCK * dil + r, ATTN_BLOCK, stride=dil)

    def unit(j, r):
        q = (q_ref[0, rows(j, r), :] * scale).astype(BF16)
        if j == 0:
            kp, vp = kp_ref[0, rows(0, r), :], vp_ref[0, rows(0, r), :]
            mask = band & (own | jnp.logical_not(first))
        else:
            kp, vp, mask = k_ref[0, rows(j - 1, r), :], v_ref[0, rows(j - 1, r), :], band
        kcat = jnp.concatenate([kp, k_ref[0, rows(j, r), :]], axis=0).astype(BF16)
        vcat = jnp.concatenate([vp, v_ref[0, rows(j, r), :]], axis=0).astype(BF16)
        outs, lses = [], []
        for h in range(LANES // HEAD):
            sl = slice(h * HEAD, (h + 1) * HEAD)
            s = jnp.where(mask, _dot_nt(q[:, sl], kcat[:, sl]), NEG)
            m = jnp.max(s, axis=-1, keepdims=True)
            e = jnp.exp(s - m)
            den = jnp.sum(e, axis=-1, keepdims=True)
            outs.append(_dot(e.astype(BF16), vcat[:, sl]) / den)
            lses.append(jnp.broadcast_to(m + jnp.log(den), (ATTN_BLOCK, HEAD)))
        o_ref[0, rows(j, r), :] = jnp.concatenate(outs, axis=1)
        l_ref[0, rows(j, r), :] = jnp.concatenate(lses, axis=1)

    for j in range(sub):
        if dil <= 4:
            for r in range(dil):
                unit(j, r)
        else:
            def body(r4, carry, j=j):
                for u in range(4):
                    unit(j, r4 * 4 + u)
                return carry
            lax.fori_loop(0, dil // 4, body, 0)


def _attn_prompt(zqkv, gi, span):
    n, t, cols = zqkv.shape
    d = DILATIONS[gi]
    aw = ATTN_HEADS * HEAD
    ng = len(DILATIONS)
    back = ATTN_BLOCK * d
    sub = span // back
    halves = aw // LANES
    col = lambda which, hp: (which * ng + gi) * halves + hp
    cur = lambda which: pl.BlockSpec((1, span, LANES), lambda b, i, hp: (b, i, col(which, hp)))
    prev = lambda which: pl.BlockSpec((1, back, LANES),
                                      lambda b, i, hp: (b, jnp.maximum(i * sub - 1, 0), col(which, hp)))
    out_spec = pl.BlockSpec((1, span, LANES), lambda b, i, hp: (b, i, hp))
    out_sds = jax.ShapeDtypeStruct((n, t, aw), F32)
    return pl.pallas_call(
        functools.partial(_attn_prompt_kernel, dil=d, sub=sub, n_back=WINDOWS[gi] // d, scale=HEAD ** -0.5),
        grid=(n, t // span, halves),
        in_specs=[cur(0), cur(1), prev(1), cur(2), prev(2)],
        out_specs=[out_spec, out_spec],
        out_shape=[out_sds, out_sds],
        compiler_params=_cparams(("arbitrary", "arbitrary", "arbitrary")),
        name=f"attn_prompt_g{gi}",
    )(zqkv, zqkv, zqkv, zqkv, zqkv)


def _attn_sample_kernel(q_ref, k_ref, v_ref, c_ref, o_ref, l_ref, *, dil, n_back, valid, scale):
    q = (q_ref[0] * scale).astype(BF16)
    tp = q.shape[0]
    cl = c_ref.shape[4]
    dist_c = (cl + lax.broadcasted_iota(jnp.int32, (tp, cl), 0)
              - lax.broadcasted_iota(jnp.int32, (tp, cl), 1))
    mask_c = (dist_c % dil == 0) & (dist_c <= dil * n_back)
    pn = lax.broadcasted_iota(jnp.int32, (tp, tp), 1)
    dist_n = lax.broadcasted_iota(jnp.int32, (tp, tp), 0) - pn
    mask_n = (dist_n >= 0) & (dist_n % dil == 0) & (dist_n <= dil * n_back) & (pn < valid)
    outs, lses = [], []
    for h in range(ATTN_HEADS):
        sl = slice(h * HEAD, (h + 1) * HEAD)
        s_c = jnp.where(mask_c, _dot(q[:, sl], c_ref[0, 0, h].astype(BF16)), NEG)
        s_n = jnp.where(mask_n, _dot_nt(q[:, sl], k_ref[0, :, sl].astype(BF16)), NEG)
        m = jnp.maximum(jnp.max(s_c, axis=-1, keepdims=True), jnp.max(s_n, axis=-1, keepdims=True))
        e_c = jnp.exp(s_c - m)
        e_n = jnp.exp(s_n - m)
        den = jnp.sum(e_c, axis=-1, keepdims=True) + jnp.sum(e_n, axis=-1, keepdims=True)
        o = (_dot_nt(e_c.astype(BF16), c_ref[0, 1, h].astype(BF16))
             + _dot(e_n.astype(BF16), v_ref[0, :, sl].astype(BF16)))
        outs.append(o / den)
        lses.append(jnp.broadcast_to(m + jnp.log(den), (tp, HEAD)))
    o_ref[0] = jnp.concatenate(outs, axis=1)
    l_ref[0] = jnp.concatenate(lses, axis=1)


def _attn_sample(zqkv, cache, gi, valid):
    n, tp, cols = zqkv.shape
    aw = ATTN_HEADS * HEAD
    ng = len(DILATIONS)
    d = DILATIONS[gi]
    cache_t = cache.transpose(0, 2, 3, 4, 1)
    spec = lambda which: pl.BlockSpec((1, tp, aw), lambda b: (b, 0, which * ng + gi))
    out_spec = pl.BlockSpec((1, tp, aw), lambda b: (b, 0, 0))
    out_sds = jax.ShapeDtypeStruct((n, tp, aw), F32)
    return pl.pallas_call(
        functools.partial(_attn_sample_kernel, dil=d, n_back=WINDOWS[gi] // d, valid=valid,
                          scale=HEAD ** -0.5),
        grid=(n,),
        in_specs=[spec(0), spec(1), spec(2),
                  pl.BlockSpec((1,) + cache_t.shape[1:], lambda b: (b, 0, 0, 0, 0))],
        out_specs=[out_spec, out_spec],
        out_shape=[out_sds, out_sds],
        compiler_params=_cparams(("arbitrary",)),
        name=f"attn_sample_g{gi}",
    )(zqkv, zqkv, zqkv, cache_t)


def _merge_kernel(x_ref, rwkv_ref, o0, o1, o2, l0, l1, l2, zg_ref, wbr_ref, wba_ref, wout_ref, out_ref):
    rwkv = rwkv_ref[...]
    ls = [l0[...], l1[...], l2[...]]
    m = jnp.maximum(jnp.maximum(ls[0], ls[1]), ls[2])
    es = [jnp.exp(l - m) for l in ls]
    attn = (es[0] * o0[...] + es[1] * o1[...] + es[2] * o2[...]) / (es[0] + es[1] + es[2])

    d = x_ref.shape[1]
    gates = zg_ref[...].astype(F32)
    mixed = (gates[:, :d] * _dot(rwkv.astype(BF16), wbr_ref[...])
             + gates[:, d:] * _dot(attn.astype(BF16), wba_ref[...]))
    out_ref[...] = x_ref[...] + _dot(mixed.astype(BF16), wout_ref[...])


def _merge(x, rwkv, os_, ls_, zg, wbr, wba, wout, tm):
    rows, d = x.shape
    rowblk = lambda a: pl.BlockSpec((tm, a.shape[1]), lambda i: (i, 0))
    full = lambda a: pl.BlockSpec(a.shape, lambda i: (0, 0))
    args = [x, rwkv, *os_, *ls_, zg, wbr, wba, wout]
    specs = [rowblk(a) for a in (x, rwkv, *os_, *ls_, zg)] + [full(wbr), full(wba), full(wout)]
    return pl.pallas_call(
        _merge_kernel,
        grid=(rows // tm,),
        in_specs=specs,
        out_specs=pl.BlockSpec((tm, d), lambda i: (i, 0)),
        out_shape=jax.ShapeDtypeStruct((rows, d), F32),
        compiler_params=_cparams(("arbitrary",)),
        name="merge_out",
    )(*args)


def _mlp_ple_kernel(x_ref, g_ref, w1_ref, w2_ref, p_ref, gp_ref, wg_ref, wp_ref, gf_ref, o_ref,
                    h_scr, acc_scr, *, final):
    j = pl.program_id(1)

    @pl.when(j == 0)
    def _():
        h_scr[...] = _rms(x_ref[...], g_ref[...]).astype(BF16)
        acc_scr[...] = x_ref[...]

    u = jnp.maximum(_dot(h_scr[...], w1_ref[...]), 0.0)
    acc_scr[...] += _dot((u * u).astype(BF16), w2_ref[...])

    @pl.when(j == pl.num_programs(1) - 1)
    def _():
        x = acc_scr[...]
        h = _rms(x, gp_ref[...]).astype(BF16)
        x = x + _sigmoid(_dot(h, wg_ref[...])) * _dot(p_ref[...].astype(BF16), wp_ref[...])
        if final:
            x = _rms(x, gf_ref[...])
        o_ref[...] = x


def _mlp_ple(x, g, w1, w2, p, gp, wg, wp, gf, final, tm, tf):
    rows, d = x.shape
    dff = w1.shape[1]
    vec = lambda a: a.reshape(1, d)
    full = lambda a: pl.BlockSpec(a.shape, lambda i, j: (0, 0))
    return pl.pallas_call(
        functools.partial(_mlp_ple_kernel, final=final),
        grid=(rows // tm, dff // tf),
        in_specs=[pl.BlockSpec((tm, d), lambda i, j: (i, 0)),
                  full(vec(g)),
                  pl.BlockSpec((d, tf), lambda i, j: (0, j)),
                  pl.BlockSpec((tf, d), lambda i, j: (j, 0)),
                  pl.BlockSpec((tm, p.shape[1]), lambda i, j: (i, 0)),
                  full(vec(gp)), full(wg), full(wp), full(vec(gf))],
        out_specs=pl.BlockSpec((tm, d), lambda i, j: (i, 0)),
        out_shape=jax.ShapeDtypeStruct((rows, d), F32),
        scratch_shapes=[pltpu.VMEM((tm, d), BF16), pltpu.VMEM((tm, d), F32)],
        compiler_params=_cparams(("arbitrary", "arbitrary")),
        name="mlp_ple",
    )(x, vec(g), w1, w2, p, vec(gp), wg, wp, vec(gf))


def _layer(x, p, wkv0, shift0, caches, prm, norm_final, final, valid):
    n, t, d = x.shape
    rows = n * t
    width = prm["decay_w0"].shape[-1]
    rwkv_cols = prm["shift_mu"].shape[-1]
    aw = ATTN_HEADS * HEAD
    qkv_cols = 3 * len(DILATIONS) * aw
    tm = min(512, rows)
    x2 = x.reshape(rows, d)

    pick = lambda cols, opts: next(o for o in opts if cols % o == 0)
    z_rwkv, z_qkv, z_gate = _in_proj(x2, prm["norm_mix"], prm["w_in"].astype(BF16),
                                     (rwkv_cols, qkv_cols, 2 * d), min(256, rows))
    z_rwkv = z_rwkv.reshape(n, t, rwkv_cols)
    z_qkv = z_qkv.reshape(n, t, qkv_cols)

    if valid == t and t % WKV_CHUNK == 0:
        rwkv, s_t = _wkv_fused(z_rwkv, shift0, _pack_state(wkv0), prm, pick(t, (512, 256, WKV_CHUNK)), 2)
        wkv_t = _unpack_state(s_t, width // HEAD)
    else:
        feats = _rwkv_prep(z_rwkv, shift0, prm, min(256, t))
        rwkv, wkv_t = _wkv(*feats, prm["lnx_w"], prm["lnx_b"], wkv0, valid)
    shift_t = z_rwkv[:, valid - 1, :]

    os_, ls_, new_kv = [], [], []
    ng = len(DILATIONS)
    for gi in range(ng):
        if caches is None:
            o, l = _attn_prompt(z_qkv, gi, max(1024, ATTN_BLOCK * DILATIONS[gi]))
            keep = min(WINDOWS[gi], t)
            kv_rows = z_qkv[:, t - keep:t]
        else:
            o, l = _attn_sample(z_qkv, caches[gi], gi, valid)
            kv_rows = z_qkv[:, :valid]
        new_kv.append(jnp.stack([kv_rows[:, :, (which * ng + gi) * aw:(which * ng + gi + 1) * aw]
                                 .reshape(n, -1, ATTN_HEADS, HEAD) for which in (1, 2)], axis=2))
        os_.append(o.reshape(rows, aw))
        ls_.append(l.reshape(rows, aw))

    flat = lambda u: u.reshape(rows, -1)
    x2 = _merge(x2, flat(rwkv), os_, ls_, z_gate,
                prm["w_branch_rwkv"].astype(BF16), prm["w_branch_attn"].astype(BF16),
                prm["w_out"].astype(BF16), min(256, rows))
    x2 = _mlp_ple(x2, prm["norm_mlp"], prm["w_ff1"].astype(BF16), prm["w_ff2"].astype(BF16),
                  p.reshape(rows, -1), prm["norm_ple"], prm["w_ple_gate"].astype(BF16),
                  prm["w_ple"].astype(BF16), norm_final, final, tm, 1024)
    return x2.reshape(n, t, d), new_kv, wkv_t, shift_t


_LAYER_PARAMS = ("norm_mix", "w_in", "shift_mu", "decay_w0", "decay_w2", "aaa_a0", "aaa_a2", "gate_g2",
                 "k_k", "k_a", "r_k", "lnx_w", "lnx_b", "w_branch_rwkv", "w_branch_attn", "w_out",
                 "norm_mlp", "w_ff1", "w_ff2", "norm_ple", "w_ple", "w_ple_gate")


def kernel(x_prompt, x_sample, cache_kv_g0, cache_kv_g1, cache_kv_g2, state_wkv, state_shift, p_prompt, p_sample, norm_mix, w_in, shift_mu, decay_w0, decay_w2, aaa_a0, aaa_a2, gate_g2, k_k, k_a, r_k, lnx_w, lnx_b, w_branch_rwkv, w_branch_attn, w_out, norm_mlp, w_ff1, w_ff2, norm_ple, w_ple, w_ple_gate, norm_final):
    stacked = dict(zip(_LAYER_PARAMS, (norm_mix, w_in, shift_mu, decay_w0, decay_w2, aaa_a0, aaa_a2,
                                       gate_g2, k_k, k_a, r_k, lnx_w, lnx_b, w_branch_rwkv,
                                       w_branch_attn, w_out, norm_mlp, w_ff1, w_ff2, norm_ple, w_ple,
                                       w_ple_gate)))
    depth = norm_mix.shape[0]
    nb_p, _, d = x_prompt.shape
    nb_s, t_s, _ = x_sample.shape
    heads = decay_w0.shape[-1] // HEAD
    rwkv_cols = shift_mu.shape[-1]
    t_pad = -(-t_s // SUBLANES) * SUBLANES
    pad_t = lambda u: jnp.pad(u, ((0, 0), (0, t_pad - t_s), (0, 0)))

    yp, ys = x_prompt, pad_t(x_sample)
    kvp, kvs = [[], [], []], [[], [], []]
    wkv_p, sh_p, wkv_s, sh_s = [], [], [], []
    for i in range(depth):
        prm = {name: val[i] for name, val in stacked.items()}
        prm["r_k"] = prm["r_k"].reshape(-1)
        last = i == depth - 1
        wkv0 = jnp.zeros((nb_p, heads, HEAD, HEAD), F32)
        shift0 = jnp.zeros((nb_p, rwkv_cols), F32)
        yp, nkv_p, w_p, s_p = _layer(yp, p_prompt[i], wkv0, shift0, None, prm, norm_final, last,
                                     x_prompt.shape[1])
        ys, nkv_s, w_s, s_s = _layer(ys, pad_t(p_sample[i]), state_wkv[i], state_shift[i],
                                     (cache_kv_g0[i], cache_kv_g1[i], cache_kv_g2[i]), prm, norm_final,
                                     last, t_s)
        for gi in range(3):
            kvp[gi].append(nkv_p[gi])
            kvs[gi].append(nkv_s[gi])
        wkv_p.append(w_p)
        sh_p.append(s_p)
        wkv_s.append(w_s)
        sh_s.append(s_s)
    return (yp, ys[:, :t_s],
            jnp.stack(kvp[0]), jnp.stack(kvp[1]), jnp.stack(kvp[2]),
            jnp.stack(wkv_p), jnp.stack(sh_p),
            jnp.stack(kvs[0]), jnp.stack(kvs[1]), jnp.stack(kvs[2]),
            jnp.stack(wkv_s), jnp.stack(sh_s))
```

```python
import functools

import jax
import jax.numpy as jnp
from jax import lax
from jax.experimental import pallas as pl
from jax.experimental.pallas import tpu as pltpu

F32 = jnp.float32
BF16 = jnp.bfloat16

HEAD = 64
LANES = 128
SUBLANES = 8
ATTN_HEADS = 4
ATTN_BLOCK = 128
WINDOWS = (128, 512, 2048)
DILATIONS = (1, 4, 16)
NORM_EPS = 1e-6
LNX_EPS = 64e-5
NEG = -1e30
VMEM_LIMIT = 56 * 1024 * 1024


def _cparams(sem):
    return pltpu.CompilerParams(dimension_semantics=sem, vmem_limit_bytes=VMEM_LIMIT)


def _dot(a, b):
    return jnp.dot(a, b, preferred_element_type=F32)


def _dot_bf16(a, b):
    return _dot(a.astype(BF16), b.astype(BF16))


def _pair_ones():
    r = lax.broadcasted_iota(jnp.int32, (LANES, LANES), 0) // HEAD
    c = lax.broadcasted_iota(jnp.int32, (LANES, LANES), 1) // HEAD
    return (r == c).astype(BF16)


def _head_sum(x, ones):
    outs = []
    for c in range(x.shape[1] // LANES):
        outs.append(_dot(x[:, c * LANES:(c + 1) * LANES].astype(BF16), ones))
    return jnp.concatenate(outs, axis=1)


def _rms(x, g):
    return x * lax.rsqrt(jnp.mean(x * x, axis=-1, keepdims=True) + NORM_EPS) * g


def _sigmoid(x):
    return 1.0 / (1.0 + jnp.exp(-x))


def _softplus(x):
    return jnp.maximum(x, 0.0) + jnp.log(1.0 + jnp.exp(-jnp.abs(x)))


def _in_proj_kernel(x_ref, g_ref, w_ref, zr_ref, zq_ref, gate_ref):
    h = _rms(x_ref[...], g_ref[...]).astype(BF16)
    c1 = zr_ref.shape[1]
    c2 = c1 + zq_ref.shape[1]
    zr_ref[...] = _dot(h, w_ref[:, :c1])
    zq_ref[...] = _dot(h, w_ref[:, c1:c2])
    gate_ref[...] = _sigmoid(_dot(h, w_ref[:, c2:])).astype(BF16)


def _in_proj(x, g, w, splits, tm):
    rows, d = x.shape
    assert sum(splits) == w.shape[1] and len(splits) == 3
    return pl.pallas_call(
        _in_proj_kernel,
        grid=(rows // tm,),
        in_specs=[pl.BlockSpec((tm, d), lambda i: (i, 0)),
                  pl.BlockSpec((1, d), lambda i: (0, 0)),
                  pl.BlockSpec(w.shape, lambda i: (0, 0), pipeline_mode=pl.Buffered(1))],
        out_specs=[pl.BlockSpec((tm, c), lambda i: (i, 0)) for c in splits],
        out_shape=[jax.ShapeDtypeStruct((rows, c), dt) for c, dt in zip(splits, (F32, F32, BF16))],
        compiler_params=_cparams(("arbitrary",)),
        name="in_proj",
    )(x, g.reshape(1, d), w)


def _rwkv_features(r, k, v, xw, xa, xg, w0, w2, a0, a2, g2, k_k, k_a, r_k, ones):
    logw = -_softplus(-(w0 + _dot_bf16(jnp.tanh(xw), w2))) - 0.5
    a = _sigmoid(a0 + _dot_bf16(xa, a2))
    g = _dot_bf16(_sigmoid(xg), g2)
    kk = k * k_k
    kk = kk / jnp.maximum(jnp.sqrt(_head_sum(kk * kk, ones)), 1e-12)
    k2 = k * (1.0 + (a - 1.0) * k_a)
    bonus = _head_sum(r * k2 * r_k, ones) * v
    return -jnp.exp(logw), k2, -kk, kk * a, g, bonus


def _rwkv_out(y, bonus, g, lnx_w, lnx_b, ones):
    mu = _head_sum(y, ones) * (1.0 / HEAD)
    yc = y - mu
    var = _head_sum(yc * yc, ones) * (1.0 / HEAD)
    return (yc * lax.rsqrt(var + LNX_EPS) * lnx_w + lnx_b + bonus) * g


def _rwkv_prep_kernel(z_ref, sh_ref, mu_ref, w0_ref, w2_ref, a0_ref, a2_ref, g2_ref, kk_ref, ka_ref,
                      rk_ref, r_o, w_o, k_o, v_o, a_o, b_o, g_o, bon_o, last_scr, *, width):
    n = pl.program_id(0)
    i = pl.program_id(1)
    z = z_ref[0]
    tt = z.shape[0]

    @pl.when(i == 0)
    def _():
        last_scr[...] = sh_ref[pl.ds(n, 1), :]

    row = lax.broadcasted_iota(jnp.int32, z.shape, 0)
    prev = jnp.where(row == 0, last_scr[...], pltpu.roll(z, 1, axis=0))
    last_scr[...] = z[tt - 1:tt, :]
    zs = z + (prev - z) * mu_ref[...]

    c0 = 3 * width
    c1 = c0 + w2_ref.shape[0]
    c2 = c1 + a2_ref.shape[0]
    r = zs[:, 0:width]
    k = zs[:, width:2 * width]
    v = zs[:, 2 * width:c0]
    xw = zs[:, c0:c1]
    xa = zs[:, c1:c2]
    xg = zs[:, c2:]

    lw, k2, a_neg, b, g, bonus = _rwkv_features(
        r, k, v, xw, xa, xg, w0_ref[...], w2_ref[...], a0_ref[...], a2_ref[...], g2_ref[...],
        kk_ref[...], ka_ref[...], rk_ref[...], _pair_ones())
    r_o[0] = r
    w_o[0] = lw
    k_o[0] = k2
    v_o[0] = v
    a_o[0] = a_neg
    b_o[0] = b
    g_o[0] = g
    bon_o[0] = bonus


def _rwkv_prep(z_rwkv, shift0, prm, tt):
    n, t, cols = z_rwkv.shape
    width = prm["decay_w0"].shape[-1]
    row = lambda x: x.reshape(1, -1)
    full = lambda a: pl.BlockSpec(a.shape, lambda b, i: (0,) * a.ndim)
    args = [shift0, row(prm["shift_mu"]), row(prm["decay_w0"]), prm["decay_w2"], row(prm["aaa_a0"]),
            prm["aaa_a2"], prm["gate_g2"], row(prm["k_k"]), row(prm["k_a"]), row(prm["r_k"])]
    out_spec = pl.BlockSpec((1, tt, width), lambda b, i: (b, i, 0))
    out_sds = jax.ShapeDtypeStruct((n, t, width), F32)
    return pl.pallas_call(
        functools.partial(_rwkv_prep_kernel, width=width),
        grid=(n, t // tt),
        in_specs=[pl.BlockSpec((1, tt, cols), lambda b, i: (b, i, 0))] + [full(a) for a in args],
        out_specs=[out_spec] * 8,
        out_shape=[out_sds] * 8,
        scratch_shapes=[pltpu.VMEM((1, cols), F32)],
        compiler_params=_cparams(("arbitrary", "arbitrary")),
        name="rwkv_prep",
    )(z_rwkv, *args)


def _wkv_kernel(r_ref, w_ref, k_ref, v_ref, a_ref, b_ref, g_ref, bon_ref, lnw_ref, lnb_ref, s0_ref,
                y_ref, st_ref, *, valid):
    ones = _pair_ones()
    row = lax.broadcasted_iota(jnp.int32, (HEAD, LANES), 0)
    lane = lax.broadcasted_iota(jnp.int32, (HEAD, LANES), 1)
    left = lane < HEAD
    diag = (row == lane % HEAD).astype(F32)

    def key_sum(x):
        hi = x.astype(BF16)
        r1 = x - hi.astype(F32)
        mid = r1.astype(BF16)
        lo = (r1 - mid.astype(F32)).astype(BF16)
        return _dot(hi, ones) + (_dot(mid, ones) + _dot(lo, ones))

    chains = [(b, p) for b in range(y_ref.shape[0]) for p in range(s0_ref.shape[1] // 2)]
    cols = lambda p: slice(p * LANES, (p + 1) * LANES)
    vt = [v_ref[b, :, cols(p)].T for b, p in chains]
    s = [jnp.concatenate([s0_ref[b, 2 * p], s0_ref[b, 2 * p + 1]], axis=1) for b, p in chains]
    ys = [[] for _ in chains]
    for j in range(y_ref.shape[1]):
        for ch, (b, p) in enumerate(chains):
            if j < valid:
                row_of = lambda ref: ref[b, j:j + 1, cols(p)]
                v_col = jnp.where(left, vt[ch][0:HEAD, j:j + 1], vt[ch][HEAD:2 * HEAD, j:j + 1])
                sa = key_sum(s[ch] * row_of(a_ref))
                s[ch] = s[ch] * jnp.exp(row_of(w_ref)) + sa * row_of(b_ref) + v_col * row_of(k_ref)
                y_col = key_sum(s[ch] * row_of(r_ref))
                ys[ch].append(jnp.sum(y_col * diag, axis=0, keepdims=True))
            else:
                ys[ch].append(jnp.zeros((1, LANES), F32))
    for ch, (b, p) in enumerate(chains):
        y_ref[b, :, cols(p)] = _rwkv_out(jnp.concatenate(ys[ch], axis=0), bon_ref[b, :, cols(p)],
                                         g_ref[b, :, cols(p)], lnw_ref[:, cols(p)], lnb_ref[:, cols(p)], ones)
        st_ref[b, 2 * p] = s[ch][:, :HEAD]
        st_ref[b, 2 * p + 1] = s[ch][:, HEAD:]


def _wkv(r, w, k, v, a, b, g, bonus, lnx_w, lnx_b, s0, valid):
    n, t, c = r.shape
    assert t == SUBLANES
    nb = 2 if n % 2 == 0 else 1
    tok = pl.BlockSpec((nb, t, c), lambda bi: (bi, 0, 0))
    vec = pl.BlockSpec((1, c), lambda bi: (0, 0))
    st = pl.BlockSpec((nb,) + s0.shape[1:], lambda bi: (bi, 0, 0, 0))
    return pl.pallas_call(
        functools.partial(_wkv_kernel, valid=valid),
        grid=(n // nb,),
        in_specs=[tok] * 8 + [vec, vec, st],
        out_specs=[tok, st],
        out_shape=[jax.ShapeDtypeStruct((n, t, c), F32), jax.ShapeDtypeStruct(s0.shape, F32)],
        compiler_params=_cparams(("arbitrary",)),
        name="wkv_scan",
    )(r, w, k, v, a, b, g, bonus, lnx_w.reshape(1, c), lnx_b.reshape(1, c), s0)


WKV_CHUNK = 64


def _bd(x):
    xb = x.astype(BF16)
    left = lax.broadcasted_iota(jnp.int32, x.shape, 1) < HEAD
    zero = jnp.zeros_like(xb)
    return jnp.concatenate([jnp.where(left, xb, zero), jnp.where(left, zero, xb)], axis=0)


def _dot_nt(a, b):
    return lax.dot_general(a, b, (((1,), (1,)), ((), ())), preferred_element_type=F32)


def _dot_tn(a, b):
    return lax.dot_general(a, b, (((0,), (0,)), ((), ())), preferred_element_type=F32)


def _wkv_chunk_local(r, lw, k, v, a, b, consts):
    tri, eye2, strict, incl, blocks = consts
    c = WKV_CHUNK
    n = range(len(r))
    zero = jnp.zeros((c, LANES), F32)

    def cumsum(x):
        hi = x.astype(BF16)
        r1 = x - hi.astype(F32)
        mid = r1.astype(BF16)
        lo = (r1 - mid.astype(F32)).astype(BF16)
        return _dot(tri, hi) + (_dot(tri, mid) + _dot(tri, lo))

    lc = [cumsum(lw[i]) for i in n]
    ltot = [lc[i][c - 1:c, :] for i in n]
    w_inv = [jnp.exp(-lc[i]) for i in n]
    ah = [a[i] * jnp.exp(lc[i] - lw[i]) for i in n]
    rh = [r[i] * jnp.exp(lc[i]) for i in n]
    sc = [_dot_nt(jnp.concatenate([ah[i], rh[i]], axis=0).astype(BF16),
                  jnp.concatenate([_bd(b[i] * w_inv[i]), _bd(k[i] * w_inv[i])], axis=0)) for i in n]
    aab = [jnp.where(strict, sc[i][:c, :LANES], zero) for i in n]

    t = [eye2 + jnp.where(blocks[0], aab[i], zero) for i in n]
    for lvl in range(1, len(blocks)):
        grow = blocks[lvl] & ~blocks[lvl - 1]
        x = [_dot(t[i].astype(BF16), _bd(jnp.where(grow, aab[i], zero))) for i in n]
        t = [t[i] + _dot(x[i].astype(BF16), _bd(t[i])) for i in n]

    vb = [_bd(v[i]) for i in n]
    akv = [_dot(jnp.where(strict, sc[i][:c, LANES:], zero).astype(BF16), vb[i]) for i in n]
    pu = [_dot(t[i].astype(BF16), jnp.concatenate([_bd(ah[i]), _bd(akv[i])], axis=1)) for i in n]

    left2 = (lax.broadcasted_iota(jnp.int32, (HEAD, 2 * LANES), 1) % LANES) < HEAD
    g, h, q, y0 = [], [], [], []
    for i in n:
        w_rem = jnp.exp(ltot[i] - lc[i])
        lhs_t = jnp.concatenate([b[i] * w_rem, k[i] * w_rem], axis=0).astype(BF16)
        rhs = jnp.concatenate([pu[i], jnp.concatenate([zero, v[i]], axis=1)], axis=0).astype(BF16)
        full = _dot_tn(lhs_t, rhs)
        gh = jnp.where(left2, full[:HEAD], full[HEAD:])
        g.append(gh[:, :LANES] + eye2 * jnp.exp(ltot[i]))
        h.append(gh[:, LANES:])
    zero_b = jnp.zeros((2 * c, LANES), BF16)
    for i in n:
        mrb = jnp.where(incl, sc[i][c:, :LANES], zero)
        mrk = jnp.where(incl, sc[i][c:, LANES:], zero)
        rhs2 = jnp.concatenate([jnp.concatenate([_bd(pu[i][:, :LANES]), _bd(pu[i][:, LANES:])], axis=1),
                                jnp.concatenate([zero_b, vb[i]], axis=1)], axis=0)
        qy = _dot(jnp.concatenate([mrb, mrk], axis=1).astype(BF16), rhs2)
        q.append(rh[i] + qy[:, :LANES])
        y0.append(qy[:, LANES:])
    return g, h, q, y0


def _wkv_fused_kernel(zr_ref, zk_ref, zv_ref, zl_ref, sr_ref, sk_ref, sv_ref, sl_ref,
                      mr_ref, mk_ref, mv_ref, ml_ref, w0_ref, w2_ref, a0_ref, a2_ref, g2_ref,
                      kk_ref, ka_ref, rk_ref, lnw_ref, lnb_ref, s0_ref, out_ref, st_ref,
                      s_scr, pr_scr, pk_scr, pv_scr, pl_scr):
    i = pl.program_id(1)
    c = WKV_CHUNK
    n_seq, tb = out_ref.shape[0], out_ref.shape[1]

    @pl.when(i == 0)
    def _():
        s_scr[...] = s0_ref[...]
        pr_scr[...] = sr_ref[...]
        pk_scr[...] = sk_ref[...]
        pv_scr[...] = sv_ref[...]
        pl_scr[...] = sl_ref[...]

    def shifted(z_ref, p_scr, mu_ref, b):
        z = z_ref[b]
        first = lax.broadcasted_iota(jnp.int32, z.shape, 0) == 0
        prev = jnp.where(first, p_scr[b:b + 1, :], pltpu.roll(z, 1, axis=0))
        p_scr[b:b + 1, :] = z[tb - 1:tb, :]
        return z + (prev - z) * mu_ref[...]

    ones = _pair_ones()
    c1 = w2_ref.shape[0]
    c2 = c1 + a2_ref.shape[0]
    feats = []
    for b in range(n_seq):
        r = shifted(zr_ref, pr_scr, mr_ref, b)
        k = shifted(zk_ref, pk_scr, mk_ref, b)
        v = shifted(zv_ref, pv_scr, mv_ref, b)
        zl = shifted(zl_ref, pl_scr, ml_ref, b)
        lw, k2, a_neg, bb, gate, bonus = _rwkv_features(
            r, k, v, zl[:, :c1], zl[:, c1:c2], zl[:, c2:], w0_ref[...], w2_ref[...], a0_ref[...],
            a2_ref[...], g2_ref[...], kk_ref[...], ka_ref[...], rk_ref[...], ones)
        feats.append((r, lw, k2, v, a_neg, bb, gate, bonus))

    ti = lax.broadcasted_iota(jnp.int32, (c, LANES), 0)
    si = lax.broadcasted_iota(jnp.int32, (c, LANES), 1) % HEAD
    tri = (lax.broadcasted_iota(jnp.int32, (c, c), 0) >= lax.broadcasted_iota(jnp.int32, (c, c), 1)
           ).astype(BF16)
    eye2 = (ti == si).astype(F32)
    sizes = [2 << l for l in range(c.bit_length() - 1)]
    blocks = [(ti // s) == (si // s) for s in sizes]
    consts = (tri, eye2, ti > si, ti >= si, blocks)

    n_chunks = tb // c
    chains = [(b, sp) for b in range(n_seq) for sp in range(out_ref.shape[2] // LANES)]
    chunks = lambda which: [feats[b][which][ci * c:(ci + 1) * c, sp * LANES:(sp + 1) * LANES]
                            for ci in range(n_chunks) for b, sp in chains]
    g, h, q, y0 = _wkv_chunk_local(*(chunks(which) for which in range(6)), consts)
    st = [s_scr[b, sp] for b, sp in chains]
    ys = [[] for _ in chains]
    for ci in range(n_chunks):
        for ch in range(len(chains)):
            u = ci * len(chains) + ch
            out = _dot(jnp.concatenate([q[u], g[u]], axis=0).astype(BF16), _bd(st[ch]))
            ys[ch].append(out[:c] + y0[u])
            st[ch] = out[c:] + h[u]
    for ch, (b, sp) in enumerate(chains):
        s_scr[b, sp] = st[ch]
    for b in range(n_seq):
        y = jnp.concatenate([jnp.concatenate(ys[ch], axis=0) for ch, (bb_, _) in enumerate(chains) if bb_ == b],
                            axis=1)
        out_ref[b] = _rwkv_out(y, feats[b][7], feats[b][6], lnw_ref[...], lnb_ref[...], ones)

    @pl.when(i == pl.num_programs(1) - 1)
    def _():
        for ch, (b, sp) in enumerate(chains):
            st_ref[b, sp] = st[ch]


def _wkv_fused(z_rwkv, shift0, s0, prm, tb, pp):
    n, t, cols = z_rwkv.shape
    width = prm["decay_w0"].shape[-1]
    bw = pp * LANES
    nblk = width // bw
    lora = cols - 3 * width
    assert width % bw == 0 and (3 * width) % lora == 0 and lora % LANES == 0
    lblk = 3 * width // lora
    row = lambda x: x.reshape(1, -1)
    mu = row(prm["shift_mu"])
    tok = lambda off: pl.BlockSpec((n, tb, bw), lambda p, i: (0, i, off + p))
    vec = lambda rows_, off: pl.BlockSpec((rows_, bw), lambda p, i: (0, off + p))
    st = pl.BlockSpec((n, pp, HEAD, LANES), lambda p, i: (0, p, 0, 0))
    in_specs = [tok(0), tok(nblk), tok(2 * nblk), pl.BlockSpec((n, tb, lora), lambda p, i: (0, i, lblk)),
                vec(n, 0), vec(n, nblk), vec(n, 2 * nblk), pl.BlockSpec((n, lora), lambda p, i: (0, lblk)),
                vec(1, 0), vec(1, nblk), vec(1, 2 * nblk), pl.BlockSpec((1, lora), lambda p, i: (0, lblk)),
                vec(1, 0), vec(prm["decay_w2"].shape[0], 0), vec(1, 0), vec(prm["aaa_a2"].shape[0], 0),
                vec(prm["gate_g2"].shape[0], 0), vec(1, 0), vec(1, 0), vec(1, 0), vec(1, 0), vec(1, 0), st]
    args = [z_rwkv] * 4 + [shift0] * 4 + [mu] * 4 + [
        row(prm["decay_w0"]), prm["decay_w2"], row(prm["aaa_a0"]), prm["aaa_a2"], prm["gate_g2"],
        row(prm["k_k"]), row(prm["k_a"]), row(prm["r_k"]), row(prm["lnx_w"]), row(prm["lnx_b"]), s0]
    return pl.pallas_call(
        _wkv_fused_kernel,
        grid=(nblk, t // tb),
        in_specs=in_specs,
        out_specs=[tok(0), st],
        out_shape=[jax.ShapeDtypeStruct((n, t, width), F32), jax.ShapeDtypeStruct(s0.shape, F32)],
        scratch_shapes=[pltpu.VMEM((n, pp, HEAD, LANES), F32), pltpu.VMEM((n, bw), F32),
                        pltpu.VMEM((n, bw), F32), pltpu.VMEM((n, bw), F32),
                        pltpu.VMEM((n, lora), F32)],
        compiler_params=_cparams(("arbitrary", "arbitrary")),
        name="wkv_fused",
    )(*args)


def _pack_state(s):
    n, h, vd, kd = s.shape
    return s.reshape(n, h // 2, 2, vd, kd).transpose(0, 1, 4, 2, 3).reshape(n, h // 2, kd, 2 * vd)


def _unpack_state(s, h):
    n, _, kd, _ = s.shape
    return s.reshape(n, h // 2, kd, 2, HEAD).transpose(0, 1, 3, 4, 2).reshape(n, h, HEAD, kd)


def _attn_prompt_kernel(q_ref, k_ref, kp_ref, v_ref, vp_ref, o_ref, l_ref, *, dil, sub, n_back, scale):
    first = pl.program_id(1) == 0
    qi = lax.broadcasted_iota(jnp.int32, (ATTN_BLOCK, 2 * ATTN_BLOCK), 0)
    ki = lax.broadcasted_iota(jnp.int32, (ATTN_BLOCK, 2 * ATTN_BLOCK), 1)
    dist = qi + ATTN_BLOCK - ki
    band = (dist >= 0) & (dist <= n_back)
    own = ki >= ATTN_BLOCK
    left = lax.broadcasted_iota(jnp.int32, (ATTN_BLOCK, LANES), 1) < HEAD
    ones_kv = jnp.ones((2 * ATTN_BLOCK, LANES), BF16)

    def rows(j, r):
        if dil == 1:
            return pl.ds(j * ATTN_BLOCK, ATTN_BLOCK)
        return pl.ds(j * ATTN_BLOCK * dil + r, ATTN_BLOCK, stride=dil)

    def units(jr):
        probs = []
        qs, kcat, vcat, masks = [], [], [], []
        for j, r in jr:
            q = (q_ref[0, rows(j, r), :] * scale).astype(BF16)
            if j == 0:
                kp, vp = kp_ref[0, rows(0, r), :], vp_ref[0, rows(0, r), :]
                masks.append(band & (own | jnp.logical_not(first)))
            else:
                kp, vp = k_ref[0, rows(j - 1, r), :], v_ref[0, rows(j - 1, r), :]
                masks.append(band)
            kcat.append(jnp.concatenate([kp, k_ref[0, rows(j, r), :]], axis=0).astype(BF16))
            vcat.append(jnp.concatenate([vp, v_ref[0, rows(j, r), :]], axis=0).astype(BF16))
            zq = jnp.zeros_like(q)
            qs += [jnp.where(left, q, zq), jnp.where(left, zq, q)]
            probs += [(len(kcat) - 1, 0), (len(kcat) - 1, 1)]
        s = [jnp.where(masks[u], _dot_nt(qs[p], kcat[u]), NEG) for p, (u, _) in enumerate(probs)]
        m = [jnp.max(x, axis=-1, keepdims=True) for x in s]
        e = [jnp.exp(s[p] - m[p]).astype(BF16) for p in range(len(probs))]
        o = [_dot(e[p], vcat[u]) for p, (u, _) in enumerate(probs)]
        dens = [_dot(e[p], ones_kv) for p in range(len(probs))]
        for u, (j, r) in enumerate(jr):
            den = jnp.where(left, dens[2 * u], dens[2 * u + 1])
            o_ref[0, rows(j, r), :] = jnp.where(left, o[2 * u], o[2 * u + 1]) / den
            l_ref[0, rows(j, r), :] = jnp.where(left, m[2 * u], m[2 * u + 1]) + jnp.log(den)

    if dil <= 4:
        units([(j, r) for j in range(sub) for r in range(dil)])
    else:
        for j in range(sub):
            def body(r4, carry, j=j):
                units([(j, r4 * 4 + u) for u in range(4)])
                return carry
            lax.fori_loop(0, dil // 4, body, 0)


def _attn_prompt(zqkv, gi, span):
    n, t, cols = zqkv.shape
    d = DILATIONS[gi]
    aw = ATTN_HEADS * HEAD
    ng = len(DILATIONS)
    back = ATTN_BLOCK * d
    sub = span // back
    halves = aw // LANES
    col = lambda which, hp: (which * ng + gi) * halves + hp
    cur = lambda which: pl.BlockSpec((1, span, LANES), lambda b, i, hp: (b, i, col(which, hp)))
    prev = lambda which: pl.BlockSpec((1, back, LANES),
                                      lambda b, i, hp: (b, jnp.maximum(i * sub - 1, 0), col(which, hp)))
    out_spec = pl.BlockSpec((1, span, LANES), lambda b, i, hp: (b, i, hp))
    out_sds = jax.ShapeDtypeStruct((n, t, aw), F32)
    return pl.pallas_call(
        functools.partial(_attn_prompt_kernel, dil=d, sub=sub, n_back=WINDOWS[gi] // d, scale=HEAD ** -0.5),
        grid=(n, t // span, halves),
        in_specs=[cur(0), cur(1), prev(1), cur(2), prev(2)],
        out_specs=[out_spec, out_spec],
        out_shape=[out_sds, out_sds],
        compiler_params=_cparams(("arbitrary", "arbitrary", "arbitrary")),
        name=f"attn_prompt_g{gi}",
    )(zqkv, zqkv, zqkv, zqkv, zqkv)


def _attn_sample_kernel(q_ref, k_ref, v_ref, c_ref, o_ref, l_ref, *, dil, n_back, valid, scale):
    nb, tp = q_ref.shape[0], q_ref.shape[1]
    cl = c_ref.shape[4]
    dist_c = (cl + lax.broadcasted_iota(jnp.int32, (tp, cl), 0)
              - lax.broadcasted_iota(jnp.int32, (tp, cl), 1))
    mask_c = (dist_c % dil == 0) & (dist_c <= dil * n_back)
    pn = lax.broadcasted_iota(jnp.int32, (tp, tp), 1)
    dist_n = lax.broadcasted_iota(jnp.int32, (tp, tp), 0) - pn
    mask_n = (dist_n >= 0) & (dist_n % dil == 0) & (dist_n <= dil * n_back) & (pn < valid)
    probs = [(b, h) for b in range(nb) for h in range(ATTN_HEADS)]
    sl = lambda h: slice(h * HEAD, (h + 1) * HEAD)
    q = [(q_ref[b, :, sl(h)] * scale).astype(BF16) for b, h in probs]
    s_c = [jnp.where(mask_c, _dot(q[p], c_ref[b, 0, h].astype(BF16)), NEG) for p, (b, h) in enumerate(probs)]
    s_n = [jnp.where(mask_n, _dot_nt(q[p], k_ref[b, :, sl(h)].astype(BF16)), NEG)
           for p, (b, h) in enumerate(probs)]
    m = [jnp.maximum(jnp.max(s_c[p], axis=-1, keepdims=True), jnp.max(s_n[p], axis=-1, keepdims=True))
         for p in range(len(probs))]
    e_c = [jnp.exp(s_c[p] - m[p]) for p in range(len(probs))]
    e_n = [jnp.exp(s_n[p] - m[p]) for p in range(len(probs))]
    den = [jnp.sum(e_c[p], axis=-1, keepdims=True) + jnp.sum(e_n[p], axis=-1, keepdims=True)
           for p in range(len(probs))]
    o = [_dot_nt(e_c[p].astype(BF16), c_ref[b, 1, h].astype(BF16))
         + _dot(e_n[p].astype(BF16), v_ref[b, :, sl(h)].astype(BF16)) for p, (b, h) in enumerate(probs)]
    for b in range(nb):
        ps = [p for p, (bb, _) in enumerate(probs) if bb == b]
        o_ref[b] = jnp.concatenate([o[p] / den[p] for p in ps], axis=1)
        l_ref[b] = jnp.concatenate([jnp.broadcast_to(m[p] + jnp.log(den[p]), (tp, HEAD)) for p in ps], axis=1)


def _attn_sample(zqkv, cache, gi, valid):
    n, tp, cols = zqkv.shape
    aw = ATTN_HEADS * HEAD
    ng = len(DILATIONS)
    d = DILATIONS[gi]
    cache_t = cache.transpose(0, 2, 3, 4, 1)
    nb = 2 if n % 2 == 0 else 1
    spec = lambda which: pl.BlockSpec((nb, tp, aw), lambda b: (b, 0, which * ng + gi))
    out_spec = pl.BlockSpec((nb, tp, aw), lambda b: (b, 0, 0))
    out_sds = jax.ShapeDtypeStruct((n, tp, aw), F32)
    return pl.pallas_call(
        functools.partial(_attn_sample_kernel, dil=d, n_back=WINDOWS[gi] // d, valid=valid,
                          scale=HEAD ** -0.5),
        grid=(n // nb,),
        in_specs=[spec(0), spec(1), spec(2),
                  pl.BlockSpec((nb,) + cache_t.shape[1:], lambda b: (b, 0, 0, 0, 0))],
        out_specs=[out_spec, out_spec],
        out_shape=[out_sds, out_sds],
        compiler_params=_cparams(("arbitrary",)),
        name=f"attn_sample_g{gi}",
    )(zqkv, zqkv, zqkv, cache_t)


def _merge_kernel(x_ref, rwkv_ref, o0, o1, o2, l0, l1, l2, zg_ref, wbr_ref, wba_ref, wout_ref, out_ref):
    rwkv = rwkv_ref[...]
    ls = [l0[...], l1[...], l2[...]]
    m = jnp.maximum(jnp.maximum(ls[0], ls[1]), ls[2])
    es = [jnp.exp(l - m) for l in ls]
    attn = (es[0] * o0[...] + es[1] * o1[...] + es[2] * o2[...]) / (es[0] + es[1] + es[2])

    d = x_ref.shape[1]
    gates = zg_ref[...].astype(F32)
    mixed = (gates[:, :d] * _dot(rwkv.astype(BF16), wbr_ref[...])
             + gates[:, d:] * _dot(attn.astype(BF16), wba_ref[...]))
    out_ref[...] = x_ref[...] + _dot(mixed.astype(BF16), wout_ref[...])


def _merge(x, rwkv, os_, ls_, zg, wbr, wba, wout, tm):
    rows, d = x.shape
    rowblk = lambda a: pl.BlockSpec((tm, a.shape[1]), lambda i: (i, 0))
    full = lambda a: pl.BlockSpec(a.shape, lambda i: (0, 0))
    args = [x, rwkv, *os_, *ls_, zg, wbr, wba, wout]
    specs = [rowblk(a) for a in (x, rwkv, *os_, *ls_, zg)] + [full(wbr), full(wba), full(wout)]
    return pl.pallas_call(
        _merge_kernel,
        grid=(rows // tm,),
        in_specs=specs,
        out_specs=pl.BlockSpec((tm, d), lambda i: (i, 0)),
        out_shape=jax.ShapeDtypeStruct((rows, d), F32),
        compiler_params=_cparams(("arbitrary",)),
        name="merge_out",
    )(*args)


def _mlp_ple_kernel(x_ref, g_ref, w1_ref, w2_ref, p_ref, gp_ref, wg_ref, wp_ref, gf_ref, o_ref,
                    h_scr, acc_scr, *, final):
    j = pl.program_id(1)

    @pl.when(j == 0)
    def _():
        h_scr[...] = _rms(x_ref[...], g_ref[...]).astype(BF16)
        acc_scr[...] = x_ref[...]

    u = jnp.maximum(_dot(h_scr[...], w1_ref[...]), 0.0)
    acc_scr[...] += _dot((u * u).astype(BF16), w2_ref[...])

    @pl.when(j == pl.num_programs(1) - 1)
    def _():
        x = acc_scr[...]
        h = _rms(x, gp_ref[...]).astype(BF16)
        x = x + _sigmoid(_dot(h, wg_ref[...])) * _dot(p_ref[...].astype(BF16), wp_ref[...])
        if final:
            x = _rms(x, gf_ref[...])
        o_ref[...] = x


def _mlp_ple(x, g, w1, w2, p, gp, wg, wp, gf, final, tm, tf):
    rows, d = x.shape
    dff = w1.shape[1]
    vec = lambda a: a.reshape(1, d)
    full = lambda a: pl.BlockSpec(a.shape, lambda i, j: (0, 0))
    return pl.pallas_call(
        functools.partial(_mlp_ple_kernel, final=final),
        grid=(rows // tm, dff // tf),
        in_specs=[pl.BlockSpec((tm, d), lambda i, j: (i, 0)),
                  full(vec(g)),
                  pl.BlockSpec((d, tf), lambda i, j: (0, j)),
                  pl.BlockSpec((tf, d), lambda i, j: (j, 0)),
                  pl.BlockSpec((tm, p.shape[1]), lambda i, j: (i, 0)),
                  full(vec(gp)), full(wg), full(wp), full(vec(gf))],
        out_specs=pl.BlockSpec((tm, d), lambda i, j: (i, 0)),
        out_shape=jax.ShapeDtypeStruct((rows, d), F32),
        scratch_shapes=[pltpu.VMEM((tm, d), BF16), pltpu.VMEM((tm, d), F32)],
        compiler_params=_cparams(("arbitrary", "arbitrary")),
        name="mlp_ple",
    )(x, vec(g), w1, w2, p, vec(gp), wg, wp, vec(gf))


def _layer(x, p, wkv0, shift0, caches, prm, norm_final, final, valid):
    n, t, d = x.shape
    rows = n * t
    width = prm["decay_w0"].shape[-1]
    rwkv_cols = prm["shift_mu"].shape[-1]
    aw = ATTN_HEADS * HEAD
    qkv_cols = 3 * len(DILATIONS) * aw
    tm = min(512, rows)
    x2 = x.reshape(rows, d)

    pick = lambda cols, opts: next(o for o in opts if cols % o == 0)
    z_rwkv, z_qkv, z_gate = _in_proj(x2, prm["norm_mix"], prm["w_in"].astype(BF16),
                                     (rwkv_cols, qkv_cols, 2 * d), min(256, rows))
    z_rwkv = z_rwkv.reshape(n, t, rwkv_cols)
    z_qkv = z_qkv.reshape(n, t, qkv_cols)

    if valid == t and t % WKV_CHUNK == 0:
        rwkv, s_t = _wkv_fused(z_rwkv, shift0, _pack_state(wkv0), prm, pick(t, (512, 256, WKV_CHUNK)), 2)
        wkv_t = _unpack_state(s_t, width // HEAD)
    else:
        feats = _rwkv_prep(z_rwkv, shift0, prm, min(256, t))
        rwkv, wkv_t = _wkv(*feats, prm["lnx_w"], prm["lnx_b"], wkv0, valid)
    shift_t = z_rwkv[:, valid - 1, :]

    os_, ls_, new_kv = [], [], []
    ng = len(DILATIONS)
    for gi in range(ng):
        if caches is None:
            o, l = _attn_prompt(z_qkv, gi, max(1024, ATTN_BLOCK * DILATIONS[gi]))
            keep = min(WINDOWS[gi], t)
            kv_rows = z_qkv[:, t - keep:t]
        else:
            o, l = _attn_sample(z_qkv, caches[gi], gi, valid)
            kv_rows = z_qkv[:, :valid]
        new_kv.append(jnp.stack([kv_rows[:, :, (which * ng + gi) * aw:(which * ng + gi + 1) * aw]
                                 .reshape(n, -1, ATTN_HEADS, HEAD) for which in (1, 2)], axis=2))
        os_.append(o.reshape(rows, aw))
        ls_.append(l.reshape(rows, aw))

    flat = lambda u: u.reshape(rows, -1)
    x2 = _merge(x2, flat(rwkv), os_, ls_, z_gate,
                prm["w_branch_rwkv"].astype(BF16), prm["w_branch_attn"].astype(BF16),
                prm["w_out"].astype(BF16), min(256, rows))
    x2 = _mlp_ple(x2, prm["norm_mlp"], prm["w_ff1"].astype(BF16), prm["w_ff2"].astype(BF16),
                  p.reshape(rows, -1), prm["norm_ple"], prm["w_ple_gate"].astype(BF16),
                  prm["w_ple"].astype(BF16), norm_final, final, tm, 1024)
    return x2.reshape(n, t, d), new_kv, wkv_t, shift_t


_LAYER_PARAMS = ("norm_mix", "w_in", "shift_mu", "decay_w0", "decay_w2", "aaa_a0", "aaa_a2", "gate_g2",
                 "k_k", "k_a", "r_k", "lnx_w", "lnx_b", "w_branch_rwkv", "w_branch_attn", "w_out",
                 "norm_mlp", "w_ff1", "w_ff2", "norm_ple", "w_ple", "w_ple_gate")


def kernel(x_prompt, x_sample, cache_kv_g0, cache_kv_g1, cache_kv_g2, state_wkv, state_shift, p_prompt, p_sample, norm_mix, w_in, shift_mu, decay_w0, decay_w2, aaa_a0, aaa_a2, gate_g2, k_k, k_a, r_k, lnx_w, lnx_b, w_branch_rwkv, w_branch_attn, w_out, norm_mlp, w_ff1, w_ff2, norm_ple, w_ple, w_ple_gate, norm_final):
    stacked = dict(zip(_LAYER_PARAMS, (norm_mix, w_in, shift_mu, decay_w0, decay_w2, aaa_a0, aaa_a2,
                                       gate_g2, k_k, k_a, r_k, lnx_w, lnx_b, w_branch_rwkv,
                                       w_branch_attn, w_out, norm_mlp, w_ff1, w_ff2, norm_ple, w_ple,
                                       w_ple_gate)))
    depth = norm_mix.shape[0]
    nb_p, _, d = x_prompt.shape
    nb_s, t_s, _ = x_sample.shape
    heads = decay_w0.shape[-1] // HEAD
    rwkv_cols = shift_mu.shape[-1]
    t_pad = -(-t_s // SUBLANES) * SUBLANES
    pad_t = lambda u: jnp.pad(u, ((0, 0), (0, t_pad - t_s), (0, 0)))

    yp, ys = x_prompt, pad_t(x_sample)
    kvp, kvs = [[], [], []], [[], [], []]
    wkv_p, sh_p, wkv_s, sh_s = [], [], [], []
    for i in range(depth):
        prm = {name: val[i] for name, val in stacked.items()}
        prm["r_k"] = prm["r_k"].reshape(-1)
        last = i == depth - 1
        wkv0 = jnp.zeros((nb_p, heads, HEAD, HEAD), F32)
        shift0 = jnp.zeros((nb_p, rwkv_cols), F32)
        yp, nkv_p, w_p, s_p = _layer(yp, p_prompt[i], wkv0, shift0, None, prm, norm_final, last,
                                     x_prompt.shape[1])
        ys, nkv_s, w_s, s_s = _layer(ys, pad_t(p_sample[i]), state_wkv[i], state_shift[i],
                                     (cache_kv_g0[i], cache_kv_g1[i], cache_kv_g2[i]), prm, norm_final,
                                     last, t_s)
        for gi in range(3):
            kvp[gi].append(nkv_p[gi])
            kvs[gi].append(nkv_s[gi])
        wkv_p.append(w_p)
        sh_p.append(s_p)
        wkv_s.append(w_s)
        sh_s.append(s_s)
    return (yp, ys[:, :t_s],
            jnp.stack(kvp[0]), jnp.stack(kvp[1]), jnp.stack(kvp[2]),
            jnp.stack(wkv_p), jnp.stack(sh_p),
            jnp.stack(kvs[0]), jnp.stack(kvs[1]), jnp.stack(kvs[2]),
            jnp.stack(wkv_s), jnp.stack(sh_s))
```

```python
import functools

import jax
import jax.numpy as jnp
from jax import lax
from jax.experimental import pallas as pl
from jax.experimental.pallas import tpu as pltpu

F32 = jnp.float32
BF16 = jnp.bfloat16

HEAD = 64
LANES = 128
SUBLANES = 8
ATTN_HEADS = 4
ATTN_BLOCK = 128
WINDOWS = (128, 512, 2048)
DILATIONS = (1, 4, 16)
NORM_EPS = 1e-6
LNX_EPS = 64e-5
NEG = -1e30
VMEM_LIMIT = 56 * 1024 * 1024


def _cparams(sem):
    return pltpu.CompilerParams(dimension_semantics=sem, vmem_limit_bytes=VMEM_LIMIT)


def _dot(a, b):
    return jnp.dot(a, b, preferred_element_type=F32)


def _dot_bf16(a, b):
    return _dot(a.astype(BF16), b.astype(BF16))


def _pair_ones():
    r = lax.broadcasted_iota(jnp.int32, (LANES, LANES), 0) // HEAD
    c = lax.broadcasted_iota(jnp.int32, (LANES, LANES), 1) // HEAD
    return (r == c).astype(BF16)


def _head_sum(x, ones):
    outs = []
    for c in range(x.shape[1] // LANES):
        outs.append(_dot(x[:, c * LANES:(c + 1) * LANES].astype(BF16), ones))
    return jnp.concatenate(outs, axis=1)


def _rms(x, g):
    return x * lax.rsqrt(jnp.mean(x * x, axis=-1, keepdims=True) + NORM_EPS) * g


def _sigmoid(x):
    return 1.0 / (1.0 + jnp.exp(-x))


def _softplus(x):
    return jnp.maximum(x, 0.0) + jnp.log(1.0 + jnp.exp(-jnp.abs(x)))


def _in_proj_kernel(x_ref, g_ref, w_ref, zr_ref, zq_ref, gate_ref):
    h = _rms(x_ref[...], g_ref[...]).astype(BF16)
    c1 = zr_ref.shape[1]
    c2 = c1 + zq_ref.shape[1]
    zr_ref[...] = _dot(h, w_ref[:, :c1])
    zq_ref[...] = _dot(h, w_ref[:, c1:c2])
    gate_ref[...] = _sigmoid(_dot(h, w_ref[:, c2:])).astype(BF16)


def _in_proj(x, g, w, splits, tm):
    rows, d = x.shape
    assert sum(splits) == w.shape[1] and len(splits) == 3
    return pl.pallas_call(
        _in_proj_kernel,
        grid=(rows // tm,),
        in_specs=[pl.BlockSpec((tm, d), lambda i: (i, 0)),
                  pl.BlockSpec((1, d), lambda i: (0, 0)),
                  pl.BlockSpec(w.shape, lambda i: (0, 0), pipeline_mode=pl.Buffered(1))],
        out_specs=[pl.BlockSpec((tm, c), lambda i: (i, 0)) for c in splits],
        out_shape=[jax.ShapeDtypeStruct((rows, c), dt) for c, dt in zip(splits, (F32, F32, BF16))],
        compiler_params=_cparams(("arbitrary",)),
        name="in_proj",
    )(x, g.reshape(1, d), w)


def _drain(steps):
    try:
        while True:
            next(steps)
    except StopIteration as stop:
        return stop.value


def _zip_stages(*gens):
    vals = [None] * len(gens)
    live = list(range(len(gens)))
    while live:
        for i in list(live):
            try:
                next(gens[i])
            except StopIteration as stop:
                vals[i] = stop.value
                live.remove(i)
    return vals


def _rwkv_features_steps(r, k, v, xw, xa, xg, w0, w2, a0, a2, g2, k_k, k_a, r_k, ones):
    pre = w0 + _dot_bf16(jnp.tanh(xw), w2)
    yield
    lw = -jnp.exp(-_softplus(-pre) - 0.5)
    yield
    a = _sigmoid(a0 + _dot_bf16(xa, a2))
    yield
    g = _dot_bf16(_sigmoid(xg), g2)
    yield
    kk = k * k_k
    norm2 = _head_sum(kk * kk, ones)
    yield
    kk = kk / jnp.maximum(jnp.sqrt(norm2), 1e-12)
    k2 = k * (1.0 + (a - 1.0) * k_a)
    yield
    bonus = _head_sum(r * k2 * r_k, ones) * v
    yield
    return lw, k2, -kk, kk * a, g, bonus


def _rwkv_features(*args):
    return _drain(_rwkv_features_steps(*args))


def _rwkv_out(y, bonus, g, lnx_w, lnx_b, ones):
    mu = _head_sum(y, ones) * (1.0 / HEAD)
    yc = y - mu
    var = _head_sum(yc * yc, ones) * (1.0 / HEAD)
    return (yc * lax.rsqrt(var + LNX_EPS) * lnx_w + lnx_b + bonus) * g


def _rwkv_prep_kernel(z_ref, sh_ref, mu_ref, w0_ref, w2_ref, a0_ref, a2_ref, g2_ref, kk_ref, ka_ref,
                      rk_ref, r_o, w_o, k_o, v_o, a_o, b_o, g_o, bon_o, last_scr, *, width):
    n = pl.program_id(0)
    i = pl.program_id(1)
    z = z_ref[0]
    tt = z.shape[0]

    @pl.when(i == 0)
    def _():
        last_scr[...] = sh_ref[pl.ds(n, 1), :]

    row = lax.broadcasted_iota(jnp.int32, z.shape, 0)
    prev = jnp.where(row == 0, last_scr[...], pltpu.roll(z, 1, axis=0))
    last_scr[...] = z[tt - 1:tt, :]
    zs = z + (prev - z) * mu_ref[...]

    c0 = 3 * width
    c1 = c0 + w2_ref.shape[0]
    c2 = c1 + a2_ref.shape[0]
    r = zs[:, 0:width]
    k = zs[:, width:2 * width]
    v = zs[:, 2 * width:c0]
    xw = zs[:, c0:c1]
    xa = zs[:, c1:c2]
    xg = zs[:, c2:]

    lw, k2, a_neg, b, g, bonus = _rwkv_features(
        r, k, v, xw, xa, xg, w0_ref[...], w2_ref[...], a0_ref[...], a2_ref[...], g2_ref[...],
        kk_ref[...], ka_ref[...], rk_ref[...], _pair_ones())
    r_o[0] = r
    w_o[0] = lw
    k_o[0] = k2
    v_o[0] = v
    a_o[0] = a_neg
    b_o[0] = b
    g_o[0] = g
    bon_o[0] = bonus


def _rwkv_prep(z_rwkv, shift0, prm, tt):
    n, t, cols = z_rwkv.shape
    width = prm["decay_w0"].shape[-1]
    row = lambda x: x.reshape(1, -1)
    full = lambda a: pl.BlockSpec(a.shape, lambda b, i: (0,) * a.ndim)
    args = [shift0, row(prm["shift_mu"]), row(prm["decay_w0"]), prm["decay_w2"], row(prm["aaa_a0"]),
            prm["aaa_a2"], prm["gate_g2"], row(prm["k_k"]), row(prm["k_a"]), row(prm["r_k"])]
    out_spec = pl.BlockSpec((1, tt, width), lambda b, i: (b, i, 0))
    out_sds = jax.ShapeDtypeStruct((n, t, width), F32)
    return pl.pallas_call(
        functools.partial(_rwkv_prep_kernel, width=width),
        grid=(n, t // tt),
        in_specs=[pl.BlockSpec((1, tt, cols), lambda b, i: (b, i, 0))] + [full(a) for a in args],
        out_specs=[out_spec] * 8,
        out_shape=[out_sds] * 8,
        scratch_shapes=[pltpu.VMEM((1, cols), F32)],
        compiler_params=_cparams(("arbitrary", "arbitrary")),
        name="rwkv_prep",
    )(z_rwkv, *args)


def _wkv_kernel(r_ref, w_ref, k_ref, v_ref, a_ref, b_ref, g_ref, bon_ref, lnw_ref, lnb_ref, s0_ref,
                y_ref, st_ref, *, valid):
    ones = _pair_ones()
    row = lax.broadcasted_iota(jnp.int32, (HEAD, LANES), 0)
    lane = lax.broadcasted_iota(jnp.int32, (HEAD, LANES), 1)
    left = lane < HEAD
    diag = (row == lane % HEAD).astype(F32)

    def key_sum(x):
        hi = x.astype(BF16)
        lo = (x - hi.astype(F32)).astype(BF16)
        return _dot(hi, ones) + _dot(lo, ones)

    chains = [(b, p) for b in range(y_ref.shape[0]) for p in range(s0_ref.shape[1] // 2)]
    cols = lambda p: slice(p * LANES, (p + 1) * LANES)
    vt = [v_ref[b, :, cols(p)].T for b, p in chains]
    s = [jnp.concatenate([s0_ref[b, 2 * p], s0_ref[b, 2 * p + 1]], axis=1) for b, p in chains]
    ys = [[] for _ in chains]
    for j in range(y_ref.shape[1]):
        for ch, (b, p) in enumerate(chains):
            if j < valid:
                row_of = lambda ref: ref[b, j:j + 1, cols(p)]
                v_col = jnp.where(left, vt[ch][0:HEAD, j:j + 1], vt[ch][HEAD:2 * HEAD, j:j + 1])
                sa = key_sum(s[ch] * row_of(a_ref))
                s[ch] = s[ch] * jnp.exp(row_of(w_ref)) + sa * row_of(b_ref) + v_col * row_of(k_ref)
                y_col = key_sum(s[ch] * row_of(r_ref))
                ys[ch].append(jnp.sum(y_col * diag, axis=0, keepdims=True))
            else:
                ys[ch].append(jnp.zeros((1, LANES), F32))
    for ch, (b, p) in enumerate(chains):
        y_ref[b, :, cols(p)] = _rwkv_out(jnp.concatenate(ys[ch], axis=0), bon_ref[b, :, cols(p)],
                                         g_ref[b, :, cols(p)], lnw_ref[:, cols(p)], lnb_ref[:, cols(p)], ones)
        st_ref[b, 2 * p] = s[ch][:, :HEAD]
        st_ref[b, 2 * p + 1] = s[ch][:, HEAD:]


def _wkv(r, w, k, v, a, b, g, bonus, lnx_w, lnx_b, s0, valid):
    n, t, c = r.shape
    assert t == SUBLANES
    nb = 2 if n % 2 == 0 else 1
    tok = pl.BlockSpec((nb, t, c), lambda bi: (bi, 0, 0))
    vec = pl.BlockSpec((1, c), lambda bi: (0, 0))
    st = pl.BlockSpec((nb,) + s0.shape[1:], lambda bi: (bi, 0, 0, 0))
    return pl.pallas_call(
        functools.partial(_wkv_kernel, valid=valid),
        grid=(n // nb,),
        in_specs=[tok] * 8 + [vec, vec, st],
        out_specs=[tok, st],
        out_shape=[jax.ShapeDtypeStruct((n, t, c), F32), jax.ShapeDtypeStruct(s0.shape, F32)],
        compiler_params=_cparams(("arbitrary",)),
        name="wkv_scan",
    )(r, w, k, v, a, b, g, bonus, lnx_w.reshape(1, c), lnx_b.reshape(1, c), s0)


WKV_CHUNK = 64


def _bd(x):
    xb = x.astype(BF16)
    left = lax.broadcasted_iota(jnp.int32, x.shape, 1) < HEAD
    zero = jnp.zeros_like(xb)
    return jnp.concatenate([jnp.where(left, xb, zero), jnp.where(left, zero, xb)], axis=0)


def _dot_nt(a, b):
    return lax.dot_general(a, b, (((1,), (1,)), ((), ())), preferred_element_type=F32)


def _dot_tn(a, b):
    return lax.dot_general(a, b, (((0,), (0,)), ((), ())), preferred_element_type=F32)


def _wkv_chunk_local_steps(r, lw, k, v, a, b, consts):
    tri, eye2, strict, incl, blocks = consts
    c = WKV_CHUNK
    n = range(len(r))
    zero = jnp.zeros((c, LANES), F32)

    def cumsum(x):
        hi = x.astype(BF16)
        lo = (x - hi.astype(F32)).astype(BF16)
        return _dot(tri, hi) + _dot(tri, lo)

    lc = [cumsum(lw[i]) for i in n]
    yield
    ltot = [lc[i][c - 1:c, :] for i in n]
    w_inv = [jnp.exp(-lc[i]) for i in n]
    ah = [a[i] * jnp.exp(lc[i] - lw[i]) for i in n]
    rh = [r[i] * jnp.exp(lc[i]) for i in n]
    yield
    sc = [_dot_nt(jnp.concatenate([ah[i], rh[i]], axis=0).astype(BF16),
                  jnp.concatenate([_bd(b[i] * w_inv[i]), _bd(k[i] * w_inv[i])], axis=0)) for i in n]
    yield
    aab = [jnp.where(strict, sc[i][:c, :LANES], zero) for i in n]

    t = [eye2 + jnp.where(blocks[0], aab[i], zero) for i in n]
    for lvl in range(1, len(blocks)):
        grow = blocks[lvl] & ~blocks[lvl - 1]
        x = [_dot(t[i].astype(BF16), _bd(jnp.where(grow, aab[i], zero))) for i in n]
        yield
        t = [t[i] + _dot(x[i].astype(BF16), _bd(t[i])) for i in n]
        yield

    vb = [_bd(v[i]) for i in n]
    akv = [_dot(jnp.where(strict, sc[i][:c, LANES:], zero).astype(BF16), vb[i]) for i in n]
    yield
    pu = [_dot(t[i].astype(BF16), jnp.concatenate([_bd(ah[i]), _bd(akv[i])], axis=1)) for i in n]
    yield

    left2 = (lax.broadcasted_iota(jnp.int32, (HEAD, 2 * LANES), 1) % LANES) < HEAD
    g, h, q, y0 = [], [], [], []
    for i in n:
        w_rem = jnp.exp(ltot[i] - lc[i])
        lhs_t = jnp.concatenate([b[i] * w_rem, k[i] * w_rem], axis=0).astype(BF16)
        rhs = jnp.concatenate([pu[i], jnp.concatenate([zero, v[i]], axis=1)], axis=0).astype(BF16)
        full = _dot_tn(lhs_t, rhs)
        gh = jnp.where(left2, full[:HEAD], full[HEAD:])
        g.append(gh[:, :LANES] + eye2 * jnp.exp(ltot[i]))
        h.append(gh[:, LANES:])
    yield
    zero_b = jnp.zeros((2 * c, LANES), BF16)
    for i in n:
        mrb = jnp.where(incl, sc[i][c:, :LANES], zero)
        mrk = jnp.where(incl, sc[i][c:, LANES:], zero)
        rhs2 = jnp.concatenate([jnp.concatenate([_bd(pu[i][:, :LANES]), _bd(pu[i][:, LANES:])], axis=1),
                                jnp.concatenate([zero_b, vb[i]], axis=1)], axis=0)
        qy = _dot(jnp.concatenate([mrb, mrk], axis=1).astype(BF16), rhs2)
        q.append(rh[i] + qy[:, :LANES])
        y0.append(qy[:, LANES:])
    yield
    return g, h, q, y0


def _wkv_fused_kernel(zr_ref, zk_ref, zv_ref, zl_ref, sr_ref, sk_ref, sv_ref, sl_ref,
                      mr_ref, mk_ref, mv_ref, ml_ref, w0_ref, w2_ref, a0_ref, a2_ref, g2_ref,
                      kk_ref, ka_ref, rk_ref, lnw_ref, lnb_ref, s0_ref, out_ref, st_ref,
                      s_scr, pr_scr, pk_scr, pv_scr, pl_scr, *, group_size):
    i = pl.program_id(1)
    c = WKV_CHUNK
    n_seq, tb = out_ref.shape[0], out_ref.shape[1]
    n_chunks = tb // c

    @pl.when(i == 0)
    def _():
        s_scr[...] = s0_ref[...]
        pr_scr[...] = sr_ref[...]
        pk_scr[...] = sk_ref[...]
        pv_scr[...] = sv_ref[...]
        pl_scr[...] = sl_ref[...]

    def shifted(z_ref, p_scr, mu_ref, b, lanes):
        z = z_ref[b, :, lanes]
        first = lax.broadcasted_iota(jnp.int32, z.shape, 0) == 0
        prev = jnp.where(first, p_scr[b:b + 1, lanes], pltpu.roll(z, 1, axis=0))
        p_scr[b:b + 1, lanes] = z[tb - 1:tb, :]
        return z + (prev - z) * mu_ref[:, lanes]

    ones = _pair_ones()
    c1 = w2_ref.shape[0]
    c2 = c1 + a2_ref.shape[0]
    ti = lax.broadcasted_iota(jnp.int32, (c, LANES), 0)
    si = lax.broadcasted_iota(jnp.int32, (c, LANES), 1) % HEAD
    tri = (lax.broadcasted_iota(jnp.int32, (c, c), 0) >= lax.broadcasted_iota(jnp.int32, (c, c), 1)
           ).astype(BF16)
    eye2 = (ti == si).astype(F32)
    sizes = [2 << l for l in range(c.bit_length() - 1)]
    blocks = [(ti // s) == (si // s) for s in sizes]
    consts = (tri, eye2, ti > si, ti >= si, blocks)
    lora_in = {}

    def feature_steps(group):
        out = []
        for b, sp in group:
            lanes = slice(sp * LANES, (sp + 1) * LANES)
            r = shifted(zr_ref, pr_scr, mr_ref, b, lanes)
            k = shifted(zk_ref, pk_scr, mk_ref, b, lanes)
            yield
            v = shifted(zv_ref, pv_scr, mv_ref, b, lanes)
            if b not in lora_in:
                lora_in[b] = shifted(zl_ref, pl_scr, ml_ref, b, slice(None))
            zl = lora_in[b]
            yield
            lw, k2, a_neg, bb, gate, bonus = yield from _rwkv_features_steps(
                r, k, v, zl[:, :c1], zl[:, c1:c2], zl[:, c2:], w0_ref[:, lanes], w2_ref[:, lanes],
                a0_ref[:, lanes], a2_ref[:, lanes], g2_ref[:, lanes], kk_ref[:, lanes], ka_ref[:, lanes],
                rk_ref[:, lanes], ones)
            out.append((r, lw, k2, v, a_neg, bb, gate, bonus))
        return out

    def local_steps(feat):
        chunks = lambda which: [f[which][ci * c:(ci + 1) * c, :] for ci in range(n_chunks) for f in feat]
        return (yield from _wkv_chunk_local_steps(*(chunks(which) for which in range(6)), consts))

    def tail_steps(group, feat, local):
        g, h, q, y0 = local
        st = [s_scr[b, sp] for b, sp in group]
        ys = [[] for _ in group]
        for ci in range(n_chunks):
            for j in range(len(group)):
                u = ci * len(group) + j
                out = _dot(jnp.concatenate([q[u], g[u]], axis=0).astype(BF16), _bd(st[j]))
                ys[j].append(out[:c] + y0[u])
                st[j] = out[c:] + h[u]
            yield
        for j, (b, sp) in enumerate(group):
            lanes = slice(sp * LANES, (sp + 1) * LANES)
            s_scr[b, sp] = st[j]
            out_ref[b, :, lanes] = _rwkv_out(jnp.concatenate(ys[j], axis=0), feat[j][7], feat[j][6],
                                             lnw_ref[:, lanes], lnb_ref[:, lanes], ones)
            yield

        @pl.when(i == pl.num_programs(1) - 1)
        def _():
            for j, (b, sp) in enumerate(group):
                st_ref[b, sp] = st[j]

    chains = [(b, sp) for b in range(n_seq) for sp in range(out_ref.shape[2] // LANES)]
    groups = [chains[j:j + group_size] for j in range(0, len(chains), group_size)]
    feats, locals_ = {}, {}
    for step in range(len(groups) + 2):
        gens, roles = [], []
        if 0 <= step - 1 < len(groups):
            gens.append(local_steps(feats[step - 1]))
            roles.append((locals_, step - 1))
        if step < len(groups):
            gens.append(feature_steps(groups[step]))
            roles.append((feats, step))
        if 0 <= step - 2 < len(groups):
            gens.append(tail_steps(groups[step - 2], feats[step - 2], locals_[step - 2]))
            roles.append(({}, step - 2))
        for (store, idx), val in zip(roles, _zip_stages(*gens)):
            store[idx] = val


def _wkv_fused(z_rwkv, shift0, s0, prm, tb, pp):
    n, t, cols = z_rwkv.shape
    width = prm["decay_w0"].shape[-1]
    bw = pp * LANES
    nblk = width // bw
    lora = cols - 3 * width
    assert width % bw == 0 and (3 * width) % lora == 0 and lora % LANES == 0
    lblk = 3 * width // lora
    row = lambda x: x.reshape(1, -1)
    mu = row(prm["shift_mu"])
    tok = lambda off: pl.BlockSpec((n, tb, bw), lambda p, i: (0, i, off + p))
    vec = lambda rows_, off: pl.BlockSpec((rows_, bw), lambda p, i: (0, off + p))
    st = pl.BlockSpec((n, pp, HEAD, LANES), lambda p, i: (0, p, 0, 0))
    in_specs = [tok(0), tok(nblk), tok(2 * nblk), pl.BlockSpec((n, tb, lora), lambda p, i: (0, i, lblk)),
                vec(n, 0), vec(n, nblk), vec(n, 2 * nblk), pl.BlockSpec((n, lora), lambda p, i: (0, lblk)),
                vec(1, 0), vec(1, nblk), vec(1, 2 * nblk), pl.BlockSpec((1, lora), lambda p, i: (0, lblk)),
                vec(1, 0), vec(prm["decay_w2"].shape[0], 0), vec(1, 0), vec(prm["aaa_a2"].shape[0], 0),
                vec(prm["gate_g2"].shape[0], 0), vec(1, 0), vec(1, 0), vec(1, 0), vec(1, 0), vec(1, 0), st]
    args = [z_rwkv] * 4 + [shift0] * 4 + [mu] * 4 + [
        row(prm["decay_w0"]), prm["decay_w2"], row(prm["aaa_a0"]), prm["aaa_a2"], prm["gate_g2"],
        row(prm["k_k"]), row(prm["k_a"]), row(prm["r_k"]), row(prm["lnx_w"]), row(prm["lnx_b"]), s0]
    return pl.pallas_call(
        functools.partial(_wkv_fused_kernel, group_size=2),
        grid=(nblk, t // tb),
        in_specs=in_specs,
        out_specs=[tok(0), st],
        out_shape=[jax.ShapeDtypeStruct((n, t, width), F32), jax.ShapeDtypeStruct(s0.shape, F32)],
        scratch_shapes=[pltpu.VMEM((n, pp, HEAD, LANES), F32), pltpu.VMEM((n, bw), F32),
                        pltpu.VMEM((n, bw), F32), pltpu.VMEM((n, bw), F32),
                        pltpu.VMEM((n, lora), F32)],
        compiler_params=_cparams(("arbitrary", "arbitrary")),
        name="wkv_fused",
    )(*args)


def _pack_state(s):
    n, h, vd, kd = s.shape
    return s.reshape(n, h // 2, 2, vd, kd).transpose(0, 1, 4, 2, 3).reshape(n, h // 2, kd, 2 * vd)


def _unpack_state(s, h):
    n, _, kd, _ = s.shape
    return s.reshape(n, h // 2, kd, 2, HEAD).transpose(0, 1, 3, 4, 2).reshape(n, h, HEAD, kd)


def _attn_prompt_kernel(q_ref, k_ref, kp_ref, v_ref, vp_ref, o_ref, l_ref, *, dil, sub, n_back, scale):
    first = pl.program_id(1) == 0
    qi = lax.broadcasted_iota(jnp.int32, (ATTN_BLOCK, 2 * ATTN_BLOCK), 0)
    ki = lax.broadcasted_iota(jnp.int32, (ATTN_BLOCK, 2 * ATTN_BLOCK), 1)
    dist = qi + ATTN_BLOCK - ki
    band = (dist >= 0) & (dist <= n_back)
    own = ki >= ATTN_BLOCK
    left = lax.broadcasted_iota(jnp.int32, (ATTN_BLOCK, LANES), 1) < HEAD
    ones_kv = jnp.ones((2 * ATTN_BLOCK, LANES), BF16)

    def rows(j, r):
        if dil == 1:
            return pl.ds(j * ATTN_BLOCK, ATTN_BLOCK)
        return pl.ds(j * ATTN_BLOCK * dil + r, ATTN_BLOCK, stride=dil)

    def units(jr):
        probs = []
        qs, kcat, vcat, masks = [], [], [], []
        for j, r in jr:
            q = (q_ref[0, rows(j, r), :] * scale).astype(BF16)
            if j == 0:
                kp, vp = kp_ref[0, rows(0, r), :], vp_ref[0, rows(0, r), :]
                masks.append(band & (own | jnp.logical_not(first)))
            else:
                kp, vp = k_ref[0, rows(j - 1, r), :], v_ref[0, rows(j - 1, r), :]
                masks.append(band)
            kcat.append(jnp.concatenate([kp, k_ref[0, rows(j, r), :]], axis=0).astype(BF16))
            vcat.append(jnp.concatenate([vp, v_ref[0, rows(j, r), :]], axis=0).astype(BF16))
            zq = jnp.zeros_like(q)
            qs += [jnp.where(left, q, zq), jnp.where(left, zq, q)]
            probs += [(len(kcat) - 1, 0), (len(kcat) - 1, 1)]
        s = [jnp.where(masks[u], _dot_nt(qs[p], kcat[u]), NEG) for p, (u, _) in enumerate(probs)]
        m = [jnp.max(x, axis=-1, keepdims=True) for x in s]
        e = [jnp.exp(s[p] - m[p]).astype(BF16) for p in range(len(probs))]
        o = [_dot(e[p], vcat[u]) for p, (u, _) in enumerate(probs)]
        dens = [_dot(e[p], ones_kv) for p in range(len(probs))]
        for u, (j, r) in enumerate(jr):
            den = jnp.where(left, dens[2 * u], dens[2 * u + 1])
            o_ref[0, rows(j, r), :] = jnp.where(left, o[2 * u], o[2 * u + 1]) / den
            l_ref[0, rows(j, r), :] = jnp.where(left, m[2 * u], m[2 * u + 1]) + jnp.log(den)

    if dil <= 4:
        units([(j, r) for j in range(sub) for r in range(dil)])
    else:
        for j in range(sub):
            def body(r4, carry, j=j):
                units([(j, r4 * 4 + u) for u in range(4)])
                return carry
            lax.fori_loop(0, dil // 4, body, 0)


def _attn_prompt(zqkv, gi, span):
    n, t, cols = zqkv.shape
    d = DILATIONS[gi]
    aw = ATTN_HEADS * HEAD
    ng = len(DILATIONS)
    back = ATTN_BLOCK * d
    sub = span // back
    halves = aw // LANES
    col = lambda which, hp: (which * ng + gi) * halves + hp
    cur = lambda which: pl.BlockSpec((1, span, LANES), lambda b, i, hp: (b, i, col(which, hp)))
    prev = lambda which: pl.BlockSpec((1, back, LANES),
                                      lambda b, i, hp: (b, jnp.maximum(i * sub - 1, 0), col(which, hp)))
    out_spec = pl.BlockSpec((1, span, LANES), lambda b, i, hp: (b, i, hp))
    out_sds = jax.ShapeDtypeStruct((n, t, aw), F32)
    return pl.pallas_call(
        functools.partial(_attn_prompt_kernel, dil=d, sub=sub, n_back=WINDOWS[gi] // d, scale=HEAD ** -0.5),
        grid=(n, t // span, halves),
        in_specs=[cur(0), cur(1), prev(1), cur(2), prev(2)],
        out_specs=[out_spec, out_spec],
        out_shape=[out_sds, out_sds],
        compiler_params=_cparams(("arbitrary", "arbitrary", "arbitrary")),
        name=f"attn_prompt_g{gi}",
    )(zqkv, zqkv, zqkv, zqkv, zqkv)


def _attn_sample_kernel(q_ref, k_ref, v_ref, c_ref, o_ref, l_ref, *, dil, n_back, valid, scale):
    nb, tp = q_ref.shape[0], q_ref.shape[1]
    cl = c_ref.shape[4]
    dist_c = (cl + lax.broadcasted_iota(jnp.int32, (tp, cl), 0)
              - lax.broadcasted_iota(jnp.int32, (tp, cl), 1))
    mask_c = (dist_c % dil == 0) & (dist_c <= dil * n_back)
    pn = lax.broadcasted_iota(jnp.int32, (tp, tp), 1)
    dist_n = lax.broadcasted_iota(jnp.int32, (tp, tp), 0) - pn
    mask_n = (dist_n >= 0) & (dist_n % dil == 0) & (dist_n <= dil * n_back) & (pn < valid)
    probs = [(b, h) for b in range(nb) for h in range(ATTN_HEADS)]
    sl = lambda h: slice(h * HEAD, (h + 1) * HEAD)
    q = [(q_ref[b, :, sl(h)] * scale).astype(BF16) for b, h in probs]
    s_c = [jnp.where(mask_c, _dot(q[p], c_ref[b, 0, h].astype(BF16)), NEG) for p, (b, h) in enumerate(probs)]
    s_n = [jnp.where(mask_n, _dot_nt(q[p], k_ref[b, :, sl(h)].astype(BF16)), NEG)
           for p, (b, h) in enumerate(probs)]
    m = [jnp.maximum(jnp.max(s_c[p], axis=-1, keepdims=True), jnp.max(s_n[p], axis=-1, keepdims=True))
         for p in range(len(probs))]
    e_c = [jnp.exp(s_c[p] - m[p]) for p in range(len(probs))]
    e_n = [jnp.exp(s_n[p] - m[p]) for p in range(len(probs))]
    den = [jnp.sum(e_c[p], axis=-1, keepdims=True) + jnp.sum(e_n[p], axis=-1, keepdims=True)
           for p in range(len(probs))]
    o = [_dot_nt(e_c[p].astype(BF16), c_ref[b, 1, h].astype(BF16))
         + _dot(e_n[p].astype(BF16), v_ref[b, :, sl(h)].astype(BF16)) for p, (b, h) in enumerate(probs)]
    for b in range(nb):
        ps = [p for p, (bb, _) in enumerate(probs) if bb == b]
        o_ref[b] = jnp.concatenate([o[p] / den[p] for p in ps], axis=1)
        l_ref[b] = jnp.concatenate([jnp.broadcast_to(m[p] + jnp.log(den[p]), (tp, HEAD)) for p in ps], axis=1)


def _attn_sample(zqkv, cache, gi, valid):
    n, tp, cols = zqkv.shape
    aw = ATTN_HEADS * HEAD
    ng = len(DILATIONS)
    d = DILATIONS[gi]
    cache_t = cache.transpose(0, 2, 3, 4, 1)
    nb = 2 if n % 2 == 0 else 1
    spec = lambda which: pl.BlockSpec((nb, tp, aw), lambda b: (b, 0, which * ng + gi))
    out_spec = pl.BlockSpec((nb, tp, aw), lambda b: (b, 0, 0))
    out_sds = jax.ShapeDtypeStruct((n, tp, aw), F32)
    return pl.pallas_call(
        functools.partial(_attn_sample_kernel, dil=d, n_back=WINDOWS[gi] // d, valid=valid,
                          scale=HEAD ** -0.5),
        grid=(n // nb,),
        in_specs=[spec(0), spec(1), spec(2),
                  pl.BlockSpec((nb,) + cache_t.shape[1:], lambda b: (b, 0, 0, 0, 0))],
        out_specs=[out_spec, out_spec],
        out_shape=[out_sds, out_sds],
        compiler_params=_cparams(("arbitrary",)),
        name=f"attn_sample_g{gi}",
    )(zqkv, zqkv, zqkv, cache_t)


def _post_kernel(x_ref, rwkv_ref, o0, o1, o2, l0, l1, l2, zg_ref, wbr_ref, wba_ref, wout_ref, g_ref,
                 w1_ref, w2_ref, p_ref, gp_ref, wg_ref, wp_ref, gf_ref, o_ref, h_scr, acc_scr, *, final):
    j = pl.program_id(1)

    @pl.when(j == 0)
    def _():
        ls = [l0[...], l1[...], l2[...]]
        m = jnp.maximum(jnp.maximum(ls[0], ls[1]), ls[2])
        es = [jnp.exp(l - m) for l in ls]
        attn = (es[0] * o0[...] + es[1] * o1[...] + es[2] * o2[...]) / (es[0] + es[1] + es[2])
        d = x_ref.shape[1]
        gates = zg_ref[...].astype(F32)
        mixed = (gates[:, :d] * _dot(rwkv_ref[...].astype(BF16), wbr_ref[...])
                 + gates[:, d:] * _dot(attn.astype(BF16), wba_ref[...]))
        x = x_ref[...] + _dot(mixed.astype(BF16), wout_ref[...])
        acc_scr[...] = x
        h_scr[...] = _rms(x, g_ref[...]).astype(BF16)

    u = jnp.maximum(_dot(h_scr[...], w1_ref[...]), 0.0)
    acc_scr[...] += _dot((u * u).astype(BF16), w2_ref[...])

    @pl.when(j == pl.num_programs(1) - 1)
    def _():
        x = acc_scr[...]
        h = _rms(x, gp_ref[...]).astype(BF16)
        x = x + _sigmoid(_dot(h, wg_ref[...])) * _dot(p_ref[...].astype(BF16), wp_ref[...])
        if final:
            x = _rms(x, gf_ref[...])
        o_ref[...] = x


def _post(x, rwkv, os_, ls_, zg, wbr, wba, wout, g, w1, w2, p, gp, wg, wp, gf, final, tm, tf):
    rows, d = x.shape
    dff = w1.shape[1]
    vec = lambda a: a.reshape(1, d)
    rowblk = lambda a: pl.BlockSpec((tm, a.shape[1]), lambda i, j: (i, 0))
    full = lambda a: pl.BlockSpec(a.shape, lambda i, j: (0, 0), pipeline_mode=pl.Buffered(1))
    row_args = [x, rwkv, *os_, *ls_, zg]
    return pl.pallas_call(
        functools.partial(_post_kernel, final=final),
        grid=(rows // tm, dff // tf),
        in_specs=[rowblk(a) for a in row_args] + [full(wbr), full(wba), full(wout), full(vec(g)),
                  pl.BlockSpec((d, tf), lambda i, j: (0, j)),
                  pl.BlockSpec((tf, d), lambda i, j: (j, 0)),
                  rowblk(p), full(vec(gp)), full(wg), full(wp), full(vec(gf))],
        out_specs=pl.BlockSpec((tm, d), lambda i, j: (i, 0)),
        out_shape=jax.ShapeDtypeStruct((rows, d), F32),
        scratch_shapes=[pltpu.VMEM((tm, d), BF16), pltpu.VMEM((tm, d), F32)],
        compiler_params=_cparams(("arbitrary", "arbitrary")),
        name="post",
    )(*row_args, wbr, wba, wout, vec(g), w1, w2, p, vec(gp), wg, wp, vec(gf))


def _layer(x, p, wkv0, shift0, caches, prm, norm_final, final, valid):
    n, t, d = x.shape
    rows = n * t
    width = prm["decay_w0"].shape[-1]
    rwkv_cols = prm["shift_mu"].shape[-1]
    aw = ATTN_HEADS * HEAD
    qkv_cols = 3 * len(DILATIONS) * aw
    tm = min(512, rows)
    x2 = x.reshape(rows, d)

    pick = lambda cols, opts: next(o for o in opts if cols % o == 0)
    z_rwkv, z_qkv, z_gate = _in_proj(x2, prm["norm_mix"], prm["w_in"].astype(BF16),
                                     (rwkv_cols, qkv_cols, 2 * d), min(256, rows))
    z_rwkv = z_rwkv.reshape(n, t, rwkv_cols)
    z_qkv = z_qkv.reshape(n, t, qkv_cols)

    if valid == t and t % WKV_CHUNK == 0:
        rwkv, s_t = _wkv_fused(z_rwkv, shift0, _pack_state(wkv0), prm, pick(t, (512, 256, WKV_CHUNK)), 2)
        wkv_t = _unpack_state(s_t, width // HEAD)
    else:
        feats = _rwkv_prep(z_rwkv, shift0, prm, min(256, t))
        rwkv, wkv_t = _wkv(*feats, prm["lnx_w"], prm["lnx_b"], wkv0, valid)
    shift_t = z_rwkv[:, valid - 1, :]

    os_, ls_, new_kv = [], [], []
    ng = len(DILATIONS)
    for gi in range(ng):
        if caches is None:
            o, l = _attn_prompt(z_qkv, gi, max(1024, ATTN_BLOCK * DILATIONS[gi]))
            keep = min(WINDOWS[gi], t)
            kv_rows = z_qkv[:, t - keep:t]
        else:
            o, l = _attn_sample(z_qkv, caches[gi], gi, valid)
            kv_rows = z_qkv[:, :valid]
        new_kv.append(kv_rows.reshape(n, -1, 3, ng, ATTN_HEADS, HEAD)[:, :, 1:, gi])
        os_.append(o.reshape(rows, aw))
        ls_.append(l.reshape(rows, aw))

    bf = lambda name: prm[name].astype(BF16)
    x2 = _post(x2, rwkv.reshape(rows, -1), os_, ls_, z_gate, bf("w_branch_rwkv"), bf("w_branch_attn"),
               bf("w_out"), prm["norm_mlp"], bf("w_ff1"), bf("w_ff2"), p.reshape(rows, -1),
               prm["norm_ple"], bf("w_ple_gate"), bf("w_ple"), norm_final, final, tm, 1024)
    return x2.reshape(n, t, d), new_kv, wkv_t, shift_t


_LAYER_PARAMS = ("norm_mix", "w_in", "shift_mu", "decay_w0", "decay_w2", "aaa_a0", "aaa_a2", "gate_g2",
                 "k_k", "k_a", "r_k", "lnx_w", "lnx_b", "w_branch_rwkv", "w_branch_attn", "w_out",
                 "norm_mlp", "w_ff1", "w_ff2", "norm_ple", "w_ple", "w_ple_gate")


def kernel(x_prompt, x_sample, cache_kv_g0, cache_kv_g1, cache_kv_g2, state_wkv, state_shift, p_prompt, p_sample, norm_mix, w_in, shift_mu, decay_w0, decay_w2, aaa_a0, aaa_a2, gate_g2, k_k, k_a, r_k, lnx_w, lnx_b, w_branch_rwkv, w_branch_attn, w_out, norm_mlp, w_ff1, w_ff2, norm_ple, w_ple, w_ple_gate, norm_final):
    stacked = dict(zip(_LAYER_PARAMS, (norm_mix, w_in, shift_mu, decay_w0, decay_w2, aaa_a0, aaa_a2,
                                       gate_g2, k_k, k_a, r_k, lnx_w, lnx_b, w_branch_rwkv,
                                       w_branch_attn, w_out, norm_mlp, w_ff1, w_ff2, norm_ple, w_ple,
                                       w_ple_gate)))
    depth = norm_mix.shape[0]
    nb_p, _, d = x_prompt.shape
    nb_s, t_s, _ = x_sample.shape
    heads = decay_w0.shape[-1] // HEAD
    rwkv_cols = shift_mu.shape[-1]
    t_pad = -(-t_s // SUBLANES) * SUBLANES
    pad_t = lambda u: jnp.pad(u, ((0, 0), (0, t_pad - t_s), (0, 0)))

    yp, ys = x_prompt, pad_t(x_sample)
    kvp, kvs = [[], [], []], [[], [], []]
    wkv_p, sh_p, wkv_s, sh_s = [], [], [], []
    for i in range(depth):
        prm = {name: val[i] for name, val in stacked.items()}
        prm["r_k"] = prm["r_k"].reshape(-1)
        last = i == depth - 1
        wkv0 = jnp.zeros((nb_p, heads, HEAD, HEAD), F32)
        shift0 = jnp.zeros((nb_p, rwkv_cols), F32)
        yp, nkv_p, w_p, s_p = _layer(yp, p_prompt[i], wkv0, shift0, None, prm, norm_final, last,
                                     x_prompt.shape[1])
        ys, nkv_s, w_s, s_s = _layer(ys, pad_t(p_sample[i]), state_wkv[i], state_shift[i],
                                     (cache_kv_g0[i], cache_kv_g1[i], cache_kv_g2[i]), prm, norm_final,
                                     last, t_s)
        for gi in range(3):
            kvp[gi].append(nkv_p[gi])
            kvs[gi].append(nkv_s[gi])
        wkv_p.append(w_p)
        sh_p.append(s_p)
        wkv_s.append(w_s)
        sh_s.append(s_s)
    return (yp, ys[:, :t_s],
            jnp.stack(kvp[0]), jnp.stack(kvp[1]), jnp.stack(kvp[2]),
            jnp.stack(wkv_p), jnp.stack(sh_p),
            jnp.stack(kvs[0]), jnp.stack(kvs[1]), jnp.stack(kvs[2]),
            jnp.stack(wkv_s), jnp.stack(sh_s))
```

```python
import functools

import jax
import jax.numpy as jnp
from jax import lax
from jax.experimental import pallas as pl
from jax.experimental.pallas import tpu as pltpu

F32 = jnp.float32
BF16 = jnp.bfloat16

HEAD = 64
LANES = 128
SUBLANES = 8
ATTN_HEADS = 4
ATTN_BLOCK = 128
WINDOWS = (128, 512, 2048)
DILATIONS = (1, 4, 16)
NORM_EPS = 1e-6
LNX_EPS = 64e-5
NEG = -1e30
VMEM_LIMIT = 56 * 1024 * 1024


def _cparams(sem):
    return pltpu.CompilerParams(dimension_semantics=sem, vmem_limit_bytes=VMEM_LIMIT)


def _dot(a, b):
    return jnp.dot(a, b, preferred_element_type=F32)


def _dot_bf16(a, b):
    return _dot(a.astype(BF16), b.astype(BF16))


def _pair_ones():
    r = lax.broadcasted_iota(jnp.int32, (LANES, LANES), 0) // HEAD
    c = lax.broadcasted_iota(jnp.int32, (LANES, LANES), 1) // HEAD
    return (r == c).astype(BF16)


def _head_sum(x, ones):
    outs = []
    for c in range(x.shape[1] // LANES):
        outs.append(_dot(x[:, c * LANES:(c + 1) * LANES].astype(BF16), ones))
    return jnp.concatenate(outs, axis=1)


def _rms(x, g):
    return x * lax.rsqrt(jnp.mean(x * x, axis=-1, keepdims=True) + NORM_EPS) * g


def _sigmoid(x):
    return 1.0 / (1.0 + jnp.exp(-x))


def _softplus(x):
    return jnp.maximum(x, 0.0) + jnp.log(1.0 + jnp.exp(-jnp.abs(x)))


def _in_proj_kernel(x_ref, g_ref, w_ref, zr_ref, zq_ref, gate_ref):
    h = _rms(x_ref[...], g_ref[...]).astype(BF16)
    c1 = zr_ref.shape[1]
    c2 = c1 + zq_ref.shape[1]
    zr_ref[...] = _dot(h, w_ref[:, :c1])
    zq_ref[...] = _dot(h, w_ref[:, c1:c2])
    gate_ref[...] = _sigmoid(_dot(h, w_ref[:, c2:])).astype(BF16)


def _in_proj(x, g, w, splits, tm):
    rows, d = x.shape
    assert sum(splits) == w.shape[1] and len(splits) == 3
    return pl.pallas_call(
        _in_proj_kernel,
        grid=(rows // tm,),
        in_specs=[pl.BlockSpec((tm, d), lambda i: (i, 0)),
                  pl.BlockSpec((1, d), lambda i: (0, 0)),
                  pl.BlockSpec(w.shape, lambda i: (0, 0), pipeline_mode=pl.Buffered(1))],
        out_specs=[pl.BlockSpec((tm, c), lambda i: (i, 0)) for c in splits],
        out_shape=[jax.ShapeDtypeStruct((rows, c), dt) for c, dt in zip(splits, (F32, F32, BF16))],
        compiler_params=_cparams(("arbitrary",)),
        name="in_proj",
    )(x, g.reshape(1, d), w)


def _drain(steps):
    try:
        while True:
            next(steps)
    except StopIteration as stop:
        return stop.value


def _zip_stages(*gens):
    vals = [None] * len(gens)
    live = list(range(len(gens)))
    while live:
        for i in list(live):
            try:
                next(gens[i])
            except StopIteration as stop:
                vals[i] = stop.value
                live.remove(i)
    return vals


def _rwkv_features_steps(r, k, v, xw, xa, xg, w0, w2, a0, a2, g2, k_k, k_a, r_k, ones):
    pre = w0 + _dot_bf16(jnp.tanh(xw), w2)
    yield
    lw = -jnp.exp(-_softplus(-pre) - 0.5)
    yield
    a = _sigmoid(a0 + _dot_bf16(xa, a2))
    yield
    g = _dot_bf16(_sigmoid(xg), g2)
    yield
    kk = k * k_k
    norm2 = _head_sum(kk * kk, ones)
    yield
    kk = kk / jnp.maximum(jnp.sqrt(norm2), 1e-12)
    k2 = k * (1.0 + (a - 1.0) * k_a)
    yield
    bonus = _head_sum(r * k2 * r_k, ones) * v
    yield
    return lw, k2, -kk, kk * a, g, bonus


def _rwkv_features(*args):
    return _drain(_rwkv_features_steps(*args))


def _rwkv_out(y, bonus, g, lnx_w, lnx_b, ones):
    mu = _head_sum(y, ones) * (1.0 / HEAD)
    yc = y - mu
    var = _head_sum(yc * yc, ones) * (1.0 / HEAD)
    return (yc * lax.rsqrt(var + LNX_EPS) * lnx_w + lnx_b + bonus) * g


def _rwkv_prep_kernel(z_ref, sh_ref, mu_ref, w0_ref, w2_ref, a0_ref, a2_ref, g2_ref, kk_ref, ka_ref,
                      rk_ref, r_o, w_o, k_o, v_o, a_o, b_o, g_o, bon_o, last_scr, *, width):
    n = pl.program_id(0)
    i = pl.program_id(1)
    z = z_ref[0]
    tt = z.shape[0]

    @pl.when(i == 0)
    def _():
        last_scr[...] = sh_ref[pl.ds(n, 1), :]

    row = lax.broadcasted_iota(jnp.int32, z.shape, 0)
    prev = jnp.where(row == 0, last_scr[...], pltpu.roll(z, 1, axis=0))
    last_scr[...] = z[tt - 1:tt, :]
    zs = z + (prev - z) * mu_ref[...]

    c0 = 3 * width
    c1 = c0 + w2_ref.shape[0]
    c2 = c1 + a2_ref.shape[0]
    r = zs[:, 0:width]
    k = zs[:, width:2 * width]
    v = zs[:, 2 * width:c0]
    xw = zs[:, c0:c1]
    xa = zs[:, c1:c2]
    xg = zs[:, c2:]

    lw, k2, a_neg, b, g, bonus = _rwkv_features(
        r, k, v, xw, xa, xg, w0_ref[...], w2_ref[...], a0_ref[...], a2_ref[...], g2_ref[...],
        kk_ref[...], ka_ref[...], rk_ref[...], _pair_ones())
    r_o[0] = r
    w_o[0] = lw
    k_o[0] = k2
    v_o[0] = v
    a_o[0] = a_neg
    b_o[0] = b
    g_o[0] = g
    bon_o[0] = bonus


def _rwkv_prep(z_rwkv, shift0, prm, tt):
    n, t, cols = z_rwkv.shape
    width = prm["decay_w0"].shape[-1]
    row = lambda x: x.reshape(1, -1)
    full = lambda a: pl.BlockSpec(a.shape, lambda b, i: (0,) * a.ndim)
    args = [shift0, row(prm["shift_mu"]), row(prm["decay_w0"]), prm["decay_w2"], row(prm["aaa_a0"]),
            prm["aaa_a2"], prm["gate_g2"], row(prm["k_k"]), row(prm["k_a"]), row(prm["r_k"])]
    out_spec = pl.BlockSpec((1, tt, width), lambda b, i: (b, i, 0))
    out_sds = jax.ShapeDtypeStruct((n, t, width), F32)
    return pl.pallas_call(
        functools.partial(_rwkv_prep_kernel, width=width),
        grid=(n, t // tt),
        in_specs=[pl.BlockSpec((1, tt, cols), lambda b, i: (b, i, 0))] + [full(a) for a in args],
        out_specs=[out_spec] * 8,
        out_shape=[out_sds] * 8,
        scratch_shapes=[pltpu.VMEM((1, cols), F32)],
        compiler_params=_cparams(("arbitrary", "arbitrary")),
        name="rwkv_prep",
    )(z_rwkv, *args)


def _wkv_kernel(r_ref, w_ref, k_ref, v_ref, a_ref, b_ref, g_ref, bon_ref, lnw_ref, lnb_ref, s0_ref,
                y_ref, st_ref, *, valid):
    ones = _pair_ones()
    row = lax.broadcasted_iota(jnp.int32, (HEAD, LANES), 0)
    lane = lax.broadcasted_iota(jnp.int32, (HEAD, LANES), 1)
    left = lane < HEAD
    diag = (row == lane % HEAD).astype(F32)

    def key_sum(x):
        hi = x.astype(BF16)
        lo = (x - hi.astype(F32)).astype(BF16)
        return _dot(hi, ones) + _dot(lo, ones)

    chains = [(b, p) for b in range(y_ref.shape[0]) for p in range(s0_ref.shape[1] // 2)]
    cols = lambda p: slice(p * LANES, (p + 1) * LANES)
    vt = [v_ref[b, :, cols(p)].T for b, p in chains]
    s = [jnp.concatenate([s0_ref[b, 2 * p], s0_ref[b, 2 * p + 1]], axis=1) for b, p in chains]
    ys = [[] for _ in chains]
    for j in range(y_ref.shape[1]):
        for ch, (b, p) in enumerate(chains):
            if j < valid:
                row_of = lambda ref: ref[b, j:j + 1, cols(p)]
                v_col = jnp.where(left, vt[ch][0:HEAD, j:j + 1], vt[ch][HEAD:2 * HEAD, j:j + 1])
                sa = key_sum(s[ch] * row_of(a_ref))
                s[ch] = s[ch] * jnp.exp(row_of(w_ref)) + sa * row_of(b_ref) + v_col * row_of(k_ref)
                y_col = key_sum(s[ch] * row_of(r_ref))
                ys[ch].append(jnp.sum(y_col * diag, axis=0, keepdims=True))
            else:
                ys[ch].append(jnp.zeros((1, LANES), F32))
    for ch, (b, p) in enumerate(chains):
        y_ref[b, :, cols(p)] = _rwkv_out(jnp.concatenate(ys[ch], axis=0), bon_ref[b, :, cols(p)],
                                         g_ref[b, :, cols(p)], lnw_ref[:, cols(p)], lnb_ref[:, cols(p)], ones)
        st_ref[b, 2 * p] = s[ch][:, :HEAD]
        st_ref[b, 2 * p + 1] = s[ch][:, HEAD:]


def _wkv(r, w, k, v, a, b, g, bonus, lnx_w, lnx_b, s0, valid):
    n, t, c = r.shape
    assert t == SUBLANES
    nb = 2 if n % 2 == 0 else 1
    tok = pl.BlockSpec((nb, t, c), lambda bi: (bi, 0, 0))
    vec = pl.BlockSpec((1, c), lambda bi: (0, 0))
    st = pl.BlockSpec((nb,) + s0.shape[1:], lambda bi: (bi, 0, 0, 0))
    return pl.pallas_call(
        functools.partial(_wkv_kernel, valid=valid),
        grid=(n // nb,),
        in_specs=[tok] * 8 + [vec, vec, st],
        out_specs=[tok, st],
        out_shape=[jax.ShapeDtypeStruct((n, t, c), F32), jax.ShapeDtypeStruct(s0.shape, F32)],
        compiler_params=_cparams(("arbitrary",)),
        name="wkv_scan",
    )(r, w, k, v, a, b, g, bonus, lnx_w.reshape(1, c), lnx_b.reshape(1, c), s0)


WKV_CHUNK = 64


def _bd(x):
    xb = x.astype(BF16)
    left = lax.broadcasted_iota(jnp.int32, x.shape, 1) < HEAD
    zero = jnp.zeros_like(xb)
    return jnp.concatenate([jnp.where(left, xb, zero), jnp.where(left, zero, xb)], axis=0)


def _dot_nt(a, b):
    return lax.dot_general(a, b, (((1,), (1,)), ((), ())), preferred_element_type=F32)


def _dot_tn(a, b):
    return lax.dot_general(a, b, (((0,), (0,)), ((), ())), preferred_element_type=F32)


def _wkv_chunk_local_steps(r, lw, k, v, a, b, consts):
    tri, eye2, strict, incl, blocks = consts
    c = WKV_CHUNK
    n = range(len(r))
    zero = jnp.zeros((c, LANES), F32)

    def cumsum(x):
        hi = x.astype(BF16)
        lo = (x - hi.astype(F32)).astype(BF16)
        return _dot(tri, hi) + _dot(tri, lo)

    lc = [cumsum(lw[i]) for i in n]
    yield
    ltot = [lc[i][c - 1:c, :] for i in n]
    w_inv = [jnp.exp(-lc[i]) for i in n]
    ah = [a[i] * jnp.exp(lc[i] - lw[i]) for i in n]
    rh = [r[i] * jnp.exp(lc[i]) for i in n]
    yield
    sc = [_dot_nt(jnp.concatenate([ah[i], rh[i]], axis=0).astype(BF16),
                  jnp.concatenate([_bd(b[i] * w_inv[i]), _bd(k[i] * w_inv[i])], axis=0)) for i in n]
    yield
    aab = [jnp.where(strict, sc[i][:c, :LANES], zero) for i in n]

    t = [eye2 + jnp.where(blocks[0], aab[i], zero) for i in n]
    for lvl in range(1, len(blocks)):
        grow = blocks[lvl] & ~blocks[lvl - 1]
        x = [_dot(t[i].astype(BF16), _bd(jnp.where(grow, aab[i], zero))) for i in n]
        yield
        t = [t[i] + _dot(x[i].astype(BF16), _bd(t[i])) for i in n]
        yield

    vb = [_bd(v[i]) for i in n]
    akv = [_dot(jnp.where(strict, sc[i][:c, LANES:], zero).astype(BF16), vb[i]) for i in n]
    yield
    pu = [_dot(t[i].astype(BF16), jnp.concatenate([_bd(ah[i]), _bd(akv[i])], axis=1)) for i in n]
    yield

    left2 = (lax.broadcasted_iota(jnp.int32, (HEAD, 2 * LANES), 1) % LANES) < HEAD
    g, h, q, y0 = [], [], [], []
    for i in n:
        w_rem = jnp.exp(ltot[i] - lc[i])
        lhs_t = jnp.concatenate([b[i] * w_rem, k[i] * w_rem], axis=0).astype(BF16)
        rhs = jnp.concatenate([pu[i], jnp.concatenate([zero, v[i]], axis=1)], axis=0).astype(BF16)
        full = _dot_tn(lhs_t, rhs)
        gh = jnp.where(left2, full[:HEAD], full[HEAD:])
        g.append(gh[:, :LANES] + eye2 * jnp.exp(ltot[i]))
        h.append(gh[:, LANES:])
    yield
    zero_b = jnp.zeros((2 * c, LANES), BF16)
    for i in n:
        mrb = jnp.where(incl, sc[i][c:, :LANES], zero)
        mrk = jnp.where(incl, sc[i][c:, LANES:], zero)
        rhs2 = jnp.concatenate([jnp.concatenate([_bd(pu[i][:, :LANES]), _bd(pu[i][:, LANES:])], axis=1),
                                jnp.concatenate([zero_b, vb[i]], axis=1)], axis=0)
        qy = _dot(jnp.concatenate([mrb, mrk], axis=1).astype(BF16), rhs2)
        q.append(rh[i] + qy[:, :LANES])
        y0.append(qy[:, LANES:])
    yield
    return g, h, q, y0


def _wkv_fused_kernel(zr_ref, zk_ref, zv_ref, zl_ref, sr_ref, sk_ref, sv_ref, sl_ref,
                      mr_ref, mk_ref, mv_ref, ml_ref, w0_ref, w2_ref, a0_ref, a2_ref, g2_ref,
                      kk_ref, ka_ref, rk_ref, lnw_ref, lnb_ref, s0_ref, out_ref, st_ref,
                      s_scr, pr_scr, pk_scr, pv_scr, pl_scr, *, group_size):
    i = pl.program_id(1)
    c = WKV_CHUNK
    n_seq, tb = out_ref.shape[0], out_ref.shape[1]
    n_chunks = tb // c

    @pl.when(i == 0)
    def _():
        s_scr[...] = s0_ref[...]
        pr_scr[...] = sr_ref[...]
        pk_scr[...] = sk_ref[...]
        pv_scr[...] = sv_ref[...]
        pl_scr[...] = sl_ref[...]

    def shifted(z_ref, p_scr, mu_ref, b, lanes):
        z = z_ref[b, :, lanes]
        first = lax.broadcasted_iota(jnp.int32, z.shape, 0) == 0
        prev = jnp.where(first, p_scr[b:b + 1, lanes], pltpu.roll(z, 1, axis=0))
        p_scr[b:b + 1, lanes] = z[tb - 1:tb, :]
        return z + (prev - z) * mu_ref[:, lanes]

    ones = _pair_ones()
    c1 = w2_ref.shape[0]
    c2 = c1 + a2_ref.shape[0]
    ti = lax.broadcasted_iota(jnp.int32, (c, LANES), 0)
    si = lax.broadcasted_iota(jnp.int32, (c, LANES), 1) % HEAD
    tri = (lax.broadcasted_iota(jnp.int32, (c, c), 0) >= lax.broadcasted_iota(jnp.int32, (c, c), 1)
           ).astype(BF16)
    eye2 = (ti == si).astype(F32)
    sizes = [2 << l for l in range(c.bit_length() - 1)]
    blocks = [(ti // s) == (si // s) for s in sizes]
    consts = (tri, eye2, ti > si, ti >= si, blocks)
    lora_in = {}

    def feature_steps(group):
        out = []
        for b, sp in group:
            lanes = slice(sp * LANES, (sp + 1) * LANES)
            r = shifted(zr_ref, pr_scr, mr_ref, b, lanes)
            k = shifted(zk_ref, pk_scr, mk_ref, b, lanes)
            yield
            v = shifted(zv_ref, pv_scr, mv_ref, b, lanes)
            if b not in lora_in:
                lora_in[b] = shifted(zl_ref, pl_scr, ml_ref, b, slice(None))
            zl = lora_in[b]
            yield
            lw, k2, a_neg, bb, gate, bonus = yield from _rwkv_features_steps(
                r, k, v, zl[:, :c1], zl[:, c1:c2], zl[:, c2:], w0_ref[:, lanes], w2_ref[:, lanes],
                a0_ref[:, lanes], a2_ref[:, lanes], g2_ref[:, lanes], kk_ref[:, lanes], ka_ref[:, lanes],
                rk_ref[:, lanes], ones)
            out.append((r, lw, k2, v, a_neg, bb, gate, bonus))
        return out

    def local_steps(feat):
        chunks = lambda which: [f[which][ci * c:(ci + 1) * c, :] for ci in range(n_chunks) for f in feat]
        return (yield from _wkv_chunk_local_steps(*(chunks(which) for which in range(6)), consts))

    def tail_steps(group, feat, local):
        g, h, q, y0 = local
        st = [s_scr[b, sp] for b, sp in group]
        ys = [[] for _ in group]
        for ci in range(n_chunks):
            for j in range(len(group)):
                u = ci * len(group) + j
                out = _dot(jnp.concatenate([q[u], g[u]], axis=0).astype(BF16), _bd(st[j]))
                ys[j].append(out[:c] + y0[u])
                st[j] = out[c:] + h[u]
            yield
        for j, (b, sp) in enumerate(group):
            lanes = slice(sp * LANES, (sp + 1) * LANES)
            s_scr[b, sp] = st[j]
            out_ref[b, :, lanes] = _rwkv_out(jnp.concatenate(ys[j], axis=0), feat[j][7], feat[j][6],
                                             lnw_ref[:, lanes], lnb_ref[:, lanes], ones)
            yield

        @pl.when(i == pl.num_programs(1) - 1)
        def _():
            for j, (b, sp) in enumerate(group):
                st_ref[b, sp] = st[j]

    chains = [(b, sp) for b in range(n_seq) for sp in range(out_ref.shape[2] // LANES)]
    groups = [chains[j:j + group_size] for j in range(0, len(chains), group_size)]
    feats, locals_ = {}, {}
    for step in range(len(groups) + 2):
        gens, roles = [], []
        if 0 <= step - 1 < len(groups):
            gens.append(local_steps(feats[step - 1]))
            roles.append((locals_, step - 1))
        if step < len(groups):
            gens.append(feature_steps(groups[step]))
            roles.append((feats, step))
        if 0 <= step - 2 < len(groups):
            gens.append(tail_steps(groups[step - 2], feats[step - 2], locals_[step - 2]))
            roles.append(({}, step - 2))
        for (store, idx), val in zip(roles, _zip_stages(*gens)):
            store[idx] = val


def _wkv_fused(z_rwkv, shift0, s0, prm, tb, pp):
    n, t, cols = z_rwkv.shape
    width = prm["decay_w0"].shape[-1]
    bw = pp * LANES
    nblk = width // bw
    lora = cols - 3 * width
    assert width % bw == 0 and (3 * width) % lora == 0 and lora % LANES == 0
    lblk = 3 * width // lora
    row = lambda x: x.reshape(1, -1)
    mu = row(prm["shift_mu"])
    tok = lambda off: pl.BlockSpec((n, tb, bw), lambda p, i: (0, i, off + p))
    vec = lambda rows_, off: pl.BlockSpec((rows_, bw), lambda p, i: (0, off + p))
    st = pl.BlockSpec((n, pp, HEAD, LANES), lambda p, i: (0, p, 0, 0))
    in_specs = [tok(0), tok(nblk), tok(2 * nblk), pl.BlockSpec((n, tb, lora), lambda p, i: (0, i, lblk)),
                vec(n, 0), vec(n, nblk), vec(n, 2 * nblk), pl.BlockSpec((n, lora), lambda p, i: (0, lblk)),
                vec(1, 0), vec(1, nblk), vec(1, 2 * nblk), pl.BlockSpec((1, lora), lambda p, i: (0, lblk)),
                vec(1, 0), vec(prm["decay_w2"].shape[0], 0), vec(1, 0), vec(prm["aaa_a2"].shape[0], 0),
                vec(prm["gate_g2"].shape[0], 0), vec(1, 0), vec(1, 0), vec(1, 0), vec(1, 0), vec(1, 0), st]
    args = [z_rwkv] * 4 + [shift0] * 4 + [mu] * 4 + [
        row(prm["decay_w0"]), prm["decay_w2"], row(prm["aaa_a0"]), prm["aaa_a2"], prm["gate_g2"],
        row(prm["k_k"]), row(prm["k_a"]), row(prm["r_k"]), row(prm["lnx_w"]), row(prm["lnx_b"]), s0]
    return pl.pallas_call(
        functools.partial(_wkv_fused_kernel, group_size=2),
        grid=(nblk, t // tb),
        in_specs=in_specs,
        out_specs=[tok(0), st],
        out_shape=[jax.ShapeDtypeStruct((n, t, width), F32), jax.ShapeDtypeStruct(s0.shape, F32)],
        scratch_shapes=[pltpu.VMEM((n, pp, HEAD, LANES), F32), pltpu.VMEM((n, bw), F32),
                        pltpu.VMEM((n, bw), F32), pltpu.VMEM((n, bw), F32),
                        pltpu.VMEM((n, lora), F32)],
        compiler_params=_cparams(("arbitrary", "arbitrary")),
        name="wkv_fused",
    )(*args)


def _pack_state(s):
    n, h, vd, kd = s.shape
    return s.reshape(n, h // 2, 2, vd, kd).transpose(0, 1, 4, 2, 3).reshape(n, h // 2, kd, 2 * vd)


def _unpack_state(s, h):
    n, _, kd, _ = s.shape
    return s.reshape(n, h // 2, kd, 2, HEAD).transpose(0, 1, 3, 4, 2).reshape(n, h, HEAD, kd)


def _attn_prompt_kernel(q_ref, k_ref, kp_ref, v_ref, vp_ref, o_ref, l_ref, *, dil, sub, n_back, scale):
    first = pl.program_id(1) == 0
    qi = lax.broadcasted_iota(jnp.int32, (ATTN_BLOCK, 2 * ATTN_BLOCK), 0)
    ki = lax.broadcasted_iota(jnp.int32, (ATTN_BLOCK, 2 * ATTN_BLOCK), 1)
    dist = qi + ATTN_BLOCK - ki
    band = (dist >= 0) & (dist <= n_back)
    own = ki >= ATTN_BLOCK
    left = lax.broadcasted_iota(jnp.int32, (ATTN_BLOCK, LANES), 1) < HEAD
    ones_kv = jnp.ones((2 * ATTN_BLOCK, LANES), BF16)

    def rows(j, r):
        if dil == 1:
            return pl.ds(j * ATTN_BLOCK, ATTN_BLOCK)
        return pl.ds(j * ATTN_BLOCK * dil + r, ATTN_BLOCK, stride=dil)

    def units(jr):
        probs = []
        qs, kcat, vcat, masks = [], [], [], []
        for j, r in jr:
            q = (q_ref[0, rows(j, r), :] * scale).astype(BF16)
            if j == 0:
                kp, vp = kp_ref[0, rows(0, r), :], vp_ref[0, rows(0, r), :]
                masks.append(band & (own | jnp.logical_not(first)))
            else:
                kp, vp = k_ref[0, rows(j - 1, r), :], v_ref[0, rows(j - 1, r), :]
                masks.append(band)
            kcat.append(jnp.concatenate([kp, k_ref[0, rows(j, r), :]], axis=0).astype(BF16))
            vcat.append(jnp.concatenate([vp, v_ref[0, rows(j, r), :]], axis=0).astype(BF16))
            zq = jnp.zeros_like(q)
            qs += [jnp.where(left, q, zq), jnp.where(left, zq, q)]
            probs += [(len(kcat) - 1, 0), (len(kcat) - 1, 1)]
        s = [jnp.where(masks[u], _dot_nt(qs[p], kcat[u]), NEG) for p, (u, _) in enumerate(probs)]
        m = [jnp.max(x, axis=-1, keepdims=True) for x in s]
        e = [jnp.exp(s[p] - m[p]).astype(BF16) for p in range(len(probs))]
        o = [_dot(e[p], vcat[u]) for p, (u, _) in enumerate(probs)]
        dens = [_dot(e[p], ones_kv) for p in range(len(probs))]
        for u, (j, r) in enumerate(jr):
            den = jnp.where(left, dens[2 * u], dens[2 * u + 1])
            o_ref[0, rows(j, r), :] = jnp.where(left, o[2 * u], o[2 * u + 1]) / den
            l_ref[0, rows(j, r), :] = jnp.where(left, m[2 * u], m[2 * u + 1]) + jnp.log(den)

    if dil <= 4:
        units([(j, r) for j in range(sub) for r in range(dil)])
    else:
        for j in range(sub):
            def body(r4, carry, j=j):
                units([(j, r4 * 4 + u) for u in range(4)])
                return carry
            lax.fori_loop(0, dil // 4, body, 0)


def _attn_prompt(zqkv, gi, span):
    n, t, cols = zqkv.shape
    d = DILATIONS[gi]
    aw = ATTN_HEADS * HEAD
    ng = len(DILATIONS)
    back = ATTN_BLOCK * d
    sub = span // back
    halves = aw // LANES
    col = lambda which, hp: (which * ng + gi) * halves + hp
    cur = lambda which: pl.BlockSpec((1, span, LANES), lambda b, i, hp: (b, i, col(which, hp)))
    prev = lambda which: pl.BlockSpec((1, back, LANES),
                                      lambda b, i, hp: (b, jnp.maximum(i * sub - 1, 0), col(which, hp)))
    out_spec = pl.BlockSpec((1, span, LANES), lambda b, i, hp: (b, i, hp))
    out_sds = jax.ShapeDtypeStruct((n, t, aw), F32)
    return pl.pallas_call(
        functools.partial(_attn_prompt_kernel, dil=d, sub=sub, n_back=WINDOWS[gi] // d, scale=HEAD ** -0.5),
        grid=(n, t // span, halves),
        in_specs=[cur(0), cur(1), prev(1), cur(2), prev(2)],
        out_specs=[out_spec, out_spec],
        out_shape=[out_sds, out_sds],
        compiler_params=_cparams(("arbitrary", "arbitrary", "arbitrary")),
        name=f"attn_prompt_g{gi}",
    )(zqkv, zqkv, zqkv, zqkv, zqkv)


def _attn_sample_kernel(q_ref, k_ref, v_ref, c_ref, o_ref, l_ref, *, dil, n_back, valid, scale):
    nb, tp = q_ref.shape[0], q_ref.shape[1]
    cl = c_ref.shape[4]
    dist_c = (cl + lax.broadcasted_iota(jnp.int32, (tp, cl), 0)
              - lax.broadcasted_iota(jnp.int32, (tp, cl), 1))
    mask_c = (dist_c % dil == 0) & (dist_c <= dil * n_back)
    pn = lax.broadcasted_iota(jnp.int32, (tp, tp), 1)
    dist_n = lax.broadcasted_iota(jnp.int32, (tp, tp), 0) - pn
    mask_n = (dist_n >= 0) & (dist_n % dil == 0) & (dist_n <= dil * n_back) & (pn < valid)
    probs = [(b, h) for b in range(nb) for h in range(ATTN_HEADS)]
    sl = lambda h: slice(h * HEAD, (h + 1) * HEAD)
    q = [(q_ref[b, :, sl(h)] * scale).astype(BF16) for b, h in probs]
    s_c = [jnp.where(mask_c, _dot(q[p], c_ref[b, 0, h].astype(BF16)), NEG) for p, (b, h) in enumerate(probs)]
    s_n = [jnp.where(mask_n, _dot_nt(q[p], k_ref[b, :, sl(h)].astype(BF16)), NEG)
           for p, (b, h) in enumerate(probs)]
    m = [jnp.maximum(jnp.max(s_c[p], axis=-1, keepdims=True), jnp.max(s_n[p], axis=-1, keepdims=True))
         for p in range(len(probs))]
    e_c = [jnp.exp(s_c[p] - m[p]) for p in range(len(probs))]
    e_n = [jnp.exp(s_n[p] - m[p]) for p in range(len(probs))]
    den = [jnp.sum(e_c[p], axis=-1, keepdims=True) + jnp.sum(e_n[p], axis=-1, keepdims=True)
           for p in range(len(probs))]
    o = [_dot_nt(e_c[p].astype(BF16), c_ref[b, 1, h].astype(BF16))
         + _dot(e_n[p].astype(BF16), v_ref[b, :, sl(h)].astype(BF16)) for p, (b, h) in enumerate(probs)]
    for b in range(nb):
        ps = [p for p, (bb, _) in enumerate(probs) if bb == b]
        o_ref[b] = jnp.concatenate([o[p] / den[p] for p in ps], axis=1)
        l_ref[b] = jnp.concatenate([jnp.broadcast_to(m[p] + jnp.log(den[p]), (tp, HEAD)) for p in ps], axis=1)


def _attn_sample(zqkv, cache, gi, valid):
    n, tp, cols = zqkv.shape
    aw = ATTN_HEADS * HEAD
    ng = len(DILATIONS)
    d = DILATIONS[gi]
    cache_t = cache.transpose(0, 2, 3, 4, 1)
    nb = 2 if n % 2 == 0 else 1
    spec = lambda which: pl.BlockSpec((nb, tp, aw), lambda b: (b, 0, which * ng + gi))
    out_spec = pl.BlockSpec((nb, tp, aw), lambda b: (b, 0, 0))
    out_sds = jax.ShapeDtypeStruct((n, tp, aw), F32)
    return pl.pallas_call(
        functools.partial(_attn_sample_kernel, dil=d, n_back=WINDOWS[gi] // d, valid=valid,
                          scale=HEAD ** -0.5),
        grid=(n // nb,),
        in_specs=[spec(0), spec(1), spec(2),
                  pl.BlockSpec((nb,) + cache_t.shape[1:], lambda b: (b, 0, 0, 0, 0))],
        out_specs=[out_spec, out_spec],
        out_shape=[out_sds, out_sds],
        compiler_params=_cparams(("arbitrary",)),
        name=f"attn_sample_g{gi}",
    )(zqkv, zqkv, zqkv, cache_t)


def _post_kernel(x_ref, rwkv_ref, o0, o1, o2, l0, l1, l2, zg_ref, wbr_ref, wba_ref, wout_ref, g_ref,
                 w1_ref, w2_ref, p_ref, gp_ref, wg_ref, wp_ref, gf_ref, o_ref, h_scr, acc_scr, *, final):
    j = pl.program_id(1)

    @pl.when(j == 0)
    def _():
        ls = [l0[...], l1[...], l2[...]]
        m = jnp.maximum(jnp.maximum(ls[0], ls[1]), ls[2])
        es = [jnp.exp(l - m) for l in ls]
        attn = (es[0] * o0[...] + es[1] * o1[...] + es[2] * o2[...]) / (es[0] + es[1] + es[2])
        d = x_ref.shape[1]
        gates = zg_ref[...].astype(F32)
        mixed = (gates[:, :d] * _dot(rwkv_ref[...].astype(BF16), wbr_ref[...])
                 + gates[:, d:] * _dot(attn.astype(BF16), wba_ref[...]))
        x = x_ref[...] + _dot(mixed.astype(BF16), wout_ref[...])
        acc_scr[...] = x
        h_scr[...] = _rms(x, g_ref[...]).astype(BF16)

    u = jnp.maximum(_dot(h_scr[...], w1_ref[...]), 0.0)
    acc_scr[...] += _dot((u * u).astype(BF16), w2_ref[...])

    @pl.when(j == pl.num_programs(1) - 1)
    def _():
        x = acc_scr[...]
        h = _rms(x, gp_ref[...]).astype(BF16)
        x = x + _sigmoid(_dot(h, wg_ref[...])) * _dot(p_ref[...].astype(BF16), wp_ref[...])
        if final:
            x = _rms(x, gf_ref[...])
        o_ref[...] = x


def _post(x, rwkv, os_, ls_, zg, wbr, wba, wout, g, w1, w2, p, gp, wg, wp, gf, final, tm, tf):
    rows, d = x.shape
    dff = w1.shape[1]
    vec = lambda a: a.reshape(1, d)
    rowblk = lambda a: pl.BlockSpec((tm, a.shape[1]), lambda i, j: (i, 0))
    full = lambda a: pl.BlockSpec(a.shape, lambda i, j: (0, 0), pipeline_mode=pl.Buffered(1))
    row_args = [x, rwkv, *os_, *ls_, zg]
    return pl.pallas_call(
        functools.partial(_post_kernel, final=final),
        grid=(rows // tm, dff // tf),
        in_specs=[rowblk(a) for a in row_args] + [full(wbr), full(wba), full(wout), full(vec(g)),
                  pl.BlockSpec((d, tf), lambda i, j: (0, j)),
                  pl.BlockSpec((tf, d), lambda i, j: (j, 0)),
                  rowblk(p), full(vec(gp)), full(wg), full(wp), full(vec(gf))],
        out_specs=pl.BlockSpec((tm, d), lambda i, j: (i, 0)),
        out_shape=jax.ShapeDtypeStruct((rows, d), F32),
        scratch_shapes=[pltpu.VMEM((tm, d), BF16), pltpu.VMEM((tm, d), F32)],
        compiler_params=_cparams(("arbitrary", "arbitrary")),
        name="post",
    )(*row_args, wbr, wba, wout, vec(g), w1, w2, p, vec(gp), wg, wp, vec(gf))


def _layer(x, p, wkv0, shift0, caches, prm, norm_final, final, valid):
    n, t, d = x.shape
    rows = n * t
    width = prm["decay_w0"].shape[-1]
    rwkv_cols = prm["shift_mu"].shape[-1]
    aw = ATTN_HEADS * HEAD
    qkv_cols = 3 * len(DILATIONS) * aw
    tm = min(512, rows)
    x2 = x.reshape(rows, d)

    pick = lambda cols, opts: next(o for o in opts if cols % o == 0)
    z_rwkv, z_qkv, z_gate = _in_proj(x2, prm["norm_mix"], prm["w_in"].astype(BF16),
                                     (rwkv_cols, qkv_cols, 2 * d), tm)
    z_rwkv = z_rwkv.reshape(n, t, rwkv_cols)
    z_qkv = z_qkv.reshape(n, t, qkv_cols)

    if valid == t and t % WKV_CHUNK == 0:
        rwkv, s_t = _wkv_fused(z_rwkv, shift0, _pack_state(wkv0), prm, pick(t, (512, 256, WKV_CHUNK)), 2)
        wkv_t = _unpack_state(s_t, width // HEAD)
    else:
        feats = _rwkv_prep(z_rwkv, shift0, prm, min(256, t))
        rwkv, wkv_t = _wkv(*feats, prm["lnx_w"], prm["lnx_b"], wkv0, valid)
    shift_t = z_rwkv[:, valid - 1, :]

    os_, ls_, new_kv = [], [], []
    ng = len(DILATIONS)
    for gi in range(ng):
        if caches is None:
            o, l = _attn_prompt(z_qkv, gi, max(1024, ATTN_BLOCK * DILATIONS[gi]))
            keep = min(WINDOWS[gi], t)
            kv_rows = z_qkv[:, t - keep:t]
        else:
            o, l = _attn_sample(z_qkv, caches[gi], gi, valid)
            kv_rows = z_qkv[:, :valid]
        new_kv.append(jnp.stack([kv_rows[:, :, (which * ng + gi) * aw:(which * ng + gi + 1) * aw]
                                 .reshape(n, -1, ATTN_HEADS, HEAD) for which in (1, 2)], axis=2))
        os_.append(o.reshape(rows, aw))
        ls_.append(l.reshape(rows, aw))

    bf = lambda name: prm[name].astype(BF16)
    x2 = _post(x2, rwkv.reshape(rows, -1), os_, ls_, z_gate, bf("w_branch_rwkv"), bf("w_branch_attn"),
               bf("w_out"), prm["norm_mlp"], bf("w_ff1"), bf("w_ff2"), p.reshape(rows, -1),
               prm["norm_ple"], bf("w_ple_gate"), bf("w_ple"), norm_final, final, tm, 1024)
    return x2.reshape(n, t, d), new_kv, wkv_t, shift_t


_LAYER_PARAMS = ("norm_mix", "w_in", "shift_mu", "decay_w0", "decay_w2", "aaa_a0", "aaa_a2", "gate_g2",
                 "k_k", "k_a", "r_k", "lnx_w", "lnx_b", "w_branch_rwkv", "w_branch_attn", "w_out",
                 "norm_mlp", "w_ff1", "w_ff2", "norm_ple", "w_ple", "w_ple_gate")


def kernel(x_prompt, x_sample, cache_kv_g0, cache_kv_g1, cache_kv_g2, state_wkv, state_shift, p_prompt, p_sample, norm_mix, w_in, shift_mu, decay_w0, decay_w2, aaa_a0, aaa_a2, gate_g2, k_k, k_a, r_k, lnx_w, lnx_b, w_branch_rwkv, w_branch_attn, w_out, norm_mlp, w_ff1, w_ff2, norm_ple, w_ple, w_ple_gate, norm_final):
    stacked = dict(zip(_LAYER_PARAMS, (norm_mix, w_in, shift_mu, decay_w0, decay_w2, aaa_a0, aaa_a2,
                                       gate_g2, k_k, k_a, r_k, lnx_w, lnx_b, w_branch_rwkv,
                                       w_branch_attn, w_out, norm_mlp, w_ff1, w_ff2, norm_ple, w_ple,
                                       w_ple_gate)))
    depth = norm_mix.shape[0]
    nb_p, _, d = x_prompt.shape
    nb_s, t_s, _ = x_sample.shape
    heads = decay_w0.shape[-1] // HEAD
    rwkv_cols = shift_mu.shape[-1]
    t_pad = -(-t_s // SUBLANES) * SUBLANES
    pad_t = lambda u: jnp.pad(u, ((0, 0), (0, t_pad - t_s), (0, 0)))

    yp, ys = x_prompt, pad_t(x_sample)
    kvp, kvs = [[], [], []], [[], [], []]
    wkv_p, sh_p, wkv_s, sh_s = [], [], [], []
    for i in range(depth):
        prm = {name: val[i] for name, val in stacked.items()}
        prm["r_k"] = prm["r_k"].reshape(-1)
        last = i == depth - 1
        wkv0 = jnp.zeros((nb_p, heads, HEAD, HEAD), F32)
        shift0 = jnp.zeros((nb_p, rwkv_cols), F32)
        yp, nkv_p, w_p, s_p = _layer(yp, p_prompt[i], wkv0, shift0, None, prm, norm_final, last,
                                     x_prompt.shape[1])
        ys, nkv_s, w_s, s_s = _layer(ys, pad_t(p_sample[i]), state_wkv[i], state_shift[i],
                                     (cache_kv_g0[i], cache_kv_g1[i], cache_kv_g2[i]), prm, norm_final,
                                     last, t_s)
        for gi in range(3):
            kvp[gi].append(nkv_p[gi])
            kvs[gi].append(nkv_s[gi])
        wkv_p.append(w_p)
        sh_p.append(s_p)
        wkv_s.append(w_s)
        sh_s.append(s_s)
    return (yp, ys[:, :t_s],
            jnp.stack(kvp[0]), jnp.stack(kvp[1]), jnp.stack(kvp[2]),
            jnp.stack(wkv_p), jnp.stack(sh_p),
            jnp.stack(kvs[0]), jnp.stack(kvs[1]), jnp.stack(kvs[2]),
            jnp.stack(wkv_s), jnp.stack(sh_s))
```

```python
import functools

import jax
import jax.numpy as jnp
from jax import lax
from jax.experimental import pallas as pl
from jax.experimental.pallas import tpu as pltpu

F32 = jnp.float32
BF16 = jnp.bfloat16

HEAD = 64
LANES = 128
SUBLANES = 8
ATTN_HEADS = 4
ATTN_BLOCK = 128
WINDOWS = (128, 512, 2048)
DILATIONS = (1, 4, 16)
NORM_EPS = 1e-6
LNX_EPS = 64e-5
NEG = -1e30
VMEM_LIMIT = 56 * 1024 * 1024


def _cparams(sem):
    return pltpu.CompilerParams(dimension_semantics=sem, vmem_limit_bytes=VMEM_LIMIT)


def _dot(a, b):
    return jnp.dot(a, b, preferred_element_type=F32)


def _dot_bf16(a, b):
    return _dot(a.astype(BF16), b.astype(BF16))


def _pair_ones():
    r = lax.broadcasted_iota(jnp.int32, (LANES, LANES), 0) // HEAD
    c = lax.broadcasted_iota(jnp.int32, (LANES, LANES), 1) // HEAD
    return (r == c).astype(BF16)


def _head_sum(x, ones):
    outs = []
    for c in range(x.shape[1] // LANES):
        outs.append(_dot(x[:, c * LANES:(c + 1) * LANES].astype(BF16), ones))
    return jnp.concatenate(outs, axis=1)


def _rms(x, g):
    return x * lax.rsqrt(jnp.mean(x * x, axis=-1, keepdims=True) + NORM_EPS) * g


def _sigmoid(x):
    return 1.0 / (1.0 + jnp.exp(-x))


DECAY_SCALE = 0.6065306597126334


def _in_proj_kernel(x_ref, g_ref, w_ref, zr_ref, zq_ref, gate_ref):
    h = _rms(x_ref[...], g_ref[...]).astype(BF16)
    c1 = zr_ref.shape[1]
    c2 = c1 + zq_ref.shape[1]
    zr_ref[...] = _dot(h, w_ref[:, :c1])
    zq_ref[...] = _dot(h, w_ref[:, c1:c2])
    gate_ref[...] = _sigmoid(_dot(h, w_ref[:, c2:])).astype(BF16)


def _in_proj(x, g, w, splits, tm):
    rows, d = x.shape
    assert sum(splits) == w.shape[1] and len(splits) == 3
    return pl.pallas_call(
        _in_proj_kernel,
        grid=(rows // tm,),
        in_specs=[pl.BlockSpec((tm, d), lambda i: (i, 0)),
                  pl.BlockSpec((1, d), lambda i: (0, 0)),
                  pl.BlockSpec(w.shape, lambda i: (0, 0), pipeline_mode=pl.Buffered(1))],
        out_specs=[pl.BlockSpec((tm, c), lambda i: (i, 0)) for c in splits],
        out_shape=[jax.ShapeDtypeStruct((rows, c), dt) for c, dt in zip(splits, (F32, F32, BF16))],
        compiler_params=_cparams(("arbitrary",)),
        name="in_proj",
    )(x, g.reshape(1, d), w)


def _drain(steps):
    try:
        while True:
            next(steps)
    except StopIteration as stop:
        return stop.value


def _zip_stages(*gens):
    vals = [None] * len(gens)
    live = list(range(len(gens)))
    while live:
        for i in list(live):
            try:
                next(gens[i])
            except StopIteration as stop:
                vals[i] = stop.value
                live.remove(i)
    return vals


def _rwkv_features_steps(r, k, v, xw, xa, xg, w0, w2, a0, a2, g2, k_k, k_a, r_k, ones):
    pre = w0 + _dot_bf16(jnp.tanh(xw), w2)
    yield
    lw = _sigmoid(pre) * (-DECAY_SCALE)
    yield
    a = _sigmoid(a0 + _dot_bf16(xa, a2))
    yield
    g = _dot_bf16(_sigmoid(xg), g2)
    yield
    kk = k * k_k
    norm2 = _head_sum(kk * kk, ones)
    yield
    kk = kk * lax.rsqrt(jnp.maximum(norm2, 1e-24))
    k2 = k * (1.0 + (a - 1.0) * k_a)
    yield
    bonus = _head_sum(r * k2 * r_k, ones) * v
    yield
    return lw, k2, -kk, kk * a, g, bonus


def _rwkv_features(*args):
    return _drain(_rwkv_features_steps(*args))


def _rwkv_out(y, bonus, g, lnx_w, lnx_b, ones):
    mu = _head_sum(y, ones) * (1.0 / HEAD)
    yc = y - mu
    var = _head_sum(yc * yc, ones) * (1.0 / HEAD)
    return (yc * lax.rsqrt(var + LNX_EPS) * lnx_w + lnx_b + bonus) * g


def _rwkv_prep_kernel(z_ref, sh_ref, mu_ref, w0_ref, w2_ref, a0_ref, a2_ref, g2_ref, kk_ref, ka_ref,
                      rk_ref, r_o, w_o, k_o, v_o, a_o, b_o, g_o, bon_o, last_scr, *, width):
    n = pl.program_id(0)
    i = pl.program_id(1)
    z = z_ref[0]
    tt = z.shape[0]

    @pl.when(i == 0)
    def _():
        last_scr[...] = sh_ref[pl.ds(n, 1), :]

    row = lax.broadcasted_iota(jnp.int32, z.shape, 0)
    prev = jnp.where(row == 0, last_scr[...], pltpu.roll(z, 1, axis=0))
    last_scr[...] = z[tt - 1:tt, :]
    zs = z + (prev - z) * mu_ref[...]

    c0 = 3 * width
    c1 = c0 + w2_ref.shape[0]
    c2 = c1 + a2_ref.shape[0]
    r = zs[:, 0:width]
    k = zs[:, width:2 * width]
    v = zs[:, 2 * width:c0]
    xw = zs[:, c0:c1]
    xa = zs[:, c1:c2]
    xg = zs[:, c2:]

    lw, k2, a_neg, b, g, bonus = _rwkv_features(
        r, k, v, xw, xa, xg, w0_ref[...], w2_ref[...], a0_ref[...], a2_ref[...], g2_ref[...],
        kk_ref[...], ka_ref[...], rk_ref[...], _pair_ones())
    r_o[0] = r
    w_o[0] = lw
    k_o[0] = k2
    v_o[0] = v
    a_o[0] = a_neg
    b_o[0] = b
    g_o[0] = g
    bon_o[0] = bonus


def _rwkv_prep(z_rwkv, shift0, prm, tt):
    n, t, cols = z_rwkv.shape
    width = prm["decay_w0"].shape[-1]
    row = lambda x: x.reshape(1, -1)
    full = lambda a: pl.BlockSpec(a.shape, lambda b, i: (0,) * a.ndim)
    args = [shift0, row(prm["shift_mu"]), row(prm["decay_w0"]), prm["decay_w2"], row(prm["aaa_a0"]),
            prm["aaa_a2"], prm["gate_g2"], row(prm["k_k"]), row(prm["k_a"]), row(prm["r_k"])]
    out_spec = pl.BlockSpec((1, tt, width), lambda b, i: (b, i, 0))
    out_sds = jax.ShapeDtypeStruct((n, t, width), F32)
    return pl.pallas_call(
        functools.partial(_rwkv_prep_kernel, width=width),
        grid=(n, t // tt),
        in_specs=[pl.BlockSpec((1, tt, cols), lambda b, i: (b, i, 0))] + [full(a) for a in args],
        out_specs=[out_spec] * 8,
        out_shape=[out_sds] * 8,
        scratch_shapes=[pltpu.VMEM((1, cols), F32)],
        compiler_params=_cparams(("arbitrary", "arbitrary")),
        name="rwkv_prep",
    )(z_rwkv, *args)


def _wkv_kernel(r_ref, w_ref, k_ref, v_ref, a_ref, b_ref, g_ref, bon_ref, lnw_ref, lnb_ref, s0_ref,
                y_ref, st_ref, *, valid):
    ones = _pair_ones()
    row = lax.broadcasted_iota(jnp.int32, (HEAD, LANES), 0)
    lane = lax.broadcasted_iota(jnp.int32, (HEAD, LANES), 1)
    left = lane < HEAD
    diag = (row == lane % HEAD).astype(F32)

    def key_sum(x):
        hi = x.astype(BF16)
        lo = (x - hi.astype(F32)).astype(BF16)
        return _dot(hi, ones) + _dot(lo, ones)

    chains = [(b, p) for b in range(y_ref.shape[0]) for p in range(s0_ref.shape[1] // 2)]
    cols = lambda p: slice(p * LANES, (p + 1) * LANES)
    vt = [v_ref[b, :, cols(p)].T for b, p in chains]
    s = [jnp.concatenate([s0_ref[b, 2 * p], s0_ref[b, 2 * p + 1]], axis=1) for b, p in chains]
    ys = [[] for _ in chains]
    for j in range(y_ref.shape[1]):
        for ch, (b, p) in enumerate(chains):
            if j < valid:
                row_of = lambda ref: ref[b, j:j + 1, cols(p)]
                v_col = jnp.where(left, vt[ch][0:HEAD, j:j + 1], vt[ch][HEAD:2 * HEAD, j:j + 1])
                sa = key_sum(s[ch] * row_of(a_ref))
                s[ch] = s[ch] * jnp.exp(row_of(w_ref)) + sa * row_of(b_ref) + v_col * row_of(k_ref)
                y_col = key_sum(s[ch] * row_of(r_ref))
                ys[ch].append(jnp.sum(y_col * diag, axis=0, keepdims=True))
            else:
                ys[ch].append(jnp.zeros((1, LANES), F32))
    for ch, (b, p) in enumerate(chains):
        y_ref[b, :, cols(p)] = _rwkv_out(jnp.concatenate(ys[ch], axis=0), bon_ref[b, :, cols(p)],
                                         g_ref[b, :, cols(p)], lnw_ref[:, cols(p)], lnb_ref[:, cols(p)], ones)
        st_ref[b, 2 * p] = s[ch][:, :HEAD]
        st_ref[b, 2 * p + 1] = s[ch][:, HEAD:]


def _wkv(r, w, k, v, a, b, g, bonus, lnx_w, lnx_b, s0, valid):
    n, t, c = r.shape
    assert t == SUBLANES
    nb = 2 if n % 2 == 0 else 1
    tok = pl.BlockSpec((nb, t, c), lambda bi: (bi, 0, 0))
    vec = pl.BlockSpec((1, c), lambda bi: (0, 0))
    st = pl.BlockSpec((nb,) + s0.shape[1:], lambda bi: (bi, 0, 0, 0))
    return pl.pallas_call(
        functools.partial(_wkv_kernel, valid=valid),
        grid=(n // nb,),
        in_specs=[tok] * 8 + [vec, vec, st],
        out_specs=[tok, st],
        out_shape=[jax.ShapeDtypeStruct((n, t, c), F32), jax.ShapeDtypeStruct(s0.shape, F32)],
        compiler_params=_cparams(("arbitrary",)),
        name="wkv_scan",
    )(r, w, k, v, a, b, g, bonus, lnx_w.reshape(1, c), lnx_b.reshape(1, c), s0)


WKV_CHUNK = 64


def _bd(x):
    xb = x.astype(BF16)
    left = lax.broadcasted_iota(jnp.int32, x.shape, 1) < HEAD
    zero = jnp.zeros_like(xb)
    return jnp.concatenate([jnp.where(left, xb, zero), jnp.where(left, zero, xb)], axis=0)


def _dot_nt(a, b):
    return lax.dot_general(a, b, (((1,), (1,)), ((), ())), preferred_element_type=F32)


def _dot_tn(a, b):
    return lax.dot_general(a, b, (((0,), (0,)), ((), ())), preferred_element_type=F32)


def _wkv_chunk_local_steps(r, lw, k, v, a, b, consts):
    tri, eye2, strict, incl, blocks = consts
    c = WKV_CHUNK
    n = range(len(r))
    zero = jnp.zeros((c, LANES), F32)

    def cumsum(x):
        hi = x.astype(BF16)
        lo = (x - hi.astype(F32)).astype(BF16)
        return _dot(tri, hi) + _dot(tri, lo)

    lc = [cumsum(lw[i]) for i in n]
    yield
    ltot = [lc[i][c - 1:c, :] for i in n]
    w_inv = [jnp.exp(-lc[i]) for i in n]
    ah = [a[i] * jnp.exp(lc[i] - lw[i]) for i in n]
    rh = [r[i] * jnp.exp(lc[i]) for i in n]
    yield
    sc = [_dot_nt(jnp.concatenate([ah[i], rh[i]], axis=0).astype(BF16),
                  jnp.concatenate([_bd(b[i] * w_inv[i]), _bd(k[i] * w_inv[i])], axis=0)) for i in n]
    yield
    aab = [jnp.where(strict, sc[i][:c, :LANES], zero) for i in n]

    t = [eye2 + jnp.where(blocks[0], aab[i], zero) for i in n]
    for lvl in range(1, len(blocks)):
        grow = blocks[lvl] & ~blocks[lvl - 1]
        x = [_dot(t[i].astype(BF16), _bd(jnp.where(grow, aab[i], zero))) for i in n]
        yield
        t = [t[i] + _dot(x[i].astype(BF16), _bd(t[i])) for i in n]
        yield

    vb = [_bd(v[i]) for i in n]
    akv = [_dot(jnp.where(strict, sc[i][:c, LANES:], zero).astype(BF16), vb[i]) for i in n]
    yield
    pu = [_dot(t[i].astype(BF16), jnp.concatenate([_bd(ah[i]), _bd(akv[i])], axis=1)) for i in n]
    yield

    left2 = (lax.broadcasted_iota(jnp.int32, (HEAD, 2 * LANES), 1) % LANES) < HEAD
    g, h, q, y0 = [], [], [], []
    for i in n:
        w_rem = jnp.exp(ltot[i] - lc[i])
        lhs_t = jnp.concatenate([b[i] * w_rem, k[i] * w_rem], axis=0).astype(BF16)
        rhs = jnp.concatenate([pu[i], jnp.concatenate([zero, v[i]], axis=1)], axis=0).astype(BF16)
        full = _dot_tn(lhs_t, rhs)
        gh = jnp.where(left2, full[:HEAD], full[HEAD:])
        g.append(gh[:, :LANES] + eye2 * jnp.exp(ltot[i]))
        h.append(gh[:, LANES:])
    yield
    zero_b = jnp.zeros((2 * c, LANES), BF16)
    for i in n:
        mrb = jnp.where(incl, sc[i][c:, :LANES], zero)
        mrk = jnp.where(incl, sc[i][c:, LANES:], zero)
        rhs2 = jnp.concatenate([jnp.concatenate([_bd(pu[i][:, :LANES]), _bd(pu[i][:, LANES:])], axis=1),
                                jnp.concatenate([zero_b, vb[i]], axis=1)], axis=0)
        qy = _dot(jnp.concatenate([mrb, mrk], axis=1).astype(BF16), rhs2)
        q.append(rh[i] + qy[:, :LANES])
        y0.append(qy[:, LANES:])
    yield
    return g, h, q, y0


def _wkv_fused_kernel(zr_ref, zk_ref, zv_ref, zl_ref, sr_ref, sk_ref, sv_ref, sl_ref,
                      mr_ref, mk_ref, mv_ref, ml_ref, w0_ref, w2_ref, a0_ref, a2_ref, g2_ref,
                      kk_ref, ka_ref, rk_ref, lnw_ref, lnb_ref, s0_ref, out_ref, st_ref,
                      s_scr, pr_scr, pk_scr, pv_scr, pl_scr, *, group_size):
    i = pl.program_id(1)
    c = WKV_CHUNK
    n_seq, tb = out_ref.shape[0], out_ref.shape[1]
    n_chunks = tb // c

    @pl.when(i == 0)
    def _():
        s_scr[...] = s0_ref[...]
        pr_scr[...] = sr_ref[...]
        pk_scr[...] = sk_ref[...]
        pv_scr[...] = sv_ref[...]
        pl_scr[...] = sl_ref[...]

    def shifted(z_ref, p_scr, mu_ref, b, lanes):
        z = z_ref[b, :, lanes]
        first = lax.broadcasted_iota(jnp.int32, z.shape, 0) == 0
        prev = jnp.where(first, p_scr[b:b + 1, lanes], pltpu.roll(z, 1, axis=0))
        p_scr[b:b + 1, lanes] = z[tb - 1:tb, :]
        return z + (prev - z) * mu_ref[:, lanes]

    ones = _pair_ones()
    c1 = w2_ref.shape[0]
    c2 = c1 + a2_ref.shape[0]
    ti = lax.broadcasted_iota(jnp.int32, (c, LANES), 0)
    si = lax.broadcasted_iota(jnp.int32, (c, LANES), 1) % HEAD
    tri = (lax.broadcasted_iota(jnp.int32, (c, c), 0) >= lax.broadcasted_iota(jnp.int32, (c, c), 1)
           ).astype(BF16)
    eye2 = (ti == si).astype(F32)
    sizes = [2 << l for l in range(c.bit_length() - 1)]
    blocks = [(ti // s) == (si // s) for s in sizes]
    consts = (tri, eye2, ti > si, ti >= si, blocks)
    lora_in = {}

    def feature_steps(group):
        out = []
        for b, sp in group:
            lanes = slice(sp * LANES, (sp + 1) * LANES)
            r = shifted(zr_ref, pr_scr, mr_ref, b, lanes)
            k = shifted(zk_ref, pk_scr, mk_ref, b, lanes)
            yield
            v = shifted(zv_ref, pv_scr, mv_ref, b, lanes)
            if b not in lora_in:
                lora_in[b] = shifted(zl_ref, pl_scr, ml_ref, b, slice(None))
            zl = lora_in[b]
            yield
            lw, k2, a_neg, bb, gate, bonus = yield from _rwkv_features_steps(
                r, k, v, zl[:, :c1], zl[:, c1:c2], zl[:, c2:], w0_ref[:, lanes], w2_ref[:, lanes],
                a0_ref[:, lanes], a2_ref[:, lanes], g2_ref[:, lanes], kk_ref[:, lanes], ka_ref[:, lanes],
                rk_ref[:, lanes], ones)
            out.append((r, lw, k2, v, a_neg, bb, gate, bonus))
        return out

    def local_steps(feat):
        chunks = lambda which: [f[which][ci * c:(ci + 1) * c, :] for ci in range(n_chunks) for f in feat]
        return (yield from _wkv_chunk_local_steps(*(chunks(which) for which in range(6)), consts))

    def tail_steps(group, feat, local):
        g, h, q, y0 = local
        st = [s_scr[b, sp] for b, sp in group]
        ys = [[] for _ in group]
        for ci in range(n_chunks):
            for j in range(len(group)):
                u = ci * len(group) + j
                out = _dot(jnp.concatenate([q[u], g[u]], axis=0).astype(BF16), _bd(st[j]))
                ys[j].append(out[:c] + y0[u])
                st[j] = out[c:] + h[u]
            yield
        for j, (b, sp) in enumerate(group):
            lanes = slice(sp * LANES, (sp + 1) * LANES)
            s_scr[b, sp] = st[j]
            out_ref[b, :, lanes] = _rwkv_out(jnp.concatenate(ys[j], axis=0), feat[j][7], feat[j][6],
                                             lnw_ref[:, lanes], lnb_ref[:, lanes], ones)
            yield

        @pl.when(i == pl.num_programs(1) - 1)
        def _():
            for j, (b, sp) in enumerate(group):
                st_ref[b, sp] = st[j]

    chains = [(b, sp) for b in range(n_seq) for sp in range(out_ref.shape[2] // LANES)]
    groups = [chains[j:j + group_size] for j in range(0, len(chains), group_size)]
    feats, locals_ = {}, {}
    for step in range(len(groups) + 2):
        gens, roles = [], []
        if 0 <= step - 1 < len(groups):
            gens.append(local_steps(feats[step - 1]))
            roles.append((locals_, step - 1))
        if step < len(groups):
            gens.append(feature_steps(groups[step]))
            roles.append((feats, step))
        if 0 <= step - 2 < len(groups):
            gens.append(tail_steps(groups[step - 2], feats[step - 2], locals_[step - 2]))
            roles.append(({}, step - 2))
        for (store, idx), val in zip(roles, _zip_stages(*gens)):
            store[idx] = val


def _wkv_fused(z_rwkv, shift0, s0, prm, tb, pp):
    n, t, cols = z_rwkv.shape
    width = prm["decay_w0"].shape[-1]
    bw = pp * LANES
    nblk = width // bw
    lora = cols - 3 * width
    assert width % bw == 0 and (3 * width) % lora == 0 and lora % LANES == 0
    lblk = 3 * width // lora
    row = lambda x: x.reshape(1, -1)
    mu = row(prm["shift_mu"])
    tok = lambda off: pl.BlockSpec((n, tb, bw), lambda p, i: (0, i, off + p))
    vec = lambda rows_, off: pl.BlockSpec((rows_, bw), lambda p, i: (0, off + p))
    st = pl.BlockSpec((n, pp, HEAD, LANES), lambda p, i: (0, p, 0, 0))
    in_specs = [tok(0), tok(nblk), tok(2 * nblk), pl.BlockSpec((n, tb, lora), lambda p, i: (0, i, lblk)),
                vec(n, 0), vec(n, nblk), vec(n, 2 * nblk), pl.BlockSpec((n, lora), lambda p, i: (0, lblk)),
                vec(1, 0), vec(1, nblk), vec(1, 2 * nblk), pl.BlockSpec((1, lora), lambda p, i: (0, lblk)),
                vec(1, 0), vec(prm["decay_w2"].shape[0], 0), vec(1, 0), vec(prm["aaa_a2"].shape[0], 0),
                vec(prm["gate_g2"].shape[0], 0), vec(1, 0), vec(1, 0), vec(1, 0), vec(1, 0), vec(1, 0), st]
    args = [z_rwkv] * 4 + [shift0] * 4 + [mu] * 4 + [
        row(prm["decay_w0"]), prm["decay_w2"], row(prm["aaa_a0"]), prm["aaa_a2"], prm["gate_g2"],
        row(prm["k_k"]), row(prm["k_a"]), row(prm["r_k"]), row(prm["lnx_w"]), row(prm["lnx_b"]), s0]
    return pl.pallas_call(
        functools.partial(_wkv_fused_kernel, group_size=2),
        grid=(nblk, t // tb),
        in_specs=in_specs,
        out_specs=[tok(0), st],
        out_shape=[jax.ShapeDtypeStruct((n, t, width), F32), jax.ShapeDtypeStruct(s0.shape, F32)],
        scratch_shapes=[pltpu.VMEM((n, pp, HEAD, LANES), F32), pltpu.VMEM((n, bw), F32),
                        pltpu.VMEM((n, bw), F32), pltpu.VMEM((n, bw), F32),
                        pltpu.VMEM((n, lora), F32)],
        compiler_params=_cparams(("arbitrary", "arbitrary")),
        name="wkv_fused",
    )(*args)


def _pack_state(s):
    n, h, vd, kd = s.shape
    return s.reshape(n, h // 2, 2, vd, kd).transpose(0, 1, 4, 2, 3).reshape(n, h // 2, kd, 2 * vd)


def _unpack_state(s, h):
    n, _, kd, _ = s.shape
    return s.reshape(n, h // 2, kd, 2, HEAD).transpose(0, 1, 3, 4, 2).reshape(n, h, HEAD, kd)


def _attn_prompt_kernel(q_ref, k_ref, kp_ref, v_ref, vp_ref, o_ref, l_ref, *, dil, sub, n_back, scale):
    first = pl.program_id(1) == 0
    qi = lax.broadcasted_iota(jnp.int32, (ATTN_BLOCK, 2 * ATTN_BLOCK), 0)
    ki = lax.broadcasted_iota(jnp.int32, (ATTN_BLOCK, 2 * ATTN_BLOCK), 1)
    dist = qi + ATTN_BLOCK - ki
    band = (dist >= 0) & (dist <= n_back)
    own = ki >= ATTN_BLOCK
    left = lax.broadcasted_iota(jnp.int32, (ATTN_BLOCK, LANES), 1) < HEAD
    ones_kv = jnp.ones((2 * ATTN_BLOCK, LANES), BF16)

    def rows(j, r):
        if dil == 1:
            return pl.ds(j * ATTN_BLOCK, ATTN_BLOCK)
        return pl.ds(j * ATTN_BLOCK * dil + r, ATTN_BLOCK, stride=dil)

    def units(jr):
        probs = []
        qs, kcat, vcat, masks = [], [], [], []
        for j, r in jr:
            q = (q_ref[0, rows(j, r), :] * scale).astype(BF16)
            if j == 0:
                kp, vp = kp_ref[0, rows(0, r), :], vp_ref[0, rows(0, r), :]
                masks.append(band & (own | jnp.logical_not(first)))
            else:
                kp, vp = k_ref[0, rows(j - 1, r), :], v_ref[0, rows(j - 1, r), :]
                masks.append(band)
            kcat.append(jnp.concatenate([kp, k_ref[0, rows(j, r), :]], axis=0).astype(BF16))
            vcat.append(jnp.concatenate([vp, v_ref[0, rows(j, r), :]], axis=0).astype(BF16))
            zq = jnp.zeros_like(q)
            qs += [jnp.where(left, q, zq), jnp.where(left, zq, q)]
            probs += [(len(kcat) - 1, 0), (len(kcat) - 1, 1)]
        s = [jnp.where(masks[u], _dot_nt(qs[p], kcat[u]), NEG) for p, (u, _) in enumerate(probs)]
        m = [jnp.max(x, axis=-1, keepdims=True) for x in s]
        e = [jnp.exp(s[p] - m[p]).astype(BF16) for p in range(len(probs))]
        o = [_dot(e[p], vcat[u]) for p, (u, _) in enumerate(probs)]
        dens = [_dot(e[p], ones_kv) for p in range(len(probs))]
        for u, (j, r) in enumerate(jr):
            den = jnp.where(left, dens[2 * u], dens[2 * u + 1])
            o_ref[0, rows(j, r), :] = jnp.where(left, o[2 * u], o[2 * u + 1]) / den
            l_ref[0, rows(j, r), :] = jnp.where(left, m[2 * u], m[2 * u + 1]) + jnp.log(den)

    if dil <= 4:
        units([(j, r) for j in range(sub) for r in range(dil)])
    else:
        for j in range(sub):
            def body(r4, carry, j=j):
                units([(j, r4 * 4 + u) for u in range(4)])
                return carry
            lax.fori_loop(0, dil // 4, body, 0)


def _attn_prompt(zqkv, gi, span):
    n, t, cols = zqkv.shape
    d = DILATIONS[gi]
    aw = ATTN_HEADS * HEAD
    ng = len(DILATIONS)
    back = ATTN_BLOCK * d
    sub = span // back
    halves = aw // LANES
    col = lambda which, hp: (which * ng + gi) * halves + hp
    cur = lambda which: pl.BlockSpec((1, span, LANES), lambda b, i, hp: (b, i, col(which, hp)))
    prev = lambda which: pl.BlockSpec((1, back, LANES),
                                      lambda b, i, hp: (b, jnp.maximum(i * sub - 1, 0), col(which, hp)))
    out_spec = pl.BlockSpec((1, span, LANES), lambda b, i, hp: (b, i, hp))
    out_sds = jax.ShapeDtypeStruct((n, t, aw), F32)
    return pl.pallas_call(
        functools.partial(_attn_prompt_kernel, dil=d, sub=sub, n_back=WINDOWS[gi] // d, scale=HEAD ** -0.5),
        grid=(n, t // span, halves),
        in_specs=[cur(0), cur(1), prev(1), cur(2), prev(2)],
        out_specs=[out_spec, out_spec],
        out_shape=[out_sds, out_sds],
        compiler_params=_cparams(("arbitrary", "arbitrary", "arbitrary")),
        name=f"attn_prompt_g{gi}",
    )(zqkv, zqkv, zqkv, zqkv, zqkv)


def _attn_sample_kernel(q_ref, k_ref, v_ref, c_ref, o_ref, l_ref, *, dil, n_back, valid, scale):
    nb, tp = q_ref.shape[0], q_ref.shape[1]
    cl = c_ref.shape[4]
    dist_c = (cl + lax.broadcasted_iota(jnp.int32, (tp, cl), 0)
              - lax.broadcasted_iota(jnp.int32, (tp, cl), 1))
    mask_c = (dist_c % dil == 0) & (dist_c <= dil * n_back)
    pn = lax.broadcasted_iota(jnp.int32, (tp, tp), 1)
    dist_n = lax.broadcasted_iota(jnp.int32, (tp, tp), 0) - pn
    mask_n = (dist_n >= 0) & (dist_n % dil == 0) & (dist_n <= dil * n_back) & (pn < valid)
    probs = [(b, h) for b in range(nb) for h in range(ATTN_HEADS)]
    sl = lambda h: slice(h * HEAD, (h + 1) * HEAD)
    q = [(q_ref[b, :, sl(h)] * scale).astype(BF16) for b, h in probs]
    s_c = [jnp.where(mask_c, _dot(q[p], c_ref[b, 0, h].astype(BF16)), NEG) for p, (b, h) in enumerate(probs)]
    s_n = [jnp.where(mask_n, _dot_nt(q[p], k_ref[b, :, sl(h)].astype(BF16)), NEG)
           for p, (b, h) in enumerate(probs)]
    m = [jnp.maximum(jnp.max(s_c[p], axis=-1, keepdims=True), jnp.max(s_n[p], axis=-1, keepdims=True))
         for p in range(len(probs))]
    e_c = [jnp.exp(s_c[p] - m[p]) for p in range(len(probs))]
    e_n = [jnp.exp(s_n[p] - m[p]) for p in range(len(probs))]
    den = [jnp.sum(e_c[p], axis=-1, keepdims=True) + jnp.sum(e_n[p], axis=-1, keepdims=True)
           for p in range(len(probs))]
    o = [_dot_nt(e_c[p].astype(BF16), c_ref[b, 1, h].astype(BF16))
         + _dot(e_n[p].astype(BF16), v_ref[b, :, sl(h)].astype(BF16)) for p, (b, h) in enumerate(probs)]
    for b in range(nb):
        ps = [p for p, (bb, _) in enumerate(probs) if bb == b]
        o_ref[b] = jnp.concatenate([o[p] / den[p] for p in ps], axis=1)
        l_ref[b] = jnp.concatenate([jnp.broadcast_to(m[p] + jnp.log(den[p]), (tp, HEAD)) for p in ps], axis=1)


def _attn_sample(zqkv, cache, gi, valid):
    n, tp, cols = zqkv.shape
    aw = ATTN_HEADS * HEAD
    ng = len(DILATIONS)
    d = DILATIONS[gi]
    cache_t = cache.transpose(0, 2, 3, 4, 1)
    nb = 2 if n % 2 == 0 else 1
    spec = lambda which: pl.BlockSpec((nb, tp, aw), lambda b: (b, 0, which * ng + gi))
    out_spec = pl.BlockSpec((nb, tp, aw), lambda b: (b, 0, 0))
    out_sds = jax.ShapeDtypeStruct((n, tp, aw), F32)
    return pl.pallas_call(
        functools.partial(_attn_sample_kernel, dil=d, n_back=WINDOWS[gi] // d, valid=valid,
                          scale=HEAD ** -0.5),
        grid=(n // nb,),
        in_specs=[spec(0), spec(1), spec(2),
                  pl.BlockSpec((nb,) + cache_t.shape[1:], lambda b: (b, 0, 0, 0, 0))],
        out_specs=[out_spec, out_spec],
        out_shape=[out_sds, out_sds],
        compiler_params=_cparams(("arbitrary",)),
        name=f"attn_sample_g{gi}",
    )(zqkv, zqkv, zqkv, cache_t)


def _post_kernel(x_ref, rwkv_ref, o0, o1, o2, l0, l1, l2, zg_ref, wbr_ref, wba_ref, wout_ref, g_ref,
                 w1_ref, w2_ref, p_ref, gp_ref, wg_ref, wp_ref, gf_ref, o_ref, h_scr, acc_scr, *, final):
    j = pl.program_id(1)

    @pl.when(j == 0)
    def _():
        ls = [l0[...], l1[...], l2[...]]
        m = jnp.maximum(jnp.maximum(ls[0], ls[1]), ls[2])
        es = [jnp.exp(l - m) for l in ls]
        attn = (es[0] * o0[...] + es[1] * o1[...] + es[2] * o2[...]) / (es[0] + es[1] + es[2])
        d = x_ref.shape[1]
        gates = zg_ref[...].astype(F32)
        mixed = (gates[:, :d] * _dot(rwkv_ref[...].astype(BF16), wbr_ref[...])
                 + gates[:, d:] * _dot(attn.astype(BF16), wba_ref[...]))
        x = x_ref[...] + _dot(mixed.astype(BF16), wout_ref[...])
        acc_scr[...] = x
        h_scr[...] = _rms(x, g_ref[...]).astype(BF16)

    u = jnp.maximum(_dot(h_scr[...], w1_ref[...]), 0.0)
    acc_scr[...] += _dot((u * u).astype(BF16), w2_ref[...])

    @pl.when(j == pl.num_programs(1) - 1)
    def _():
        x = acc_scr[...]
        h = _rms(x, gp_ref[...]).astype(BF16)
        x = x + _sigmoid(_dot(h, wg_ref[...])) * _dot(p_ref[...].astype(BF16), wp_ref[...])
        if final:
            x = _rms(x, gf_ref[...])
        o_ref[...] = x


def _post(x, rwkv, os_, ls_, zg, wbr, wba, wout, g, w1, w2, p, gp, wg, wp, gf, final, tm, tf):
    rows, d = x.shape
    dff = w1.shape[1]
    vec = lambda a: a.reshape(1, d)
    rowblk = lambda a: pl.BlockSpec((tm, a.shape[1]), lambda i, j: (i, 0))
    full = lambda a: pl.BlockSpec(a.shape, lambda i, j: (0, 0), pipeline_mode=pl.Buffered(1))
    row_args = [x, rwkv, *os_, *ls_, zg]
    return pl.pallas_call(
        functools.partial(_post_kernel, final=final),
        grid=(rows // tm, dff // tf),
        in_specs=[rowblk(a) for a in row_args] + [full(wbr), full(wba), full(wout), full(vec(g)),
                  pl.BlockSpec((d, tf), lambda i, j: (0, j)),
                  pl.BlockSpec((tf, d), lambda i, j: (j, 0)),
                  rowblk(p), full(vec(gp)), full(wg), full(wp), full(vec(gf))],
        out_specs=pl.BlockSpec((tm, d), lambda i, j: (i, 0)),
        out_shape=jax.ShapeDtypeStruct((rows, d), F32),
        scratch_shapes=[pltpu.VMEM((tm, d), BF16), pltpu.VMEM((tm, d), F32)],
        compiler_params=_cparams(("arbitrary", "arbitrary")),
        name="post",
    )(*row_args, wbr, wba, wout, vec(g), w1, w2, p, vec(gp), wg, wp, vec(gf))


def _layer(x, p, wkv0, shift0, caches, prm, norm_final, final, valid):
    n, t, d = x.shape
    rows = n * t
    width = prm["decay_w0"].shape[-1]
    rwkv_cols = prm["shift_mu"].shape[-1]
    aw = ATTN_HEADS * HEAD
    qkv_cols = 3 * len(DILATIONS) * aw
    tm = min(512, rows)
    x2 = x.reshape(rows, d)

    pick = lambda cols, opts: next(o for o in opts if cols % o == 0)
    z_rwkv, z_qkv, z_gate = _in_proj(x2, prm["norm_mix"], prm["w_in"].astype(BF16),
                                     (rwkv_cols, qkv_cols, 2 * d), tm)
    z_rwkv = z_rwkv.reshape(n, t, rwkv_cols)
    z_qkv = z_qkv.reshape(n, t, qkv_cols)

    if valid == t and t % WKV_CHUNK == 0:
        rwkv, s_t = _wkv_fused(z_rwkv, shift0, _pack_state(wkv0), prm, pick(t, (512, 256, WKV_CHUNK)), 2)
        wkv_t = _unpack_state(s_t, width // HEAD)
    else:
        feats = _rwkv_prep(z_rwkv, shift0, prm, min(256, t))
        rwkv, wkv_t = _wkv(*feats, prm["lnx_w"], prm["lnx_b"], wkv0, valid)
    shift_t = z_rwkv[:, valid - 1, :]

    os_, ls_, new_kv = [], [], []
    ng = len(DILATIONS)
    for gi in range(ng):
        if caches is None:
            o, l = _attn_prompt(z_qkv, gi, max(1024, ATTN_BLOCK * DILATIONS[gi]))
            keep = min(WINDOWS[gi], t)
            kv_rows = z_qkv[:, t - keep:t]
        else:
            o, l = _attn_sample(z_qkv, caches[gi], gi, valid)
            kv_rows = z_qkv[:, :valid]
        new_kv.append(jnp.stack([kv_rows[:, :, (which * ng + gi) * aw:(which * ng + gi + 1) * aw]
                                 .reshape(n, -1, ATTN_HEADS, HEAD) for which in (1, 2)], axis=2))
        os_.append(o.reshape(rows, aw))
        ls_.append(l.reshape(rows, aw))

    bf = lambda name: prm[name].astype(BF16)
    x2 = _post(x2, rwkv.reshape(rows, -1), os_, ls_, z_gate, bf("w_branch_rwkv"), bf("w_branch_attn"),
               bf("w_out"), prm["norm_mlp"], bf("w_ff1"), bf("w_ff2"), p.reshape(rows, -1),
               prm["norm_ple"], bf("w_ple_gate"), bf("w_ple"), norm_final, final, tm, 1024)
    return x2.reshape(n, t, d), new_kv, wkv_t, shift_t


_LAYER_PARAMS = ("norm_mix", "w_in", "shift_mu", "decay_w0", "decay_w2", "aaa_a0", "aaa_a2", "gate_g2",
                 "k_k", "k_a", "r_k", "lnx_w", "lnx_b", "w_branch_rwkv", "w_branch_attn", "w_out",
                 "norm_mlp", "w_ff1", "w_ff2", "norm_ple", "w_ple", "w_ple_gate")


def kernel(x_prompt, x_sample, cache_kv_g0, cache_kv_g1, cache_kv_g2, state_wkv, state_shift, p_prompt, p_sample, norm_mix, w_in, shift_mu, decay_w0, decay_w2, aaa_a0, aaa_a2, gate_g2, k_k, k_a, r_k, lnx_w, lnx_b, w_branch_rwkv, w_branch_attn, w_out, norm_mlp, w_ff1, w_ff2, norm_ple, w_ple, w_ple_gate, norm_final):
    stacked = dict(zip(_LAYER_PARAMS, (norm_mix, w_in, shift_mu, decay_w0, decay_w2, aaa_a0, aaa_a2,
                                       gate_g2, k_k, k_a, r_k, lnx_w, lnx_b, w_branch_rwkv,
                                       w_branch_attn, w_out, norm_mlp, w_ff1, w_ff2, norm_ple, w_ple,
                                       w_ple_gate)))
    depth = norm_mix.shape[0]
    nb_p, _, d = x_prompt.shape
    nb_s, t_s, _ = x_sample.shape
    heads = decay_w0.shape[-1] // HEAD
    rwkv_cols = shift_mu.shape[-1]
    t_pad = -(-t_s // SUBLANES) * SUBLANES
    pad_t = lambda u: jnp.pad(u, ((0, 0), (0, t_pad - t_s), (0, 0)))

    yp, ys = x_prompt, pad_t(x_sample)
    kvp, kvs = [[], [], []], [[], [], []]
    wkv_p, sh_p, wkv_s, sh_s = [], [], [], []
    for i in range(depth):
        prm = {name: val[i] for name, val in stacked.items()}
        prm["r_k"] = prm["r_k"].reshape(-1)
        last = i == depth - 1
        wkv0 = jnp.zeros((nb_p, heads, HEAD, HEAD), F32)
        shift0 = jnp.zeros((nb_p, rwkv_cols), F32)
        yp, nkv_p, w_p, s_p = _layer(yp, p_prompt[i], wkv0, shift0, None, prm, norm_final, last,
                                     x_prompt.shape[1])
        ys, nkv_s, w_s, s_s = _layer(ys, pad_t(p_sample[i]), state_wkv[i], state_shift[i],
                                     (cache_kv_g0[i], cache_kv_g1[i], cache_kv_g2[i]), prm, norm_final,
                                     last, t_s)
        for gi in range(3):
            kvp[gi].append(nkv_p[gi])
            kvs[gi].append(nkv_s[gi])
        wkv_p.append(w_p)
        sh_p.append(s_p)
        wkv_s.append(w_s)
        sh_s.append(s_s)
    return (yp, ys[:, :t_s],
            jnp.stack(kvp[0]), jnp.stack(kvp[1]), jnp.stack(kvp[2]),
            jnp.stack(wkv_p), jnp.stack(sh_p),
            jnp.stack(kvs[0]), jnp.stack(kvs[1]), jnp.stack(kvs[2]),
            jnp.stack(wkv_s), jnp.stack(sh_s))
```

```python
import functools

import jax
import jax.numpy as jnp
from jax import lax
from jax.experimental import pallas as pl
from jax.experimental.pallas import tpu as pltpu

F32 = jnp.float32
BF16 = jnp.bfloat16

HEAD = 64
LANES = 128
SUBLANES = 8
ATTN_HEADS = 4
ATTN_BLOCK = 128
WINDOWS = (128, 512, 2048)
DILATIONS = (1, 4, 16)
NORM_EPS = 1e-6
LNX_EPS = 64e-5
NEG = -1e30
VMEM_LIMIT = 56 * 1024 * 1024


def _cparams(sem):
    return pltpu.CompilerParams(dimension_semantics=sem, vmem_limit_bytes=VMEM_LIMIT)


def _dot(a, b):
    return jnp.dot(a, b, preferred_element_type=F32)


def _dot_bf16(a, b):
    return _dot(a.astype(BF16), b.astype(BF16))


def _pair_ones():
    r = lax.broadcasted_iota(jnp.int32, (LANES, LANES), 0) // HEAD
    c = lax.broadcasted_iota(jnp.int32, (LANES, LANES), 1) // HEAD
    return (r == c).astype(BF16)


def _head_sum(x, ones):
    outs = []
    for c in range(x.shape[1] // LANES):
        outs.append(_dot(x[:, c * LANES:(c + 1) * LANES].astype(BF16), ones))
    return jnp.concatenate(outs, axis=1)


def _rms(x, g):
    return x * lax.rsqrt(jnp.mean(x * x, axis=-1, keepdims=True) + NORM_EPS) * g


def _sigmoid(x):
    return 1.0 / (1.0 + jnp.exp(-x))


DECAY_SCALE = 0.6065306597126334


def _in_proj_kernel(x_ref, g_ref, w_ref, zr_ref, zq_ref, gate_ref):
    h = _rms(x_ref[...], g_ref[...]).astype(BF16)
    c1 = zr_ref.shape[1]
    c2 = c1 + zq_ref.shape[1]
    zr_ref[...] = _dot(h, w_ref[:, :c1])
    zq_ref[...] = _dot(h, w_ref[:, c1:c2])
    gate_ref[...] = _sigmoid(_dot(h, w_ref[:, c2:])).astype(BF16)


def _in_proj(x, g, w, splits, tm):
    rows, d = x.shape
    assert sum(splits) == w.shape[1] and len(splits) == 3
    return pl.pallas_call(
        _in_proj_kernel,
        grid=(rows // tm,),
        in_specs=[pl.BlockSpec((tm, d), lambda i: (i, 0)),
                  pl.BlockSpec((1, d), lambda i: (0, 0)),
                  pl.BlockSpec(w.shape, lambda i: (0, 0), pipeline_mode=pl.Buffered(1))],
        out_specs=[pl.BlockSpec((tm, c), lambda i: (i, 0)) for c in splits],
        out_shape=[jax.ShapeDtypeStruct((rows, c), dt) for c, dt in zip(splits, (F32, F32, BF16))],
        compiler_params=_cparams(("arbitrary",)),
        name="in_proj",
    )(x, g.reshape(1, d), w)


def _drain(steps):
    try:
        while True:
            next(steps)
    except StopIteration as stop:
        return stop.value


def _zip_stages(*gens):
    vals = [None] * len(gens)
    live = list(range(len(gens)))
    while live:
        for i in list(live):
            try:
                next(gens[i])
            except StopIteration as stop:
                vals[i] = stop.value
                live.remove(i)
    return vals


def _rwkv_features_steps(r, k, v, xw, xa, xg, w0, w2, a0, a2, g2, k_k, k_a, r_k, ones):
    pre = w0 + _dot_bf16(jnp.tanh(xw), w2)
    yield
    lw = _sigmoid(pre) * (-DECAY_SCALE)
    yield
    a = _sigmoid(a0 + _dot_bf16(xa, a2))
    yield
    g = _dot_bf16(_sigmoid(xg), g2)
    yield
    kk = k * k_k
    norm2 = _head_sum(kk * kk, ones)
    yield
    kk = kk * lax.rsqrt(jnp.maximum(norm2, 1e-24))
    k2 = k * (1.0 + (a - 1.0) * k_a)
    yield
    bonus = _head_sum(r * k2 * r_k, ones) * v
    yield
    return lw, k2, -kk, kk * a, g, bonus


def _rwkv_features(*args):
    return _drain(_rwkv_features_steps(*args))


def _rwkv_out(y, bonus, g, lnx_w, lnx_b, ones):
    mu = _head_sum(y, ones) * (1.0 / HEAD)
    yc = y - mu
    var = _head_sum(yc * yc, ones) * (1.0 / HEAD)
    return (yc * lax.rsqrt(var + LNX_EPS) * lnx_w + lnx_b + bonus) * g


def _rwkv_prep_kernel(z_ref, sh_ref, mu_ref, w0_ref, w2_ref, a0_ref, a2_ref, g2_ref, kk_ref, ka_ref,
                      rk_ref, r_o, w_o, k_o, v_o, a_o, b_o, g_o, bon_o, *, width):
    nb, t, cols = z_ref.shape
    first = lax.broadcasted_iota(jnp.int32, (t, cols), 0) == 0
    zs = []
    for b in range(nb):
        z = z_ref[b]
        prev = jnp.where(first, sh_ref[b:b + 1, :], pltpu.roll(z, 1, axis=0))
        zs.append(z + (prev - z) * mu_ref[...])
    zs = jnp.concatenate(zs, axis=0)

    c0 = 3 * width
    c1 = c0 + w2_ref.shape[0]
    c2 = c1 + a2_ref.shape[0]
    r, k, v = zs[:, :width], zs[:, width:2 * width], zs[:, 2 * width:c0]
    lw, k2, a_neg, b, g, bonus = _rwkv_features(
        r, k, v, zs[:, c0:c1], zs[:, c1:c2], zs[:, c2:], w0_ref[...], w2_ref[...], a0_ref[...],
        a2_ref[...], g2_ref[...], kk_ref[...], ka_ref[...], rk_ref[...], _pair_ones())
    for ref, val in ((r_o, r), (w_o, lw), (k_o, k2), (v_o, v), (a_o, a_neg), (b_o, b), (g_o, g), (bon_o, bonus)):
        for bi in range(nb):
            ref[bi] = val[bi * t:(bi + 1) * t, :]


def _rwkv_prep(z_rwkv, shift0, prm, nb):
    n, t, cols = z_rwkv.shape
    width = prm["decay_w0"].shape[-1]
    assert n % nb == 0
    row = lambda x: x.reshape(1, -1)
    full = lambda a: pl.BlockSpec(a.shape, lambda i: (0,) * a.ndim)
    args = [row(prm["shift_mu"]), row(prm["decay_w0"]), prm["decay_w2"], row(prm["aaa_a0"]),
            prm["aaa_a2"], prm["gate_g2"], row(prm["k_k"]), row(prm["k_a"]), row(prm["r_k"])]
    out_spec = pl.BlockSpec((nb, t, width), lambda i: (i, 0, 0))
    out_sds = jax.ShapeDtypeStruct((n, t, width), F32)
    return pl.pallas_call(
        functools.partial(_rwkv_prep_kernel, width=width),
        grid=(n // nb,),
        in_specs=[pl.BlockSpec((nb, t, cols), lambda i: (i, 0, 0)), pl.BlockSpec((nb, cols), lambda i: (i, 0))]
                 + [full(a) for a in args],
        out_specs=[out_spec] * 8,
        out_shape=[out_sds] * 8,
        compiler_params=_cparams(("arbitrary",)),
        name="rwkv_prep",
    )(z_rwkv, shift0, *args)


def _wkv_kernel(r_ref, w_ref, k_ref, v_ref, a_ref, b_ref, g_ref, bon_ref, lnw_ref, lnb_ref, s0_ref,
                y_ref, st_ref, *, valid):
    ones = _pair_ones()
    row = lax.broadcasted_iota(jnp.int32, (HEAD, LANES), 0)
    lane = lax.broadcasted_iota(jnp.int32, (HEAD, LANES), 1)
    left = lane < HEAD
    diag = (row == lane % HEAD).astype(F32)

    def key_sum(x):
        hi = x.astype(BF16)
        lo = (x - hi.astype(F32)).astype(BF16)
        return _dot(hi, ones) + _dot(lo, ones)

    chains = [(b, p) for b in range(y_ref.shape[0]) for p in range(s0_ref.shape[1] // 2)]
    cols = lambda p: slice(p * LANES, (p + 1) * LANES)
    vt = [v_ref[b, :, cols(p)].T for b, p in chains]
    s = [jnp.concatenate([s0_ref[b, 2 * p], s0_ref[b, 2 * p + 1]], axis=1) for b, p in chains]
    ys = [[] for _ in chains]
    for j in range(y_ref.shape[1]):
        for ch, (b, p) in enumerate(chains):
            if j < valid:
                row_of = lambda ref: ref[b, j:j + 1, cols(p)]
                v_col = jnp.where(left, vt[ch][0:HEAD, j:j + 1], vt[ch][HEAD:2 * HEAD, j:j + 1])
                sa = key_sum(s[ch] * row_of(a_ref))
                s[ch] = s[ch] * jnp.exp(row_of(w_ref)) + sa * row_of(b_ref) + v_col * row_of(k_ref)
                y_col = key_sum(s[ch] * row_of(r_ref))
                ys[ch].append(jnp.sum(y_col * diag, axis=0, keepdims=True))
            else:
                ys[ch].append(jnp.zeros((1, LANES), F32))
    for ch, (b, p) in enumerate(chains):
        y_ref[b, :, cols(p)] = _rwkv_out(jnp.concatenate(ys[ch], axis=0), bon_ref[b, :, cols(p)],
                                         g_ref[b, :, cols(p)], lnw_ref[:, cols(p)], lnb_ref[:, cols(p)], ones)
        st_ref[b, 2 * p] = s[ch][:, :HEAD]
        st_ref[b, 2 * p + 1] = s[ch][:, HEAD:]


def _wkv(r, w, k, v, a, b, g, bonus, lnx_w, lnx_b, s0, valid):
    n, t, c = r.shape
    assert t == SUBLANES
    nb = 2 if n % 2 == 0 else 1
    tok = pl.BlockSpec((nb, t, c), lambda bi: (bi, 0, 0))
    vec = pl.BlockSpec((1, c), lambda bi: (0, 0))
    st = pl.BlockSpec((nb,) + s0.shape[1:], lambda bi: (bi, 0, 0, 0))
    return pl.pallas_call(
        functools.partial(_wkv_kernel, valid=valid),
        grid=(n // nb,),
        in_specs=[tok] * 8 + [vec, vec, st],
        out_specs=[tok, st],
        out_shape=[jax.ShapeDtypeStruct((n, t, c), F32), jax.ShapeDtypeStruct(s0.shape, F32)],
        compiler_params=_cparams(("arbitrary",)),
        name="wkv_scan",
    )(r, w, k, v, a, b, g, bonus, lnx_w.reshape(1, c), lnx_b.reshape(1, c), s0)


WKV_CHUNK = 64


def _bd(x):
    xb = x.astype(BF16)
    left = lax.broadcasted_iota(jnp.int32, x.shape, 1) < HEAD
    zero = jnp.zeros_like(xb)
    return jnp.concatenate([jnp.where(left, xb, zero), jnp.where(left, zero, xb)], axis=0)


def _dot_nt(a, b):
    return lax.dot_general(a, b, (((1,), (1,)), ((), ())), preferred_element_type=F32)


def _dot_tn(a, b):
    return lax.dot_general(a, b, (((0,), (0,)), ((), ())), preferred_element_type=F32)


def _wkv_chunk_local_steps(r, lw, k, v, a, b, consts):
    tri, eye2, strict, incl, blocks = consts
    c = WKV_CHUNK
    n = range(len(r))
    zero = jnp.zeros((c, LANES), F32)

    def cumsum(x):
        hi = x.astype(BF16)
        lo = (x - hi.astype(F32)).astype(BF16)
        return _dot(tri, hi) + _dot(tri, lo)

    lc = [cumsum(lw[i]) for i in n]
    yield
    ltot = [lc[i][c - 1:c, :] for i in n]
    w_inv = [jnp.exp(-lc[i]) for i in n]
    ah = [a[i] * jnp.exp(lc[i] - lw[i]) for i in n]
    rh = [r[i] * jnp.exp(lc[i]) for i in n]
    yield
    sc = [_dot_nt(jnp.concatenate([ah[i], rh[i]], axis=0).astype(BF16),
                  jnp.concatenate([_bd(b[i] * w_inv[i]), _bd(k[i] * w_inv[i])], axis=0)) for i in n]
    yield
    aab = [jnp.where(strict, sc[i][:c, :LANES], zero) for i in n]

    t = [eye2 + jnp.where(blocks[0], aab[i], zero) for i in n]
    for lvl in range(1, len(blocks)):
        grow = blocks[lvl] & ~blocks[lvl - 1]
        x = [_dot(t[i].astype(BF16), _bd(jnp.where(grow, aab[i], zero))) for i in n]
        yield
        t = [t[i] + _dot(x[i].astype(BF16), _bd(t[i])) for i in n]
        yield

    vb = [_bd(v[i]) for i in n]
    akv = [_dot(jnp.where(strict, sc[i][:c, LANES:], zero).astype(BF16), vb[i]) for i in n]
    yield
    pu = [_dot(t[i].astype(BF16), jnp.concatenate([_bd(ah[i]), _bd(akv[i])], axis=1)) for i in n]
    yield

    left2 = (lax.broadcasted_iota(jnp.int32, (HEAD, 2 * LANES), 1) % LANES) < HEAD
    g, h, q, y0 = [], [], [], []
    for i in n:
        w_rem = jnp.exp(ltot[i] - lc[i])
        lhs_t = jnp.concatenate([b[i] * w_rem, k[i] * w_rem], axis=0).astype(BF16)
        rhs = jnp.concatenate([pu[i], jnp.concatenate([zero, v[i]], axis=1)], axis=0).astype(BF16)
        full = _dot_tn(lhs_t, rhs)
        gh = jnp.where(left2, full[:HEAD], full[HEAD:])
        g.append(gh[:, :LANES] + eye2 * jnp.exp(ltot[i]))
        h.append(gh[:, LANES:])
    yield
    zero_b = jnp.zeros((2 * c, LANES), BF16)
    for i in n:
        mrb = jnp.where(incl, sc[i][c:, :LANES], zero)
        mrk = jnp.where(incl, sc[i][c:, LANES:], zero)
        rhs2 = jnp.concatenate([jnp.concatenate([_bd(pu[i][:, :LANES]), _bd(pu[i][:, LANES:])], axis=1),
                                jnp.concatenate([zero_b, vb[i]], axis=1)], axis=0)
        qy = _dot(jnp.concatenate([mrb, mrk], axis=1).astype(BF16), rhs2)
        q.append(rh[i] + qy[:, :LANES])
        y0.append(qy[:, LANES:])
    yield
    return g, h, q, y0


def _wkv_fused_kernel(zr_ref, zk_ref, zv_ref, zl_ref, sr_ref, sk_ref, sv_ref, sl_ref,
                      mr_ref, mk_ref, mv_ref, ml_ref, w0_ref, w2_ref, a0_ref, a2_ref, g2_ref,
                      kk_ref, ka_ref, rk_ref, lnw_ref, lnb_ref, s0_ref, out_ref, st_ref,
                      s_scr, pr_scr, pk_scr, pv_scr, pl_scr, *, group_size):
    i = pl.program_id(1)
    c = WKV_CHUNK
    n_seq, tb = out_ref.shape[0], out_ref.shape[1]
    n_chunks = tb // c

    @pl.when(i == 0)
    def _():
        s_scr[...] = s0_ref[...]
        pr_scr[...] = sr_ref[...]
        pk_scr[...] = sk_ref[...]
        pv_scr[...] = sv_ref[...]
        pl_scr[...] = sl_ref[...]

    def shifted(z_ref, p_scr, mu_ref, b, lanes):
        z = z_ref[b, :, lanes]
        first = lax.broadcasted_iota(jnp.int32, z.shape, 0) == 0
        prev = jnp.where(first, p_scr[b:b + 1, lanes], pltpu.roll(z, 1, axis=0))
        p_scr[b:b + 1, lanes] = z[tb - 1:tb, :]
        return z + (prev - z) * mu_ref[:, lanes]

    ones = _pair_ones()
    c1 = w2_ref.shape[0]
    c2 = c1 + a2_ref.shape[0]
    ti = lax.broadcasted_iota(jnp.int32, (c, LANES), 0)
    si = lax.broadcasted_iota(jnp.int32, (c, LANES), 1) % HEAD
    tri = (lax.broadcasted_iota(jnp.int32, (c, c), 0) >= lax.broadcasted_iota(jnp.int32, (c, c), 1)
           ).astype(BF16)
    eye2 = (ti == si).astype(F32)
    sizes = [2 << l for l in range(c.bit_length() - 1)]
    blocks = [(ti // s) == (si // s) for s in sizes]
    consts = (tri, eye2, ti > si, ti >= si, blocks)
    lora_in = {}

    def feature_steps(group):
        out = []
        for b, sp in group:
            lanes = slice(sp * LANES, (sp + 1) * LANES)
            r = shifted(zr_ref, pr_scr, mr_ref, b, lanes)
            k = shifted(zk_ref, pk_scr, mk_ref, b, lanes)
            yield
            v = shifted(zv_ref, pv_scr, mv_ref, b, lanes)
            if b not in lora_in:
                lora_in[b] = shifted(zl_ref, pl_scr, ml_ref, b, slice(None))
            zl = lora_in[b]
            yield
            lw, k2, a_neg, bb, gate, bonus = yield from _rwkv_features_steps(
                r, k, v, zl[:, :c1], zl[:, c1:c2], zl[:, c2:], w0_ref[:, lanes], w2_ref[:, lanes],
                a0_ref[:, lanes], a2_ref[:, lanes], g2_ref[:, lanes], kk_ref[:, lanes], ka_ref[:, lanes],
                rk_ref[:, lanes], ones)
            out.append((r, lw, k2, v, a_neg, bb, gate, bonus))
        return out

    def local_steps(feat):
        chunks = lambda which: [f[which][ci * c:(ci + 1) * c, :] for ci in range(n_chunks) for f in feat]
        return (yield from _wkv_chunk_local_steps(*(chunks(which) for which in range(6)), consts))

    def tail_steps(group, feat, local):
        g, h, q, y0 = local
        st = [s_scr[b, sp] for b, sp in group]
        ys = [[] for _ in group]
        for ci in range(n_chunks):
            for j in range(len(group)):
                u = ci * len(group) + j
                out = _dot(jnp.concatenate([q[u], g[u]], axis=0).astype(BF16), _bd(st[j]))
                ys[j].append(out[:c] + y0[u])
                st[j] = out[c:] + h[u]
            yield
        for j, (b, sp) in enumerate(group):
            lanes = slice(sp * LANES, (sp + 1) * LANES)
            s_scr[b, sp] = st[j]
            out_ref[b, :, lanes] = _rwkv_out(jnp.concatenate(ys[j], axis=0), feat[j][7], feat[j][6],
                                             lnw_ref[:, lanes], lnb_ref[:, lanes], ones)
            yield

        @pl.when(i == pl.num_programs(1) - 1)
        def _():
            for j, (b, sp) in enumerate(group):
                st_ref[b, sp] = st[j]

    chains = [(b, sp) for b in range(n_seq) for sp in range(out_ref.shape[2] // LANES)]
    groups = [chains[j:j + group_size] for j in range(0, len(chains), group_size)]
    feats, locals_ = {}, {}
    for step in range(len(groups) + 2):
        gens, roles = [], []
        if 0 <= step - 1 < len(groups):
            gens.append(local_steps(feats[step - 1]))
            roles.append((locals_, step - 1))
        if step < len(groups):
            gens.append(feature_steps(groups[step]))
            roles.append((feats, step))
        if 0 <= step - 2 < len(groups):
            gens.append(tail_steps(groups[step - 2], feats[step - 2], locals_[step - 2]))
            roles.append(({}, step - 2))
        for (store, idx), val in zip(roles, _zip_stages(*gens)):
            store[idx] = val


def _wkv_fused(z_rwkv, shift0, s0, prm, tb, pp):
    n, t, cols = z_rwkv.shape
    width = prm["decay_w0"].shape[-1]
    bw = pp * LANES
    nblk = width // bw
    lora = cols - 3 * width
    assert width % bw == 0 and (3 * width) % lora == 0 and lora % LANES == 0
    lblk = 3 * width // lora
    row = lambda x: x.reshape(1, -1)
    mu = row(prm["shift_mu"])
    tok = lambda off: pl.BlockSpec((n, tb, bw), lambda p, i: (0, i, off + p))
    vec = lambda rows_, off: pl.BlockSpec((rows_, bw), lambda p, i: (0, off + p))
    st = pl.BlockSpec((n, pp, HEAD, LANES), lambda p, i: (0, p, 0, 0))
    in_specs = [tok(0), tok(nblk), tok(2 * nblk), pl.BlockSpec((n, tb, lora), lambda p, i: (0, i, lblk)),
                vec(n, 0), vec(n, nblk), vec(n, 2 * nblk), pl.BlockSpec((n, lora), lambda p, i: (0, lblk)),
                vec(1, 0), vec(1, nblk), vec(1, 2 * nblk), pl.BlockSpec((1, lora), lambda p, i: (0, lblk)),
                vec(1, 0), vec(prm["decay_w2"].shape[0], 0), vec(1, 0), vec(prm["aaa_a2"].shape[0], 0),
                vec(prm["gate_g2"].shape[0], 0), vec(1, 0), vec(1, 0), vec(1, 0), vec(1, 0), vec(1, 0), st]
    args = [z_rwkv] * 4 + [shift0] * 4 + [mu] * 4 + [
        row(prm["decay_w0"]), prm["decay_w2"], row(prm["aaa_a0"]), prm["aaa_a2"], prm["gate_g2"],
        row(prm["k_k"]), row(prm["k_a"]), row(prm["r_k"]), row(prm["lnx_w"]), row(prm["lnx_b"]), s0]
    return pl.pallas_call(
        functools.partial(_wkv_fused_kernel, group_size=2),
        grid=(nblk, t // tb),
        in_specs=in_specs,
        out_specs=[tok(0), st],
        out_shape=[jax.ShapeDtypeStruct((n, t, width), F32), jax.ShapeDtypeStruct(s0.shape, F32)],
        scratch_shapes=[pltpu.VMEM((n, pp, HEAD, LANES), F32), pltpu.VMEM((n, bw), F32),
                        pltpu.VMEM((n, bw), F32), pltpu.VMEM((n, bw), F32),
                        pltpu.VMEM((n, lora), F32)],
        compiler_params=_cparams(("arbitrary", "arbitrary")),
        name="wkv_fused",
    )(*args)


def _pack_state(s):
    n, h, vd, kd = s.shape
    return s.reshape(n, h // 2, 2, vd, kd).transpose(0, 1, 4, 2, 3).reshape(n, h // 2, kd, 2 * vd)


def _unpack_state(s, h):
    n, _, kd, _ = s.shape
    return s.reshape(n, h // 2, kd, 2, HEAD).transpose(0, 1, 3, 4, 2).reshape(n, h, HEAD, kd)


def _attn_prompt_kernel(q_ref, k_ref, kp_ref, v_ref, vp_ref, o_ref, l_ref, *, dil, sub, n_back, scale):
    first = pl.program_id(1) == 0
    qi = lax.broadcasted_iota(jnp.int32, (ATTN_BLOCK, 2 * ATTN_BLOCK), 0)
    ki = lax.broadcasted_iota(jnp.int32, (ATTN_BLOCK, 2 * ATTN_BLOCK), 1)
    dist = qi + ATTN_BLOCK - ki
    band = (dist >= 0) & (dist <= n_back)
    own = ki >= ATTN_BLOCK
    left = lax.broadcasted_iota(jnp.int32, (ATTN_BLOCK, LANES), 1) < HEAD
    ones_kv = jnp.ones((2 * ATTN_BLOCK, LANES), BF16)

    def rows(j, r):
        if dil == 1:
            return pl.ds(j * ATTN_BLOCK, ATTN_BLOCK)
        return pl.ds(j * ATTN_BLOCK * dil + r, ATTN_BLOCK, stride=dil)

    def units(jr):
        probs = []
        qs, kcat, vcat, masks = [], [], [], []
        for j, r in jr:
            q = (q_ref[0, rows(j, r), :] * scale).astype(BF16)
            if j == 0:
                kp, vp = kp_ref[0, rows(0, r), :], vp_ref[0, rows(0, r), :]
                masks.append(band & (own | jnp.logical_not(first)))
            else:
                kp, vp = k_ref[0, rows(j - 1, r), :], v_ref[0, rows(j - 1, r), :]
                masks.append(band)
            kcat.append(jnp.concatenate([kp, k_ref[0, rows(j, r), :]], axis=0).astype(BF16))
            vcat.append(jnp.concatenate([vp, v_ref[0, rows(j, r), :]], axis=0).astype(BF16))
            zq = jnp.zeros_like(q)
            qs += [jnp.where(left, q, zq), jnp.where(left, zq, q)]
            probs += [(len(kcat) - 1, 0), (len(kcat) - 1, 1)]
        s = [jnp.where(masks[u], _dot_nt(qs[p], kcat[u]), NEG) for p, (u, _) in enumerate(probs)]
        m = [jnp.max(x, axis=-1, keepdims=True) for x in s]
        e = [jnp.exp(s[p] - m[p]).astype(BF16) for p in range(len(probs))]
        o = [_dot(e[p], vcat[u]) for p, (u, _) in enumerate(probs)]
        dens = [_dot(e[p], ones_kv) for p in range(len(probs))]
        for u, (j, r) in enumerate(jr):
            den = jnp.where(left, dens[2 * u], dens[2 * u + 1])
            o_ref[0, rows(j, r), :] = jnp.where(left, o[2 * u], o[2 * u + 1]) / den
            l_ref[0, rows(j, r), :] = jnp.where(left, m[2 * u], m[2 * u + 1]) + jnp.log(den)

    if dil <= 4:
        units([(j, r) for j in range(sub) for r in range(dil)])
    else:
        for j in range(sub):
            def body(r4, carry, j=j):
                units([(j, r4 * 4 + u) for u in range(4)])
                return carry
            lax.fori_loop(0, dil // 4, body, 0)


def _attn_prompt(zqkv, gi, span):
    n, t, cols = zqkv.shape
    d = DILATIONS[gi]
    aw = ATTN_HEADS * HEAD
    ng = len(DILATIONS)
    back = ATTN_BLOCK * d
    sub = span // back
    halves = aw // LANES
    col = lambda which, hp: (which * ng + gi) * halves + hp
    cur = lambda which: pl.BlockSpec((1, span, LANES), lambda b, i, hp: (b, i, col(which, hp)))
    prev = lambda which: pl.BlockSpec((1, back, LANES),
                                      lambda b, i, hp: (b, jnp.maximum(i * sub - 1, 0), col(which, hp)))
    out_spec = pl.BlockSpec((1, span, LANES), lambda b, i, hp: (b, i, hp))
    out_sds = jax.ShapeDtypeStruct((n, t, aw), F32)
    return pl.pallas_call(
        functools.partial(_attn_prompt_kernel, dil=d, sub=sub, n_back=WINDOWS[gi] // d, scale=HEAD ** -0.5),
        grid=(n, t // span, halves),
        in_specs=[cur(0), cur(1), prev(1), cur(2), prev(2)],
        out_specs=[out_spec, out_spec],
        out_shape=[out_sds, out_sds],
        compiler_params=_cparams(("arbitrary", "arbitrary", "arbitrary")),
        name=f"attn_prompt_g{gi}",
    )(zqkv, zqkv, zqkv, zqkv, zqkv)


def _attn_sample_kernel(q_ref, k_ref, v_ref, c_ref, o_ref, l_ref, *, dil, n_back, valid, scale):
    nb, tp = q_ref.shape[0], q_ref.shape[1]
    cl = c_ref.shape[4]
    dist_c = (cl + lax.broadcasted_iota(jnp.int32, (tp, cl), 0)
              - lax.broadcasted_iota(jnp.int32, (tp, cl), 1))
    mask_c = (dist_c % dil == 0) & (dist_c <= dil * n_back)
    pn = lax.broadcasted_iota(jnp.int32, (tp, tp), 1)
    dist_n = lax.broadcasted_iota(jnp.int32, (tp, tp), 0) - pn
    mask_n = (dist_n >= 0) & (dist_n % dil == 0) & (dist_n <= dil * n_back) & (pn < valid)
    probs = [(b, h) for b in range(nb) for h in range(ATTN_HEADS)]
    sl = lambda h: slice(h * HEAD, (h + 1) * HEAD)
    q = [(q_ref[b, :, sl(h)] * scale).astype(BF16) for b, h in probs]
    s_c = [jnp.where(mask_c, _dot(q[p], c_ref[b, 0, h].astype(BF16)), NEG) for p, (b, h) in enumerate(probs)]
    s_n = [jnp.where(mask_n, _dot_nt(q[p], k_ref[b, :, sl(h)].astype(BF16)), NEG)
           for p, (b, h) in enumerate(probs)]
    m = [jnp.maximum(jnp.max(s_c[p], axis=-1, keepdims=True), jnp.max(s_n[p], axis=-1, keepdims=True))
         for p in range(len(probs))]
    e_c = [jnp.exp(s_c[p] - m[p]) for p in range(len(probs))]
    e_n = [jnp.exp(s_n[p] - m[p]) for p in range(len(probs))]
    den = [jnp.sum(e_c[p], axis=-1, keepdims=True) + jnp.sum(e_n[p], axis=-1, keepdims=True)
           for p in range(len(probs))]
    o = [_dot_nt(e_c[p].astype(BF16), c_ref[b, 1, h].astype(BF16))
         + _dot(e_n[p].astype(BF16), v_ref[b, :, sl(h)].astype(BF16)) for p, (b, h) in enumerate(probs)]
    for b in range(nb):
        ps = [p for p, (bb, _) in enumerate(probs) if bb == b]
        o_ref[b] = jnp.concatenate([o[p] / den[p] for p in ps], axis=1)
        l_ref[b] = jnp.concatenate([jnp.broadcast_to(m[p] + jnp.log(den[p]), (tp, HEAD)) for p in ps], axis=1)


def _attn_sample(zqkv, cache, gi, valid):
    n, tp, cols = zqkv.shape
    aw = ATTN_HEADS * HEAD
    ng = len(DILATIONS)
    d = DILATIONS[gi]
    cache_t = cache.transpose(0, 2, 3, 4, 1)
    nb = 2 if n % 2 == 0 else 1
    spec = lambda which: pl.BlockSpec((nb, tp, aw), lambda b: (b, 0, which * ng + gi))
    out_spec = pl.BlockSpec((nb, tp, aw), lambda b: (b, 0, 0))
    out_sds = jax.ShapeDtypeStruct((n, tp, aw), F32)
    return pl.pallas_call(
        functools.partial(_attn_sample_kernel, dil=d, n_back=WINDOWS[gi] // d, valid=valid,
                          scale=HEAD ** -0.5),
        grid=(n // nb,),
        in_specs=[spec(0), spec(1), spec(2),
                  pl.BlockSpec((nb,) + cache_t.shape[1:], lambda b: (b, 0, 0, 0, 0))],
        out_specs=[out_spec, out_spec],
        out_shape=[out_sds, out_sds],
        compiler_params=_cparams(("arbitrary",)),
        name=f"attn_sample_g{gi}",
    )(zqkv, zqkv, zqkv, cache_t)


def _post_kernel(x_ref, rwkv_ref, o0, o1, o2, l0, l1, l2, zg_ref, wbr_ref, wba_ref, wout_ref, g_ref,
                 w1_ref, w2_ref, p_ref, gp_ref, wg_ref, wp_ref, gf_ref, o_ref, h_scr, acc_scr, *, final):
    j = pl.program_id(1)

    @pl.when(j == 0)
    def _():
        ls = [l0[...], l1[...], l2[...]]
        m = jnp.maximum(jnp.maximum(ls[0], ls[1]), ls[2])
        es = [jnp.exp(l - m) for l in ls]
        attn = (es[0] * o0[...] + es[1] * o1[...] + es[2] * o2[...]) / (es[0] + es[1] + es[2])
        d = x_ref.shape[1]
        gates = zg_ref[...].astype(F32)
        mixed = (gates[:, :d] * _dot(rwkv_ref[...].astype(BF16), wbr_ref[...])
                 + gates[:, d:] * _dot(attn.astype(BF16), wba_ref[...]))
        x = x_ref[...] + _dot(mixed.astype(BF16), wout_ref[...])
        acc_scr[...] = x
        h_scr[...] = _rms(x, g_ref[...]).astype(BF16)

    u = jnp.maximum(_dot(h_scr[...], w1_ref[...]), 0.0)
    acc_scr[...] += _dot((u * u).astype(BF16), w2_ref[...])

    @pl.when(j == pl.num_programs(1) - 1)
    def _():
        x = acc_scr[...]
        h = _rms(x, gp_ref[...]).astype(BF16)
        x = x + _sigmoid(_dot(h, wg_ref[...])) * _dot(p_ref[...].astype(BF16), wp_ref[...])
        if final:
            x = _rms(x, gf_ref[...])
        o_ref[...] = x


def _post(x, rwkv, os_, ls_, zg, wbr, wba, wout, g, w1, w2, p, gp, wg, wp, gf, final, tm, tf):
    rows, d = x.shape
    dff = w1.shape[1]
    vec = lambda a: a.reshape(1, d)
    rowblk = lambda a: pl.BlockSpec((tm, a.shape[1]), lambda i, j: (i, 0))
    full = lambda a: pl.BlockSpec(a.shape, lambda i, j: (0, 0), pipeline_mode=pl.Buffered(1))
    row_args = [x, rwkv, *os_, *ls_, zg]
    return pl.pallas_call(
        functools.partial(_post_kernel, final=final),
        grid=(rows // tm, dff // tf),
        in_specs=[rowblk(a) for a in row_args] + [full(wbr), full(wba), full(wout), full(vec(g)),
                  pl.BlockSpec((d, tf), lambda i, j: (0, j)),
                  pl.BlockSpec((tf, d), lambda i, j: (j, 0)),
                  rowblk(p), full(vec(gp)), full(wg), full(wp), full(vec(gf))],
        out_specs=pl.BlockSpec((tm, d), lambda i, j: (i, 0)),
        out_shape=jax.ShapeDtypeStruct((rows, d), F32),
        scratch_shapes=[pltpu.VMEM((tm, d), BF16), pltpu.VMEM((tm, d), F32)],
        compiler_params=_cparams(("arbitrary", "arbitrary")),
        name="post",
    )(*row_args, wbr, wba, wout, vec(g), w1, w2, p, vec(gp), wg, wp, vec(gf))


def _layer(x, p, wkv0, shift0, caches, prm, norm_final, final, valid):
    n, t, d = x.shape
    rows = n * t
    width = prm["decay_w0"].shape[-1]
    rwkv_cols = prm["shift_mu"].shape[-1]
    aw = ATTN_HEADS * HEAD
    qkv_cols = 3 * len(DILATIONS) * aw
    tm = min(512, rows)
    x2 = x.reshape(rows, d)

    pick = lambda cols, opts: next(o for o in opts if cols % o == 0)
    z_rwkv, z_qkv, z_gate = _in_proj(x2, prm["norm_mix"], prm["w_in"].astype(BF16),
                                     (rwkv_cols, qkv_cols, 2 * d), tm)
    z_rwkv = z_rwkv.reshape(n, t, rwkv_cols)
    z_qkv = z_qkv.reshape(n, t, qkv_cols)

    if valid == t and t % WKV_CHUNK == 0:
        rwkv, s_t = _wkv_fused(z_rwkv, shift0, _pack_state(wkv0), prm, pick(t, (512, 256, WKV_CHUNK)), 4)
        wkv_t = _unpack_state(s_t, width // HEAD)
    else:
        feats = _rwkv_prep(z_rwkv, shift0, prm, SUBLANES if n % SUBLANES == 0 else 1)
        rwkv, wkv_t = _wkv(*feats, prm["lnx_w"], prm["lnx_b"], wkv0, valid)
    shift_t = z_rwkv[:, valid - 1, :]

    os_, ls_, new_kv = [], [], []
    ng = len(DILATIONS)
    for gi in range(ng):
        if caches is None:
            o, l = _attn_prompt(z_qkv, gi, max(1024, ATTN_BLOCK * DILATIONS[gi]))
            keep = min(WINDOWS[gi], t)
            kv_rows = z_qkv[:, t - keep:t]
        else:
            o, l = _attn_sample(z_qkv, caches[gi], gi, valid)
            kv_rows = z_qkv[:, :valid]
        new_kv.append(jnp.stack([kv_rows[:, :, (which * ng + gi) * aw:(which * ng + gi + 1) * aw]
                                 .reshape(n, -1, ATTN_HEADS, HEAD) for which in (1, 2)], axis=2))
        os_.append(o.reshape(rows, aw))
        ls_.append(l.reshape(rows, aw))

    bf = lambda name: prm[name].astype(BF16)
    x2 = _post(x2, rwkv.reshape(rows, -1), os_, ls_, z_gate, bf("w_branch_rwkv"), bf("w_branch_attn"),
               bf("w_out"), prm["norm_mlp"], bf("w_ff1"), bf("w_ff2"), p.reshape(rows, -1),
               prm["norm_ple"], bf("w_ple_gate"), bf("w_ple"), norm_final, final, tm, 1024)
    return x2.reshape(n, t, d), new_kv, wkv_t, shift_t


_LAYER_PARAMS = ("norm_mix", "w_in", "shift_mu", "decay_w0", "decay_w2", "aaa_a0", "aaa_a2", "gate_g2",
                 "k_k", "k_a", "r_k", "lnx_w", "lnx_b", "w_branch_rwkv", "w_branch_attn", "w_out",
                 "norm_mlp", "w_ff1", "w_ff2", "norm_ple", "w_ple", "w_ple_gate")


def kernel(x_prompt, x_sample, cache_kv_g0, cache_kv_g1, cache_kv_g2, state_wkv, state_shift, p_prompt, p_sample, norm_mix, w_in, shift_mu, decay_w0, decay_w2, aaa_a0, aaa_a2, gate_g2, k_k, k_a, r_k, lnx_w, lnx_b, w_branch_rwkv, w_branch_attn, w_out, norm_mlp, w_ff1, w_ff2, norm_ple, w_ple, w_ple_gate, norm_final):
    stacked = dict(zip(_LAYER_PARAMS, (norm_mix, w_in, shift_mu, decay_w0, decay_w2, aaa_a0, aaa_a2,
                                       gate_g2, k_k, k_a, r_k, lnx_w, lnx_b, w_branch_rwkv,
                                       w_branch_attn, w_out, norm_mlp, w_ff1, w_ff2, norm_ple, w_ple,
                                       w_ple_gate)))
    depth = norm_mix.shape[0]
    nb_p, _, d = x_prompt.shape
    nb_s, t_s, _ = x_sample.shape
    heads = decay_w0.shape[-1] // HEAD
    rwkv_cols = shift_mu.shape[-1]
    t_pad = -(-t_s // SUBLANES) * SUBLANES
    pad_t = lambda u: jnp.pad(u, ((0, 0), (0, t_pad - t_s), (0, 0)))

    yp, ys = x_prompt, pad_t(x_sample)
    kvp, kvs = [[], [], []], [[], [], []]
    wkv_p, sh_p, wkv_s, sh_s = [], [], [], []
    for i in range(depth):
        prm = {name: val[i] for name, val in stacked.items()}
        prm["r_k"] = prm["r_k"].reshape(-1)
        last = i == depth - 1
        wkv0 = jnp.zeros((nb_p, heads, HEAD, HEAD), F32)
        shift0 = jnp.zeros((nb_p, rwkv_cols), F32)
        yp, nkv_p, w_p, s_p = _layer(yp, p_prompt[i], wkv0, shift0, None, prm, norm_final, last,
                                     x_prompt.shape[1])
        ys, nkv_s, w_s, s_s = _layer(ys, pad_t(p_sample[i]), state_wkv[i], state_shift[i],
                                     (cache_kv_g0[i], cache_kv_g1[i], cache_kv_g2[i]), prm, norm_final,
                                     last, t_s)
        for gi in range(3):
            kvp[gi].append(nkv_p[gi])
            kvs[gi].append(nkv_s[gi])
        wkv_p.append(w_p)
        sh_p.append(s_p)
        wkv_s.append(w_s)
        sh_s.append(s_s)
    return (yp, ys[:, :t_s],
            jnp.stack(kvp[0]), jnp.stack(kvp[1]), jnp.stack(kvp[2]),
            jnp.stack(wkv_p), jnp.stack(sh_p),
            jnp.stack(kvs[0]), jnp.stack(kvs[1]), jnp.stack(kvs[2]),
            jnp.stack(wkv_s), jnp.stack(sh_s))
```

```python
import functools

import jax
import jax.numpy as jnp
from jax import lax
from jax.experimental import pallas as pl
from jax.experimental.pallas import tpu as pltpu

F32 = jnp.float32
BF16 = jnp.bfloat16

HEAD = 64
LANES = 128
SUBLANES = 8
ATTN_HEADS = 4
ATTN_BLOCK = 128
WINDOWS = (128, 512, 2048)
DILATIONS = (1, 4, 16)
NORM_EPS = 1e-6
LNX_EPS = 64e-5
NEG = -1e30
VMEM_LIMIT = 56 * 1024 * 1024


def _cparams(sem):
    return pltpu.CompilerParams(dimension_semantics=sem, vmem_limit_bytes=VMEM_LIMIT)


def _dot(a, b):
    return jnp.dot(a, b, preferred_element_type=F32)


def _dot_bf16(a, b):
    return _dot(a.astype(BF16), b.astype(BF16))


def _pair_ones():
    r = lax.broadcasted_iota(jnp.int32, (LANES, LANES), 0) // HEAD
    c = lax.broadcasted_iota(jnp.int32, (LANES, LANES), 1) // HEAD
    return (r == c).astype(BF16)


def _head_sum(x, ones):
    outs = []
    for c in range(x.shape[1] // LANES):
        outs.append(_dot(x[:, c * LANES:(c + 1) * LANES].astype(BF16), ones))
    return jnp.concatenate(outs, axis=1)


def _rms(x, g):
    return x * lax.rsqrt(jnp.mean(x * x, axis=-1, keepdims=True) + NORM_EPS) * g


def _sigmoid(x):
    return 1.0 / (1.0 + jnp.exp(-x))


DECAY_SCALE = 0.6065306597126334


def _in_proj_kernel(x_ref, g_ref, w_ref, zr_ref, zq_ref, gate_ref):
    h = _rms(x_ref[...], g_ref[...]).astype(BF16)
    c1 = zr_ref.shape[1]
    c2 = c1 + zq_ref.shape[1]
    zr_ref[...] = _dot(h, w_ref[:, :c1])
    zq_ref[...] = _dot(h, w_ref[:, c1:c2])
    gate_ref[...] = _sigmoid(_dot(h, w_ref[:, c2:])).astype(BF16)


def _in_proj(x, g, w, splits, tm):
    rows, d = x.shape
    assert sum(splits) == w.shape[1] and len(splits) == 3
    return pl.pallas_call(
        _in_proj_kernel,
        grid=(rows // tm,),
        in_specs=[pl.BlockSpec((tm, d), lambda i: (i, 0)),
                  pl.BlockSpec((1, d), lambda i: (0, 0)),
                  pl.BlockSpec(w.shape, lambda i: (0, 0), pipeline_mode=pl.Buffered(1))],
        out_specs=[pl.BlockSpec((tm, c), lambda i: (i, 0)) for c in splits],
        out_shape=[jax.ShapeDtypeStruct((rows, c), dt) for c, dt in zip(splits, (F32, F32, BF16))],
        compiler_params=_cparams(("arbitrary",)),
        name="in_proj",
    )(x, g.reshape(1, d), w)


def _drain(steps):
    try:
        while True:
            next(steps)
    except StopIteration as stop:
        return stop.value


def _zip_stages(*gens):
    vals = [None] * len(gens)
    live = list(range(len(gens)))
    while live:
        for i in list(live):
            try:
                next(gens[i])
            except StopIteration as stop:
                vals[i] = stop.value
                live.remove(i)
    return vals


def _rwkv_features_steps(r, k, v, xw, xa, xg, w0, w2, a0, a2, g2, k_k, k_a, r_k, ones):
    pre = w0 + _dot_bf16(jnp.tanh(xw), w2)
    yield
    lw = _sigmoid(pre) * (-DECAY_SCALE)
    yield
    a = _sigmoid(a0 + _dot_bf16(xa, a2))
    yield
    g = _dot_bf16(_sigmoid(xg), g2)
    yield
    kk = k * k_k
    norm2 = _head_sum(kk * kk, ones)
    yield
    kk = kk * lax.rsqrt(jnp.maximum(norm2, 1e-24))
    k2 = k * (1.0 + (a - 1.0) * k_a)
    yield
    bonus = _head_sum(r * k2 * r_k, ones) * v
    yield
    return lw, k2, -kk, kk * a, g, bonus


def _rwkv_features(*args):
    return _drain(_rwkv_features_steps(*args))


def _rwkv_out(y, bonus, g, lnx_w, lnx_b, ones):
    mu = _head_sum(y, ones) * (1.0 / HEAD)
    yc = y - mu
    var = _head_sum(yc * yc, ones) * (1.0 / HEAD)
    return (yc * lax.rsqrt(var + LNX_EPS) * lnx_w + lnx_b + bonus) * g


def _rwkv_prep_kernel(z_ref, sh_ref, mu_ref, w0_ref, w2_ref, a0_ref, a2_ref, g2_ref, kk_ref, ka_ref,
                      rk_ref, r_o, w_o, k_o, v_o, a_o, b_o, g_o, bon_o, *, width):
    nb, t, cols = z_ref.shape
    first = lax.broadcasted_iota(jnp.int32, (t, cols), 0) == 0
    zs = []
    for b in range(nb):
        z = z_ref[b]
        prev = jnp.where(first, sh_ref[b:b + 1, :], pltpu.roll(z, 1, axis=0))
        zs.append(z + (prev - z) * mu_ref[...])
    zs = jnp.concatenate(zs, axis=0)

    c0 = 3 * width
    c1 = c0 + w2_ref.shape[0]
    c2 = c1 + a2_ref.shape[0]
    r, k, v = zs[:, :width], zs[:, width:2 * width], zs[:, 2 * width:c0]
    lw, k2, a_neg, b, g, bonus = _rwkv_features(
        r, k, v, zs[:, c0:c1], zs[:, c1:c2], zs[:, c2:], w0_ref[...], w2_ref[...], a0_ref[...],
        a2_ref[...], g2_ref[...], kk_ref[...], ka_ref[...], rk_ref[...], _pair_ones())
    for ref, val in ((r_o, r), (w_o, lw), (k_o, k2), (v_o, v), (a_o, a_neg), (b_o, b), (g_o, g), (bon_o, bonus)):
        for bi in range(nb):
            ref[bi] = val[bi * t:(bi + 1) * t, :]


def _rwkv_prep(z_rwkv, shift0, prm, nb):
    n, t, cols = z_rwkv.shape
    width = prm["decay_w0"].shape[-1]
    assert n % nb == 0
    row = lambda x: x.reshape(1, -1)
    full = lambda a: pl.BlockSpec(a.shape, lambda i: (0,) * a.ndim)
    args = [row(prm["shift_mu"]), row(prm["decay_w0"]), prm["decay_w2"], row(prm["aaa_a0"]),
            prm["aaa_a2"], prm["gate_g2"], row(prm["k_k"]), row(prm["k_a"]), row(prm["r_k"])]
    out_spec = pl.BlockSpec((nb, t, width), lambda i: (i, 0, 0))
    out_sds = jax.ShapeDtypeStruct((n, t, width), F32)
    return pl.pallas_call(
        functools.partial(_rwkv_prep_kernel, width=width),
        grid=(n // nb,),
        in_specs=[pl.BlockSpec((nb, t, cols), lambda i: (i, 0, 0)), pl.BlockSpec((nb, cols), lambda i: (i, 0))]
                 + [full(a) for a in args],
        out_specs=[out_spec] * 8,
        out_shape=[out_sds] * 8,
        compiler_params=_cparams(("arbitrary",)),
        name="rwkv_prep",
    )(z_rwkv, shift0, *args)


def _wkv_kernel(r_ref, w_ref, k_ref, v_ref, a_ref, b_ref, g_ref, bon_ref, lnw_ref, lnb_ref, s0_ref,
                y_ref, st_ref, *, valid):
    ones = _pair_ones()
    row = lax.broadcasted_iota(jnp.int32, (HEAD, LANES), 0)
    lane = lax.broadcasted_iota(jnp.int32, (HEAD, LANES), 1)
    left = lane < HEAD
    diag = (row == lane % HEAD).astype(F32)

    def key_sum(x):
        hi = x.astype(BF16)
        lo = (x - hi.astype(F32)).astype(BF16)
        return _dot(hi, ones) + _dot(lo, ones)

    chains = [(b, p) for b in range(y_ref.shape[0]) for p in range(s0_ref.shape[1] // 2)]
    cols = lambda p: slice(p * LANES, (p + 1) * LANES)
    vt = [v_ref[b, :, cols(p)].T for b, p in chains]
    s = [jnp.concatenate([s0_ref[b, 2 * p], s0_ref[b, 2 * p + 1]], axis=1) for b, p in chains]
    ys = [[] for _ in chains]
    for j in range(y_ref.shape[1]):
        for ch, (b, p) in enumerate(chains):
            if j < valid:
                row_of = lambda ref: ref[b, j:j + 1, cols(p)]
                v_col = jnp.where(left, vt[ch][0:HEAD, j:j + 1], vt[ch][HEAD:2 * HEAD, j:j + 1])
                sa = key_sum(s[ch] * row_of(a_ref))
                s[ch] = s[ch] * jnp.exp(row_of(w_ref)) + sa * row_of(b_ref) + v_col * row_of(k_ref)
                y_col = key_sum(s[ch] * row_of(r_ref))
                ys[ch].append(jnp.sum(y_col * diag, axis=0, keepdims=True))
            else:
                ys[ch].append(jnp.zeros((1, LANES), F32))
    for ch, (b, p) in enumerate(chains):
        y_ref[b, :, cols(p)] = _rwkv_out(jnp.concatenate(ys[ch], axis=0), bon_ref[b, :, cols(p)],
                                         g_ref[b, :, cols(p)], lnw_ref[:, cols(p)], lnb_ref[:, cols(p)], ones)
        st_ref[b, 2 * p] = s[ch][:, :HEAD]
        st_ref[b, 2 * p + 1] = s[ch][:, HEAD:]


def _wkv(r, w, k, v, a, b, g, bonus, lnx_w, lnx_b, s0, valid):
    n, t, c = r.shape
    assert t == SUBLANES
    nb = 2 if n % 2 == 0 else 1
    tok = pl.BlockSpec((nb, t, c), lambda bi: (bi, 0, 0))
    vec = pl.BlockSpec((1, c), lambda bi: (0, 0))
    st = pl.BlockSpec((nb,) + s0.shape[1:], lambda bi: (bi, 0, 0, 0))
    return pl.pallas_call(
        functools.partial(_wkv_kernel, valid=valid),
        grid=(n // nb,),
        in_specs=[tok] * 8 + [vec, vec, st],
        out_specs=[tok, st],
        out_shape=[jax.ShapeDtypeStruct((n, t, c), F32), jax.ShapeDtypeStruct(s0.shape, F32)],
        compiler_params=_cparams(("arbitrary",)),
        name="wkv_scan",
    )(r, w, k, v, a, b, g, bonus, lnx_w.reshape(1, c), lnx_b.reshape(1, c), s0)


WKV_CHUNK = 64


def _bd(x):
    xb = x.astype(BF16)
    left = lax.broadcasted_iota(jnp.int32, x.shape, 1) < HEAD
    zero = jnp.zeros_like(xb)
    return jnp.concatenate([jnp.where(left, xb, zero), jnp.where(left, zero, xb)], axis=0)


def _dot_nt(a, b):
    return lax.dot_general(a, b, (((1,), (1,)), ((), ())), preferred_element_type=F32)


def _dot_tn(a, b):
    return lax.dot_general(a, b, (((0,), (0,)), ((), ())), preferred_element_type=F32)


def _wkv_chunk_local_steps(r, lw, k, v, a, b, consts):
    tri, eye2, strict, incl, blocks = consts
    c = WKV_CHUNK
    n = range(len(r))
    zero = jnp.zeros((c, LANES), F32)

    def cumsum(x):
        hi = x.astype(BF16)
        lo = (x - hi.astype(F32)).astype(BF16)
        return _dot(tri, hi) + _dot(tri, lo)

    lc = [cumsum(lw[i]) for i in n]
    yield
    w_inc = [jnp.exp(lc[i]) for i in n]
    w_inv = [1.0 / w_inc[i] for i in n]
    ah = [a[i] * jnp.exp(lc[i] - lw[i]) for i in n]
    rh = [r[i] * w_inc[i] for i in n]
    yield
    sc = [_dot_nt(jnp.concatenate([ah[i], rh[i]], axis=0).astype(BF16),
                  jnp.concatenate([_bd(b[i] * w_inv[i]), _bd(k[i] * w_inv[i])], axis=0)) for i in n]
    yield
    aab = [jnp.where(strict, sc[i][:c, :LANES], zero) for i in n]

    t = [eye2 + jnp.where(blocks[0], aab[i], zero) for i in n]
    for lvl in range(1, len(blocks)):
        grow = blocks[lvl] & ~blocks[lvl - 1]
        x = [_dot(t[i].astype(BF16), _bd(jnp.where(grow, aab[i], zero))) for i in n]
        yield
        t = [t[i] + _dot(x[i].astype(BF16), _bd(t[i])) for i in n]
        yield

    vb = [_bd(v[i]) for i in n]
    akv = [_dot(jnp.where(strict, sc[i][:c, LANES:], zero).astype(BF16), vb[i]) for i in n]
    yield
    pu = [_dot(t[i].astype(BF16), jnp.concatenate([_bd(ah[i]), _bd(akv[i])], axis=1)) for i in n]
    yield

    left2 = (lax.broadcasted_iota(jnp.int32, (HEAD, 2 * LANES), 1) % LANES) < HEAD
    g, h, q, y0 = [], [], [], []
    for i in n:
        wc = w_inc[i][c - 1:c, :]
        w_rem = wc * w_inv[i]
        lhs_t = jnp.concatenate([b[i] * w_rem, k[i] * w_rem], axis=0).astype(BF16)
        rhs = jnp.concatenate([pu[i], jnp.concatenate([zero, v[i]], axis=1)], axis=0).astype(BF16)
        full = _dot_tn(lhs_t, rhs)
        gh = jnp.where(left2, full[:HEAD], full[HEAD:])
        g.append(gh[:, :LANES] + eye2 * wc)
        h.append(gh[:, LANES:])
    yield
    zero_b = jnp.zeros((2 * c, LANES), BF16)
    for i in n:
        mrb = jnp.where(incl, sc[i][c:, :LANES], zero)
        mrk = jnp.where(incl, sc[i][c:, LANES:], zero)
        rhs2 = jnp.concatenate([jnp.concatenate([_bd(pu[i][:, :LANES]), _bd(pu[i][:, LANES:])], axis=1),
                                jnp.concatenate([zero_b, vb[i]], axis=1)], axis=0)
        qy = _dot(jnp.concatenate([mrb, mrk], axis=1).astype(BF16), rhs2)
        q.append(rh[i] + qy[:, :LANES])
        y0.append(qy[:, LANES:])
    yield
    return g, h, q, y0


def _wkv_fused_kernel(zr_ref, zk_ref, zv_ref, zl_ref, sr_ref, sk_ref, sv_ref, sl_ref,
                      mr_ref, mk_ref, mv_ref, ml_ref, w0_ref, w2_ref, a0_ref, a2_ref, g2_ref,
                      kk_ref, ka_ref, rk_ref, lnw_ref, lnb_ref, s0_ref, out_ref, st_ref,
                      s_scr, pr_scr, pk_scr, pv_scr, pl_scr, *, group_size):
    i = pl.program_id(1)
    c = WKV_CHUNK
    n_seq, tb = out_ref.shape[0], out_ref.shape[1]
    n_chunks = tb // c

    @pl.when(i == 0)
    def _():
        s_scr[...] = s0_ref[...]
        pr_scr[...] = sr_ref[...]
        pk_scr[...] = sk_ref[...]
        pv_scr[...] = sv_ref[...]
        pl_scr[...] = sl_ref[...]

    def shifted(z_ref, p_scr, mu_ref, b, lanes):
        z = z_ref[b, :, lanes]
        first = lax.broadcasted_iota(jnp.int32, z.shape, 0) == 0
        prev = jnp.where(first, p_scr[b:b + 1, lanes], pltpu.roll(z, 1, axis=0))
        p_scr[b:b + 1, lanes] = z[tb - 1:tb, :]
        return z + (prev - z) * mu_ref[:, lanes]

    ones = _pair_ones()
    c1 = w2_ref.shape[0]
    c2 = c1 + a2_ref.shape[0]
    ti = lax.broadcasted_iota(jnp.int32, (c, LANES), 0)
    si = lax.broadcasted_iota(jnp.int32, (c, LANES), 1) % HEAD
    tri = (lax.broadcasted_iota(jnp.int32, (c, c), 0) >= lax.broadcasted_iota(jnp.int32, (c, c), 1)
           ).astype(BF16)
    eye2 = (ti == si).astype(F32)
    sizes = [2 << l for l in range(c.bit_length() - 1)]
    blocks = [(ti // s) == (si // s) for s in sizes]
    consts = (tri, eye2, ti > si, ti >= si, blocks)
    lora_in = {}

    def feature_steps(group):
        out = []
        for b, sp in group:
            lanes = slice(sp * LANES, (sp + 1) * LANES)
            r = shifted(zr_ref, pr_scr, mr_ref, b, lanes)
            k = shifted(zk_ref, pk_scr, mk_ref, b, lanes)
            yield
            v = shifted(zv_ref, pv_scr, mv_ref, b, lanes)
            if b not in lora_in:
                lora_in[b] = shifted(zl_ref, pl_scr, ml_ref, b, slice(None))
            zl = lora_in[b]
            yield
            lw, k2, a_neg, bb, gate, bonus = yield from _rwkv_features_steps(
                r, k, v, zl[:, :c1], zl[:, c1:c2], zl[:, c2:], w0_ref[:, lanes], w2_ref[:, lanes],
                a0_ref[:, lanes], a2_ref[:, lanes], g2_ref[:, lanes], kk_ref[:, lanes], ka_ref[:, lanes],
                rk_ref[:, lanes], ones)
            out.append((r, lw, k2, v, a_neg, bb, gate, bonus))
        return out

    def local_steps(feat):
        chunks = lambda which: [f[which][ci * c:(ci + 1) * c, :] for ci in range(n_chunks) for f in feat]
        return (yield from _wkv_chunk_local_steps(*(chunks(which) for which in range(6)), consts))

    def tail_steps(group, feat, local):
        g, h, q, y0 = local
        st = [s_scr[b, sp] for b, sp in group]
        ys = [[] for _ in group]
        for ci in range(n_chunks):
            for j in range(len(group)):
                u = ci * len(group) + j
                out = _dot(jnp.concatenate([q[u], g[u]], axis=0).astype(BF16), _bd(st[j]))
                ys[j].append(out[:c] + y0[u])
                st[j] = out[c:] + h[u]
            yield
        for j, (b, sp) in enumerate(group):
            lanes = slice(sp * LANES, (sp + 1) * LANES)
            s_scr[b, sp] = st[j]
            out_ref[b, :, lanes] = _rwkv_out(jnp.concatenate(ys[j], axis=0), feat[j][7], feat[j][6],
                                             lnw_ref[:, lanes], lnb_ref[:, lanes], ones)
            yield

        @pl.when(i == pl.num_programs(1) - 1)
        def _():
            for j, (b, sp) in enumerate(group):
                st_ref[b, sp] = st[j]

    chains = [(b, sp) for b in range(n_seq) for sp in range(out_ref.shape[2] // LANES)]
    groups = [chains[j:j + group_size] for j in range(0, len(chains), group_size)]
    feats, locals_ = {}, {}
    for step in range(len(groups) + 2):
        gens, roles = [], []
        if 0 <= step - 1 < len(groups):
            gens.append(local_steps(feats[step - 1]))
            roles.append((locals_, step - 1))
        if step < len(groups):
            gens.append(feature_steps(groups[step]))
            roles.append((feats, step))
        if 0 <= step - 2 < len(groups):
            gens.append(tail_steps(groups[step - 2], feats[step - 2], locals_[step - 2]))
            roles.append(({}, step - 2))
        for (store, idx), val in zip(roles, _zip_stages(*gens)):
            store[idx] = val


def _wkv_fused(z_rwkv, shift0, s0, prm, tb, pp):
    n, t, cols = z_rwkv.shape
    width = prm["decay_w0"].shape[-1]
    bw = pp * LANES
    nblk = width // bw
    lora = cols - 3 * width
    assert width % bw == 0 and (3 * width) % lora == 0 and lora % LANES == 0
    lblk = 3 * width // lora
    row = lambda x: x.reshape(1, -1)
    mu = row(prm["shift_mu"])
    tok = lambda off: pl.BlockSpec((n, tb, bw), lambda p, i: (0, i, off + p))
    vec = lambda rows_, off: pl.BlockSpec((rows_, bw), lambda p, i: (0, off + p))
    st = pl.BlockSpec((n, pp, HEAD, LANES), lambda p, i: (0, p, 0, 0))
    in_specs = [tok(0), tok(nblk), tok(2 * nblk), pl.BlockSpec((n, tb, lora), lambda p, i: (0, i, lblk)),
                vec(n, 0), vec(n, nblk), vec(n, 2 * nblk), pl.BlockSpec((n, lora), lambda p, i: (0, lblk)),
                vec(1, 0), vec(1, nblk), vec(1, 2 * nblk), pl.BlockSpec((1, lora), lambda p, i: (0, lblk)),
                vec(1, 0), vec(prm["decay_w2"].shape[0], 0), vec(1, 0), vec(prm["aaa_a2"].shape[0], 0),
                vec(prm["gate_g2"].shape[0], 0), vec(1, 0), vec(1, 0), vec(1, 0), vec(1, 0), vec(1, 0), st]
    args = [z_rwkv] * 4 + [shift0] * 4 + [mu] * 4 + [
        row(prm["decay_w0"]), prm["decay_w2"], row(prm["aaa_a0"]), prm["aaa_a2"], prm["gate_g2"],
        row(prm["k_k"]), row(prm["k_a"]), row(prm["r_k"]), row(prm["lnx_w"]), row(prm["lnx_b"]), s0]
    return pl.pallas_call(
        functools.partial(_wkv_fused_kernel, group_size=2),
        grid=(nblk, t // tb),
        in_specs=in_specs,
        out_specs=[tok(0), st],
        out_shape=[jax.ShapeDtypeStruct((n, t, width), F32), jax.ShapeDtypeStruct(s0.shape, F32)],
        scratch_shapes=[pltpu.VMEM((n, pp, HEAD, LANES), F32), pltpu.VMEM((n, bw), F32),
                        pltpu.VMEM((n, bw), F32), pltpu.VMEM((n, bw), F32),
                        pltpu.VMEM((n, lora), F32)],
        compiler_params=_cparams(("arbitrary", "arbitrary")),
        name="wkv_fused",
    )(*args)


def _pack_state(s):
    n, h, vd, kd = s.shape
    return s.reshape(n, h // 2, 2, vd, kd).transpose(0, 1, 4, 2, 3).reshape(n, h // 2, kd, 2 * vd)


def _unpack_state(s, h):
    n, _, kd, _ = s.shape
    return s.reshape(n, h // 2, kd, 2, HEAD).transpose(0, 1, 3, 4, 2).reshape(n, h, HEAD, kd)


def _attn_prompt_kernel(q_ref, k_ref, kp_ref, v_ref, vp_ref, o_ref, l_ref, *, dil, sub, n_back, scale):
    first = pl.program_id(1) == 0
    qi = lax.broadcasted_iota(jnp.int32, (ATTN_BLOCK, 2 * ATTN_BLOCK), 0)
    ki = lax.broadcasted_iota(jnp.int32, (ATTN_BLOCK, 2 * ATTN_BLOCK), 1)
    dist = qi + ATTN_BLOCK - ki
    band = (dist >= 0) & (dist <= n_back)
    own = ki >= ATTN_BLOCK
    left = lax.broadcasted_iota(jnp.int32, (ATTN_BLOCK, LANES), 1) < HEAD
    ones_kv = jnp.ones((2 * ATTN_BLOCK, LANES), BF16)

    def rows(j, r):
        if dil == 1:
            return pl.ds(j * ATTN_BLOCK, ATTN_BLOCK)
        return pl.ds(j * ATTN_BLOCK * dil + r, ATTN_BLOCK, stride=dil)

    def units(jr):
        probs = []
        qs, kcat, vcat, masks = [], [], [], []
        for j, r in jr:
            q = (q_ref[0, rows(j, r), :] * scale).astype(BF16)
            if j == 0:
                kp, vp = kp_ref[0, rows(0, r), :], vp_ref[0, rows(0, r), :]
                masks.append(band & (own | jnp.logical_not(first)))
            else:
                kp, vp = k_ref[0, rows(j - 1, r), :], v_ref[0, rows(j - 1, r), :]
                masks.append(band)
            kcat.append(jnp.concatenate([kp, k_ref[0, rows(j, r), :]], axis=0).astype(BF16))
            vcat.append(jnp.concatenate([vp, v_ref[0, rows(j, r), :]], axis=0).astype(BF16))
            zq = jnp.zeros_like(q)
            qs += [jnp.where(left, q, zq), jnp.where(left, zq, q)]
            probs += [(len(kcat) - 1, 0), (len(kcat) - 1, 1)]
        s = [jnp.where(masks[u], _dot_nt(qs[p], kcat[u]), NEG) for p, (u, _) in enumerate(probs)]
        m = [jnp.max(x, axis=-1, keepdims=True) for x in s]
        e = [jnp.exp(s[p] - m[p]).astype(BF16) for p in range(len(probs))]
        o = [_dot(e[p], vcat[u]) for p, (u, _) in enumerate(probs)]
        dens = [_dot(e[p], ones_kv) for p in range(len(probs))]
        for u, (j, r) in enumerate(jr):
            den = jnp.where(left, dens[2 * u], dens[2 * u + 1])
            o_ref[0, rows(j, r), :] = jnp.where(left, o[2 * u], o[2 * u + 1]) / den
            l_ref[0, rows(j, r), :] = jnp.where(left, m[2 * u], m[2 * u + 1]) + jnp.log(den)

    if dil <= 4:
        units([(j, r) for j in range(sub) for r in range(dil)])
    else:
        for j in range(sub):
            def body(r4, carry, j=j):
                units([(j, r4 * 4 + u) for u in range(4)])
                return carry
            lax.fori_loop(0, dil // 4, body, 0)


def _attn_prompt(zqkv, gi, span):
    n, t, cols = zqkv.shape
    d = DILATIONS[gi]
    aw = ATTN_HEADS * HEAD
    ng = len(DILATIONS)
    back = ATTN_BLOCK * d
    sub = span // back
    halves = aw // LANES
    col = lambda which, hp: (which * ng + gi) * halves + hp
    cur = lambda which: pl.BlockSpec((1, span, LANES), lambda b, i, hp: (b, i, col(which, hp)))
    prev = lambda which: pl.BlockSpec((1, back, LANES),
                                      lambda b, i, hp: (b, jnp.maximum(i * sub - 1, 0), col(which, hp)))
    out_spec = pl.BlockSpec((1, span, LANES), lambda b, i, hp: (b, i, hp))
    out_sds = jax.ShapeDtypeStruct((n, t, aw), F32)
    return pl.pallas_call(
        functools.partial(_attn_prompt_kernel, dil=d, sub=sub, n_back=WINDOWS[gi] // d, scale=HEAD ** -0.5),
        grid=(n, t // span, halves),
        in_specs=[cur(0), cur(1), prev(1), cur(2), prev(2)],
        out_specs=[out_spec, out_spec],
        out_shape=[out_sds, out_sds],
        compiler_params=_cparams(("arbitrary", "arbitrary", "arbitrary")),
        name=f"attn_prompt_g{gi}",
    )(zqkv, zqkv, zqkv, zqkv, zqkv)


def _attn_sample_kernel(q_ref, k_ref, v_ref, c_ref, o_ref, l_ref, *, dil, n_back, valid, scale):
    nb, tp = q_ref.shape[0], q_ref.shape[1]
    cl = c_ref.shape[4]
    dist_c = (cl + lax.broadcasted_iota(jnp.int32, (tp, cl), 0)
              - lax.broadcasted_iota(jnp.int32, (tp, cl), 1))
    mask_c = (dist_c % dil == 0) & (dist_c <= dil * n_back)
    pn = lax.broadcasted_iota(jnp.int32, (tp, tp), 1)
    dist_n = lax.broadcasted_iota(jnp.int32, (tp, tp), 0) - pn
    mask_n = (dist_n >= 0) & (dist_n % dil == 0) & (dist_n <= dil * n_back) & (pn < valid)
    probs = [(b, h) for b in range(nb) for h in range(ATTN_HEADS)]
    sl = lambda h: slice(h * HEAD, (h + 1) * HEAD)
    q = [(q_ref[b, :, sl(h)] * scale).astype(BF16) for b, h in probs]
    s_c = [jnp.where(mask_c, _dot(q[p], c_ref[b, 0, h].astype(BF16)), NEG) for p, (b, h) in enumerate(probs)]
    s_n = [jnp.where(mask_n, _dot_nt(q[p], k_ref[b, :, sl(h)].astype(BF16)), NEG)
           for p, (b, h) in enumerate(probs)]
    m = [jnp.maximum(jnp.max(s_c[p], axis=-1, keepdims=True), jnp.max(s_n[p], axis=-1, keepdims=True))
         for p in range(len(probs))]
    e_c = [jnp.exp(s_c[p] - m[p]) for p in range(len(probs))]
    e_n = [jnp.exp(s_n[p] - m[p]) for p in range(len(probs))]
    den = [jnp.sum(e_c[p], axis=-1, keepdims=True) + jnp.sum(e_n[p], axis=-1, keepdims=True)
           for p in range(len(probs))]
    o = [_dot_nt(e_c[p].astype(BF16), c_ref[b, 1, h].astype(BF16))
         + _dot(e_n[p].astype(BF16), v_ref[b, :, sl(h)].astype(BF16)) for p, (b, h) in enumerate(probs)]
    for b in range(nb):
        ps = [p for p, (bb, _) in enumerate(probs) if bb == b]
        o_ref[b] = jnp.concatenate([o[p] / den[p] for p in ps], axis=1)
        l_ref[b] = jnp.concatenate([jnp.broadcast_to(m[p] + jnp.log(den[p]), (tp, HEAD)) for p in ps], axis=1)


def _attn_sample(zqkv, cache, gi, valid):
    n, tp, cols = zqkv.shape
    aw = ATTN_HEADS * HEAD
    ng = len(DILATIONS)
    d = DILATIONS[gi]
    cache_t = cache.transpose(0, 2, 3, 4, 1)
    nb = 2 if n % 2 == 0 else 1
    spec = lambda which: pl.BlockSpec((nb, tp, aw), lambda b: (b, 0, which * ng + gi))
    out_spec = pl.BlockSpec((nb, tp, aw), lambda b: (b, 0, 0))
    out_sds = jax.ShapeDtypeStruct((n, tp, aw), F32)
    return pl.pallas_call(
        functools.partial(_attn_sample_kernel, dil=d, n_back=WINDOWS[gi] // d, valid=valid,
                          scale=HEAD ** -0.5),
        grid=(n // nb,),
        in_specs=[spec(0), spec(1), spec(2),
                  pl.BlockSpec((nb,) + cache_t.shape[1:], lambda b: (b, 0, 0, 0, 0))],
        out_specs=[out_spec, out_spec],
        out_shape=[out_sds, out_sds],
        compiler_params=_cparams(("arbitrary",)),
        name=f"attn_sample_g{gi}",
    )(zqkv, zqkv, zqkv, cache_t)


def _post_kernel(x_ref, rwkv_ref, o0, o1, o2, l0, l1, l2, zg_ref, wbr_ref, wba_ref, wout_ref, g_ref,
                 w1_ref, w2_ref, p_ref, gp_ref, wg_ref, wp_ref, gf_ref, o_ref, h_scr, acc_scr, *, final):
    j = pl.program_id(1)

    @pl.when(j == 0)
    def _():
        ls = [l0[...], l1[...], l2[...]]
        m = jnp.maximum(jnp.maximum(ls[0], ls[1]), ls[2])
        es = [jnp.exp(l - m) for l in ls]
        attn = (es[0] * o0[...] + es[1] * o1[...] + es[2] * o2[...]) / (es[0] + es[1] + es[2])
        d = x_ref.shape[1]
        gates = zg_ref[...].astype(F32)
        mixed = (gates[:, :d] * _dot(rwkv_ref[...].astype(BF16), wbr_ref[...])
                 + gates[:, d:] * _dot(attn.astype(BF16), wba_ref[...]))
        x = x_ref[...] + _dot(mixed.astype(BF16), wout_ref[...])
        acc_scr[...] = x
        h_scr[...] = _rms(x, g_ref[...]).astype(BF16)

    u = jnp.maximum(_dot(h_scr[...], w1_ref[...]), 0.0)
    acc_scr[...] += _dot((u * u).astype(BF16), w2_ref[...])

    @pl.when(j == pl.num_programs(1) - 1)
    def _():
        x = acc_scr[...]
        h = _rms(x, gp_ref[...]).astype(BF16)
        x = x + _sigmoid(_dot(h, wg_ref[...])) * _dot(p_ref[...].astype(BF16), wp_ref[...])
        if final:
            x = _rms(x, gf_ref[...])
        o_ref[...] = x


def _post(x, rwkv, os_, ls_, zg, wbr, wba, wout, g, w1, w2, p, gp, wg, wp, gf, final, tm, tf):
    rows, d = x.shape
    dff = w1.shape[1]
    vec = lambda a: a.reshape(1, d)
    rowblk = lambda a: pl.BlockSpec((tm, a.shape[1]), lambda i, j: (i, 0))
    full = lambda a: pl.BlockSpec(a.shape, lambda i, j: (0, 0), pipeline_mode=pl.Buffered(1))
    row_args = [x, rwkv, *os_, *ls_, zg]
    return pl.pallas_call(
        functools.partial(_post_kernel, final=final),
        grid=(rows // tm, dff // tf),
        in_specs=[rowblk(a) for a in row_args] + [full(wbr), full(wba), full(wout), full(vec(g)),
                  pl.BlockSpec((d, tf), lambda i, j: (0, j)),
                  pl.BlockSpec((tf, d), lambda i, j: (j, 0)),
                  rowblk(p), full(vec(gp)), full(wg), full(wp), full(vec(gf))],
        out_specs=pl.BlockSpec((tm, d), lambda i, j: (i, 0)),
        out_shape=jax.ShapeDtypeStruct((rows, d), F32),
        scratch_shapes=[pltpu.VMEM((tm, d), BF16), pltpu.VMEM((tm, d), F32)],
        compiler_params=_cparams(("arbitrary", "arbitrary")),
        name="post",
    )(*row_args, wbr, wba, wout, vec(g), w1, w2, p, vec(gp), wg, wp, vec(gf))


def _layer(x, p, wkv0, shift0, caches, prm, norm_final, final, valid):
    n, t, d = x.shape
    rows = n * t
    width = prm["decay_w0"].shape[-1]
    rwkv_cols = prm["shift_mu"].shape[-1]
    aw = ATTN_HEADS * HEAD
    qkv_cols = 3 * len(DILATIONS) * aw
    tm = min(512, rows)
    x2 = x.reshape(rows, d)

    pick = lambda cols, opts: next(o for o in opts if cols % o == 0)
    z_rwkv, z_qkv, z_gate = _in_proj(x2, prm["norm_mix"], prm["w_in"].astype(BF16),
                                     (rwkv_cols, qkv_cols, 2 * d), tm)
    z_rwkv = z_rwkv.reshape(n, t, rwkv_cols)
    z_qkv = z_qkv.reshape(n, t, qkv_cols)

    if valid == t and t % WKV_CHUNK == 0:
        rwkv, s_t = _wkv_fused(z_rwkv, shift0, _pack_state(wkv0), prm, pick(t, (512, 256, WKV_CHUNK)), 4)
        wkv_t = _unpack_state(s_t, width // HEAD)
    else:
        feats = _rwkv_prep(z_rwkv, shift0, prm, SUBLANES if n % SUBLANES == 0 else 1)
        rwkv, wkv_t = _wkv(*feats, prm["lnx_w"], prm["lnx_b"], wkv0, valid)
    shift_t = z_rwkv[:, valid - 1, :]

    os_, ls_, new_kv = [], [], []
    ng = len(DILATIONS)
    for gi in range(ng):
        if caches is None:
            o, l = _attn_prompt(z_qkv, gi, max(1024, ATTN_BLOCK * DILATIONS[gi]))
            keep = min(WINDOWS[gi], t)
            kv_rows = z_qkv[:, t - keep:t]
        else:
            o, l = _attn_sample(z_qkv, caches[gi], gi, valid)
            kv_rows = z_qkv[:, :valid]
        new_kv.append(jnp.stack([kv_rows[:, :, (which * ng + gi) * aw:(which * ng + gi + 1) * aw]
                                 .reshape(n, -1, ATTN_HEADS, HEAD) for which in (1, 2)], axis=2))
        os_.append(o.reshape(rows, aw))
        ls_.append(l.reshape(rows, aw))

    bf = lambda name: prm[name].astype(BF16)
    x2 = _post(x2, rwkv.reshape(rows, -1), os_, ls_, z_gate, bf("w_branch_rwkv"), bf("w_branch_attn"),
               bf("w_out"), prm["norm_mlp"], bf("w_ff1"), bf("w_ff2"), p.reshape(rows, -1),
               prm["norm_ple"], bf("w_ple_gate"), bf("w_ple"), norm_final, final, tm, 2048)
    return x2.reshape(n, t, d), new_kv, wkv_t, shift_t


_LAYER_PARAMS = ("norm_mix", "w_in", "shift_mu", "decay_w0", "decay_w2", "aaa_a0", "aaa_a2", "gate_g2",
                 "k_k", "k_a", "r_k", "lnx_w", "lnx_b", "w_branch_rwkv", "w_branch_attn", "w_out",
                 "norm_mlp", "w_ff1", "w_ff2", "norm_ple", "w_ple", "w_ple_gate")


def kernel(x_prompt, x_sample, cache_kv_g0, cache_kv_g1, cache_kv_g2, state_wkv, state_shift, p_prompt, p_sample, norm_mix, w_in, shift_mu, decay_w0, decay_w2, aaa_a0, aaa_a2, gate_g2, k_k, k_a, r_k, lnx_w, lnx_b, w_branch_rwkv, w_branch_attn, w_out, norm_mlp, w_ff1, w_ff2, norm_ple, w_ple, w_ple_gate, norm_final):
    stacked = dict(zip(_LAYER_PARAMS, (norm_mix, w_in, shift_mu, decay_w0, decay_w2, aaa_a0, aaa_a2,
                                       gate_g2, k_k, k_a, r_k, lnx_w, lnx_b, w_branch_rwkv,
                                       w_branch_attn, w_out, norm_mlp, w_ff1, w_ff2, norm_ple, w_ple,
                                       w_ple_gate)))
    depth = norm_mix.shape[0]
    nb_p, _, d = x_prompt.shape
    nb_s, t_s, _ = x_sample.shape
    heads = decay_w0.shape[-1] // HEAD
    rwkv_cols = shift_mu.shape[-1]
    t_pad = -(-t_s // SUBLANES) * SUBLANES
    pad_t = lambda u: jnp.pad(u, ((0, 0), (0, t_pad - t_s), (0, 0)))

    yp, ys = x_prompt, pad_t(x_sample)
    kvp, kvs = [[], [], []], [[], [], []]
    wkv_p, sh_p, wkv_s, sh_s = [], [], [], []
    for i in range(depth):
        prm = {name: val[i] for name, val in stacked.items()}
        prm["r_k"] = prm["r_k"].reshape(-1)
        last = i == depth - 1
        wkv0 = jnp.zeros((nb_p, heads, HEAD, HEAD), F32)
        shift0 = jnp.zeros((nb_p, rwkv_cols), F32)
        yp, nkv_p, w_p, s_p = _layer(yp, p_prompt[i], wkv0, shift0, None, prm, norm_final, last,
                                     x_prompt.shape[1])
        ys, nkv_s, w_s, s_s = _layer(ys, pad_t(p_sample[i]), state_wkv[i], state_shift[i],
                                     (cache_kv_g0[i], cache_kv_g1[i], cache_kv_g2[i]), prm, norm_final,
                                     last, t_s)
        for gi in range(3):
            kvp[gi].append(nkv_p[gi])
            kvs[gi].append(nkv_s[gi])
        wkv_p.append(w_p)
        sh_p.append(s_p)
        wkv_s.append(w_s)
        sh_s.append(s_s)
    return (yp, ys[:, :t_s],
            jnp.stack(kvp[0]), jnp.stack(kvp[1]), jnp.stack(kvp[2]),
            jnp.stack(wkv_p), jnp.stack(sh_p),
            jnp.stack(kvs[0]), jnp.stack(kvs[1]), jnp.stack(kvs[2]),
            jnp.stack(wkv_s), jnp.stack(sh_s))
```

```python
import functools

import jax
import jax.numpy as jnp
from jax import lax
from jax.experimental import pallas as pl
from jax.experimental.pallas import tpu as pltpu

F32 = jnp.float32
BF16 = jnp.bfloat16

HEAD = 64
LANES = 128
SUBLANES = 8
ATTN_HEADS = 4
ATTN_BLOCK = 128
WINDOWS = (128, 512, 2048)
DILATIONS = (1, 4, 16)
NORM_EPS = 1e-6
LNX_EPS = 64e-5
NEG = -1e30
VMEM_LIMIT = 56 * 1024 * 1024


def _cparams(sem):
    return pltpu.CompilerParams(dimension_semantics=sem, vmem_limit_bytes=VMEM_LIMIT)


def _dot(a, b):
    return jnp.dot(a, b, preferred_element_type=F32)


def _dot_bf16(a, b):
    return _dot(a.astype(BF16), b.astype(BF16))


def _pair_ones():
    r = lax.broadcasted_iota(jnp.int32, (LANES, LANES), 0) // HEAD
    c = lax.broadcasted_iota(jnp.int32, (LANES, LANES), 1) // HEAD
    return (r == c).astype(BF16)


def _head_sum(x, ones):
    outs = []
    for c in range(x.shape[1] // LANES):
        outs.append(_dot(x[:, c * LANES:(c + 1) * LANES].astype(BF16), ones))
    return jnp.concatenate(outs, axis=1)


def _rms(x, g):
    return x * lax.rsqrt(jnp.mean(x * x, axis=-1, keepdims=True) + NORM_EPS) * g


def _sigmoid(x):
    return 1.0 / (1.0 + jnp.exp(-x))


DECAY_SCALE = 0.6065306597126334


def _in_proj_kernel(x_ref, g_ref, w_ref, zr_ref, zq_ref, gate_ref):
    h = _rms(x_ref[...], g_ref[...]).astype(BF16)
    c1 = zr_ref.shape[1]
    c2 = c1 + zq_ref.shape[1]
    zr_ref[...] = _dot(h, w_ref[:, :c1])
    zq_ref[...] = _dot(h, w_ref[:, c1:c2])
    gate_ref[...] = _sigmoid(_dot(h, w_ref[:, c2:])).astype(BF16)


def _in_proj(x, g, w, splits, tm):
    rows, d = x.shape
    assert sum(splits) == w.shape[1] and len(splits) == 3
    return pl.pallas_call(
        _in_proj_kernel,
        grid=(rows // tm,),
        in_specs=[pl.BlockSpec((tm, d), lambda i: (i, 0)),
                  pl.BlockSpec((1, d), lambda i: (0, 0)),
                  pl.BlockSpec(w.shape, lambda i: (0, 0), pipeline_mode=pl.Buffered(1))],
        out_specs=[pl.BlockSpec((tm, c), lambda i: (i, 0)) for c in splits],
        out_shape=[jax.ShapeDtypeStruct((rows, c), dt) for c, dt in zip(splits, (F32, F32, BF16))],
        compiler_params=_cparams(("arbitrary",)),
        name="in_proj",
    )(x, g.reshape(1, d), w)


def _drain(steps):
    try:
        while True:
            next(steps)
    except StopIteration as stop:
        return stop.value


def _zip_stages(*gens):
    vals = [None] * len(gens)
    live = list(range(len(gens)))
    while live:
        for i in list(live):
            try:
                next(gens[i])
            except StopIteration as stop:
                vals[i] = stop.value
                live.remove(i)
    return vals


def _rwkv_features_steps(r, k, v, xw, xa, xg, w0, w2, a0, a2, g2, k_k, k_a, r_k, ones):
    pre = w0 + _dot_bf16(jnp.tanh(xw), w2)
    yield
    lw = _sigmoid(pre) * (-DECAY_SCALE)
    yield
    a = _sigmoid(a0 + _dot_bf16(xa, a2))
    yield
    g = _dot_bf16(_sigmoid(xg), g2)
    yield
    kk = k * k_k
    norm2 = _head_sum(kk * kk, ones)
    yield
    kk = kk * lax.rsqrt(jnp.maximum(norm2, 1e-24))
    k2 = k * (1.0 + (a - 1.0) * k_a)
    yield
    bonus = _head_sum(r * k2 * r_k, ones) * v
    yield
    return lw, k2, -kk, kk * a, g, bonus


def _rwkv_features(*args):
    return _drain(_rwkv_features_steps(*args))


def _rwkv_out(y, bonus, g, lnx_w, lnx_b, ones):
    mu = _head_sum(y, ones) * (1.0 / HEAD)
    yc = y - mu
    var = _head_sum(yc * yc, ones) * (1.0 / HEAD)
    return (yc * lax.rsqrt(var + LNX_EPS) * lnx_w + lnx_b + bonus) * g


def _rwkv_prep_kernel(z_ref, sh_ref, mu_ref, w0_ref, w2_ref, a0_ref, a2_ref, g2_ref, kk_ref, ka_ref,
                      rk_ref, r_o, w_o, k_o, v_o, a_o, b_o, g_o, bon_o, *, width):
    nb, t, cols = z_ref.shape
    first = lax.broadcasted_iota(jnp.int32, (t, cols), 0) == 0
    zs = []
    for b in range(nb):
        z = z_ref[b]
        prev = jnp.where(first, sh_ref[b:b + 1, :], pltpu.roll(z, 1, axis=0))
        zs.append(z + (prev - z) * mu_ref[...])
    zs = jnp.concatenate(zs, axis=0)

    c0 = 3 * width
    c1 = c0 + w2_ref.shape[0]
    c2 = c1 + a2_ref.shape[0]
    r, k, v = zs[:, :width], zs[:, width:2 * width], zs[:, 2 * width:c0]
    lw, k2, a_neg, b, g, bonus = _rwkv_features(
        r, k, v, zs[:, c0:c1], zs[:, c1:c2], zs[:, c2:], w0_ref[...], w2_ref[...], a0_ref[...],
        a2_ref[...], g2_ref[...], kk_ref[...], ka_ref[...], rk_ref[...], _pair_ones())
    for ref, val in ((r_o, r), (w_o, lw), (k_o, k2), (v_o, v), (a_o, a_neg), (b_o, b), (g_o, g), (bon_o, bonus)):
        for bi in range(nb):
            ref[bi] = val[bi * t:(bi + 1) * t, :]


def _rwkv_prep(z_rwkv, shift0, prm, nb):
    n, t, cols = z_rwkv.shape
    width = prm["decay_w0"].shape[-1]
    assert n % nb == 0
    row = lambda x: x.reshape(1, -1)
    full = lambda a: pl.BlockSpec(a.shape, lambda i: (0,) * a.ndim)
    args = [row(prm["shift_mu"]), row(prm["decay_w0"]), prm["decay_w2"], row(prm["aaa_a0"]),
            prm["aaa_a2"], prm["gate_g2"], row(prm["k_k"]), row(prm["k_a"]), row(prm["r_k"])]
    out_spec = pl.BlockSpec((nb, t, width), lambda i: (i, 0, 0))
    out_sds = jax.ShapeDtypeStruct((n, t, width), F32)
    return pl.pallas_call(
        functools.partial(_rwkv_prep_kernel, width=width),
        grid=(n // nb,),
        in_specs=[pl.BlockSpec((nb, t, cols), lambda i: (i, 0, 0)), pl.BlockSpec((nb, cols), lambda i: (i, 0))]
                 + [full(a) for a in args],
        out_specs=[out_spec] * 8,
        out_shape=[out_sds] * 8,
        compiler_params=_cparams(("arbitrary",)),
        name="rwkv_prep",
    )(z_rwkv, shift0, *args)


def _wkv_kernel(r_ref, w_ref, k_ref, v_ref, a_ref, b_ref, g_ref, bon_ref, lnw_ref, lnb_ref, s0_ref,
                y_ref, st_ref, *, valid):
    ones = _pair_ones()
    row = lax.broadcasted_iota(jnp.int32, (HEAD, LANES), 0)
    lane = lax.broadcasted_iota(jnp.int32, (HEAD, LANES), 1)
    left = lane < HEAD
    diag = (row == lane % HEAD).astype(F32)

    def key_sum(x):
        hi = x.astype(BF16)
        lo = (x - hi.astype(F32)).astype(BF16)
        return _dot(hi, ones) + _dot(lo, ones)

    chains = [(b, p) for b in range(y_ref.shape[0]) for p in range(s0_ref.shape[1] // 2)]
    cols = lambda p: slice(p * LANES, (p + 1) * LANES)
    vt = [v_ref[b, :, cols(p)].T for b, p in chains]
    s = [jnp.concatenate([s0_ref[b, 2 * p], s0_ref[b, 2 * p + 1]], axis=1) for b, p in chains]
    ys = [[] for _ in chains]
    for j in range(y_ref.shape[1]):
        for ch, (b, p) in enumerate(chains):
            if j < valid:
                row_of = lambda ref: ref[b, j:j + 1, cols(p)]
                v_col = jnp.where(left, vt[ch][0:HEAD, j:j + 1], vt[ch][HEAD:2 * HEAD, j:j + 1])
                sa = key_sum(s[ch] * row_of(a_ref))
                s[ch] = s[ch] * jnp.exp(row_of(w_ref)) + sa * row_of(b_ref) + v_col * row_of(k_ref)
                y_col = key_sum(s[ch] * row_of(r_ref))
                ys[ch].append(jnp.sum(y_col * diag, axis=0, keepdims=True))
            else:
                ys[ch].append(jnp.zeros((1, LANES), F32))
    for ch, (b, p) in enumerate(chains):
        y_ref[b, :, cols(p)] = _rwkv_out(jnp.concatenate(ys[ch], axis=0), bon_ref[b, :, cols(p)],
                                         g_ref[b, :, cols(p)], lnw_ref[:, cols(p)], lnb_ref[:, cols(p)], ones)
        st_ref[b, 2 * p] = s[ch][:, :HEAD]
        st_ref[b, 2 * p + 1] = s[ch][:, HEAD:]


def _wkv(r, w, k, v, a, b, g, bonus, lnx_w, lnx_b, s0, valid):
    n, t, c = r.shape
    assert t == SUBLANES
    nb = 2 if n % 2 == 0 else 1
    tok = pl.BlockSpec((nb, t, c), lambda bi: (bi, 0, 0))
    vec = pl.BlockSpec((1, c), lambda bi: (0, 0))
    st = pl.BlockSpec((nb,) + s0.shape[1:], lambda bi: (bi, 0, 0, 0))
    return pl.pallas_call(
        functools.partial(_wkv_kernel, valid=valid),
        grid=(n // nb,),
        in_specs=[tok] * 8 + [vec, vec, st],
        out_specs=[tok, st],
        out_shape=[jax.ShapeDtypeStruct((n, t, c), F32), jax.ShapeDtypeStruct(s0.shape, F32)],
        compiler_params=_cparams(("arbitrary",)),
        name="wkv_scan",
    )(r, w, k, v, a, b, g, bonus, lnx_w.reshape(1, c), lnx_b.reshape(1, c), s0)


WKV_CHUNK = 64


def _bd(x):
    xb = x.astype(BF16)
    left = lax.broadcasted_iota(jnp.int32, x.shape, 1) < HEAD
    zero = jnp.zeros_like(xb)
    return jnp.concatenate([jnp.where(left, xb, zero), jnp.where(left, zero, xb)], axis=0)


def _dot_nt(a, b):
    return lax.dot_general(a, b, (((1,), (1,)), ((), ())), preferred_element_type=F32)


def _dot_tn(a, b):
    return lax.dot_general(a, b, (((0,), (0,)), ((), ())), preferred_element_type=F32)


def _wkv_chunk_local_steps(r, lw, k, v, a, b, consts):
    tri, eye2, strict, incl, blocks = consts
    c = WKV_CHUNK
    n = range(len(r))
    zero = jnp.zeros((c, LANES), F32)

    def cumsum(x):
        hi = x.astype(BF16)
        lo = (x - hi.astype(F32)).astype(BF16)
        return _dot(tri, hi) + _dot(tri, lo)

    lc = [cumsum(lw[i]) for i in n]
    yield
    w_inc = [jnp.exp(lc[i]) for i in n]
    w_inv = [1.0 / w_inc[i] for i in n]
    ah = [a[i] * jnp.exp(lc[i] - lw[i]) for i in n]
    rh = [r[i] * w_inc[i] for i in n]
    yield
    sc = [_dot_nt(jnp.concatenate([ah[i], rh[i]], axis=0).astype(BF16),
                  jnp.concatenate([_bd(b[i] * w_inv[i]), _bd(k[i] * w_inv[i])], axis=0)) for i in n]
    yield
    aab = [jnp.where(strict, sc[i][:c, :LANES], zero) for i in n]

    t = [eye2 + jnp.where(blocks[0], aab[i], zero) for i in n]
    for lvl in range(1, len(blocks)):
        grow = blocks[lvl] & ~blocks[lvl - 1]
        x = [_dot(t[i].astype(BF16), _bd(jnp.where(grow, aab[i], zero))) for i in n]
        yield
        t = [t[i] + _dot(x[i].astype(BF16), _bd(t[i])) for i in n]
        yield

    vb = [_bd(v[i]) for i in n]
    akv = [_dot(jnp.where(strict, sc[i][:c, LANES:], zero).astype(BF16), vb[i]) for i in n]
    yield
    pu = [_dot(t[i].astype(BF16), jnp.concatenate([_bd(ah[i]), _bd(akv[i])], axis=1)) for i in n]
    yield

    left2 = (lax.broadcasted_iota(jnp.int32, (HEAD, 2 * LANES), 1) % LANES) < HEAD
    g, h, q, y0 = [], [], [], []
    for i in n:
        wc = w_inc[i][c - 1:c, :]
        w_rem = wc * w_inv[i]
        lhs_t = jnp.concatenate([b[i] * w_rem, k[i] * w_rem], axis=0).astype(BF16)
        rhs = jnp.concatenate([pu[i], jnp.concatenate([zero, v[i]], axis=1)], axis=0).astype(BF16)
        full = _dot_tn(lhs_t, rhs)
        gh = jnp.where(left2, full[:HEAD], full[HEAD:])
        g.append(gh[:, :LANES] + eye2 * wc)
        h.append(gh[:, LANES:])
    yield
    zero_b = jnp.zeros((2 * c, LANES), BF16)
    for i in n:
        mrb = jnp.where(incl, sc[i][c:, :LANES], zero)
        mrk = jnp.where(incl, sc[i][c:, LANES:], zero)
        rhs2 = jnp.concatenate([jnp.concatenate([_bd(pu[i][:, :LANES]), _bd(pu[i][:, LANES:])], axis=1),
                                jnp.concatenate([zero_b, vb[i]], axis=1)], axis=0)
        qy = _dot(jnp.concatenate([mrb, mrk], axis=1).astype(BF16), rhs2)
        q.append(rh[i] + qy[:, :LANES])
        y0.append(qy[:, LANES:])
    yield
    return g, h, q, y0


def _wkv_fused_kernel(zr_ref, zk_ref, zv_ref, zl_ref, sr_ref, sk_ref, sv_ref, sl_ref,
                      mr_ref, mk_ref, mv_ref, ml_ref, w0_ref, w2_ref, a0_ref, a2_ref, g2_ref,
                      kk_ref, ka_ref, rk_ref, lnw_ref, lnb_ref, s0_ref, out_ref, st_ref,
                      s_scr, pr_scr, pk_scr, pv_scr, pl_scr, *, group_size):
    i = pl.program_id(1)
    c = WKV_CHUNK
    n_seq, tb = out_ref.shape[0], out_ref.shape[1]
    n_chunks = tb // c

    @pl.when(i == 0)
    def _():
        s_scr[...] = s0_ref[...]
        pr_scr[...] = sr_ref[...]
        pk_scr[...] = sk_ref[...]
        pv_scr[...] = sv_ref[...]
        pl_scr[...] = sl_ref[...]

    def shifted(z_ref, p_scr, mu_ref, b, lanes):
        z = z_ref[b, :, lanes]
        first = lax.broadcasted_iota(jnp.int32, z.shape, 0) == 0
        prev = jnp.where(first, p_scr[b:b + 1, lanes], pltpu.roll(z, 1, axis=0))
        p_scr[b:b + 1, lanes] = z[tb - 1:tb, :]
        return z + (prev - z) * mu_ref[:, lanes]

    ones = _pair_ones()
    c1 = w2_ref.shape[0]
    c2 = c1 + a2_ref.shape[0]
    ti = lax.broadcasted_iota(jnp.int32, (c, LANES), 0)
    si = lax.broadcasted_iota(jnp.int32, (c, LANES), 1) % HEAD
    tri = (lax.broadcasted_iota(jnp.int32, (c, c), 0) >= lax.broadcasted_iota(jnp.int32, (c, c), 1)
           ).astype(BF16)
    eye2 = (ti == si).astype(F32)
    sizes = [2 << l for l in range(c.bit_length() - 1)]
    blocks = [(ti // s) == (si // s) for s in sizes]
    consts = (tri, eye2, ti > si, ti >= si, blocks)
    lora_in = {}

    def feature_steps(group):
        out = []
        for b, sp in group:
            lanes = slice(sp * LANES, (sp + 1) * LANES)
            r = shifted(zr_ref, pr_scr, mr_ref, b, lanes)
            k = shifted(zk_ref, pk_scr, mk_ref, b, lanes)
            yield
            v = shifted(zv_ref, pv_scr, mv_ref, b, lanes)
            if b not in lora_in:
                lora_in[b] = shifted(zl_ref, pl_scr, ml_ref, b, slice(None))
            zl = lora_in[b]
            yield
            lw, k2, a_neg, bb, gate, bonus = yield from _rwkv_features_steps(
                r, k, v, zl[:, :c1], zl[:, c1:c2], zl[:, c2:], w0_ref[:, lanes], w2_ref[:, lanes],
                a0_ref[:, lanes], a2_ref[:, lanes], g2_ref[:, lanes], kk_ref[:, lanes], ka_ref[:, lanes],
                rk_ref[:, lanes], ones)
            out.append((r, lw, k2, v, a_neg, bb, gate, bonus))
        return out

    def local_steps(feat):
        chunks = lambda which: [f[which][ci * c:(ci + 1) * c, :] for ci in range(n_chunks) for f in feat]
        return (yield from _wkv_chunk_local_steps(*(chunks(which) for which in range(6)), consts))

    def tail_steps(group, feat, local):
        g, h, q, y0 = local
        st = [s_scr[b, sp] for b, sp in group]
        ys = [[] for _ in group]
        for ci in range(n_chunks):
            for j in range(len(group)):
                u = ci * len(group) + j
                out = _dot(jnp.concatenate([q[u], g[u]], axis=0).astype(BF16), _bd(st[j]))
                ys[j].append(out[:c] + y0[u])
                st[j] = out[c:] + h[u]
            yield
        for j, (b, sp) in enumerate(group):
            lanes = slice(sp * LANES, (sp + 1) * LANES)
            s_scr[b, sp] = st[j]
            out_ref[b, :, lanes] = _rwkv_out(jnp.concatenate(ys[j], axis=0), feat[j][7], feat[j][6],
                                             lnw_ref[:, lanes], lnb_ref[:, lanes], ones)
            yield

        @pl.when(i == pl.num_programs(1) - 1)
        def _():
            for j, (b, sp) in enumerate(group):
                st_ref[b, sp] = st[j]

    chains = [(b, sp) for b in range(n_seq) for sp in range(out_ref.shape[2] // LANES)]
    groups = [chains[j:j + group_size] for j in range(0, len(chains), group_size)]
    feats, locals_ = {}, {}
    for step in range(len(groups) + 2):
        gens, roles = [], []
        if 0 <= step - 1 < len(groups):
            gens.append(local_steps(feats[step - 1]))
            roles.append((locals_, step - 1))
        if step < len(groups):
            gens.append(feature_steps(groups[step]))
            roles.append((feats, step))
        if 0 <= step - 2 < len(groups):
            gens.append(tail_steps(groups[step - 2], feats[step - 2], locals_[step - 2]))
            roles.append(({}, step - 2))
        for (store, idx), val in zip(roles, _zip_stages(*gens)):
            store[idx] = val


def _wkv_fused(z_rwkv, shift0, s0, prm, tb, pp):
    n, t, cols = z_rwkv.shape
    width = prm["decay_w0"].shape[-1]
    bw = pp * LANES
    nblk = width // bw
    lora = cols - 3 * width
    assert width % bw == 0 and (3 * width) % lora == 0 and lora % LANES == 0
    lblk = 3 * width // lora
    row = lambda x: x.reshape(1, -1)
    mu = row(prm["shift_mu"])
    tok = lambda off: pl.BlockSpec((n, tb, bw), lambda p, i: (0, i, off + p))
    vec = lambda rows_, off: pl.BlockSpec((rows_, bw), lambda p, i: (0, off + p))
    st = pl.BlockSpec((n, pp, HEAD, LANES), lambda p, i: (0, p, 0, 0))
    in_specs = [tok(0), tok(nblk), tok(2 * nblk), pl.BlockSpec((n, tb, lora), lambda p, i: (0, i, lblk)),
                vec(n, 0), vec(n, nblk), vec(n, 2 * nblk), pl.BlockSpec((n, lora), lambda p, i: (0, lblk)),
                vec(1, 0), vec(1, nblk), vec(1, 2 * nblk), pl.BlockSpec((1, lora), lambda p, i: (0, lblk)),
                vec(1, 0), vec(prm["decay_w2"].shape[0], 0), vec(1, 0), vec(prm["aaa_a2"].shape[0], 0),
                vec(prm["gate_g2"].shape[0], 0), vec(1, 0), vec(1, 0), vec(1, 0), vec(1, 0), vec(1, 0), st]
    args = [z_rwkv] * 4 + [shift0] * 4 + [mu] * 4 + [
        row(prm["decay_w0"]), prm["decay_w2"], row(prm["aaa_a0"]), prm["aaa_a2"], prm["gate_g2"],
        row(prm["k_k"]), row(prm["k_a"]), row(prm["r_k"]), row(prm["lnx_w"]), row(prm["lnx_b"]), s0]
    return pl.pallas_call(
        functools.partial(_wkv_fused_kernel, group_size=2),
        grid=(nblk, t // tb),
        in_specs=in_specs,
        out_specs=[tok(0), st],
        out_shape=[jax.ShapeDtypeStruct((n, t, width), F32), jax.ShapeDtypeStruct(s0.shape, F32)],
        scratch_shapes=[pltpu.VMEM((n, pp, HEAD, LANES), F32), pltpu.VMEM((n, bw), F32),
                        pltpu.VMEM((n, bw), F32), pltpu.VMEM((n, bw), F32),
                        pltpu.VMEM((n, lora), F32)],
        compiler_params=_cparams(("arbitrary", "arbitrary")),
        name="wkv_fused",
    )(*args)


def _pack_state(s):
    n, h, vd, kd = s.shape
    return s.reshape(n, h // 2, 2, vd, kd).transpose(0, 1, 4, 2, 3).reshape(n, h // 2, kd, 2 * vd)


def _unpack_state(s, h):
    n, _, kd, _ = s.shape
    return s.reshape(n, h // 2, kd, 2, HEAD).transpose(0, 1, 3, 4, 2).reshape(n, h, HEAD, kd)


def _attn_prompt_kernel(q_ref, k_ref, kp_ref, v_ref, vp_ref, o_ref, l_ref, *, dil, sub, n_back, scale):
    first = pl.program_id(1) == 0
    qi = lax.broadcasted_iota(jnp.int32, (ATTN_BLOCK, 2 * ATTN_BLOCK), 0)
    ki = lax.broadcasted_iota(jnp.int32, (ATTN_BLOCK, 2 * ATTN_BLOCK), 1)
    dist = qi + ATTN_BLOCK - ki
    band = (dist >= 0) & (dist <= n_back)
    own = ki >= ATTN_BLOCK
    left = lax.broadcasted_iota(jnp.int32, (ATTN_BLOCK, LANES), 1) < HEAD
    ones_kv = jnp.ones((2 * ATTN_BLOCK, LANES), BF16)

    def rows(j, r):
        if dil == 1:
            return pl.ds(j * ATTN_BLOCK, ATTN_BLOCK)
        return pl.ds(j * ATTN_BLOCK * dil + r, ATTN_BLOCK, stride=dil)

    def units(jr):
        probs = []
        qs, kcat, vcat, masks = [], [], [], []
        for j, r in jr:
            q = (q_ref[0, rows(j, r), :] * scale).astype(BF16)
            if j == 0:
                kp, vp = kp_ref[0, rows(0, r), :], vp_ref[0, rows(0, r), :]
                masks.append(band & (own | jnp.logical_not(first)))
            else:
                kp, vp = k_ref[0, rows(j - 1, r), :], v_ref[0, rows(j - 1, r), :]
                masks.append(band)
            kcat.append(jnp.concatenate([kp, k_ref[0, rows(j, r), :]], axis=0).astype(BF16))
            vcat.append(jnp.concatenate([vp, v_ref[0, rows(j, r), :]], axis=0).astype(BF16))
            zq = jnp.zeros_like(q)
            qs += [jnp.where(left, q, zq), jnp.where(left, zq, q)]
            probs += [(len(kcat) - 1, 0), (len(kcat) - 1, 1)]
        s = [jnp.where(masks[u], _dot_nt(qs[p], kcat[u]), NEG) for p, (u, _) in enumerate(probs)]
        m = [jnp.max(x, axis=-1, keepdims=True) for x in s]
        e = [jnp.exp(s[p] - m[p]).astype(BF16) for p in range(len(probs))]
        o = [_dot(e[p], vcat[u]) for p, (u, _) in enumerate(probs)]
        dens = [_dot(e[p], ones_kv) for p in range(len(probs))]
        for u, (j, r) in enumerate(jr):
            den = jnp.where(left, dens[2 * u], dens[2 * u + 1])
            o_ref[0, rows(j, r), :] = jnp.where(left, o[2 * u], o[2 * u + 1]) / den
            l_ref[0, rows(j, r), :] = jnp.where(left, m[2 * u], m[2 * u + 1]) + jnp.log(den)

    if dil <= 4:
        units([(j, r) for j in range(sub) for r in range(dil)])
    else:
        for j in range(sub):
            def body(r4, carry, j=j):
                units([(j, r4 * 4 + u) for u in range(4)])
                return carry
            lax.fori_loop(0, dil // 4, body, 0)


def _attn_prompt(zqkv, gi, span):
    n, t, cols = zqkv.shape
    d = DILATIONS[gi]
    aw = ATTN_HEADS * HEAD
    ng = len(DILATIONS)
    back = ATTN_BLOCK * d
    sub = span // back
    halves = aw // LANES
    col = lambda which, hp: (which * ng + gi) * halves + hp
    cur = lambda which: pl.BlockSpec((1, span, LANES), lambda b, i, hp: (b, i, col(which, hp)))
    prev = lambda which: pl.BlockSpec((1, back, LANES),
                                      lambda b, i, hp: (b, jnp.maximum(i * sub - 1, 0), col(which, hp)))
    out_spec = pl.BlockSpec((1, span, LANES), lambda b, i, hp: (b, i, hp))
    out_sds = jax.ShapeDtypeStruct((n, t, aw), F32)
    return pl.pallas_call(
        functools.partial(_attn_prompt_kernel, dil=d, sub=sub, n_back=WINDOWS[gi] // d, scale=HEAD ** -0.5),
        grid=(n, t // span, halves),
        in_specs=[cur(0), cur(1), prev(1), cur(2), prev(2)],
        out_specs=[out_spec, out_spec],
        out_shape=[out_sds, out_sds],
        compiler_params=_cparams(("arbitrary", "arbitrary", "arbitrary")),
        name=f"attn_prompt_g{gi}",
    )(zqkv, zqkv, zqkv, zqkv, zqkv)


def _attn_sample_kernel(q_ref, k_ref, v_ref, c_ref, o_ref, l_ref, *, dil, n_back, valid, scale):
    nb, tp = q_ref.shape[0], q_ref.shape[1]
    cl = c_ref.shape[4]
    dist_c = (cl + lax.broadcasted_iota(jnp.int32, (tp, cl), 0)
              - lax.broadcasted_iota(jnp.int32, (tp, cl), 1))
    mask_c = (dist_c % dil == 0) & (dist_c <= dil * n_back)
    pn = lax.broadcasted_iota(jnp.int32, (tp, tp), 1)
    dist_n = lax.broadcasted_iota(jnp.int32, (tp, tp), 0) - pn
    mask_n = (dist_n >= 0) & (dist_n % dil == 0) & (dist_n <= dil * n_back) & (pn < valid)
    probs = [(b, h) for b in range(nb) for h in range(ATTN_HEADS)]
    sl = lambda h: slice(h * HEAD, (h + 1) * HEAD)
    q = [(q_ref[b, :, sl(h)] * scale).astype(BF16) for b, h in probs]
    s_c = [jnp.where(mask_c, _dot(q[p], c_ref[b, 0, h].astype(BF16)), NEG) for p, (b, h) in enumerate(probs)]
    s_n = [jnp.where(mask_n, _dot_nt(q[p], k_ref[b, :, sl(h)].astype(BF16)), NEG)
           for p, (b, h) in enumerate(probs)]
    m = [jnp.maximum(jnp.max(s_c[p], axis=-1, keepdims=True), jnp.max(s_n[p], axis=-1, keepdims=True))
         for p in range(len(probs))]
    e_c = [jnp.exp(s_c[p] - m[p]) for p in range(len(probs))]
    e_n = [jnp.exp(s_n[p] - m[p]) for p in range(len(probs))]
    den = [jnp.sum(e_c[p], axis=-1, keepdims=True) + jnp.sum(e_n[p], axis=-1, keepdims=True)
           for p in range(len(probs))]
    o = [_dot_nt(e_c[p].astype(BF16), c_ref[b, 1, h].astype(BF16))
         + _dot(e_n[p].astype(BF16), v_ref[b, :, sl(h)].astype(BF16)) for p, (b, h) in enumerate(probs)]
    for b in range(nb):
        ps = [p for p, (bb, _) in enumerate(probs) if bb == b]
        o_ref[b] = jnp.concatenate([o[p] / den[p] for p in ps], axis=1)
        l_ref[b] = jnp.concatenate([jnp.broadcast_to(m[p] + jnp.log(den[p]), (tp, HEAD)) for p in ps], axis=1)


def _attn_sample(zqkv, cache, gi, valid):
    n, tp, cols = zqkv.shape
    aw = ATTN_HEADS * HEAD
    ng = len(DILATIONS)
    d = DILATIONS[gi]
    cache_t = cache.transpose(0, 2, 3, 4, 1)
    nb = 2 if n % 2 == 0 else 1
    spec = lambda which: pl.BlockSpec((nb, tp, aw), lambda b: (b, 0, which * ng + gi))
    out_spec = pl.BlockSpec((nb, tp, aw), lambda b: (b, 0, 0))
    out_sds = jax.ShapeDtypeStruct((n, tp, aw), F32)
    return pl.pallas_call(
        functools.partial(_attn_sample_kernel, dil=d, n_back=WINDOWS[gi] // d, valid=valid,
                          scale=HEAD ** -0.5),
        grid=(n // nb,),
        in_specs=[spec(0), spec(1), spec(2),
                  pl.BlockSpec((nb,) + cache_t.shape[1:], lambda b: (b, 0, 0, 0, 0))],
        out_specs=[out_spec, out_spec],
        out_shape=[out_sds, out_sds],
        compiler_params=_cparams(("arbitrary",)),
        name=f"attn_sample_g{gi}",
    )(zqkv, zqkv, zqkv, cache_t)


def _post_kernel(x_ref, rwkv_ref, o0, o1, o2, l0, l1, l2, zg_ref, wbr_ref, wba_ref, wout_ref, g_ref,
                 w1_ref, w2_ref, p_ref, gp_ref, wg_ref, wp_ref, gf_ref, o_ref, h_scr, acc_scr, *, final, steps):
    j = pl.program_id(1)

    def merge():
        ls = [l0[...], l1[...], l2[...]]
        m = jnp.maximum(jnp.maximum(ls[0], ls[1]), ls[2])
        es = [jnp.exp(l - m) for l in ls]
        attn = (es[0] * o0[...] + es[1] * o1[...] + es[2] * o2[...]) / (es[0] + es[1] + es[2])
        d = x_ref.shape[1]
        gates = zg_ref[...].astype(F32)
        mixed = (gates[:, :d] * _dot(rwkv_ref[...].astype(BF16), wbr_ref[...])
                 + gates[:, d:] * _dot(attn.astype(BF16), wba_ref[...]))
        return x_ref[...] + _dot(mixed.astype(BF16), wout_ref[...])

    def mlp_slab(h):
        u = jnp.maximum(_dot(h, w1_ref[...]), 0.0)
        return _dot((u * u).astype(BF16), w2_ref[...])

    def embed(x):
        h = _rms(x, gp_ref[...]).astype(BF16)
        x = x + _sigmoid(_dot(h, wg_ref[...])) * _dot(p_ref[...].astype(BF16), wp_ref[...])
        return _rms(x, gf_ref[...]) if final else x

    if steps == 1:
        x = merge()
        o_ref[...] = embed(x + mlp_slab(_rms(x, g_ref[...]).astype(BF16)))
        return

    @pl.when(j == 0)
    def _():
        x = merge()
        acc_scr[...] = x
        h_scr[...] = _rms(x, g_ref[...]).astype(BF16)

    acc_scr[...] += mlp_slab(h_scr[...])

    @pl.when(j == steps - 1)
    def _():
        o_ref[...] = embed(acc_scr[...])


def _post(x, rwkv, os_, ls_, zg, wbr, wba, wout, g, w1, w2, p, gp, wg, wp, gf, final, tm, tf):
    rows, d = x.shape
    dff = w1.shape[1]
    steps = dff // tf
    vec = lambda a: a.reshape(1, d)
    rowblk = lambda a: pl.BlockSpec((tm, a.shape[1]), lambda i, j: (i, 0))
    full = lambda a: pl.BlockSpec(a.shape, lambda i, j: (0, 0), pipeline_mode=pl.Buffered(1))
    slab1 = full(w1) if steps == 1 else pl.BlockSpec((d, tf), lambda i, j: (0, j))
    slab2 = full(w2) if steps == 1 else pl.BlockSpec((tf, d), lambda i, j: (j, 0))
    row_args = [x, rwkv, *os_, *ls_, zg]
    return pl.pallas_call(
        functools.partial(_post_kernel, final=final, steps=steps),
        grid=(rows // tm, steps),
        in_specs=[rowblk(a) for a in row_args] + [full(wbr), full(wba), full(wout), full(vec(g)), slab1, slab2,
                  rowblk(p), full(vec(gp)), full(wg), full(wp), full(vec(gf))],
        out_specs=pl.BlockSpec((tm, d), lambda i, j: (i, 0)),
        out_shape=jax.ShapeDtypeStruct((rows, d), F32),
        scratch_shapes=[pltpu.VMEM((tm, d), BF16), pltpu.VMEM((tm, d), F32)],
        compiler_params=_cparams(("arbitrary", "arbitrary")),
        name="post",
    )(*row_args, wbr, wba, wout, vec(g), w1, w2, p, vec(gp), wg, wp, vec(gf))


def _layer(x, p, wkv0, shift0, caches, prm, norm_final, final, valid):
    n, t, d = x.shape
    rows = n * t
    width = prm["decay_w0"].shape[-1]
    rwkv_cols = prm["shift_mu"].shape[-1]
    aw = ATTN_HEADS * HEAD
    qkv_cols = 3 * len(DILATIONS) * aw
    tm = min(512, rows)
    x2 = x.reshape(rows, d)

    pick = lambda cols, opts: next(o for o in opts if cols % o == 0)
    z_rwkv, z_qkv, z_gate = _in_proj(x2, prm["norm_mix"], prm["w_in"].astype(BF16),
                                     (rwkv_cols, qkv_cols, 2 * d), tm)
    z_rwkv = z_rwkv.reshape(n, t, rwkv_cols)
    z_qkv = z_qkv.reshape(n, t, qkv_cols)

    if valid == t and t % WKV_CHUNK == 0:
        rwkv, s_t = _wkv_fused(z_rwkv, shift0, _pack_state(wkv0), prm, pick(t, (512, 256, WKV_CHUNK)), 4)
        wkv_t = _unpack_state(s_t, width // HEAD)
    else:
        feats = _rwkv_prep(z_rwkv, shift0, prm, SUBLANES if n % SUBLANES == 0 else 1)
        rwkv, wkv_t = _wkv(*feats, prm["lnx_w"], prm["lnx_b"], wkv0, valid)
    shift_t = z_rwkv[:, valid - 1, :]

    os_, ls_, new_kv = [], [], []
    ng = len(DILATIONS)
    for gi in range(ng):
        if caches is None:
            o, l = _attn_prompt(z_qkv, gi, max(1024, ATTN_BLOCK * DILATIONS[gi]))
            keep = min(WINDOWS[gi], t)
            kv_rows = z_qkv[:, t - keep:t]
        else:
            o, l = _attn_sample(z_qkv, caches[gi], gi, valid)
            kv_rows = z_qkv[:, :valid]
        new_kv.append(jnp.stack([kv_rows[:, :, (which * ng + gi) * aw:(which * ng + gi + 1) * aw]
                                 .reshape(n, -1, ATTN_HEADS, HEAD) for which in (1, 2)], axis=2))
        os_.append(o.reshape(rows, aw))
        ls_.append(l.reshape(rows, aw))

    bf = lambda name: prm[name].astype(BF16)
    x2 = _post(x2, rwkv.reshape(rows, -1), os_, ls_, z_gate, bf("w_branch_rwkv"), bf("w_branch_attn"),
               bf("w_out"), prm["norm_mlp"], bf("w_ff1"), bf("w_ff2"), p.reshape(rows, -1),
               prm["norm_ple"], bf("w_ple_gate"), bf("w_ple"), norm_final, final, min(256, rows),
               prm["w_ff1"].shape[1])
    return x2.reshape(n, t, d), new_kv, wkv_t, shift_t


_LAYER_PARAMS = ("norm_mix", "w_in", "shift_mu", "decay_w0", "decay_w2", "aaa_a0", "aaa_a2", "gate_g2",
                 "k_k", "k_a", "r_k", "lnx_w", "lnx_b", "w_branch_rwkv", "w_branch_attn", "w_out",
                 "norm_mlp", "w_ff1", "w_ff2", "norm_ple", "w_ple", "w_ple_gate")


def kernel(x_prompt, x_sample, cache_kv_g0, cache_kv_g1, cache_kv_g2, state_wkv, state_shift, p_prompt, p_sample, norm_mix, w_in, shift_mu, decay_w0, decay_w2, aaa_a0, aaa_a2, gate_g2, k_k, k_a, r_k, lnx_w, lnx_b, w_branch_rwkv, w_branch_attn, w_out, norm_mlp, w_ff1, w_ff2, norm_ple, w_ple, w_ple_gate, norm_final):
    stacked = dict(zip(_LAYER_PARAMS, (norm_mix, w_in, shift_mu, decay_w0, decay_w2, aaa_a0, aaa_a2,
                                       gate_g2, k_k, k_a, r_k, lnx_w, lnx_b, w_branch_rwkv,
                                       w_branch_attn, w_out, norm_mlp, w_ff1, w_ff2, norm_ple, w_ple,
                                       w_ple_gate)))
    depth = norm_mix.shape[0]
    nb_p, _, d = x_prompt.shape
    nb_s, t_s, _ = x_sample.shape
    heads = decay_w0.shape[-1] // HEAD
    rwkv_cols = shift_mu.shape[-1]
    t_pad = -(-t_s // SUBLANES) * SUBLANES
    pad_t = lambda u: jnp.pad(u, ((0, 0), (0, t_pad - t_s), (0, 0)))

    yp, ys = x_prompt, pad_t(x_sample)
    kvp, kvs = [[], [], []], [[], [], []]
    wkv_p, sh_p, wkv_s, sh_s = [], [], [], []
    for i in range(depth):
        prm = {name: val[i] for name, val in stacked.items()}
        prm["r_k"] = prm["r_k"].reshape(-1)
        last = i == depth - 1
        wkv0 = jnp.zeros((nb_p, heads, HEAD, HEAD), F32)
        shift0 = jnp.zeros((nb_p, rwkv_cols), F32)
        yp, nkv_p, w_p, s_p = _layer(yp, p_prompt[i], wkv0, shift0, None, prm, norm_final, last,
                                     x_prompt.shape[1])
        ys, nkv_s, w_s, s_s = _layer(ys, pad_t(p_sample[i]), state_wkv[i], state_shift[i],
                                     (cache_kv_g0[i], cache_kv_g1[i], cache_kv_g2[i]), prm, norm_final,
                                     last, t_s)
        for gi in range(3):
            kvp[gi].append(nkv_p[gi])
            kvs[gi].append(nkv_s[gi])
        wkv_p.append(w_p)
        sh_p.append(s_p)
        wkv_s.append(w_s)
        sh_s.append(s_s)
    return (yp, ys[:, :t_s],
            jnp.stack(kvp[0]), jnp.stack(kvp[1]), jnp.stack(kvp[2]),
            jnp.stack(wkv_p), jnp.stack(sh_p),
            jnp.stack(kvs[0]), jnp.stack(kvs[1]), jnp.stack(kvs[2]),
            jnp.stack(wkv_s), jnp.stack(sh_s))
```

```python
import functools

import jax
import jax.numpy as jnp
from jax import lax
from jax.experimental import pallas as pl
from jax.experimental.pallas import tpu as pltpu

F32 = jnp.float32
BF16 = jnp.bfloat16

HEAD = 64
LANES = 128
SUBLANES = 8
ATTN_HEADS = 4
ATTN_BLOCK = 128
WINDOWS = (128, 512, 2048)
DILATIONS = (1, 4, 16)
NORM_EPS = 1e-6
LNX_EPS = 64e-5
NEG = -1e30
VMEM_LIMIT = 56 * 1024 * 1024


def _cparams(sem):
    return pltpu.CompilerParams(dimension_semantics=sem, vmem_limit_bytes=VMEM_LIMIT)


def _dot(a, b):
    return jnp.dot(a, b, preferred_element_type=F32)


def _dot_bf16(a, b):
    return _dot(a.astype(BF16), b.astype(BF16))


def _pair_ones():
    r = lax.broadcasted_iota(jnp.int32, (LANES, LANES), 0) // HEAD
    c = lax.broadcasted_iota(jnp.int32, (LANES, LANES), 1) // HEAD
    return (r == c).astype(BF16)


def _head_sum(x, ones):
    outs = []
    for c in range(x.shape[1] // LANES):
        outs.append(_dot(x[:, c * LANES:(c + 1) * LANES].astype(BF16), ones))
    return jnp.concatenate(outs, axis=1)


def _rms(x, g):
    return x * lax.rsqrt(jnp.mean(x * x, axis=-1, keepdims=True) + NORM_EPS) * g


def _sigmoid(x):
    return 1.0 / (1.0 + jnp.exp(-x))


DECAY_SCALE = 0.6065306597126334


def _in_proj_kernel(x_ref, g_ref, w_ref, zr_ref, zq_ref, gate_ref):
    h = _rms(x_ref[...], g_ref[...]).astype(BF16)
    c1 = zr_ref.shape[1]
    c2 = c1 + zq_ref.shape[1]
    zr_ref[...] = _dot(h, w_ref[:, :c1])
    zq_ref[...] = _dot(h, w_ref[:, c1:c2])
    gate_ref[...] = _sigmoid(_dot(h, w_ref[:, c2:])).astype(BF16)


def _in_proj(x, g, w, splits, tm):
    rows, d = x.shape
    assert sum(splits) == w.shape[1] and len(splits) == 3
    return pl.pallas_call(
        _in_proj_kernel,
        grid=(rows // tm,),
        in_specs=[pl.BlockSpec((tm, d), lambda i: (i, 0)),
                  pl.BlockSpec((1, d), lambda i: (0, 0)),
                  pl.BlockSpec(w.shape, lambda i: (0, 0), pipeline_mode=pl.Buffered(1))],
        out_specs=[pl.BlockSpec((tm, c), lambda i: (i, 0)) for c in splits],
        out_shape=[jax.ShapeDtypeStruct((rows, c), dt) for c, dt in zip(splits, (F32, F32, BF16))],
        compiler_params=_cparams(("arbitrary",)),
        name="in_proj",
    )(x, g.reshape(1, d), w)


def _drain(steps):
    try:
        while True:
            next(steps)
    except StopIteration as stop:
        return stop.value


def _zip_stages(*gens):
    vals = [None] * len(gens)
    live = list(range(len(gens)))
    while live:
        for i in list(live):
            try:
                next(gens[i])
            except StopIteration as stop:
                vals[i] = stop.value
                live.remove(i)
    return vals


def _rwkv_features_steps(r, k, v, xw, xa, xg, w0, w2, a0, a2, g2, k_k, k_a, r_k, ones):
    pre = w0 + _dot_bf16(jnp.tanh(xw), w2)
    yield
    lw = _sigmoid(pre) * (-DECAY_SCALE)
    yield
    a = _sigmoid(a0 + _dot_bf16(xa, a2))
    yield
    g = _dot_bf16(_sigmoid(xg), g2)
    yield
    kk = k * k_k
    norm2 = _head_sum(kk * kk, ones)
    yield
    kk = kk * lax.rsqrt(jnp.maximum(norm2, 1e-24))
    k2 = k * (1.0 + (a - 1.0) * k_a)
    yield
    bonus = _head_sum(r * k2 * r_k, ones) * v
    yield
    return lw, k2, -kk, kk * a, g, bonus


def _rwkv_features(*args):
    return _drain(_rwkv_features_steps(*args))


def _rwkv_out(y, bonus, g, lnx_w, lnx_b, ones):
    mu = _head_sum(y, ones) * (1.0 / HEAD)
    yc = y - mu
    var = _head_sum(yc * yc, ones) * (1.0 / HEAD)
    return (yc * lax.rsqrt(var + LNX_EPS) * lnx_w + lnx_b + bonus) * g


def _rwkv_prep_kernel(z_ref, sh_ref, mu_ref, w0_ref, w2_ref, a0_ref, a2_ref, g2_ref, kk_ref, ka_ref,
                      rk_ref, r_o, w_o, k_o, v_o, a_o, b_o, g_o, bon_o, *, width):
    nb, t, cols = z_ref.shape
    first = lax.broadcasted_iota(jnp.int32, (t, cols), 0) == 0
    zs = []
    for b in range(nb):
        z = z_ref[b]
        prev = jnp.where(first, sh_ref[b:b + 1, :], pltpu.roll(z, 1, axis=0))
        zs.append(z + (prev - z) * mu_ref[...])
    zs = jnp.concatenate(zs, axis=0)

    c0 = 3 * width
    c1 = c0 + w2_ref.shape[0]
    c2 = c1 + a2_ref.shape[0]
    r, k, v = zs[:, :width], zs[:, width:2 * width], zs[:, 2 * width:c0]
    lw, k2, a_neg, b, g, bonus = _rwkv_features(
        r, k, v, zs[:, c0:c1], zs[:, c1:c2], zs[:, c2:], w0_ref[...], w2_ref[...], a0_ref[...],
        a2_ref[...], g2_ref[...], kk_ref[...], ka_ref[...], rk_ref[...], _pair_ones())
    for ref, val in ((r_o, r), (w_o, lw), (k_o, k2), (v_o, v), (a_o, a_neg), (b_o, b), (g_o, g), (bon_o, bonus)):
        for bi in range(nb):
            ref[bi] = val[bi * t:(bi + 1) * t, :]


def _rwkv_prep(z_rwkv, shift0, prm, nb):
    n, t, cols = z_rwkv.shape
    width = prm["decay_w0"].shape[-1]
    assert n % nb == 0
    row = lambda x: x.reshape(1, -1)
    full = lambda a: pl.BlockSpec(a.shape, lambda i: (0,) * a.ndim)
    args = [row(prm["shift_mu"]), row(prm["decay_w0"]), prm["decay_w2"], row(prm["aaa_a0"]),
            prm["aaa_a2"], prm["gate_g2"], row(prm["k_k"]), row(prm["k_a"]), row(prm["r_k"])]
    out_spec = pl.BlockSpec((nb, t, width), lambda i: (i, 0, 0))
    out_sds = jax.ShapeDtypeStruct((n, t, width), F32)
    return pl.pallas_call(
        functools.partial(_rwkv_prep_kernel, width=width),
        grid=(n // nb,),
        in_specs=[pl.BlockSpec((nb, t, cols), lambda i: (i, 0, 0)), pl.BlockSpec((nb, cols), lambda i: (i, 0))]
                 + [full(a) for a in args],
        out_specs=[out_spec] * 8,
        out_shape=[out_sds] * 8,
        compiler_params=_cparams(("arbitrary",)),
        name="rwkv_prep",
    )(z_rwkv, shift0, *args)


def _wkv_kernel(r_ref, w_ref, k_ref, v_ref, a_ref, b_ref, g_ref, bon_ref, lnw_ref, lnb_ref, s0_ref,
                y_ref, st_ref, *, valid):
    ones = _pair_ones()
    row = lax.broadcasted_iota(jnp.int32, (HEAD, LANES), 0)
    lane = lax.broadcasted_iota(jnp.int32, (HEAD, LANES), 1)
    left = lane < HEAD
    diag = (row == lane % HEAD).astype(F32)

    def key_sum(x):
        hi = x.astype(BF16)
        lo = (x - hi.astype(F32)).astype(BF16)
        return _dot(hi, ones) + _dot(lo, ones)

    chains = [(b, p) for b in range(y_ref.shape[0]) for p in range(s0_ref.shape[1] // 2)]
    cols = lambda p: slice(p * LANES, (p + 1) * LANES)
    vt = [v_ref[b, :, cols(p)].T for b, p in chains]
    s = [jnp.concatenate([s0_ref[b, 2 * p], s0_ref[b, 2 * p + 1]], axis=1) for b, p in chains]
    ys = [[] for _ in chains]
    for j in range(y_ref.shape[1]):
        for ch, (b, p) in enumerate(chains):
            if j < valid:
                row_of = lambda ref: ref[b, j:j + 1, cols(p)]
                v_col = jnp.where(left, vt[ch][0:HEAD, j:j + 1], vt[ch][HEAD:2 * HEAD, j:j + 1])
                sa = key_sum(s[ch] * row_of(a_ref))
                s[ch] = s[ch] * jnp.exp(row_of(w_ref)) + sa * row_of(b_ref) + v_col * row_of(k_ref)
                y_col = key_sum(s[ch] * row_of(r_ref))
                ys[ch].append(jnp.sum(y_col * diag, axis=0, keepdims=True))
            else:
                ys[ch].append(jnp.zeros((1, LANES), F32))
    for ch, (b, p) in enumerate(chains):
        y_ref[b, :, cols(p)] = _rwkv_out(jnp.concatenate(ys[ch], axis=0), bon_ref[b, :, cols(p)],
                                         g_ref[b, :, cols(p)], lnw_ref[:, cols(p)], lnb_ref[:, cols(p)], ones)
        st_ref[b, 2 * p] = s[ch][:, :HEAD]
        st_ref[b, 2 * p + 1] = s[ch][:, HEAD:]


def _wkv(r, w, k, v, a, b, g, bonus, lnx_w, lnx_b, s0, valid):
    n, t, c = r.shape
    assert t == SUBLANES
    nb = next(c for c in (4, 2, 1) if n % c == 0)
    tok = pl.BlockSpec((nb, t, c), lambda bi: (bi, 0, 0))
    vec = pl.BlockSpec((1, c), lambda bi: (0, 0))
    st = pl.BlockSpec((nb,) + s0.shape[1:], lambda bi: (bi, 0, 0, 0))
    return pl.pallas_call(
        functools.partial(_wkv_kernel, valid=valid),
        grid=(n // nb,),
        in_specs=[tok] * 8 + [vec, vec, st],
        out_specs=[tok, st],
        out_shape=[jax.ShapeDtypeStruct((n, t, c), F32), jax.ShapeDtypeStruct(s0.shape, F32)],
        compiler_params=_cparams(("arbitrary",)),
        name="wkv_scan",
    )(r, w, k, v, a, b, g, bonus, lnx_w.reshape(1, c), lnx_b.reshape(1, c), s0)


WKV_CHUNK = 64


def _bd(x):
    xb = x.astype(BF16)
    left = lax.broadcasted_iota(jnp.int32, x.shape, 1) < HEAD
    zero = jnp.zeros_like(xb)
    return jnp.concatenate([jnp.where(left, xb, zero), jnp.where(left, zero, xb)], axis=0)


def _dot_nt(a, b):
    return lax.dot_general(a, b, (((1,), (1,)), ((), ())), preferred_element_type=F32)


def _dot_tn(a, b):
    return lax.dot_general(a, b, (((0,), (0,)), ((), ())), preferred_element_type=F32)


def _wkv_chunk_local_steps(r, lw, k, v, a, b, consts):
    tri, eye2, strict, incl, blocks = consts
    c = WKV_CHUNK
    n = range(len(r))
    zero = jnp.zeros((c, LANES), F32)

    def cumsum(x):
        hi = x.astype(BF16)
        lo = (x - hi.astype(F32)).astype(BF16)
        return _dot(tri, hi) + _dot(tri, lo)

    lc = [cumsum(lw[i]) for i in n]
    yield
    w_inc = [jnp.exp(lc[i]) for i in n]
    w_inv = [1.0 / w_inc[i] for i in n]
    ah = [a[i] * jnp.exp(lc[i] - lw[i]) for i in n]
    rh = [r[i] * w_inc[i] for i in n]
    yield
    sc = [_dot_nt(jnp.concatenate([ah[i], rh[i]], axis=0).astype(BF16),
                  jnp.concatenate([_bd(b[i] * w_inv[i]), _bd(k[i] * w_inv[i])], axis=0)) for i in n]
    yield
    aab = [jnp.where(strict, sc[i][:c, :LANES], zero) for i in n]

    t = [eye2 + jnp.where(blocks[0], aab[i], zero) for i in n]
    for lvl in range(1, len(blocks)):
        grow = blocks[lvl] & ~blocks[lvl - 1]
        x = [_dot(t[i].astype(BF16), _bd(jnp.where(grow, aab[i], zero))) for i in n]
        yield
        t = [t[i] + _dot(x[i].astype(BF16), _bd(t[i])) for i in n]
        yield

    vb = [_bd(v[i]) for i in n]
    akv = [_dot(jnp.where(strict, sc[i][:c, LANES:], zero).astype(BF16), vb[i]) for i in n]
    yield
    pu = [_dot(t[i].astype(BF16), jnp.concatenate([_bd(ah[i]), _bd(akv[i])], axis=1)) for i in n]
    yield

    left2 = (lax.broadcasted_iota(jnp.int32, (HEAD, 2 * LANES), 1) % LANES) < HEAD
    g, h, q, y0 = [], [], [], []
    for i in n:
        wc = w_inc[i][c - 1:c, :]
        w_rem = wc * w_inv[i]
        lhs_t = jnp.concatenate([b[i] * w_rem, k[i] * w_rem], axis=0).astype(BF16)
        rhs = jnp.concatenate([pu[i], jnp.concatenate([zero, v[i]], axis=1)], axis=0).astype(BF16)
        full = _dot_tn(lhs_t, rhs)
        gh = jnp.where(left2, full[:HEAD], full[HEAD:])
        g.append(gh[:, :LANES] + eye2 * wc)
        h.append(gh[:, LANES:])
    yield
    zero_b = jnp.zeros((2 * c, LANES), BF16)
    for i in n:
        mrb = jnp.where(incl, sc[i][c:, :LANES], zero)
        mrk = jnp.where(incl, sc[i][c:, LANES:], zero)
        rhs2 = jnp.concatenate([jnp.concatenate([_bd(pu[i][:, :LANES]), _bd(pu[i][:, LANES:])], axis=1),
                                jnp.concatenate([zero_b, vb[i]], axis=1)], axis=0)
        qy = _dot(jnp.concatenate([mrb, mrk], axis=1).astype(BF16), rhs2)
        q.append(rh[i] + qy[:, :LANES])
        y0.append(qy[:, LANES:])
    yield
    return g, h, q, y0


def _wkv_fused_kernel(zr_ref, zk_ref, zv_ref, zl_ref, sr_ref, sk_ref, sv_ref, sl_ref,
                      mr_ref, mk_ref, mv_ref, ml_ref, w0_ref, w2_ref, a0_ref, a2_ref, g2_ref,
                      kk_ref, ka_ref, rk_ref, lnw_ref, lnb_ref, s0_ref, out_ref, st_ref,
                      s_scr, pr_scr, pk_scr, pv_scr, pl_scr, *, group_size):
    i = pl.program_id(1)
    c = WKV_CHUNK
    n_seq, tb = out_ref.shape[0], out_ref.shape[1]
    n_chunks = tb // c

    @pl.when(i == 0)
    def _():
        s_scr[...] = s0_ref[...]
        pr_scr[...] = sr_ref[...]
        pk_scr[...] = sk_ref[...]
        pv_scr[...] = sv_ref[...]
        pl_scr[...] = sl_ref[...]

    def shifted(z_ref, p_scr, mu_ref, b, lanes):
        z = z_ref[b, :, lanes]
        first = lax.broadcasted_iota(jnp.int32, z.shape, 0) == 0
        prev = jnp.where(first, p_scr[b:b + 1, lanes], pltpu.roll(z, 1, axis=0))
        p_scr[b:b + 1, lanes] = z[tb - 1:tb, :]
        return z + (prev - z) * mu_ref[:, lanes]

    ones = _pair_ones()
    c1 = w2_ref.shape[0]
    c2 = c1 + a2_ref.shape[0]
    ti = lax.broadcasted_iota(jnp.int32, (c, LANES), 0)
    si = lax.broadcasted_iota(jnp.int32, (c, LANES), 1) % HEAD
    tri = (lax.broadcasted_iota(jnp.int32, (c, c), 0) >= lax.broadcasted_iota(jnp.int32, (c, c), 1)
           ).astype(BF16)
    eye2 = (ti == si).astype(F32)
    sizes = [2 << l for l in range(c.bit_length() - 1)]
    blocks = [(ti // s) == (si // s) for s in sizes]
    consts = (tri, eye2, ti > si, ti >= si, blocks)
    lora_in = {}

    def feature_steps(group):
        out = []
        for b, sp in group:
            lanes = slice(sp * LANES, (sp + 1) * LANES)
            r = shifted(zr_ref, pr_scr, mr_ref, b, lanes)
            k = shifted(zk_ref, pk_scr, mk_ref, b, lanes)
            yield
            v = shifted(zv_ref, pv_scr, mv_ref, b, lanes)
            if b not in lora_in:
                lora_in[b] = shifted(zl_ref, pl_scr, ml_ref, b, slice(None))
            zl = lora_in[b]
            yield
            lw, k2, a_neg, bb, gate, bonus = yield from _rwkv_features_steps(
                r, k, v, zl[:, :c1], zl[:, c1:c2], zl[:, c2:], w0_ref[:, lanes], w2_ref[:, lanes],
                a0_ref[:, lanes], a2_ref[:, lanes], g2_ref[:, lanes], kk_ref[:, lanes], ka_ref[:, lanes],
                rk_ref[:, lanes], ones)
            out.append((r, lw, k2, v, a_neg, bb, gate, bonus))
        return out

    def local_steps(feat):
        chunks = lambda which: [f[which][ci * c:(ci + 1) * c, :] for ci in range(n_chunks) for f in feat]
        return (yield from _wkv_chunk_local_steps(*(chunks(which) for which in range(6)), consts))

    def tail_steps(group, feat, local):
        g, h, q, y0 = local
        st = [s_scr[b, sp] for b, sp in group]
        ys = [[] for _ in group]
        for ci in range(n_chunks):
            for j in range(len(group)):
                u = ci * len(group) + j
                out = _dot(jnp.concatenate([q[u], g[u]], axis=0).astype(BF16), _bd(st[j]))
                ys[j].append(out[:c] + y0[u])
                st[j] = out[c:] + h[u]
            yield
        for j, (b, sp) in enumerate(group):
            lanes = slice(sp * LANES, (sp + 1) * LANES)
            s_scr[b, sp] = st[j]
            out_ref[b, :, lanes] = _rwkv_out(jnp.concatenate(ys[j], axis=0), feat[j][7], feat[j][6],
                                             lnw_ref[:, lanes], lnb_ref[:, lanes], ones)
            yield

        @pl.when(i == pl.num_programs(1) - 1)
        def _():
            for j, (b, sp) in enumerate(group):
                st_ref[b, sp] = st[j]

    chains = [(b, sp) for b in range(n_seq) for sp in range(out_ref.shape[2] // LANES)]
    groups = [chains[j:j + group_size] for j in range(0, len(chains), group_size)]
    feats, locals_ = {}, {}
    for step in range(len(groups) + 2):
        gens, roles = [], []
        if 0 <= step - 1 < len(groups):
            gens.append(local_steps(feats[step - 1]))
            roles.append((locals_, step - 1))
        if step < len(groups):
            gens.append(feature_steps(groups[step]))
            roles.append((feats, step))
        if 0 <= step - 2 < len(groups):
            gens.append(tail_steps(groups[step - 2], feats[step - 2], locals_[step - 2]))
            roles.append(({}, step - 2))
        for (store, idx), val in zip(roles, _zip_stages(*gens)):
            store[idx] = val


def _wkv_fused(z_rwkv, shift0, s0, prm, tb, pp):
    n, t, cols = z_rwkv.shape
    width = prm["decay_w0"].shape[-1]
    bw = pp * LANES
    nblk = width // bw
    lora = cols - 3 * width
    assert width % bw == 0 and (3 * width) % lora == 0 and lora % LANES == 0
    lblk = 3 * width // lora
    row = lambda x: x.reshape(1, -1)
    mu = row(prm["shift_mu"])
    tok = lambda off: pl.BlockSpec((n, tb, bw), lambda p, i: (0, i, off + p))
    vec = lambda rows_, off: pl.BlockSpec((rows_, bw), lambda p, i: (0, off + p))
    st = pl.BlockSpec((n, pp, HEAD, LANES), lambda p, i: (0, p, 0, 0))
    in_specs = [tok(0), tok(nblk), tok(2 * nblk), pl.BlockSpec((n, tb, lora), lambda p, i: (0, i, lblk)),
                vec(n, 0), vec(n, nblk), vec(n, 2 * nblk), pl.BlockSpec((n, lora), lambda p, i: (0, lblk)),
                vec(1, 0), vec(1, nblk), vec(1, 2 * nblk), pl.BlockSpec((1, lora), lambda p, i: (0, lblk)),
                vec(1, 0), vec(prm["decay_w2"].shape[0], 0), vec(1, 0), vec(prm["aaa_a2"].shape[0], 0),
                vec(prm["gate_g2"].shape[0], 0), vec(1, 0), vec(1, 0), vec(1, 0), vec(1, 0), vec(1, 0), st]
    args = [z_rwkv] * 4 + [shift0] * 4 + [mu] * 4 + [
        row(prm["decay_w0"]), prm["decay_w2"], row(prm["aaa_a0"]), prm["aaa_a2"], prm["gate_g2"],
        row(prm["k_k"]), row(prm["k_a"]), row(prm["r_k"]), row(prm["lnx_w"]), row(prm["lnx_b"]), s0]
    return pl.pallas_call(
        functools.partial(_wkv_fused_kernel, group_size=2),
        grid=(nblk, t // tb),
        in_specs=in_specs,
        out_specs=[tok(0), st],
        out_shape=[jax.ShapeDtypeStruct((n, t, width), F32), jax.ShapeDtypeStruct(s0.shape, F32)],
        scratch_shapes=[pltpu.VMEM((n, pp, HEAD, LANES), F32), pltpu.VMEM((n, bw), F32),
                        pltpu.VMEM((n, bw), F32), pltpu.VMEM((n, bw), F32),
                        pltpu.VMEM((n, lora), F32)],
        compiler_params=_cparams(("arbitrary", "arbitrary")),
        name="wkv_fused",
    )(*args)


def _pack_state(s):
    n, h, vd, kd = s.shape
    return s.reshape(n, h // 2, 2, vd, kd).transpose(0, 1, 4, 2, 3).reshape(n, h // 2, kd, 2 * vd)


def _unpack_state(s, h):
    n, _, kd, _ = s.shape
    return s.reshape(n, h // 2, kd, 2, HEAD).transpose(0, 1, 3, 4, 2).reshape(n, h, HEAD, kd)


def _attn_prompt_kernel(q_ref, k_ref, kp_ref, v_ref, vp_ref, o_ref, l_ref, *, dil, sub, n_back, scale):
    first = pl.program_id(1) == 0
    qi = lax.broadcasted_iota(jnp.int32, (ATTN_BLOCK, 2 * ATTN_BLOCK), 0)
    ki = lax.broadcasted_iota(jnp.int32, (ATTN_BLOCK, 2 * ATTN_BLOCK), 1)
    dist = qi + ATTN_BLOCK - ki
    band = (dist >= 0) & (dist <= n_back)
    own = ki >= ATTN_BLOCK
    left = lax.broadcasted_iota(jnp.int32, (ATTN_BLOCK, LANES), 1) < HEAD
    ones_kv = jnp.ones((2 * ATTN_BLOCK, LANES), BF16)

    def rows(j, r):
        if dil == 1:
            return pl.ds(j * ATTN_BLOCK, ATTN_BLOCK)
        return pl.ds(j * ATTN_BLOCK * dil + r, ATTN_BLOCK, stride=dil)

    def units(jr):
        probs = []
        qs, kcat, vcat, masks = [], [], [], []
        for j, r in jr:
            q = (q_ref[0, rows(j, r), :] * scale).astype(BF16)
            if j == 0:
                kp, vp = kp_ref[0, rows(0, r), :], vp_ref[0, rows(0, r), :]
                masks.append(band & (own | jnp.logical_not(first)))
            else:
                kp, vp = k_ref[0, rows(j - 1, r), :], v_ref[0, rows(j - 1, r), :]
                masks.append(band)
            kcat.append(jnp.concatenate([kp, k_ref[0, rows(j, r), :]], axis=0).astype(BF16))
            vcat.append(jnp.concatenate([vp, v_ref[0, rows(j, r), :]], axis=0).astype(BF16))
            zq = jnp.zeros_like(q)
            qs += [jnp.where(left, q, zq), jnp.where(left, zq, q)]
            probs += [(len(kcat) - 1, 0), (len(kcat) - 1, 1)]
        s = [jnp.where(masks[u], _dot_nt(qs[p], kcat[u]), NEG) for p, (u, _) in enumerate(probs)]
        m = [jnp.max(x, axis=-1, keepdims=True) for x in s]
        e = [jnp.exp(s[p] - m[p]).astype(BF16) for p in range(len(probs))]
        o = [_dot(e[p], vcat[u]) for p, (u, _) in enumerate(probs)]
        dens = [_dot(e[p], ones_kv) for p in range(len(probs))]
        for u, (j, r) in enumerate(jr):
            den = jnp.where(left, dens[2 * u], dens[2 * u + 1])
            o_ref[0, rows(j, r), :] = jnp.where(left, o[2 * u], o[2 * u + 1]) / den
            l_ref[0, rows(j, r), :] = jnp.where(left, m[2 * u], m[2 * u + 1]) + jnp.log(den)

    if dil <= 4:
        units([(j, r) for j in range(sub) for r in range(dil)])
    else:
        for j in range(sub):
            def body(r4, carry, j=j):
                units([(j, r4 * 4 + u) for u in range(4)])
                return carry
            lax.fori_loop(0, dil // 4, body, 0)


def _attn_prompt(zqkv, gi, span):
    n, t, cols = zqkv.shape
    d = DILATIONS[gi]
    aw = ATTN_HEADS * HEAD
    ng = len(DILATIONS)
    back = ATTN_BLOCK * d
    sub = span // back
    halves = aw // LANES
    col = lambda which, hp: (which * ng + gi) * halves + hp
    cur = lambda which: pl.BlockSpec((1, span, LANES), lambda b, i, hp: (b, i, col(which, hp)))
    prev = lambda which: pl.BlockSpec((1, back, LANES),
                                      lambda b, i, hp: (b, jnp.maximum(i * sub - 1, 0), col(which, hp)))
    out_spec = pl.BlockSpec((1, span, LANES), lambda b, i, hp: (b, i, hp))
    out_sds = jax.ShapeDtypeStruct((n, t, aw), F32)
    return pl.pallas_call(
        functools.partial(_attn_prompt_kernel, dil=d, sub=sub, n_back=WINDOWS[gi] // d, scale=HEAD ** -0.5),
        grid=(n, t // span, halves),
        in_specs=[cur(0), cur(1), prev(1), cur(2), prev(2)],
        out_specs=[out_spec, out_spec],
        out_shape=[out_sds, out_sds],
        compiler_params=_cparams(("arbitrary", "arbitrary", "arbitrary")),
        name=f"attn_prompt_g{gi}",
    )(zqkv, zqkv, zqkv, zqkv, zqkv)


def _attn_sample_kernel(q_ref, k_ref, v_ref, c_ref, o_ref, l_ref, *, dil, n_back, valid, scale):
    nb, tp = q_ref.shape[0], q_ref.shape[1]
    cl = c_ref.shape[4]
    dist_c = (cl + lax.broadcasted_iota(jnp.int32, (tp, cl), 0)
              - lax.broadcasted_iota(jnp.int32, (tp, cl), 1))
    mask_c = (dist_c % dil == 0) & (dist_c <= dil * n_back)
    pn = lax.broadcasted_iota(jnp.int32, (tp, tp), 1)
    dist_n = lax.broadcasted_iota(jnp.int32, (tp, tp), 0) - pn
    mask_n = (dist_n >= 0) & (dist_n % dil == 0) & (dist_n <= dil * n_back) & (pn < valid)
    probs = [(b, h) for b in range(nb) for h in range(ATTN_HEADS)]
    sl = lambda h: slice(h * HEAD, (h + 1) * HEAD)
    q = [(q_ref[b, :, sl(h)] * scale).astype(BF16) for b, h in probs]
    s_c = [jnp.where(mask_c, _dot(q[p], c_ref[b, 0, h].astype(BF16)), NEG) for p, (b, h) in enumerate(probs)]
    s_n = [jnp.where(mask_n, _dot_nt(q[p], k_ref[b, :, sl(h)].astype(BF16)), NEG)
           for p, (b, h) in enumerate(probs)]
    m = [jnp.maximum(jnp.max(s_c[p], axis=-1, keepdims=True), jnp.max(s_n[p], axis=-1, keepdims=True))
         for p in range(len(probs))]
    e_c = [jnp.exp(s_c[p] - m[p]) for p in range(len(probs))]
    e_n = [jnp.exp(s_n[p] - m[p]) for p in range(len(probs))]
    den = [jnp.sum(e_c[p], axis=-1, keepdims=True) + jnp.sum(e_n[p], axis=-1, keepdims=True)
           for p in range(len(probs))]
    o = [_dot_nt(e_c[p].astype(BF16), c_ref[b, 1, h].astype(BF16))
         + _dot(e_n[p].astype(BF16), v_ref[b, :, sl(h)].astype(BF16)) for p, (b, h) in enumerate(probs)]
    for b in range(nb):
        ps = [p for p, (bb, _) in enumerate(probs) if bb == b]
        o_ref[b] = jnp.concatenate([o[p] / den[p] for p in ps], axis=1)
        l_ref[b] = jnp.concatenate([jnp.broadcast_to(m[p] + jnp.log(den[p]), (tp, HEAD)) for p in ps], axis=1)


def _attn_sample(zqkv, cache, gi, valid):
    n, tp, cols = zqkv.shape
    aw = ATTN_HEADS * HEAD
    ng = len(DILATIONS)
    d = DILATIONS[gi]
    cache_t = cache.transpose(0, 2, 3, 4, 1)
    nb = next(c for c in (4, 2, 1) if n % c == 0)
    spec = lambda which: pl.BlockSpec((nb, tp, aw), lambda b: (b, 0, which * ng + gi))
    out_spec = pl.BlockSpec((nb, tp, aw), lambda b: (b, 0, 0))
    out_sds = jax.ShapeDtypeStruct((n, tp, aw), F32)
    return pl.pallas_call(
        functools.partial(_attn_sample_kernel, dil=d, n_back=WINDOWS[gi] // d, valid=valid,
                          scale=HEAD ** -0.5),
        grid=(n // nb,),
        in_specs=[spec(0), spec(1), spec(2),
                  pl.BlockSpec((nb,) + cache_t.shape[1:], lambda b: (b, 0, 0, 0, 0))],
        out_specs=[out_spec, out_spec],
        out_shape=[out_sds, out_sds],
        compiler_params=_cparams(("arbitrary",)),
        name=f"attn_sample_g{gi}",
    )(zqkv, zqkv, zqkv, cache_t)


def _post_kernel(x_ref, rwkv_ref, o0, o1, o2, l0, l1, l2, zg_ref, wbr_ref, wba_ref, wout_ref, g_ref,
                 w1_ref, w2_ref, p_ref, gp_ref, wg_ref, wp_ref, gf_ref, o_ref, h_scr, acc_scr, *, final):
    j = pl.program_id(1)

    @pl.when(j == 0)
    def _():
        ls = [l0[...], l1[...], l2[...]]
        m = jnp.maximum(jnp.maximum(ls[0], ls[1]), ls[2])
        es = [jnp.exp(l - m) for l in ls]
        attn = (es[0] * o0[...] + es[1] * o1[...] + es[2] * o2[...]) / (es[0] + es[1] + es[2])
        d = x_ref.shape[1]
        gates = zg_ref[...].astype(F32)
        mixed = (gates[:, :d] * _dot(rwkv_ref[...].astype(BF16), wbr_ref[...])
                 + gates[:, d:] * _dot(attn.astype(BF16), wba_ref[...]))
        x = x_ref[...] + _dot(mixed.astype(BF16), wout_ref[...])
        acc_scr[...] = x
        h_scr[...] = _rms(x, g_ref[...]).astype(BF16)

    u = jnp.maximum(_dot(h_scr[...], w1_ref[...]), 0.0)
    acc_scr[...] += _dot((u * u).astype(BF16), w2_ref[...])

    @pl.when(j == pl.num_programs(1) - 1)
    def _():
        x = acc_scr[...]
        h = _rms(x, gp_ref[...]).astype(BF16)
        x = x + _sigmoid(_dot(h, wg_ref[...])) * _dot(p_ref[...].astype(BF16), wp_ref[...])
        if final:
            x = _rms(x, gf_ref[...])
        o_ref[...] = x


def _post(x, rwkv, os_, ls_, zg, wbr, wba, wout, g, w1, w2, p, gp, wg, wp, gf, final, tm, tf):
    rows, d = x.shape
    dff = w1.shape[1]
    vec = lambda a: a.reshape(1, d)
    rowblk = lambda a: pl.BlockSpec((tm, a.shape[1]), lambda i, j: (i, 0))
    full = lambda a: pl.BlockSpec(a.shape, lambda i, j: (0, 0), pipeline_mode=pl.Buffered(1))
    row_args = [x, rwkv, *os_, *ls_, zg]
    return pl.pallas_call(
        functools.partial(_post_kernel, final=final),
        grid=(rows // tm, dff // tf),
        in_specs=[rowblk(a) for a in row_args] + [full(wbr), full(wba), full(wout), full(vec(g)),
                  pl.BlockSpec((d, tf), lambda i, j: (0, j)),
                  pl.BlockSpec((tf, d), lambda i, j: (j, 0)),
                  rowblk(p), full(vec(gp)), full(wg), full(wp), full(vec(gf))],
        out_specs=pl.BlockSpec((tm, d), lambda i, j: (i, 0)),
        out_shape=jax.ShapeDtypeStruct((rows, d), F32),
        scratch_shapes=[pltpu.VMEM((tm, d), BF16), pltpu.VMEM((tm, d), F32)],
        compiler_params=_cparams(("arbitrary", "arbitrary")),
        name="post",
    )(*row_args, wbr, wba, wout, vec(g), w1, w2, p, vec(gp), wg, wp, vec(gf))


def _layer(x, p, wkv0, shift0, caches, prm, norm_final, final, valid):
    n, t, d = x.shape
    rows = n * t
    width = prm["decay_w0"].shape[-1]
    rwkv_cols = prm["shift_mu"].shape[-1]
    aw = ATTN_HEADS * HEAD
    qkv_cols = 3 * len(DILATIONS) * aw
    tm = min(512, rows)
    x2 = x.reshape(rows, d)

    pick = lambda cols, opts: next(o for o in opts if cols % o == 0)
    z_rwkv, z_qkv, z_gate = _in_proj(x2, prm["norm_mix"], prm["w_in"].astype(BF16),
                                     (rwkv_cols, qkv_cols, 2 * d), tm)
    z_rwkv = z_rwkv.reshape(n, t, rwkv_cols)
    z_qkv = z_qkv.reshape(n, t, qkv_cols)

    if valid == t and t % WKV_CHUNK == 0:
        rwkv, s_t = _wkv_fused(z_rwkv, shift0, _pack_state(wkv0), prm, pick(t, (512, 256, WKV_CHUNK)), 4)
        wkv_t = _unpack_state(s_t, width // HEAD)
    else:
        feats = _rwkv_prep(z_rwkv, shift0, prm, SUBLANES if n % SUBLANES == 0 else 1)
        rwkv, wkv_t = _wkv(*feats, prm["lnx_w"], prm["lnx_b"], wkv0, valid)
    shift_t = z_rwkv[:, valid - 1, :]

    os_, ls_, new_kv = [], [], []
    ng = len(DILATIONS)
    for gi in range(ng):
        if caches is None:
            o, l = _attn_prompt(z_qkv, gi, max(1024, ATTN_BLOCK * DILATIONS[gi]))
            keep = min(WINDOWS[gi], t)
            kv_rows = z_qkv[:, t - keep:t]
        else:
            o, l = _attn_sample(z_qkv, caches[gi], gi, valid)
            kv_rows = z_qkv[:, :valid]
        new_kv.append(jnp.stack([kv_rows[:, :, (which * ng + gi) * aw:(which * ng + gi + 1) * aw]
                                 .reshape(n, -1, ATTN_HEADS, HEAD) for which in (1, 2)], axis=2))
        os_.append(o.reshape(rows, aw))
        ls_.append(l.reshape(rows, aw))

    bf = lambda name: prm[name].astype(BF16)
    x2 = _post(x2, rwkv.reshape(rows, -1), os_, ls_, z_gate, bf("w_branch_rwkv"), bf("w_branch_attn"),
               bf("w_out"), prm["norm_mlp"], bf("w_ff1"), bf("w_ff2"), p.reshape(rows, -1),
               prm["norm_ple"], bf("w_ple_gate"), bf("w_ple"), norm_final, final, tm, 2048)
    return x2.reshape(n, t, d), new_kv, wkv_t, shift_t


_LAYER_PARAMS = ("norm_mix", "w_in", "shift_mu", "decay_w0", "decay_w2", "aaa_a0", "aaa_a2", "gate_g2",
                 "k_k", "k_a", "r_k", "lnx_w", "lnx_b", "w_branch_rwkv", "w_branch_attn", "w_out",
                 "norm_mlp", "w_ff1", "w_ff2", "norm_ple", "w_ple", "w_ple_gate")


def kernel(x_prompt, x_sample, cache_kv_g0, cache_kv_g1, cache_kv_g2, state_wkv, state_shift, p_prompt, p_sample, norm_mix, w_in, shift_mu, decay_w0, decay_w2, aaa_a0, aaa_a2, gate_g2, k_k, k_a, r_k, lnx_w, lnx_b, w_branch_rwkv, w_branch_attn, w_out, norm_mlp, w_ff1, w_ff2, norm_ple, w_ple, w_ple_gate, norm_final):
    stacked = dict(zip(_LAYER_PARAMS, (norm_mix, w_in, shift_mu, decay_w0, decay_w2, aaa_a0, aaa_a2,
                                       gate_g2, k_k, k_a, r_k, lnx_w, lnx_b, w_branch_rwkv,
                                       w_branch_attn, w_out, norm_mlp, w_ff1, w_ff2, norm_ple, w_ple,
                                       w_ple_gate)))
    depth = norm_mix.shape[0]
    nb_p, _, d = x_prompt.shape
    nb_s, t_s, _ = x_sample.shape
    heads = decay_w0.shape[-1] // HEAD
    rwkv_cols = shift_mu.shape[-1]
    t_pad = -(-t_s // SUBLANES) * SUBLANES
    pad_t = lambda u: jnp.pad(u, ((0, 0), (0, t_pad - t_s), (0, 0)))

    yp, ys = x_prompt, pad_t(x_sample)
    kvp, kvs = [[], [], []], [[], [], []]
    wkv_p, sh_p, wkv_s, sh_s = [], [], [], []
    for i in range(depth):
        prm = {name: val[i] for name, val in stacked.items()}
        prm["r_k"] = prm["r_k"].reshape(-1)
        last = i == depth - 1
        wkv0 = jnp.zeros((nb_p, heads, HEAD, HEAD), F32)
        shift0 = jnp.zeros((nb_p, rwkv_cols), F32)
        yp, nkv_p, w_p, s_p = _layer(yp, p_prompt[i], wkv0, shift0, None, prm, norm_final, last,
                                     x_prompt.shape[1])
        ys, nkv_s, w_s, s_s = _layer(ys, pad_t(p_sample[i]), state_wkv[i], state_shift[i],
                                     (cache_kv_g0[i], cache_kv_g1[i], cache_kv_g2[i]), prm, norm_final,
                                     last, t_s)
        for gi in range(3):
            kvp[gi].append(nkv_p[gi])
            kvs[gi].append(nkv_s[gi])
        wkv_p.append(w_p)
        sh_p.append(s_p)
        wkv_s.append(w_s)
        sh_s.append(s_s)
    return (yp, ys[:, :t_s],
            jnp.stack(kvp[0]), jnp.stack(kvp[1]), jnp.stack(kvp[2]),
            jnp.stack(wkv_p), jnp.stack(sh_p),
            jnp.stack(kvs[0]), jnp.stack(kvs[1]), jnp.stack(kvs[2]),
            jnp.stack(wkv_s), jnp.stack(sh_s))
```

```python
import functools

import jax
import jax.numpy as jnp
from jax import lax
from jax.experimental import pallas as pl
from jax.experimental.pallas import tpu as pltpu

F32 = jnp.float32
BF16 = jnp.bfloat16

HEAD = 64
LANES = 128
SUBLANES = 8
ATTN_HEADS = 4
ATTN_BLOCK = 128
WINDOWS = (128, 512, 2048)
DILATIONS = (1, 4, 16)
NORM_EPS = 1e-6
LNX_EPS = 64e-5
NEG = -1e30
VMEM_LIMIT = 56 * 1024 * 1024


def _cparams(sem):
    return pltpu.CompilerParams(dimension_semantics=sem, vmem_limit_bytes=VMEM_LIMIT)


def _dot(a, b):
    return jnp.dot(a, b, preferred_element_type=F32)


def _dot_bf16(a, b):
    return _dot(a.astype(BF16), b.astype(BF16))


def _pair_ones():
    r = lax.broadcasted_iota(jnp.int32, (LANES, LANES), 0) // HEAD
    c = lax.broadcasted_iota(jnp.int32, (LANES, LANES), 1) // HEAD
    return (r == c).astype(BF16)


def _head_sum(x, ones):
    outs = []
    for c in range(x.shape[1] // LANES):
        outs.append(_dot(x[:, c * LANES:(c + 1) * LANES].astype(BF16), ones))
    return jnp.concatenate(outs, axis=1)


def _rms(x, g):
    return x * lax.rsqrt(jnp.mean(x * x, axis=-1, keepdims=True) + NORM_EPS) * g


def _sigmoid(x):
    return 1.0 / (1.0 + jnp.exp(-x))


DECAY_SCALE = 0.6065306597126334


def _in_proj_kernel(x_ref, g_ref, w_ref, zr_ref, zq_ref, gate_ref):
    h = _rms(x_ref[...], g_ref[...]).astype(BF16)
    c1 = zr_ref.shape[1]
    c2 = c1 + zq_ref.shape[1]
    zr_ref[...] = _dot(h, w_ref[:, :c1])
    zq_ref[...] = _dot(h, w_ref[:, c1:c2])
    gate_ref[...] = _sigmoid(_dot(h, w_ref[:, c2:])).astype(BF16)


def _in_proj(x, g, w, splits, tm):
    rows, d = x.shape
    assert sum(splits) == w.shape[1] and len(splits) == 3
    return pl.pallas_call(
        _in_proj_kernel,
        grid=(rows // tm,),
        in_specs=[pl.BlockSpec((tm, d), lambda i: (i, 0)),
                  pl.BlockSpec((1, d), lambda i: (0, 0)),
                  pl.BlockSpec(w.shape, lambda i: (0, 0), pipeline_mode=pl.Buffered(1))],
        out_specs=[pl.BlockSpec((tm, c), lambda i: (i, 0)) for c in splits],
        out_shape=[jax.ShapeDtypeStruct((rows, c), dt) for c, dt in zip(splits, (F32, F32, BF16))],
        compiler_params=_cparams(("arbitrary",)),
        name="in_proj",
    )(x, g.reshape(1, d), w)


def _drain(steps):
    try:
        while True:
            next(steps)
    except StopIteration as stop:
        return stop.value


def _zip_stages(*gens):
    vals = [None] * len(gens)
    live = list(range(len(gens)))
    while live:
        for i in list(live):
            try:
                next(gens[i])
            except StopIteration as stop:
                vals[i] = stop.value
                live.remove(i)
    return vals


def _rwkv_features_steps(r, k, v, xw, xa, xg, w0, w2, a0, a2, g2, k_k, k_a, r_k, ones):
    pre = w0 + _dot_bf16(jnp.tanh(xw), w2)
    yield
    lw = _sigmoid(pre) * (-DECAY_SCALE)
    yield
    a = _sigmoid(a0 + _dot_bf16(xa, a2))
    yield
    g = _dot_bf16(_sigmoid(xg), g2)
    yield
    kk = k * k_k
    norm2 = _head_sum(kk * kk, ones)
    yield
    kk = kk * lax.rsqrt(jnp.maximum(norm2, 1e-24))
    k2 = k * (1.0 + (a - 1.0) * k_a)
    yield
    bonus = _head_sum(r * k2 * r_k, ones) * v
    yield
    return lw, k2, -kk, kk * a, g, bonus


def _rwkv_features(*args):
    return _drain(_rwkv_features_steps(*args))


def _rwkv_out(y, bonus, g, lnx_w, lnx_b, ones):
    mu = _head_sum(y, ones) * (1.0 / HEAD)
    yc = y - mu
    var = _head_sum(yc * yc, ones) * (1.0 / HEAD)
    return (yc * lax.rsqrt(var + LNX_EPS) * lnx_w + lnx_b + bonus) * g


def _rwkv_prep_kernel(z_ref, sh_ref, mu_ref, w0_ref, w2_ref, a0_ref, a2_ref, g2_ref, kk_ref, ka_ref,
                      rk_ref, r_o, w_o, k_o, v_o, a_o, b_o, g_o, bon_o, *, width):
    nb, t, cols = z_ref.shape
    first = lax.broadcasted_iota(jnp.int32, (t, cols), 0) == 0
    zs = []
    for b in range(nb):
        z = z_ref[b]
        prev = jnp.where(first, sh_ref[b:b + 1, :], pltpu.roll(z, 1, axis=0))
        zs.append(z + (prev - z) * mu_ref[...])
    zs = jnp.concatenate(zs, axis=0)

    c0 = 3 * width
    c1 = c0 + w2_ref.shape[0]
    c2 = c1 + a2_ref.shape[0]
    r, k, v = zs[:, :width], zs[:, width:2 * width], zs[:, 2 * width:c0]
    lw, k2, a_neg, b, g, bonus = _rwkv_features(
        r, k, v, zs[:, c0:c1], zs[:, c1:c2], zs[:, c2:], w0_ref[...], w2_ref[...], a0_ref[...],
        a2_ref[...], g2_ref[...], kk_ref[...], ka_ref[...], rk_ref[...], _pair_ones())
    for ref, val in ((r_o, r), (w_o, lw), (k_o, k2), (v_o, v), (a_o, a_neg), (b_o, b), (g_o, g), (bon_o, bonus)):
        for bi in range(nb):
            ref[bi] = val[bi * t:(bi + 1) * t, :]


def _rwkv_prep(z_rwkv, shift0, prm, nb):
    n, t, cols = z_rwkv.shape
    width = prm["decay_w0"].shape[-1]
    assert n % nb == 0
    row = lambda x: x.reshape(1, -1)
    full = lambda a: pl.BlockSpec(a.shape, lambda i: (0,) * a.ndim)
    args = [row(prm["shift_mu"]), row(prm["decay_w0"]), prm["decay_w2"], row(prm["aaa_a0"]),
            prm["aaa_a2"], prm["gate_g2"], row(prm["k_k"]), row(prm["k_a"]), row(prm["r_k"])]
    out_spec = pl.BlockSpec((nb, t, width), lambda i: (i, 0, 0))
    out_sds = jax.ShapeDtypeStruct((n, t, width), F32)
    return pl.pallas_call(
        functools.partial(_rwkv_prep_kernel, width=width),
        grid=(n // nb,),
        in_specs=[pl.BlockSpec((nb, t, cols), lambda i: (i, 0, 0)), pl.BlockSpec((nb, cols), lambda i: (i, 0))]
                 + [full(a) for a in args],
        out_specs=[out_spec] * 8,
        out_shape=[out_sds] * 8,
        compiler_params=_cparams(("arbitrary",)),
        name="rwkv_prep",
    )(z_rwkv, shift0, *args)


def _wkv_kernel(r_ref, w_ref, k_ref, v_ref, a_ref, b_ref, g_ref, bon_ref, lnw_ref, lnb_ref, s0_ref,
                y_ref, st_ref, *, valid):
    ones = _pair_ones()
    row = lax.broadcasted_iota(jnp.int32, (HEAD, LANES), 0)
    lane = lax.broadcasted_iota(jnp.int32, (HEAD, LANES), 1)
    left = lane < HEAD
    diag = (row == lane % HEAD).astype(F32)

    def key_sum(x):
        hi = x.astype(BF16)
        lo = (x - hi.astype(F32)).astype(BF16)
        return _dot(hi, ones) + _dot(lo, ones)

    chains = [(b, p) for b in range(y_ref.shape[0]) for p in range(s0_ref.shape[1] // 2)]
    cols = lambda p: slice(p * LANES, (p + 1) * LANES)
    vt = [v_ref[b, :, cols(p)].T for b, p in chains]
    s = [jnp.concatenate([s0_ref[b, 2 * p], s0_ref[b, 2 * p + 1]], axis=1) for b, p in chains]
    ys = [[] for _ in chains]
    for j in range(y_ref.shape[1]):
        for ch, (b, p) in enumerate(chains):
            if j < valid:
                row_of = lambda ref: ref[b, j:j + 1, cols(p)]
                v_col = jnp.where(left, vt[ch][0:HEAD, j:j + 1], vt[ch][HEAD:2 * HEAD, j:j + 1])
                sa = key_sum(s[ch] * row_of(a_ref))
                s[ch] = s[ch] * jnp.exp(row_of(w_ref)) + sa * row_of(b_ref) + v_col * row_of(k_ref)
                y_col = key_sum(s[ch] * row_of(r_ref))
                ys[ch].append(jnp.sum(y_col * diag, axis=0, keepdims=True))
            else:
                ys[ch].append(jnp.zeros((1, LANES), F32))
    for ch, (b, p) in enumerate(chains):
        y_ref[b, :, cols(p)] = _rwkv_out(jnp.concatenate(ys[ch], axis=0), bon_ref[b, :, cols(p)],
                                         g_ref[b, :, cols(p)], lnw_ref[:, cols(p)], lnb_ref[:, cols(p)], ones)
        st_ref[b, 2 * p] = s[ch][:, :HEAD]
        st_ref[b, 2 * p + 1] = s[ch][:, HEAD:]


def _wkv(r, w, k, v, a, b, g, bonus, lnx_w, lnx_b, s0, valid):
    n, t, c = r.shape
    assert t == SUBLANES
    nb = next(c for c in (4, 2, 1) if n % c == 0)
    tok = pl.BlockSpec((nb, t, c), lambda bi: (bi, 0, 0))
    vec = pl.BlockSpec((1, c), lambda bi: (0, 0))
    st = pl.BlockSpec((nb,) + s0.shape[1:], lambda bi: (bi, 0, 0, 0))
    return pl.pallas_call(
        functools.partial(_wkv_kernel, valid=valid),
        grid=(n // nb,),
        in_specs=[tok] * 8 + [vec, vec, st],
        out_specs=[tok, st],
        out_shape=[jax.ShapeDtypeStruct((n, t, c), F32), jax.ShapeDtypeStruct(s0.shape, F32)],
        compiler_params=_cparams(("arbitrary",)),
        name="wkv_scan",
    )(r, w, k, v, a, b, g, bonus, lnx_w.reshape(1, c), lnx_b.reshape(1, c), s0)


WKV_CHUNK = 64


def _bd(x):
    xb = x.astype(BF16)
    left = lax.broadcasted_iota(jnp.int32, x.shape, 1) < HEAD
    zero = jnp.zeros_like(xb)
    return jnp.concatenate([jnp.where(left, xb, zero), jnp.where(left, zero, xb)], axis=0)


def _dot_nt(a, b):
    return lax.dot_general(a, b, (((1,), (1,)), ((), ())), preferred_element_type=F32)


def _dot_tn(a, b):
    return lax.dot_general(a, b, (((0,), (0,)), ((), ())), preferred_element_type=F32)


def _wkv_chunk_local_steps(r, lw, k, v, a, b, consts):
    tri, eye2, strict, incl, blocks = consts
    c = WKV_CHUNK
    n = range(len(r))
    zero = jnp.zeros((c, LANES), F32)

    def cumsum(x):
        hi = x.astype(BF16)
        lo = (x - hi.astype(F32)).astype(BF16)
        return _dot(tri, hi) + _dot(tri, lo)

    lc = [cumsum(lw[i]) for i in n]
    yield
    w_inc = [jnp.exp(lc[i]) for i in n]
    w_inv = [1.0 / w_inc[i] for i in n]
    ah = [a[i] * jnp.exp(lc[i] - lw[i]) for i in n]
    rh = [r[i] * w_inc[i] for i in n]
    yield
    sc = [_dot_nt(jnp.concatenate([ah[i], rh[i]], axis=0).astype(BF16),
                  jnp.concatenate([_bd(b[i] * w_inv[i]), _bd(k[i] * w_inv[i])], axis=0)) for i in n]
    yield
    aab = [jnp.where(strict, sc[i][:c, :LANES], zero) for i in n]

    t = [eye2 + jnp.where(blocks[0], aab[i], zero) for i in n]
    for lvl in range(1, len(blocks)):
        grow = blocks[lvl] & ~blocks[lvl - 1]
        x = [_dot(t[i].astype(BF16), _bd(jnp.where(grow, aab[i], zero))) for i in n]
        yield
        t = [t[i] + _dot(x[i].astype(BF16), _bd(t[i])) for i in n]
        yield

    vb = [_bd(v[i]) for i in n]
    akv = [_dot(jnp.where(strict, sc[i][:c, LANES:], zero).astype(BF16), vb[i]) for i in n]
    yield
    pu = [_dot(t[i].astype(BF16), jnp.concatenate([_bd(ah[i]), _bd(akv[i])], axis=1)) for i in n]
    yield

    left2 = (lax.broadcasted_iota(jnp.int32, (HEAD, 2 * LANES), 1) % LANES) < HEAD
    g, h, q, y0 = [], [], [], []
    for i in n:
        wc = w_inc[i][c - 1:c, :]
        w_rem = wc * w_inv[i]
        lhs_t = jnp.concatenate([b[i] * w_rem, k[i] * w_rem], axis=0).astype(BF16)
        rhs = jnp.concatenate([pu[i], jnp.concatenate([zero, v[i]], axis=1)], axis=0).astype(BF16)
        full = _dot_tn(lhs_t, rhs)
        gh = jnp.where(left2, full[:HEAD], full[HEAD:])
        g.append(gh[:, :LANES] + eye2 * wc)
        h.append(gh[:, LANES:])
    yield
    zero_b = jnp.zeros((2 * c, LANES), BF16)
    for i in n:
        mrb = jnp.where(incl, sc[i][c:, :LANES], zero)
        mrk = jnp.where(incl, sc[i][c:, LANES:], zero)
        rhs2 = jnp.concatenate([jnp.concatenate([_bd(pu[i][:, :LANES]), _bd(pu[i][:, LANES:])], axis=1),
                                jnp.concatenate([zero_b, vb[i]], axis=1)], axis=0)
        qy = _dot(jnp.concatenate([mrb, mrk], axis=1).astype(BF16), rhs2)
        q.append(rh[i] + qy[:, :LANES])
        y0.append(qy[:, LANES:])
    yield
    return g, h, q, y0


def _wkv_fused_kernel(zr_ref, zk_ref, zv_ref, zl_ref, sr_ref, sk_ref, sv_ref, sl_ref,
                      mr_ref, mk_ref, mv_ref, ml_ref, w0_ref, w2_ref, a0_ref, a2_ref, g2_ref,
                      kk_ref, ka_ref, rk_ref, lnw_ref, lnb_ref, s0_ref, out_ref, st_ref,
                      s_scr, pr_scr, pk_scr, pv_scr, pl_scr, *, group_size):
    i = pl.program_id(1)
    c = WKV_CHUNK
    n_seq, tb = out_ref.shape[0], out_ref.shape[1]
    n_chunks = tb // c

    @pl.when(i == 0)
    def _():
        s_scr[...] = s0_ref[...]
        pr_scr[...] = sr_ref[...]
        pk_scr[...] = sk_ref[...]
        pv_scr[...] = sv_ref[...]
        pl_scr[...] = sl_ref[...]

    def shifted(z_ref, p_scr, mu_ref, b, lanes):
        z = z_ref[b, :, lanes]
        first = lax.broadcasted_iota(jnp.int32, z.shape, 0) == 0
        prev = jnp.where(first, p_scr[b:b + 1, lanes], pltpu.roll(z, 1, axis=0))
        p_scr[b:b + 1, lanes] = z[tb - 1:tb, :]
        return z + (prev - z) * mu_ref[:, lanes]

    ones = _pair_ones()
    c1 = w2_ref.shape[0]
    c2 = c1 + a2_ref.shape[0]
    ti = lax.broadcasted_iota(jnp.int32, (c, LANES), 0)
    si = lax.broadcasted_iota(jnp.int32, (c, LANES), 1) % HEAD
    tri = (lax.broadcasted_iota(jnp.int32, (c, c), 0) >= lax.broadcasted_iota(jnp.int32, (c, c), 1)
           ).astype(BF16)
    eye2 = (ti == si).astype(F32)
    sizes = [2 << l for l in range(c.bit_length() - 1)]
    blocks = [(ti // s) == (si // s) for s in sizes]
    consts = (tri, eye2, ti > si, ti >= si, blocks)
    lora_in = {}

    def feature_steps(group):
        out = []
        for b, sp in group:
            lanes = slice(sp * LANES, (sp + 1) * LANES)
            r = shifted(zr_ref, pr_scr, mr_ref, b, lanes)
            k = shifted(zk_ref, pk_scr, mk_ref, b, lanes)
            yield
            v = shifted(zv_ref, pv_scr, mv_ref, b, lanes)
            if b not in lora_in:
                lora_in[b] = shifted(zl_ref, pl_scr, ml_ref, b, slice(None))
            zl = lora_in[b]
            yield
            lw, k2, a_neg, bb, gate, bonus = yield from _rwkv_features_steps(
                r, k, v, zl[:, :c1], zl[:, c1:c2], zl[:, c2:], w0_ref[:, lanes], w2_ref[:, lanes],
                a0_ref[:, lanes], a2_ref[:, lanes], g2_ref[:, lanes], kk_ref[:, lanes], ka_ref[:, lanes],
                rk_ref[:, lanes], ones)
            out.append((r, lw, k2, v, a_neg, bb, gate, bonus))
        return out

    def local_steps(feat):
        chunks = lambda which: [f[which][ci * c:(ci + 1) * c, :] for ci in range(n_chunks) for f in feat]
        return (yield from _wkv_chunk_local_steps(*(chunks(which) for which in range(6)), consts))

    def tail_steps(group, feat, local):
        g, h, q, y0 = local
        st = [s_scr[b, sp] for b, sp in group]
        ys = [[] for _ in group]
        for ci in range(n_chunks):
            for j in range(len(group)):
                u = ci * len(group) + j
                out = _dot(jnp.concatenate([q[u], g[u]], axis=0).astype(BF16), _bd(st[j]))
                ys[j].append(out[:c] + y0[u])
                st[j] = out[c:] + h[u]
            yield
        for j, (b, sp) in enumerate(group):
            lanes = slice(sp * LANES, (sp + 1) * LANES)
            s_scr[b, sp] = st[j]
            out_ref[b, :, lanes] = _rwkv_out(jnp.concatenate(ys[j], axis=0), feat[j][7], feat[j][6],
                                             lnw_ref[:, lanes], lnb_ref[:, lanes], ones)
            yield

        @pl.when(i == pl.num_programs(1) - 1)
        def _():
            for j, (b, sp) in enumerate(group):
                st_ref[b, sp] = st[j]

    chains = [(b, sp) for b in range(n_seq) for sp in range(out_ref.shape[2] // LANES)]
    groups = [chains[j:j + group_size] for j in range(0, len(chains), group_size)]
    feats, locals_ = {}, {}
    for step in range(len(groups) + 2):
        gens, roles = [], []
        if 0 <= step - 1 < len(groups):
            gens.append(local_steps(feats[step - 1]))
            roles.append((locals_, step - 1))
        if step < len(groups):
            gens.append(feature_steps(groups[step]))
            roles.append((feats, step))
        if 0 <= step - 2 < len(groups):
            gens.append(tail_steps(groups[step - 2], feats[step - 2], locals_[step - 2]))
            roles.append(({}, step - 2))
        for (store, idx), val in zip(roles, _zip_stages(*gens)):
            store[idx] = val


def _wkv_fused(z_rwkv, shift0, s0, prm, tb, pp):
    n, t, cols = z_rwkv.shape
    width = prm["decay_w0"].shape[-1]
    bw = pp * LANES
    nblk = width // bw
    lora = cols - 3 * width
    assert width % bw == 0 and (3 * width) % lora == 0 and lora % LANES == 0
    lblk = 3 * width // lora
    row = lambda x: x.reshape(1, -1)
    mu = row(prm["shift_mu"])
    tok = lambda off: pl.BlockSpec((n, tb, bw), lambda p, i: (0, i, off + p))
    vec = lambda rows_, off: pl.BlockSpec((rows_, bw), lambda p, i: (0, off + p))
    st = pl.BlockSpec((n, pp, HEAD, LANES), lambda p, i: (0, p, 0, 0))
    in_specs = [tok(0), tok(nblk), tok(2 * nblk), pl.BlockSpec((n, tb, lora), lambda p, i: (0, i, lblk)),
                vec(n, 0), vec(n, nblk), vec(n, 2 * nblk), pl.BlockSpec((n, lora), lambda p, i: (0, lblk)),
                vec(1, 0), vec(1, nblk), vec(1, 2 * nblk), pl.BlockSpec((1, lora), lambda p, i: (0, lblk)),
                vec(1, 0), vec(prm["decay_w2"].shape[0], 0), vec(1, 0), vec(prm["aaa_a2"].shape[0], 0),
                vec(prm["gate_g2"].shape[0], 0), vec(1, 0), vec(1, 0), vec(1, 0), vec(1, 0), vec(1, 0), st]
    args = [z_rwkv] * 4 + [shift0] * 4 + [mu] * 4 + [
        row(prm["decay_w0"]), prm["decay_w2"], row(prm["aaa_a0"]), prm["aaa_a2"], prm["gate_g2"],
        row(prm["k_k"]), row(prm["k_a"]), row(prm["r_k"]), row(prm["lnx_w"]), row(prm["lnx_b"]), s0]
    return pl.pallas_call(
        functools.partial(_wkv_fused_kernel, group_size=2),
        grid=(nblk, t // tb),
        in_specs=in_specs,
        out_specs=[tok(0), st],
        out_shape=[jax.ShapeDtypeStruct((n, t, width), F32), jax.ShapeDtypeStruct(s0.shape, F32)],
        scratch_shapes=[pltpu.VMEM((n, pp, HEAD, LANES), F32), pltpu.VMEM((n, bw), F32),
                        pltpu.VMEM((n, bw), F32), pltpu.VMEM((n, bw), F32),
                        pltpu.VMEM((n, lora), F32)],
        compiler_params=_cparams(("arbitrary", "arbitrary")),
        name="wkv_fused",
    )(*args)


def _pack_state(s):
    n, h, vd, kd = s.shape
    return s.reshape(n, h // 2, 2, vd, kd).transpose(0, 1, 4, 2, 3).reshape(n, h // 2, kd, 2 * vd)


def _unpack_state(s, h):
    n, _, kd, _ = s.shape
    return s.reshape(n, h // 2, kd, 2, HEAD).transpose(0, 1, 3, 4, 2).reshape(n, h, HEAD, kd)


def _attn_prompt_kernel(q_ref, k_ref, kp_ref, v_ref, vp_ref, o_ref, l_ref, *, dil, sub, n_back, scale):
    first = pl.program_id(1) == 0
    qi = lax.broadcasted_iota(jnp.int32, (ATTN_BLOCK, 2 * ATTN_BLOCK), 0)
    ki = lax.broadcasted_iota(jnp.int32, (ATTN_BLOCK, 2 * ATTN_BLOCK), 1)
    dist = qi + ATTN_BLOCK - ki
    band = (dist >= 0) & (dist <= n_back)
    own = ki >= ATTN_BLOCK
    left = lax.broadcasted_iota(jnp.int32, (ATTN_BLOCK, LANES), 1) < HEAD
    ones_kv = jnp.ones((2 * ATTN_BLOCK, LANES), BF16)

    def rows(j, r):
        if dil == 1:
            return pl.ds(j * ATTN_BLOCK, ATTN_BLOCK)
        return pl.ds(j * ATTN_BLOCK * dil + r, ATTN_BLOCK, stride=dil)

    def units(jr):
        probs = []
        qs, kcat, vcat, masks = [], [], [], []
        for j, r in jr:
            q = (q_ref[0, rows(j, r), :] * scale).astype(BF16)
            if j == 0:
                kp, vp = kp_ref[0, rows(0, r), :], vp_ref[0, rows(0, r), :]
                masks.append(band & (own | jnp.logical_not(first)))
            else:
                kp, vp = k_ref[0, rows(j - 1, r), :], v_ref[0, rows(j - 1, r), :]
                masks.append(band)
            kcat.append(jnp.concatenate([kp, k_ref[0, rows(j, r), :]], axis=0).astype(BF16))
            vcat.append(jnp.concatenate([vp, v_ref[0, rows(j, r), :]], axis=0).astype(BF16))
            zq = jnp.zeros_like(q)
            qs += [jnp.where(left, q, zq), jnp.where(left, zq, q)]
            probs += [(len(kcat) - 1, 0), (len(kcat) - 1, 1)]
        s = [jnp.where(masks[u], _dot_nt(qs[p], kcat[u]), NEG) for p, (u, _) in enumerate(probs)]
        m = [jnp.max(x, axis=-1, keepdims=True) for x in s]
        e = [jnp.exp(s[p] - m[p]).astype(BF16) for p in range(len(probs))]
        o = [_dot(e[p], vcat[u]) for p, (u, _) in enumerate(probs)]
        dens = [_dot(e[p], ones_kv) for p in range(len(probs))]
        for u, (j, r) in enumerate(jr):
            den = jnp.where(left, dens[2 * u], dens[2 * u + 1])
            o_ref[0, rows(j, r), :] = jnp.where(left, o[2 * u], o[2 * u + 1]) / den
            l_ref[0, rows(j, r), :] = jnp.where(left, m[2 * u], m[2 * u + 1]) + jnp.log(den)

    if dil <= 4:
        units([(j, r) for j in range(sub) for r in range(dil)])
    else:
        for j in range(sub):
            def body(r4, carry, j=j):
                units([(j, r4 * 4 + u) for u in range(4)])
                return carry
            lax.fori_loop(0, dil // 4, body, 0)


def _attn_prompt(zqkv, gi, span):
    n, t, cols = zqkv.shape
    d = DILATIONS[gi]
    aw = ATTN_HEADS * HEAD
    ng = len(DILATIONS)
    back = ATTN_BLOCK * d
    sub = span // back
    halves = aw // LANES
    col = lambda which, hp: (which * ng + gi) * halves + hp
    cur = lambda which: pl.BlockSpec((1, span, LANES), lambda b, i, hp: (b, i, col(which, hp)))
    prev = lambda which: pl.BlockSpec((1, back, LANES),
                                      lambda b, i, hp: (b, jnp.maximum(i * sub - 1, 0), col(which, hp)))
    out_spec = pl.BlockSpec((1, span, LANES), lambda b, i, hp: (b, i, hp))
    out_sds = jax.ShapeDtypeStruct((n, t, aw), F32)
    return pl.pallas_call(
        functools.partial(_attn_prompt_kernel, dil=d, sub=sub, n_back=WINDOWS[gi] // d, scale=HEAD ** -0.5),
        grid=(n, t // span, halves),
        in_specs=[cur(0), cur(1), prev(1), cur(2), prev(2)],
        out_specs=[out_spec, out_spec],
        out_shape=[out_sds, out_sds],
        compiler_params=_cparams(("arbitrary", "arbitrary", "arbitrary")),
        name=f"attn_prompt_g{gi}",
    )(zqkv, zqkv, zqkv, zqkv, zqkv)


def _attn_sample_kernel(q_ref, k_ref, v_ref, c_ref, o_ref, l_ref, *, dil, n_back, valid, scale):
    nb, tp = q_ref.shape[0], q_ref.shape[1]
    cl = c_ref.shape[4]
    dist_c = (cl + lax.broadcasted_iota(jnp.int32, (tp, cl), 0)
              - lax.broadcasted_iota(jnp.int32, (tp, cl), 1))
    mask_c = (dist_c % dil == 0) & (dist_c <= dil * n_back)
    pn = lax.broadcasted_iota(jnp.int32, (tp, tp), 1)
    dist_n = lax.broadcasted_iota(jnp.int32, (tp, tp), 0) - pn
    mask_n = (dist_n >= 0) & (dist_n % dil == 0) & (dist_n <= dil * n_back) & (pn < valid)
    probs = [(b, h) for b in range(nb) for h in range(ATTN_HEADS)]
    sl = lambda h: slice(h * HEAD, (h + 1) * HEAD)
    q = [(q_ref[b, :, sl(h)] * scale).astype(BF16) for b, h in probs]
    s_c = [jnp.where(mask_c, _dot(q[p], c_ref[b, 0, h].astype(BF16)), NEG) for p, (b, h) in enumerate(probs)]
    s_n = [jnp.where(mask_n, _dot_nt(q[p], k_ref[b, :, sl(h)].astype(BF16)), NEG)
           for p, (b, h) in enumerate(probs)]
    m = [jnp.maximum(jnp.max(s_c[p], axis=-1, keepdims=True), jnp.max(s_n[p], axis=-1, keepdims=True))
         for p in range(len(probs))]
    e_c = [jnp.exp(s_c[p] - m[p]) for p in range(len(probs))]
    e_n = [jnp.exp(s_n[p] - m[p]) for p in range(len(probs))]
    den = [jnp.sum(e_c[p], axis=-1, keepdims=True) + jnp.sum(e_n[p], axis=-1, keepdims=True)
           for p in range(len(probs))]
    o = [_dot_nt(e_c[p].astype(BF16), c_ref[b, 1, h].astype(BF16))
         + _dot(e_n[p].astype(BF16), v_ref[b, :, sl(h)].astype(BF16)) for p, (b, h) in enumerate(probs)]
    for b in range(nb):
        ps = [p for p, (bb, _) in enumerate(probs) if bb == b]
        o_ref[b] = jnp.concatenate([o[p] / den[p] for p in ps], axis=1)
        l_ref[b] = jnp.concatenate([jnp.broadcast_to(m[p] + jnp.log(den[p]), (tp, HEAD)) for p in ps], axis=1)


def _attn_sample(zqkv, cache, gi, valid):
    n, tp, cols = zqkv.shape
    aw = ATTN_HEADS * HEAD
    ng = len(DILATIONS)
    d = DILATIONS[gi]
    cache_t = cache.transpose(0, 2, 3, 4, 1)
    nb = next(c for c in (4, 2, 1) if n % c == 0)
    spec = lambda which: pl.BlockSpec((nb, tp, aw), lambda b: (b, 0, which * ng + gi))
    out_spec = pl.BlockSpec((nb, tp, aw), lambda b: (b, 0, 0))
    out_sds = jax.ShapeDtypeStruct((n, tp, aw), F32)
    return pl.pallas_call(
        functools.partial(_attn_sample_kernel, dil=d, n_back=WINDOWS[gi] // d, valid=valid,
                          scale=HEAD ** -0.5),
        grid=(n // nb,),
        in_specs=[spec(0), spec(1), spec(2),
                  pl.BlockSpec((nb,) + cache_t.shape[1:], lambda b: (b, 0, 0, 0, 0))],
        out_specs=[out_spec, out_spec],
        out_shape=[out_sds, out_sds],
        compiler_params=_cparams(("arbitrary",)),
        name=f"attn_sample_g{gi}",
    )(zqkv, zqkv, zqkv, cache_t)


def _post_kernel(x_ref, rwkv_ref, o0, o1, o2, l0, l1, l2, zg_ref, wbr_ref, wba_ref, wout_ref, g_ref,
                 w1_ref, w2_ref, p_ref, gp_ref, wg_ref, wp_ref, gf_ref, o_ref, h_scr, acc_scr, *, final):
    j = pl.program_id(1)

    @pl.when(j == 0)
    def _():
        ls = [l0[...], l1[...], l2[...]]
        m = jnp.maximum(jnp.maximum(ls[0], ls[1]), ls[2])
        es = [jnp.exp(l - m) for l in ls]
        attn = (es[0] * o0[...] + es[1] * o1[...] + es[2] * o2[...]) / (es[0] + es[1] + es[2])
        d = x_ref.shape[1]
        gates = zg_ref[...].astype(F32)
        mixed = (gates[:, :d] * _dot(rwkv_ref[...].astype(BF16), wbr_ref[...])
                 + gates[:, d:] * _dot(attn.astype(BF16), wba_ref[...]))
        x = x_ref[...] + _dot(mixed.astype(BF16), wout_ref[...])
        acc_scr[...] = x
        h_scr[...] = _rms(x, g_ref[...]).astype(BF16)

    u = jnp.maximum(_dot(h_scr[...], w1_ref[...]), 0.0)
    acc_scr[...] += _dot((u * u).astype(BF16), w2_ref[...])

    @pl.when(j == pl.num_programs(1) - 1)
    def _():
        x = acc_scr[...]
        h = _rms(x, gp_ref[...]).astype(BF16)
        x = x + _sigmoid(_dot(h, wg_ref[...])) * _dot(p_ref[...].astype(BF16), wp_ref[...])
        if final:
            x = _rms(x, gf_ref[...])
        o_ref[...] = x


def _post(x, rwkv, os_, ls_, zg, wbr, wba, wout, g, w1, w2, p, gp, wg, wp, gf, final, tm, tf):
    rows, d = x.shape
    dff = w1.shape[1]
    vec = lambda a: a.reshape(1, d)
    rowblk = lambda a: pl.BlockSpec((tm, a.shape[1]), lambda i, j: (i, 0))
    full = lambda a: pl.BlockSpec(a.shape, lambda i, j: (0, 0), pipeline_mode=pl.Buffered(1))
    row_args = [x, rwkv, *os_, *ls_, zg]
    return pl.pallas_call(
        functools.partial(_post_kernel, final=final),
        grid=(rows // tm, dff // tf),
        in_specs=[rowblk(a) for a in row_args] + [full(wbr), full(wba), full(wout), full(vec(g)),
                  pl.BlockSpec((d, tf), lambda i, j: (0, j)),
                  pl.BlockSpec((tf, d), lambda i, j: (j, 0)),
                  rowblk(p), full(vec(gp)), full(wg), full(wp), full(vec(gf))],
        out_specs=pl.BlockSpec((tm, d), lambda i, j: (i, 0)),
        out_shape=jax.ShapeDtypeStruct((rows, d), F32),
        scratch_shapes=[pltpu.VMEM((tm, d), BF16), pltpu.VMEM((tm, d), F32)],
        compiler_params=_cparams(("arbitrary", "arbitrary")),
        name="post",
    )(*row_args, wbr, wba, wout, vec(g), w1, w2, p, vec(gp), wg, wp, vec(gf))


def _layer(x, p, wkv0, shift0, caches, prm, norm_final, final, valid):
    n, t, d = x.shape
    rows = n * t
    width = prm["decay_w0"].shape[-1]
    rwkv_cols = prm["shift_mu"].shape[-1]
    aw = ATTN_HEADS * HEAD
    qkv_cols = 3 * len(DILATIONS) * aw
    tm = min(512, rows)
    x2 = x.reshape(rows, d)

    pick = lambda cols, opts: next(o for o in opts if cols % o == 0)
    z_rwkv, z_qkv, z_gate = _in_proj(x2, prm["norm_mix"], prm["w_in"].astype(BF16),
                                     (rwkv_cols, qkv_cols, 2 * d), tm)
    z_rwkv = z_rwkv.reshape(n, t, rwkv_cols)
    z_qkv = z_qkv.reshape(n, t, qkv_cols)

    if valid == t and t % WKV_CHUNK == 0:
        rwkv, s_t = _wkv_fused(z_rwkv, shift0, _pack_state(wkv0), prm, pick(t, (512, 256, WKV_CHUNK)), 8)
        wkv_t = _unpack_state(s_t, width // HEAD)
    else:
        feats = _rwkv_prep(z_rwkv, shift0, prm, SUBLANES if n % SUBLANES == 0 else 1)
        rwkv, wkv_t = _wkv(*feats, prm["lnx_w"], prm["lnx_b"], wkv0, valid)
    shift_t = z_rwkv[:, valid - 1, :]

    os_, ls_, new_kv = [], [], []
    ng = len(DILATIONS)
    for gi in range(ng):
        if caches is None:
            o, l = _attn_prompt(z_qkv, gi, max(1024, ATTN_BLOCK * DILATIONS[gi]))
            keep = min(WINDOWS[gi], t)
            kv_rows = z_qkv[:, t - keep:t]
        else:
            o, l = _attn_sample(z_qkv, caches[gi], gi, valid)
            kv_rows = z_qkv[:, :valid]
        new_kv.append(jnp.stack([kv_rows[:, :, (which * ng + gi) * aw:(which * ng + gi + 1) * aw]
                                 .reshape(n, -1, ATTN_HEADS, HEAD) for which in (1, 2)], axis=2))
        os_.append(o.reshape(rows, aw))
        ls_.append(l.reshape(rows, aw))

    bf = lambda name: prm[name].astype(BF16)
    x2 = _post(x2, rwkv.reshape(rows, -1), os_, ls_, z_gate, bf("w_branch_rwkv"), bf("w_branch_attn"),
               bf("w_out"), prm["norm_mlp"], bf("w_ff1"), bf("w_ff2"), p.reshape(rows, -1),
               prm["norm_ple"], bf("w_ple_gate"), bf("w_ple"), norm_final, final, tm, 2048)
    return x2.reshape(n, t, d), new_kv, wkv_t, shift_t


_LAYER_PARAMS = ("norm_mix", "w_in", "shift_mu", "decay_w0", "decay_w2", "aaa_a0", "aaa_a2", "gate_g2",
                 "k_k", "k_a", "r_k", "lnx_w", "lnx_b", "w_branch_rwkv", "w_branch_attn", "w_out",
                 "norm_mlp", "w_ff1", "w_ff2", "norm_ple", "w_ple", "w_ple_gate")


def kernel(x_prompt, x_sample, cache_kv_g0, cache_kv_g1, cache_kv_g2, state_wkv, state_shift, p_prompt, p_sample, norm_mix, w_in, shift_mu, decay_w0, decay_w2, aaa_a0, aaa_a2, gate_g2, k_k, k_a, r_k, lnx_w, lnx_b, w_branch_rwkv, w_branch_attn, w_out, norm_mlp, w_ff1, w_ff2, norm_ple, w_ple, w_ple_gate, norm_final):
    stacked = dict(zip(_LAYER_PARAMS, (norm_mix, w_in, shift_mu, decay_w0, decay_w2, aaa_a0, aaa_a2,
                                       gate_g2, k_k, k_a, r_k, lnx_w, lnx_b, w_branch_rwkv,
                                       w_branch_attn, w_out, norm_mlp, w_ff1, w_ff2, norm_ple, w_ple,
                                       w_ple_gate)))
    depth = norm_mix.shape[0]
    nb_p, _, d = x_prompt.shape
    nb_s, t_s, _ = x_sample.shape
    heads = decay_w0.shape[-1] // HEAD
    rwkv_cols = shift_mu.shape[-1]
    t_pad = -(-t_s // SUBLANES) * SUBLANES
    pad_t = lambda u: jnp.pad(u, ((0, 0), (0, t_pad - t_s), (0, 0)))

    yp, ys = x_prompt, pad_t(x_sample)
    kvp, kvs = [[], [], []], [[], [], []]
    wkv_p, sh_p, wkv_s, sh_s = [], [], [], []
    for i in range(depth):
        prm = {name: val[i] for name, val in stacked.items()}
        prm["r_k"] = prm["r_k"].reshape(-1)
        last = i == depth - 1
        wkv0 = jnp.zeros((nb_p, heads, HEAD, HEAD), F32)
        shift0 = jnp.zeros((nb_p, rwkv_cols), F32)
        yp, nkv_p, w_p, s_p = _layer(yp, p_prompt[i], wkv0, shift0, None, prm, norm_final, last,
                                     x_prompt.shape[1])
        ys, nkv_s, w_s, s_s = _layer(ys, pad_t(p_sample[i]), state_wkv[i], state_shift[i],
                                     (cache_kv_g0[i], cache_kv_g1[i], cache_kv_g2[i]), prm, norm_final,
                                     last, t_s)
        for gi in range(3):
            kvp[gi].append(nkv_p[gi])
            kvs[gi].append(nkv_s[gi])
        wkv_p.append(w_p)
        sh_p.append(s_p)
        wkv_s.append(w_s)
        sh_s.append(s_s)
    return (yp, ys[:, :t_s],
            jnp.stack(kvp[0]), jnp.stack(kvp[1]), jnp.stack(kvp[2]),
            jnp.stack(wkv_p), jnp.stack(sh_p),
            jnp.stack(kvs[0]), jnp.stack(kvs[1]), jnp.stack(kvs[2]),
            jnp.stack(wkv_s), jnp.stack(sh_s))
```

```python
import functools

import jax
import jax.numpy as jnp
from jax import lax
from jax.experimental import pallas as pl
from jax.experimental.pallas import tpu as pltpu

F32 = jnp.float32
BF16 = jnp.bfloat16

HEAD = 64
LANES = 128
SUBLANES = 8
ATTN_HEADS = 4
ATTN_BLOCK = 128
WINDOWS = (128, 512, 2048)
DILATIONS = (1, 4, 16)
NORM_EPS = 1e-6
LNX_EPS = 64e-5
NEG = -1e30
VMEM_LIMIT = 56 * 1024 * 1024


def _cparams(sem):
    return pltpu.CompilerParams(dimension_semantics=sem, vmem_limit_bytes=VMEM_LIMIT)


def _dot(a, b):
    return jnp.dot(a, b, preferred_element_type=F32)


def _dot_bf16(a, b):
    return _dot(a.astype(BF16), b.astype(BF16))


def _pair_ones():
    r = lax.broadcasted_iota(jnp.int32, (LANES, LANES), 0) // HEAD
    c = lax.broadcasted_iota(jnp.int32, (LANES, LANES), 1) // HEAD
    return (r == c).astype(BF16)


def _head_sum(x, ones):
    outs = []
    for c in range(x.shape[1] // LANES):
        outs.append(_dot(x[:, c * LANES:(c + 1) * LANES].astype(BF16), ones))
    return jnp.concatenate(outs, axis=1)


def _rms(x, g):
    return x * lax.rsqrt(jnp.mean(x * x, axis=-1, keepdims=True) + NORM_EPS) * g


def _sigmoid(x):
    return 1.0 / (1.0 + jnp.exp(-x))


DECAY_SCALE = 0.6065306597126334


def _in_proj_kernel(x_ref, g_ref, w_ref, zr_ref, zq_ref, gate_ref):
    h = _rms(x_ref[...], g_ref[...]).astype(BF16)
    c1 = zr_ref.shape[1]
    c2 = c1 + zq_ref.shape[1]
    zr_ref[...] = _dot(h, w_ref[:, :c1])
    zq_ref[...] = _dot(h, w_ref[:, c1:c2])
    gate_ref[...] = _sigmoid(_dot(h, w_ref[:, c2:])).astype(BF16)


def _in_proj(x, g, w, splits, tm):
    rows, d = x.shape
    assert sum(splits) == w.shape[1] and len(splits) == 3
    return pl.pallas_call(
        _in_proj_kernel,
        grid=(rows // tm,),
        in_specs=[pl.BlockSpec((tm, d), lambda i: (i, 0)),
                  pl.BlockSpec((1, d), lambda i: (0, 0)),
                  pl.BlockSpec(w.shape, lambda i: (0, 0), pipeline_mode=pl.Buffered(1))],
        out_specs=[pl.BlockSpec((tm, c), lambda i: (i, 0)) for c in splits],
        out_shape=[jax.ShapeDtypeStruct((rows, c), dt) for c, dt in zip(splits, (F32, F32, BF16))],
        compiler_params=_cparams(("arbitrary",)),
        name="in_proj",
    )(x, g.reshape(1, d), w)


def _drain(steps):
    try:
        while True:
            next(steps)
    except StopIteration as stop:
        return stop.value


def _zip_stages(*gens):
    vals = [None] * len(gens)
    live = list(range(len(gens)))
    while live:
        for i in list(live):
            try:
                next(gens[i])
            except StopIteration as stop:
                vals[i] = stop.value
                live.remove(i)
    return vals


def _rwkv_features_steps(r, k, v, xw, xa, xg, w0, w2, a0, a2, g2, k_k, k_a, r_k, ones):
    pre = w0 + _dot_bf16(jnp.tanh(xw), w2)
    yield
    lw = _sigmoid(pre) * (-DECAY_SCALE)
    yield
    a = _sigmoid(a0 + _dot_bf16(xa, a2))
    yield
    g = _dot_bf16(_sigmoid(xg), g2)
    yield
    kk = k * k_k
    norm2 = _head_sum(kk * kk, ones)
    yield
    kk = kk * lax.rsqrt(jnp.maximum(norm2, 1e-24))
    k2 = k * (1.0 + (a - 1.0) * k_a)
    yield
    bonus = _head_sum(r * k2 * r_k, ones) * v
    yield
    return lw, k2, -kk, kk * a, g, bonus


def _rwkv_features(*args):
    return _drain(_rwkv_features_steps(*args))


def _rwkv_out(y, bonus, g, lnx_w, lnx_b, ones):
    mu = _head_sum(y, ones) * (1.0 / HEAD)
    yc = y - mu
    var = _head_sum(yc * yc, ones) * (1.0 / HEAD)
    return (yc * lax.rsqrt(var + LNX_EPS) * lnx_w + lnx_b + bonus) * g


def _rwkv_prep_kernel(z_ref, sh_ref, mu_ref, w0_ref, w2_ref, a0_ref, a2_ref, g2_ref, kk_ref, ka_ref,
                      rk_ref, r_o, w_o, k_o, v_o, a_o, b_o, g_o, bon_o, *, width):
    nb, t, cols = z_ref.shape
    first = lax.broadcasted_iota(jnp.int32, (t, cols), 0) == 0
    zs = []
    for b in range(nb):
        z = z_ref[b]
        prev = jnp.where(first, sh_ref[b:b + 1, :], pltpu.roll(z, 1, axis=0))
        zs.append(z + (prev - z) * mu_ref[...])
    zs = jnp.concatenate(zs, axis=0)

    c0 = 3 * width
    c1 = c0 + w2_ref.shape[0]
    c2 = c1 + a2_ref.shape[0]
    r, k, v = zs[:, :width], zs[:, width:2 * width], zs[:, 2 * width:c0]
    lw, k2, a_neg, b, g, bonus = _rwkv_features(
        r, k, v, zs[:, c0:c1], zs[:, c1:c2], zs[:, c2:], w0_ref[...], w2_ref[...], a0_ref[...],
        a2_ref[...], g2_ref[...], kk_ref[...], ka_ref[...], rk_ref[...], _pair_ones())
    for ref, val in ((r_o, r), (w_o, lw), (k_o, k2), (v_o, v), (a_o, a_neg), (b_o, b), (g_o, g), (bon_o, bonus)):
        for bi in range(nb):
            ref[bi] = val[bi * t:(bi + 1) * t, :]


def _rwkv_prep(z_rwkv, shift0, prm, nb):
    n, t, cols = z_rwkv.shape
    width = prm["decay_w0"].shape[-1]
    assert n % nb == 0
    row = lambda x: x.reshape(1, -1)
    full = lambda a: pl.BlockSpec(a.shape, lambda i: (0,) * a.ndim)
    args = [row(prm["shift_mu"]), row(prm["decay_w0"]), prm["decay_w2"], row(prm["aaa_a0"]),
            prm["aaa_a2"], prm["gate_g2"], row(prm["k_k"]), row(prm["k_a"]), row(prm["r_k"])]
    out_spec = pl.BlockSpec((nb, t, width), lambda i: (i, 0, 0))
    out_sds = jax.ShapeDtypeStruct((n, t, width), F32)
    return pl.pallas_call(
        functools.partial(_rwkv_prep_kernel, width=width),
        grid=(n // nb,),
        in_specs=[pl.BlockSpec((nb, t, cols), lambda i: (i, 0, 0)), pl.BlockSpec((nb, cols), lambda i: (i, 0))]
                 + [full(a) for a in args],
        out_specs=[out_spec] * 8,
        out_shape=[out_sds] * 8,
        compiler_params=_cparams(("arbitrary",)),
        name="rwkv_prep",
    )(z_rwkv, shift0, *args)


def _wkv_kernel(r_ref, w_ref, k_ref, v_ref, a_ref, b_ref, g_ref, bon_ref, lnw_ref, lnb_ref, s0_ref,
                y_ref, st_ref, *, valid):
    ones = _pair_ones()
    row = lax.broadcasted_iota(jnp.int32, (HEAD, LANES), 0)
    lane = lax.broadcasted_iota(jnp.int32, (HEAD, LANES), 1)
    left = lane < HEAD
    diag = (row == lane % HEAD).astype(F32)

    def key_sum(x):
        hi = x.astype(BF16)
        lo = (x - hi.astype(F32)).astype(BF16)
        return _dot(hi, ones) + _dot(lo, ones)

    chains = [(b, p) for b in range(y_ref.shape[0]) for p in range(s0_ref.shape[1] // 2)]
    cols = lambda p: slice(p * LANES, (p + 1) * LANES)
    vt = [v_ref[b, :, cols(p)].T for b, p in chains]
    s = [jnp.concatenate([s0_ref[b, 2 * p], s0_ref[b, 2 * p + 1]], axis=1) for b, p in chains]
    ys = [[] for _ in chains]
    for j in range(y_ref.shape[1]):
        for ch, (b, p) in enumerate(chains):
            if j < valid:
                row_of = lambda ref: ref[b, j:j + 1, cols(p)]
                v_col = jnp.where(left, vt[ch][0:HEAD, j:j + 1], vt[ch][HEAD:2 * HEAD, j:j + 1])
                sa = key_sum(s[ch] * row_of(a_ref))
                s[ch] = s[ch] * jnp.exp(row_of(w_ref)) + sa * row_of(b_ref) + v_col * row_of(k_ref)
                y_col = key_sum(s[ch] * row_of(r_ref))
                ys[ch].append(jnp.sum(y_col * diag, axis=0, keepdims=True))
            else:
                ys[ch].append(jnp.zeros((1, LANES), F32))
    for ch, (b, p) in enumerate(chains):
        y_ref[b, :, cols(p)] = _rwkv_out(jnp.concatenate(ys[ch], axis=0), bon_ref[b, :, cols(p)],
                                         g_ref[b, :, cols(p)], lnw_ref[:, cols(p)], lnb_ref[:, cols(p)], ones)
        st_ref[b, 2 * p] = s[ch][:, :HEAD]
        st_ref[b, 2 * p + 1] = s[ch][:, HEAD:]


def _wkv(r, w, k, v, a, b, g, bonus, lnx_w, lnx_b, s0, valid):
    n, t, c = r.shape
    assert t == SUBLANES
    nb = next(c for c in (4, 2, 1) if n % c == 0)
    tok = pl.BlockSpec((nb, t, c), lambda bi: (bi, 0, 0))
    vec = pl.BlockSpec((1, c), lambda bi: (0, 0))
    st = pl.BlockSpec((nb,) + s0.shape[1:], lambda bi: (bi, 0, 0, 0))
    return pl.pallas_call(
        functools.partial(_wkv_kernel, valid=valid),
        grid=(n // nb,),
        in_specs=[tok] * 8 + [vec, vec, st],
        out_specs=[tok, st],
        out_shape=[jax.ShapeDtypeStruct((n, t, c), F32), jax.ShapeDtypeStruct(s0.shape, F32)],
        compiler_params=_cparams(("arbitrary",)),
        name="wkv_scan",
    )(r, w, k, v, a, b, g, bonus, lnx_w.reshape(1, c), lnx_b.reshape(1, c), s0)


WKV_CHUNK = 64


def _bd(x):
    xb = x.astype(BF16)
    left = lax.broadcasted_iota(jnp.int32, x.shape, 1) < HEAD
    zero = jnp.zeros_like(xb)
    return jnp.concatenate([jnp.where(left, xb, zero), jnp.where(left, zero, xb)], axis=0)


def _dot_nt(a, b):
    return lax.dot_general(a, b, (((1,), (1,)), ((), ())), preferred_element_type=F32)


def _dot_tn(a, b):
    return lax.dot_general(a, b, (((0,), (0,)), ((), ())), preferred_element_type=F32)


def _wkv_chunk_local_steps(r, lw, k, v, a, b, consts):
    tri, eye2, strict, incl, blocks = consts
    c = WKV_CHUNK
    n = range(len(r))
    zero = jnp.zeros((c, LANES), F32)

    def cumsum(x):
        hi = x.astype(BF16)
        lo = (x - hi.astype(F32)).astype(BF16)
        return _dot(tri, hi) + _dot(tri, lo)

    lc = [cumsum(lw[i]) for i in n]
    yield
    w_inc = [jnp.exp(lc[i]) for i in n]
    w_inv = [1.0 / w_inc[i] for i in n]
    ah = [a[i] * jnp.exp(lc[i] - lw[i]) for i in n]
    rh = [r[i] * w_inc[i] for i in n]
    yield
    sc = [_dot_nt(jnp.concatenate([ah[i], rh[i]], axis=0).astype(BF16),
                  jnp.concatenate([_bd(b[i] * w_inv[i]), _bd(k[i] * w_inv[i])], axis=0)) for i in n]
    yield
    aab = [jnp.where(strict, sc[i][:c, :LANES], zero) for i in n]

    t = [eye2 + jnp.where(blocks[0], aab[i], zero) for i in n]
    for lvl in range(1, len(blocks)):
        grow = blocks[lvl] & ~blocks[lvl - 1]
        x = [_dot(t[i].astype(BF16), _bd(jnp.where(grow, aab[i], zero))) for i in n]
        yield
        t = [t[i] + _dot(x[i].astype(BF16), _bd(t[i])) for i in n]
        yield

    vb = [_bd(v[i]) for i in n]
    akv = [_dot(jnp.where(strict, sc[i][:c, LANES:], zero).astype(BF16), vb[i]) for i in n]
    yield
    pu = [_dot(t[i].astype(BF16), jnp.concatenate([_bd(ah[i]), _bd(akv[i])], axis=1)) for i in n]
    yield

    left2 = (lax.broadcasted_iota(jnp.int32, (HEAD, 2 * LANES), 1) % LANES) < HEAD
    g, h, q, y0 = [], [], [], []
    for i in n:
        wc = w_inc[i][c - 1:c, :]
        w_rem = wc * w_inv[i]
        lhs_t = jnp.concatenate([b[i] * w_rem, k[i] * w_rem], axis=0).astype(BF16)
        rhs = jnp.concatenate([pu[i], jnp.concatenate([zero, v[i]], axis=1)], axis=0).astype(BF16)
        full = _dot_tn(lhs_t, rhs)
        gh = jnp.where(left2, full[:HEAD], full[HEAD:])
        g.append(gh[:, :LANES] + eye2 * wc)
        h.append(gh[:, LANES:])
    yield
    zero_b = jnp.zeros((2 * c, LANES), BF16)
    for i in n:
        mrb = jnp.where(incl, sc[i][c:, :LANES], zero)
        mrk = jnp.where(incl, sc[i][c:, LANES:], zero)
        rhs2 = jnp.concatenate([jnp.concatenate([_bd(pu[i][:, :LANES]), _bd(pu[i][:, LANES:])], axis=1),
                                jnp.concatenate([zero_b, vb[i]], axis=1)], axis=0)
        qy = _dot(jnp.concatenate([mrb, mrk], axis=1).astype(BF16), rhs2)
        q.append(rh[i] + qy[:, :LANES])
        y0.append(qy[:, LANES:])
    yield
    return g, h, q, y0


def _wkv_fused_kernel(zr_ref, zk_ref, zv_ref, zl_ref, sr_ref, sk_ref, sv_ref, sl_ref,
                      mr_ref, mk_ref, mv_ref, ml_ref, w0_ref, w2_ref, a0_ref, a2_ref, g2_ref,
                      kk_ref, ka_ref, rk_ref, lnw_ref, lnb_ref, s0_ref, out_ref, st_ref,
                      s_scr, pr_scr, pk_scr, pv_scr, pl_scr, *, group_size):
    i = pl.program_id(1)
    c = WKV_CHUNK
    n_seq, tb = out_ref.shape[0], out_ref.shape[1]
    n_chunks = tb // c

    @pl.when(i == 0)
    def _():
        s_scr[...] = s0_ref[...]
        pr_scr[...] = sr_ref[...]
        pk_scr[...] = sk_ref[...]
        pv_scr[...] = sv_ref[...]
        pl_scr[...] = sl_ref[...]

    def shifted(z_ref, p_scr, mu_ref, b, lanes):
        z = z_ref[b, :, lanes]
        first = lax.broadcasted_iota(jnp.int32, z.shape, 0) == 0
        prev = jnp.where(first, p_scr[b:b + 1, lanes], pltpu.roll(z, 1, axis=0))
        p_scr[b:b + 1, lanes] = z[tb - 1:tb, :]
        return z + (prev - z) * mu_ref[:, lanes]

    ones = _pair_ones()
    c1 = w2_ref.shape[0]
    c2 = c1 + a2_ref.shape[0]
    ti = lax.broadcasted_iota(jnp.int32, (c, LANES), 0)
    si = lax.broadcasted_iota(jnp.int32, (c, LANES), 1) % HEAD
    tri = (lax.broadcasted_iota(jnp.int32, (c, c), 0) >= lax.broadcasted_iota(jnp.int32, (c, c), 1)
           ).astype(BF16)
    eye2 = (ti == si).astype(F32)
    sizes = [2 << l for l in range(c.bit_length() - 1)]
    blocks = [(ti // s) == (si // s) for s in sizes]
    consts = (tri, eye2, ti > si, ti >= si, blocks)
    lora_in = {}

    def feature_steps(group):
        out = []
        for b, sp in group:
            lanes = slice(sp * LANES, (sp + 1) * LANES)
            r = shifted(zr_ref, pr_scr, mr_ref, b, lanes)
            k = shifted(zk_ref, pk_scr, mk_ref, b, lanes)
            yield
            v = shifted(zv_ref, pv_scr, mv_ref, b, lanes)
            if b not in lora_in:
                lora_in[b] = shifted(zl_ref, pl_scr, ml_ref, b, slice(None))
            zl = lora_in[b]
            yield
            lw, k2, a_neg, bb, gate, bonus = yield from _rwkv_features_steps(
                r, k, v, zl[:, :c1], zl[:, c1:c2], zl[:, c2:], w0_ref[:, lanes], w2_ref[:, lanes],
                a0_ref[:, lanes], a2_ref[:, lanes], g2_ref[:, lanes], kk_ref[:, lanes], ka_ref[:, lanes],
                rk_ref[:, lanes], ones)
            out.append((r, lw, k2, v, a_neg, bb, gate, bonus))
        return out

    def local_steps(feat):
        chunks = lambda which: [f[which][ci * c:(ci + 1) * c, :] for ci in range(n_chunks) for f in feat]
        return (yield from _wkv_chunk_local_steps(*(chunks(which) for which in range(6)), consts))

    def tail_steps(group, feat, local):
        g, h, q, y0 = local
        st = [s_scr[b, sp] for b, sp in group]
        ys = [[] for _ in group]
        for ci in range(n_chunks):
            for j in range(len(group)):
                u = ci * len(group) + j
                out = _dot(jnp.concatenate([q[u], g[u]], axis=0).astype(BF16), _bd(st[j]))
                ys[j].append(out[:c] + y0[u])
                st[j] = out[c:] + h[u]
            yield
        for j, (b, sp) in enumerate(group):
            lanes = slice(sp * LANES, (sp + 1) * LANES)
            s_scr[b, sp] = st[j]
            out_ref[b, :, lanes] = _rwkv_out(jnp.concatenate(ys[j], axis=0), feat[j][7], feat[j][6],
                                             lnw_ref[:, lanes], lnb_ref[:, lanes], ones)
            yield

        @pl.when(i == pl.num_programs(1) - 1)
        def _():
            for j, (b, sp) in enumerate(group):
                st_ref[b, sp] = st[j]

    chains = [(b, sp) for b in range(n_seq) for sp in range(out_ref.shape[2] // LANES)]
    groups = [chains[j:j + group_size] for j in range(0, len(chains), group_size)]
    feats, locals_ = {}, {}
    for step in range(len(groups) + 2):
        gens, roles = [], []
        if 0 <= step - 1 < len(groups):
            gens.append(local_steps(feats[step - 1]))
            roles.append((locals_, step - 1))
        if step < len(groups):
            gens.append(feature_steps(groups[step]))
            roles.append((feats, step))
        if 0 <= step - 2 < len(groups):
            gens.append(tail_steps(groups[step - 2], feats[step - 2], locals_[step - 2]))
            roles.append(({}, step - 2))
        for (store, idx), val in zip(roles, _zip_stages(*gens)):
            store[idx] = val


def _wkv_fused(z_rwkv, shift0, s0, prm, tb, pp):
    n, t, cols = z_rwkv.shape
    width = prm["decay_w0"].shape[-1]
    bw = pp * LANES
    nblk = width // bw
    lora = cols - 3 * width
    assert width % bw == 0 and (3 * width) % lora == 0 and lora % LANES == 0
    lblk = 3 * width // lora
    row = lambda x: x.reshape(1, -1)
    mu = row(prm["shift_mu"])
    tok = lambda off: pl.BlockSpec((n, tb, bw), lambda p, i: (0, i, off + p))
    vec = lambda rows_, off: pl.BlockSpec((rows_, bw), lambda p, i: (0, off + p))
    st = pl.BlockSpec((n, pp, HEAD, LANES), lambda p, i: (0, p, 0, 0))
    in_specs = [tok(0), tok(nblk), tok(2 * nblk), pl.BlockSpec((n, tb, lora), lambda p, i: (0, i, lblk)),
                vec(n, 0), vec(n, nblk), vec(n, 2 * nblk), pl.BlockSpec((n, lora), lambda p, i: (0, lblk)),
                vec(1, 0), vec(1, nblk), vec(1, 2 * nblk), pl.BlockSpec((1, lora), lambda p, i: (0, lblk)),
                vec(1, 0), vec(prm["decay_w2"].shape[0], 0), vec(1, 0), vec(prm["aaa_a2"].shape[0], 0),
                vec(prm["gate_g2"].shape[0], 0), vec(1, 0), vec(1, 0), vec(1, 0), vec(1, 0), vec(1, 0), st]
    args = [z_rwkv] * 4 + [shift0] * 4 + [mu] * 4 + [
        row(prm["decay_w0"]), prm["decay_w2"], row(prm["aaa_a0"]), prm["aaa_a2"], prm["gate_g2"],
        row(prm["k_k"]), row(prm["k_a"]), row(prm["r_k"]), row(prm["lnx_w"]), row(prm["lnx_b"]), s0]
    return pl.pallas_call(
        functools.partial(_wkv_fused_kernel, group_size=2),
        grid=(nblk, t // tb),
        in_specs=in_specs,
        out_specs=[tok(0), st],
        out_shape=[jax.ShapeDtypeStruct((n, t, width), F32), jax.ShapeDtypeStruct(s0.shape, F32)],
        scratch_shapes=[pltpu.VMEM((n, pp, HEAD, LANES), F32), pltpu.VMEM((n, bw), F32),
                        pltpu.VMEM((n, bw), F32), pltpu.VMEM((n, bw), F32),
                        pltpu.VMEM((n, lora), F32)],
        compiler_params=_cparams(("arbitrary", "arbitrary")),
        name="wkv_fused",
    )(*args)


def _pack_state(s):
    n, h, vd, kd = s.shape
    return s.reshape(n, h // 2, 2, vd, kd).transpose(0, 1, 4, 2, 3).reshape(n, h // 2, kd, 2 * vd)


def _unpack_state(s, h):
    n, _, kd, _ = s.shape
    return s.reshape(n, h // 2, kd, 2, HEAD).transpose(0, 1, 3, 4, 2).reshape(n, h, HEAD, kd)


def _attn_prompt_kernel(q_ref, k_ref, kp_ref, v_ref, vp_ref, o_ref, l_ref, *, dil, sub, n_back, scale):
    first = pl.program_id(1) == 0
    qi = lax.broadcasted_iota(jnp.int32, (ATTN_BLOCK, 2 * ATTN_BLOCK), 0)
    ki = lax.broadcasted_iota(jnp.int32, (ATTN_BLOCK, 2 * ATTN_BLOCK), 1)
    dist = qi + ATTN_BLOCK - ki
    band = (dist >= 0) & (dist <= n_back)
    own = ki >= ATTN_BLOCK
    left = lax.broadcasted_iota(jnp.int32, (ATTN_BLOCK, LANES), 1) < HEAD
    ones_kv = jnp.ones((2 * ATTN_BLOCK, LANES), BF16)

    def rows(j, r):
        if dil == 1:
            return pl.ds(j * ATTN_BLOCK, ATTN_BLOCK)
        return pl.ds(j * ATTN_BLOCK * dil + r, ATTN_BLOCK, stride=dil)

    def units(jr):
        probs = []
        qs, kcat, vcat, masks = [], [], [], []
        for j, r in jr:
            q = (q_ref[0, rows(j, r), :] * scale).astype(BF16)
            if j == 0:
                kp, vp = kp_ref[0, rows(0, r), :], vp_ref[0, rows(0, r), :]
                masks.append(band & (own | jnp.logical_not(first)))
            else:
                kp, vp = k_ref[0, rows(j - 1, r), :], v_ref[0, rows(j - 1, r), :]
                masks.append(band)
            kcat.append(jnp.concatenate([kp, k_ref[0, rows(j, r), :]], axis=0).astype(BF16))
            vcat.append(jnp.concatenate([vp, v_ref[0, rows(j, r), :]], axis=0).astype(BF16))
            zq = jnp.zeros_like(q)
            qs += [jnp.where(left, q, zq), jnp.where(left, zq, q)]
            probs += [(len(kcat) - 1, 0), (len(kcat) - 1, 1)]
        s = [jnp.where(masks[u], _dot_nt(qs[p], kcat[u]), NEG) for p, (u, _) in enumerate(probs)]
        m = [jnp.max(x, axis=-1, keepdims=True) for x in s]
        e = [jnp.exp(s[p] - m[p]).astype(BF16) for p in range(len(probs))]
        o = [_dot(e[p], vcat[u]) for p, (u, _) in enumerate(probs)]
        dens = [_dot(e[p], ones_kv) for p in range(len(probs))]
        for u, (j, r) in enumerate(jr):
            den = jnp.where(left, dens[2 * u], dens[2 * u + 1])
            o_ref[0, rows(j, r), :] = jnp.where(left, o[2 * u], o[2 * u + 1]) / den
            l_ref[0, rows(j, r), :] = jnp.where(left, m[2 * u], m[2 * u + 1]) + jnp.log(den)

    if dil <= 4:
        units([(j, r) for j in range(sub) for r in range(dil)])
    else:
        for j in range(sub):
            def body(r4, carry, j=j):
                units([(j, r4 * 4 + u) for u in range(4)])
                return carry
            lax.fori_loop(0, dil // 4, body, 0)


def _attn_prompt(zqkv, gi, span):
    n, t, cols = zqkv.shape
    d = DILATIONS[gi]
    aw = ATTN_HEADS * HEAD
    ng = len(DILATIONS)
    back = ATTN_BLOCK * d
    sub = span // back
    halves = aw // LANES
    col = lambda which, hp: (which * ng + gi) * halves + hp
    cur = lambda which: pl.BlockSpec((1, span, LANES), lambda b, i, hp: (b, i, col(which, hp)))
    prev = lambda which: pl.BlockSpec((1, back, LANES),
                                      lambda b, i, hp: (b, jnp.maximum(i * sub - 1, 0), col(which, hp)))
    out_spec = pl.BlockSpec((1, span, LANES), lambda b, i, hp: (b, i, hp))
    out_sds = jax.ShapeDtypeStruct((n, t, aw), F32)
    return pl.pallas_call(
        functools.partial(_attn_prompt_kernel, dil=d, sub=sub, n_back=WINDOWS[gi] // d, scale=HEAD ** -0.5),
        grid=(n, t // span, halves),
        in_specs=[cur(0), cur(1), prev(1), cur(2), prev(2)],
        out_specs=[out_spec, out_spec],
        out_shape=[out_sds, out_sds],
        compiler_params=_cparams(("arbitrary", "arbitrary", "arbitrary")),
        name=f"attn_prompt_g{gi}",
    )(zqkv, zqkv, zqkv, zqkv, zqkv)


def _attn_sample_kernel(q_ref, k_ref, v_ref, c_ref, o_ref, l_ref, *, dil, n_back, valid, scale):
    nb, tp = q_ref.shape[0], q_ref.shape[1]
    cl = c_ref.shape[4]
    dist_c = (cl + lax.broadcasted_iota(jnp.int32, (tp, cl), 0)
              - lax.broadcasted_iota(jnp.int32, (tp, cl), 1))
    mask_c = (dist_c % dil == 0) & (dist_c <= dil * n_back)
    pn = lax.broadcasted_iota(jnp.int32, (tp, tp), 1)
    dist_n = lax.broadcasted_iota(jnp.int32, (tp, tp), 0) - pn
    mask_n = (dist_n >= 0) & (dist_n % dil == 0) & (dist_n <= dil * n_back) & (pn < valid)
    probs = [(b, h) for b in range(nb) for h in range(ATTN_HEADS)]
    sl = lambda h: slice(h * HEAD, (h + 1) * HEAD)
    q = [(q_ref[b, :, sl(h)] * scale).astype(BF16) for b, h in probs]
    s_c = [jnp.where(mask_c, _dot(q[p], c_ref[b, 0, h].astype(BF16)), NEG) for p, (b, h) in enumerate(probs)]
    s_n = [jnp.where(mask_n, _dot_nt(q[p], k_ref[b, :, sl(h)].astype(BF16)), NEG)
           for p, (b, h) in enumerate(probs)]
    m = [jnp.maximum(jnp.max(s_c[p], axis=-1, keepdims=True), jnp.max(s_n[p], axis=-1, keepdims=True))
         for p in range(len(probs))]
    e_c = [jnp.exp(s_c[p] - m[p]) for p in range(len(probs))]
    e_n = [jnp.exp(s_n[p] - m[p]) for p in range(len(probs))]
    den = [jnp.sum(e_c[p], axis=-1, keepdims=True) + jnp.sum(e_n[p], axis=-1, keepdims=True)
           for p in range(len(probs))]
    o = [_dot_nt(e_c[p].astype(BF16), c_ref[b, 1, h].astype(BF16))
         + _dot(e_n[p].astype(BF16), v_ref[b, :, sl(h)].astype(BF16)) for p, (b, h) in enumerate(probs)]
    for b in range(nb):
        ps = [p for p, (bb, _) in enumerate(probs) if bb == b]
        o_ref[b] = jnp.concatenate([o[p] / den[p] for p in ps], axis=1)
        l_ref[b] = jnp.concatenate([jnp.broadcast_to(m[p] + jnp.log(den[p]), (tp, HEAD)) for p in ps], axis=1)


def _attn_sample(zqkv, cache, gi, valid):
    n, tp, cols = zqkv.shape
    aw = ATTN_HEADS * HEAD
    ng = len(DILATIONS)
    d = DILATIONS[gi]
    cache_t = cache.transpose(0, 2, 3, 4, 1)
    nb = next(c for c in (4, 2, 1) if n % c == 0)
    spec = lambda which: pl.BlockSpec((nb, tp, aw), lambda b: (b, 0, which * ng + gi))
    out_spec = pl.BlockSpec((nb, tp, aw), lambda b: (b, 0, 0))
    out_sds = jax.ShapeDtypeStruct((n, tp, aw), F32)
    return pl.pallas_call(
        functools.partial(_attn_sample_kernel, dil=d, n_back=WINDOWS[gi] // d, valid=valid,
                          scale=HEAD ** -0.5),
        grid=(n // nb,),
        in_specs=[spec(0), spec(1), spec(2),
                  pl.BlockSpec((nb,) + cache_t.shape[1:], lambda b: (b, 0, 0, 0, 0))],
        out_specs=[out_spec, out_spec],
        out_shape=[out_sds, out_sds],
        compiler_params=_cparams(("arbitrary",)),
        name=f"attn_sample_g{gi}",
    )(zqkv, zqkv, zqkv, cache_t)


def _post_kernel(x_ref, rwkv_ref, o0, o1, o2, l0, l1, l2, zg_ref, wbr_ref, wba_ref, wout_ref, g_ref,
                 w1_ref, w2_ref, p_ref, gp_ref, wg_ref, wp_ref, gf_ref, o_ref, h_scr, acc_scr, *, final):
    j = pl.program_id(1)

    @pl.when(j == 0)
    def _():
        ls = [l0[...], l1[...], l2[...]]
        m = jnp.maximum(jnp.maximum(ls[0], ls[1]), ls[2])
        es = [jnp.exp(l - m) for l in ls]
        attn = (es[0] * o0[...] + es[1] * o1[...] + es[2] * o2[...]) / (es[0] + es[1] + es[2])
        d = x_ref.shape[1]
        gates = zg_ref[...].astype(F32)
        mixed = (gates[:, :d] * _dot(rwkv_ref[...].astype(BF16), wbr_ref[...])
                 + gates[:, d:] * _dot(attn.astype(BF16), wba_ref[...]))
        x = x_ref[...] + _dot(mixed.astype(BF16), wout_ref[...])
        acc_scr[...] = x
        h_scr[...] = _rms(x, g_ref[...]).astype(BF16)

    u = jnp.maximum(_dot(h_scr[...], w1_ref[...]), 0.0)
    acc_scr[...] += _dot((u * u).astype(BF16), w2_ref[...])

    @pl.when(j == pl.num_programs(1) - 1)
    def _():
        x = acc_scr[...]
        h = _rms(x, gp_ref[...]).astype(BF16)
        x = x + _sigmoid(_dot(h, wg_ref[...])) * _dot(p_ref[...].astype(BF16), wp_ref[...])
        if final:
            x = _rms(x, gf_ref[...])
        o_ref[...] = x


def _post(x, rwkv, os_, ls_, zg, wbr, wba, wout, g, w1, w2, p, gp, wg, wp, gf, final, tm, tf):
    rows, d = x.shape
    dff = w1.shape[1]
    vec = lambda a: a.reshape(1, d)
    rowblk = lambda a: pl.BlockSpec((tm, a.shape[1]), lambda i, j: (i, 0))
    full = lambda a: pl.BlockSpec(a.shape, lambda i, j: (0, 0), pipeline_mode=pl.Buffered(1))
    row_args = [x, rwkv, *os_, *ls_, zg]
    return pl.pallas_call(
        functools.partial(_post_kernel, final=final),
        grid=(rows // tm, dff // tf),
        in_specs=[rowblk(a) for a in row_args] + [full(wbr), full(wba), full(wout), full(vec(g)),
                  pl.BlockSpec((d, tf), lambda i, j: (0, j)),
                  pl.BlockSpec((tf, d), lambda i, j: (j, 0)),
                  rowblk(p), full(vec(gp)), full(wg), full(wp), full(vec(gf))],
        out_specs=pl.BlockSpec((tm, d), lambda i, j: (i, 0)),
        out_shape=jax.ShapeDtypeStruct((rows, d), F32),
        scratch_shapes=[pltpu.VMEM((tm, d), BF16), pltpu.VMEM((tm, d), F32)],
        compiler_params=_cparams(("arbitrary", "arbitrary")),
        name="post",
    )(*row_args, wbr, wba, wout, vec(g), w1, w2, p, vec(gp), wg, wp, vec(gf))


def _layer(x, p, wkv0, shift0, caches, prm, norm_final, final, valid):
    n, t, d = x.shape
    rows = n * t
    width = prm["decay_w0"].shape[-1]
    rwkv_cols = prm["shift_mu"].shape[-1]
    aw = ATTN_HEADS * HEAD
    qkv_cols = 3 * len(DILATIONS) * aw
    tm = min(512, rows)
    x2 = x.reshape(rows, d)

    pick = lambda cols, opts: next(o for o in opts if cols % o == 0)
    z_rwkv, z_qkv, z_gate = _in_proj(x2, prm["norm_mix"], prm["w_in"].astype(BF16),
                                     (rwkv_cols, qkv_cols, 2 * d), tm)
    z_rwkv = z_rwkv.reshape(n, t, rwkv_cols)
    z_qkv = z_qkv.reshape(n, t, qkv_cols)

    if valid == t and t % WKV_CHUNK == 0:
        rwkv, s_t = _wkv_fused(z_rwkv, shift0, _pack_state(wkv0), prm, pick(t, (512, 256, WKV_CHUNK)), 4)
        wkv_t = _unpack_state(s_t, width // HEAD)
    else:
        feats = _rwkv_prep(z_rwkv, shift0, prm, SUBLANES if n % SUBLANES == 0 else 1)
        rwkv, wkv_t = _wkv(*feats, prm["lnx_w"], prm["lnx_b"], wkv0, valid)
    shift_t = z_rwkv[:, valid - 1, :]

    os_, ls_, new_kv = [], [], []
    ng = len(DILATIONS)
    for gi in range(ng):
        if caches is None:
            o, l = _attn_prompt(z_qkv, gi, max(2048, ATTN_BLOCK * DILATIONS[gi]))
            keep = min(WINDOWS[gi], t)
            kv_rows = z_qkv[:, t - keep:t]
        else:
            o, l = _attn_sample(z_qkv, caches[gi], gi, valid)
            kv_rows = z_qkv[:, :valid]
        new_kv.append(jnp.stack([kv_rows[:, :, (which * ng + gi) * aw:(which * ng + gi + 1) * aw]
                                 .reshape(n, -1, ATTN_HEADS, HEAD) for which in (1, 2)], axis=2))
        os_.append(o.reshape(rows, aw))
        ls_.append(l.reshape(rows, aw))

    bf = lambda name: prm[name].astype(BF16)
    x2 = _post(x2, rwkv.reshape(rows, -1), os_, ls_, z_gate, bf("w_branch_rwkv"), bf("w_branch_attn"),
               bf("w_out"), prm["norm_mlp"], bf("w_ff1"), bf("w_ff2"), p.reshape(rows, -1),
               prm["norm_ple"], bf("w_ple_gate"), bf("w_ple"), norm_final, final, tm, 2048)
    return x2.reshape(n, t, d), new_kv, wkv_t, shift_t


_LAYER_PARAMS = ("norm_mix", "w_in", "shift_mu", "decay_w0", "decay_w2", "aaa_a0", "aaa_a2", "gate_g2",
                 "k_k", "k_a", "r_k", "lnx_w", "lnx_b", "w_branch_rwkv", "w_branch_attn", "w_out",
                 "norm_mlp", "w_ff1", "w_ff2", "norm_ple", "w_ple", "w_ple_gate")


def kernel(x_prompt, x_sample, cache_kv_g0, cache_kv_g1, cache_kv_g2, state_wkv, state_shift, p_prompt, p_sample, norm_mix, w_in, shift_mu, decay_w0, decay_w2, aaa_a0, aaa_a2, gate_g2, k_k, k_a, r_k, lnx_w, lnx_b, w_branch_rwkv, w_branch_attn, w_out, norm_mlp, w_ff1, w_ff2, norm_ple, w_ple, w_ple_gate, norm_final):
    stacked = dict(zip(_LAYER_PARAMS, (norm_mix, w_in, shift_mu, decay_w0, decay_w2, aaa_a0, aaa_a2,
                                       gate_g2, k_k, k_a, r_k, lnx_w, lnx_b, w_branch_rwkv,
                                       w_branch_attn, w_out, norm_mlp, w_ff1, w_ff2, norm_ple, w_ple,
                                       w_ple_gate)))
    depth = norm_mix.shape[0]
    nb_p, _, d = x_prompt.shape
    nb_s, t_s, _ = x_sample.shape
    heads = decay_w0.shape[-1] // HEAD
    rwkv_cols = shift_mu.shape[-1]
    t_pad = -(-t_s // SUBLANES) * SUBLANES
    pad_t = lambda u: jnp.pad(u, ((0, 0), (0, t_pad - t_s), (0, 0)))

    yp, ys = x_prompt, pad_t(x_sample)
    kvp, kvs = [[], [], []], [[], [], []]
    wkv_p, sh_p, wkv_s, sh_s = [], [], [], []
    for i in range(depth):
        prm = {name: val[i] for name, val in stacked.items()}
        prm["r_k"] = prm["r_k"].reshape(-1)
        last = i == depth - 1
        wkv0 = jnp.zeros((nb_p, heads, HEAD, HEAD), F32)
        shift0 = jnp.zeros((nb_p, rwkv_cols), F32)
        yp, nkv_p, w_p, s_p = _layer(yp, p_prompt[i], wkv0, shift0, None, prm, norm_final, last,
                                     x_prompt.shape[1])
        ys, nkv_s, w_s, s_s = _layer(ys, pad_t(p_sample[i]), state_wkv[i], state_shift[i],
                                     (cache_kv_g0[i], cache_kv_g1[i], cache_kv_g2[i]), prm, norm_final,
                                     last, t_s)
        for gi in range(3):
            kvp[gi].append(nkv_p[gi])
            kvs[gi].append(nkv_s[gi])
        wkv_p.append(w_p)
        sh_p.append(s_p)
        wkv_s.append(w_s)
        sh_s.append(s_s)
    return (yp, ys[:, :t_s],
            jnp.stack(kvp[0]), jnp.stack(kvp[1]), jnp.stack(kvp[2]),
            jnp.stack(wkv_p), jnp.stack(sh_p),
            jnp.stack(kvs[0]), jnp.stack(kvs[1]), jnp.stack(kvs[2]),
            jnp.stack(wkv_s), jnp.stack(sh_s))
```
